```python
import jax, jax.numpy as jnp
from jax import lax
import numpy as np

D_MODEL = 1024
BATCH = 8
SEQ = 8192
DEPTH = 2

HEAD_DIM = 64
W_A = D_MODEL // 4
W_B = D_MODEL // 2
W_C = D_MODEL // 4
NH_A = W_A // HEAD_DIM
NH_B = W_B // HEAD_DIM
NG_C = 4
G_C = W_C // NG_C
N_HEADS_TOTAL = NH_A + NH_B + W_C // HEAD_DIM
MIX_WIDTH = W_A + W_B + W_C
IN_COLS = 2 * W_A + 3 * W_B + W_C
CHUNK = 128
SB_BLOCK = 128
POOL_WINDOWS = (2, 4, 8, 16)
D_FF = ((8 * D_MODEL // 3) + 127) // 128 * 128
CONV_WIDTH = 3
EPS = 1e-6

kernel_name = "hybrid_sgu_stickbreak_pool_convffn"


def rmsnorm(x, g):
    xf = x.astype(jnp.float32)
    y = xf * lax.rsqrt(jnp.mean(xf * xf, axis=-1, keepdims=True) + EPS)
    return (y * g.astype(jnp.float32)).astype(x.dtype)


def chunked_sgu(a, sgu_norm_g, sgu_w, sgu_b):
    B, S, _ = a.shape
    u, v = jnp.split(jax.nn.gelu(a, approximate=False), 2, axis=-1)
    v = rmsnorm(v.reshape(B, S, NH_A, HEAD_DIM), sgu_norm_g.reshape(NH_A, HEAD_DIM))
    v = v.reshape(B, S // CHUNK, CHUNK, NH_A, HEAD_DIM)
    tril = jnp.tril(jnp.ones((CHUNK, CHUNK), dtype=bool))
    wm = jnp.where(tril[None], sgu_w, jnp.zeros_like(sgu_w)).astype(v.dtype)
    s = jnp.einsum('hts,bcshd->bcthd', wm, v)
    s = s + jnp.transpose(sgu_b)[None, None, :, :, None].astype(v.dtype)
    return u * s.reshape(B, S, W_A)


def _stick_breaking_block(qb, k, v, q_start):
    T = qb.shape[2]
    S = k.shape[2]
    z = jnp.einsum('bhtd,bhsd->bhts', qb, k).astype(jnp.float32) * (HEAD_DIM ** -0.5)
    t_idx = q_start + jnp.arange(T, dtype=jnp.int32)
    s_idx = jnp.arange(S, dtype=jnp.int32)
    mask = (s_idx[None, :] < t_idx[:, None])[None, None]
    log_1m = jnp.where(mask, jax.nn.log_sigmoid(-z), 0.0)
    tail = lax.cumsum(log_1m, axis=3, reverse=True) - log_1m
    log_a = jax.nn.log_sigmoid(z) + tail
    att = jnp.where(mask, jnp.exp(log_a), 0.0)
    out = jnp.einsum('bhts,bhsd->bhtd', att, v.astype(jnp.float32))
    return out.astype(qb.dtype)


def stick_breaking_attention(q, k, v):
    B, S, H, d = q.shape
    nb = S // SB_BLOCK
    qt = jnp.transpose(q, (0, 2, 1, 3)).reshape(B, H, nb, SB_BLOCK, d)
    qt = jnp.transpose(qt, (2, 0, 1, 3, 4))
    kt = jnp.transpose(k, (0, 2, 1, 3))
    vt = jnp.transpose(v, (0, 2, 1, 3))
    starts = jnp.arange(nb, dtype=jnp.int32) * SB_BLOCK
    out = lax.map(lambda args: _stick_breaking_block(args[0], kt, vt, args[1]), (qt, starts))
    out = jnp.transpose(out, (1, 0, 3, 2, 4))
    return out.reshape(B, S, H * d)


def causal_pool_mixer(p, pool_w, pool_scale):
    B, S, _ = p.shape
    pf = p.astype(jnp.float32)
    csum = jnp.cumsum(pf, axis=1)
    pos = jnp.arange(1, S + 1, dtype=jnp.float32)
    outs = []
    for g, w in enumerate(POOL_WINDOWS):
        c = csum[..., g * G_C:(g + 1) * G_C]
        shifted = jnp.pad(c, ((0, 0), (w, 0), (0, 0)))[:, :S]
        cnt = jnp.minimum(pos, float(w))[None, :, None]
        d = (c - shifted) / cnt - pf[..., g * G_C:(g + 1) * G_C]
        outs.append(jnp.einsum('bsc,cd->bsd', d, pool_w[g].astype(jnp.float32)))
    y = jnp.concatenate(outs, axis=-1) * pool_scale.astype(jnp.float32)
    return y.astype(p.dtype)


def conv_gated_ffn(h, w_up, conv_w, conv_b, w_down):
    S = h.shape[1]
    z = h @ w_up
    zp = jnp.pad(z, ((0, 0), (CONV_WIDTH - 1, 0), (0, 0)))
    zc = conv_b
    for k in range(CONV_WIDTH):
        zc = zc + zp[:, k:k + S] * conv_w[k]
    g, u = jnp.split(zc, 2, axis=-1)
    return (jax.nn.silu(g) * u) @ w_down


def _fwd_setup_inputs(seed: int = 0) -> dict:
    key = jax.random.key(seed)
    ks = jax.random.split(key, 16)
    f32 = jnp.float32

    def nrm(k, shape, scale):
        return jax.random.normal(k, shape, f32) * scale

    return {
        "x": nrm(ks[0], (BATCH, SEQ, D_MODEL), 1.0),
        "norm1_g": 1.0 + nrm(ks[1], (DEPTH, D_MODEL), 0.05),
        "w_in": nrm(ks[2], (DEPTH, D_MODEL, IN_COLS), D_MODEL ** -0.5),
        "sgu_norm_g": 1.0 + nrm(ks[3], (DEPTH, W_A), 0.05),
        "sgu_w": nrm(ks[4], (DEPTH, NH_A, CHUNK, CHUNK), 0.05),
        "sgu_b": 1.0 + nrm(ks[5], (DEPTH, NH_A, CHUNK), 0.05),
        "pool_w": nrm(ks[6], (DEPTH, NG_C, G_C, G_C), G_C ** -0.5),
        "pool_scale": 1.0 + nrm(ks[7], (DEPTH, W_C), 0.05),
        "mix_norm_g": 1.0 + nrm(ks[8], (DEPTH, MIX_WIDTH), 0.05),
        "w_o": nrm(ks[9], (DEPTH, MIX_WIDTH, D_MODEL), MIX_WIDTH ** -0.5),
        "norm2_g": 1.0 + nrm(ks[10], (DEPTH, D_MODEL), 0.05),
        "w_up": nrm(ks[11], (DEPTH, D_MODEL, 2 * D_FF), D_MODEL ** -0.5),
        "conv_w": nrm(ks[12], (DEPTH, CONV_WIDTH, 2 * D_FF), CONV_WIDTH ** -0.5),
        "conv_b": nrm(ks[13], (DEPTH, 2 * D_FF), 0.01),
        "w_down": nrm(ks[14], (DEPTH, D_FF, D_MODEL), D_FF ** -0.5),
        "final_g": 1.0 + nrm(ks[15], (D_MODEL,), 0.05),
    }


def _fwd_reference(x, norm1_g, w_in, sgu_norm_g, sgu_w, sgu_b, pool_w, pool_scale,
              mix_norm_g, w_o, norm2_g, w_up, conv_w, conv_b, w_down, final_g):
    B, S, D = x.shape
    for l in range(DEPTH):
        h = rmsnorm(x, norm1_g[l])
        proj = h @ w_in[l]
        a_in = proj[..., :2 * W_A]
        qkv = proj[..., 2 * W_A:2 * W_A + 3 * W_B]
        p_in = proj[..., 2 * W_A + 3 * W_B:]
        q, k, v = jnp.split(qkv.reshape(B, S, 3, NH_B, HEAD_DIM), 3, axis=2)
        y_a = chunked_sgu(a_in, sgu_norm_g[l], sgu_w[l], sgu_b[l])
        y_b = stick_breaking_attention(q[:, :, 0], k[:, :, 0], v[:, :, 0])
        y_c = causal_pool_mixer(p_in, pool_w[l], pool_scale[l])
        y = jnp.concatenate([y_a, y_b, y_c], axis=-1).reshape(B, S, N_HEADS_TOTAL, HEAD_DIM)
        y = rmsnorm(y, mix_norm_g[l].reshape(N_HEADS_TOTAL, HEAD_DIM)).reshape(B, S, MIX_WIDTH)
        x = x + y @ w_o[l]
        h = rmsnorm(x, norm2_g[l])
        x = x + conv_gated_ffn(h, w_up[l], conv_w[l], conv_b[l], w_down[l])
    return rmsnorm(x, final_g)


import jax as _jax
import jax.numpy as _jnp

TWIN_FORMAT = 'train_step'
FWD_PARAMS = ['x', 'norm1_g', 'w_in', 'sgu_norm_g', 'sgu_w', 'sgu_b', 'pool_w', 'pool_scale', 'mix_norm_g', 'w_o', 'norm2_g', 'w_up', 'conv_w', 'conv_b', 'w_down', 'final_g']
TWIN_WEIGHTS = ['norm1_g', 'w_in', 'sgu_norm_g', 'sgu_w', 'sgu_b', 'pool_w', 'pool_scale', 'mix_norm_g', 'w_o', 'norm2_g', 'w_up', 'conv_w', 'conv_b', 'w_down', 'final_g']
TWIN_DIFF_INPUT = 'x'
TWIN_INPUTS = ['x', 'norm1_g', 'w_in', 'sgu_norm_g', 'sgu_w', 'sgu_b', 'pool_w', 'pool_scale', 'mix_norm_g', 'w_o', 'norm2_g', 'w_up', 'conv_w', 'conv_b', 'w_down', 'final_g', 'loss_target', 'm_norm1_g', 'm_w_in', 'm_sgu_norm_g', 'm_sgu_w', 'm_sgu_b', 'm_pool_w', 'm_pool_scale', 'm_mix_norm_g', 'm_w_o', 'm_norm2_g', 'm_w_up', 'm_conv_w', 'm_conv_b', 'm_w_down', 'm_final_g', 'v_norm1_g', 'v_w_in', 'v_sgu_norm_g', 'v_sgu_w', 'v_sgu_b', 'v_pool_w', 'v_pool_scale', 'v_mix_norm_g', 'v_w_o', 'v_norm2_g', 'v_w_up', 'v_conv_w', 'v_conv_b', 'v_w_down', 'v_final_g']
TWIN_OUTPUTS = ['loss', 'grad_x', 'grad_norm1_g', 'grad_w_in', 'grad_sgu_norm_g', 'grad_sgu_w', 'grad_sgu_b', 'grad_pool_w', 'grad_pool_scale', 'grad_mix_norm_g', 'grad_w_o', 'grad_norm2_g', 'grad_w_up', 'grad_conv_w', 'grad_conv_b', 'grad_w_down', 'grad_final_g', 'delta_norm1_g', 'delta_w_in', 'delta_sgu_norm_g', 'delta_sgu_w', 'delta_sgu_b', 'delta_pool_w', 'delta_pool_scale', 'delta_mix_norm_g', 'delta_w_o', 'delta_norm2_g', 'delta_w_up', 'delta_conv_w', 'delta_conv_b', 'delta_w_down', 'delta_final_g', 'new_m_norm1_g', 'new_m_w_in', 'new_m_sgu_norm_g', 'new_m_sgu_w', 'new_m_sgu_b', 'new_m_pool_w', 'new_m_pool_scale', 'new_m_mix_norm_g', 'new_m_w_o', 'new_m_norm2_g', 'new_m_w_up', 'new_m_conv_w', 'new_m_conv_b', 'new_m_w_down', 'new_m_final_g', 'new_v_norm1_g', 'new_v_w_in', 'new_v_sgu_norm_g', 'new_v_sgu_w', 'new_v_sgu_b', 'new_v_pool_w', 'new_v_pool_scale', 'new_v_mix_norm_g', 'new_v_w_o', 'new_v_norm2_g', 'new_v_w_up', 'new_v_conv_w', 'new_v_conv_b', 'new_v_w_down', 'new_v_final_g']
TWIN_LEAF_KINDS = {'loss': 'loss', 'grad_x': 'grad_x', 'grad_norm1_g': 'grad_w', 'grad_w_in': 'grad_w', 'grad_sgu_norm_g': 'grad_w', 'grad_sgu_w': 'grad_w', 'grad_sgu_b': 'grad_w', 'grad_pool_w': 'grad_w', 'grad_pool_scale': 'grad_w', 'grad_mix_norm_g': 'grad_w', 'grad_w_o': 'grad_w', 'grad_norm2_g': 'grad_w', 'grad_w_up': 'grad_w', 'grad_conv_w': 'grad_w', 'grad_conv_b': 'grad_w', 'grad_w_down': 'grad_w', 'grad_final_g': 'grad_w', 'delta_norm1_g': 'delta_w', 'delta_w_in': 'delta_w', 'delta_sgu_norm_g': 'delta_w', 'delta_sgu_w': 'delta_w', 'delta_sgu_b': 'delta_w', 'delta_pool_w': 'delta_w', 'delta_pool_scale': 'delta_w', 'delta_mix_norm_g': 'delta_w', 'delta_w_o': 'delta_w', 'delta_norm2_g': 'delta_w', 'delta_w_up': 'delta_w', 'delta_conv_w': 'delta_w', 'delta_conv_b': 'delta_w', 'delta_w_down': 'delta_w', 'delta_final_g': 'delta_w', 'new_m_norm1_g': 'new_m', 'new_m_w_in': 'new_m', 'new_m_sgu_norm_g': 'new_m', 'new_m_sgu_w': 'new_m', 'new_m_sgu_b': 'new_m', 'new_m_pool_w': 'new_m', 'new_m_pool_scale': 'new_m', 'new_m_mix_norm_g': 'new_m', 'new_m_w_o': 'new_m', 'new_m_norm2_g': 'new_m', 'new_m_w_up': 'new_m', 'new_m_conv_w': 'new_m', 'new_m_conv_b': 'new_m', 'new_m_w_down': 'new_m', 'new_m_final_g': 'new_m', 'new_v_norm1_g': 'new_v', 'new_v_w_in': 'new_v', 'new_v_sgu_norm_g': 'new_v', 'new_v_sgu_w': 'new_v', 'new_v_sgu_b': 'new_v', 'new_v_pool_w': 'new_v', 'new_v_pool_scale': 'new_v', 'new_v_mix_norm_g': 'new_v', 'new_v_w_o': 'new_v', 'new_v_norm2_g': 'new_v', 'new_v_w_up': 'new_v', 'new_v_conv_w': 'new_v', 'new_v_conv_b': 'new_v', 'new_v_w_down': 'new_v', 'new_v_final_g': 'new_v'}


def _forward(args):
    return _fwd_reference(*[args[k] for k in FWD_PARAMS])


def _output_shape():
    out = _jax.eval_shape(lambda: _forward(_fwd_setup_inputs(0)))
    return out.shape, out.dtype

N_MICROBATCH = 1
ADAM_LR = 0.001
ADAM_B1 = 0.9
ADAM_B2 = 0.999
ADAM_EPS = 1e-08
ADAM_WD = 0.01
ADAM_STEP = 10
PER_EXAMPLE_BATCH_AXIS = {'x': 0, 'loss_target': 0}
SHARED_INPUTS = []
_WEIGHT_DTYPES = {'norm1_g': _jnp.float32, 'w_in': _jnp.float32, 'sgu_norm_g': _jnp.float32, 'sgu_w': _jnp.float32, 'sgu_b': _jnp.float32, 'pool_w': _jnp.float32, 'pool_scale': _jnp.float32, 'mix_norm_g': _jnp.float32, 'w_o': _jnp.float32, 'norm2_g': _jnp.float32, 'w_up': _jnp.float32, 'conv_w': _jnp.float32, 'conv_b': _jnp.float32, 'w_down': _jnp.float32, 'final_g': _jnp.float32}
MOMENT_SCALE = {'norm1_g': 2.123510e-01, 'w_in': 1.404927e-01, 'sgu_norm_g': 7.288944e-02, 'sgu_w': 7.898951e-02, 'sgu_b': 4.994058e-02, 'pool_w': 1.891487e-01, 'pool_scale': 2.012790e-01, 'mix_norm_g': 1.976821e-01, 'w_o': 1.949353e-01, 'norm2_g': 1.494008e-01, 'w_up': 5.986381e-02, 'conv_w': 6.067005e-02, 'conv_b': 6.527453e-02, 'w_down': 9.881153e-02, 'final_g': 6.404483e+01}


def _to_microbatches(a, axis):
    t = _jnp.moveaxis(a, axis, 0)
    t = t.reshape((N_MICROBATCH, t.shape[0] // N_MICROBATCH) + t.shape[1:])
    return _jnp.moveaxis(t, 1, axis + 1)


def setup_inputs(seed: int = 0) -> dict:
    inp = _fwd_setup_inputs(seed)
    key = _jax.random.fold_in(_jax.random.key(seed), 7919)
    shape, _ = _output_shape()
    out = dict(inp)
    out["loss_target"] = _jax.random.normal(_jax.random.fold_in(key, 0), shape, _jnp.float32)
    for i, name in enumerate(TWIN_WEIGHTS):
        w = inp[name].astype(_jnp.float32)
        if MOMENT_SCALE is None:
            s = _jnp.sqrt(_jnp.mean(_jnp.square(w)) + 1e-30)
        else:
            s = MOMENT_SCALE[name]
        km, kv = _jax.random.split(_jax.random.fold_in(key, i + 1))
        out[name] = w
        out["m_" + name] = s * _jax.random.normal(km, w.shape, _jnp.float32)
        out["v_" + name] = (s * s) * _jax.random.uniform(kv, w.shape, _jnp.float32, 0.5, 1.5)
    if N_MICROBATCH > 1:
        for name, axis in PER_EXAMPLE_BATCH_AXIS.items():
            out[name] = _to_microbatches(out[name], axis)
    return {'x': out['x'], 'norm1_g': out['norm1_g'], 'w_in': out['w_in'], 'sgu_norm_g': out['sgu_norm_g'], 'sgu_w': out['sgu_w'], 'sgu_b': out['sgu_b'], 'pool_w': out['pool_w'], 'pool_scale': out['pool_scale'], 'mix_norm_g': out['mix_norm_g'], 'w_o': out['w_o'], 'norm2_g': out['norm2_g'], 'w_up': out['w_up'], 'conv_w': out['conv_w'], 'conv_b': out['conv_b'], 'w_down': out['w_down'], 'final_g': out['final_g'], 'loss_target': out['loss_target'], 'm_norm1_g': out['m_norm1_g'], 'm_w_in': out['m_w_in'], 'm_sgu_norm_g': out['m_sgu_norm_g'], 'm_sgu_w': out['m_sgu_w'], 'm_sgu_b': out['m_sgu_b'], 'm_pool_w': out['m_pool_w'], 'm_pool_scale': out['m_pool_scale'], 'm_mix_norm_g': out['m_mix_norm_g'], 'm_w_o': out['m_w_o'], 'm_norm2_g': out['m_norm2_g'], 'm_w_up': out['m_w_up'], 'm_conv_w': out['m_conv_w'], 'm_conv_b': out['m_conv_b'], 'm_w_down': out['m_w_down'], 'm_final_g': out['m_final_g'], 'v_norm1_g': out['v_norm1_g'], 'v_w_in': out['v_w_in'], 'v_sgu_norm_g': out['v_sgu_norm_g'], 'v_sgu_w': out['v_sgu_w'], 'v_sgu_b': out['v_sgu_b'], 'v_pool_w': out['v_pool_w'], 'v_pool_scale': out['v_pool_scale'], 'v_mix_norm_g': out['v_mix_norm_g'], 'v_w_o': out['v_w_o'], 'v_norm2_g': out['v_norm2_g'], 'v_w_up': out['v_w_up'], 'v_conv_w': out['v_conv_w'], 'v_conv_b': out['v_conv_b'], 'v_w_down': out['v_w_down'], 'v_final_g': out['v_final_g']}


def _loss(weights, diff, rest, loss_target):
    with _jax.named_scope("forward"):
        args = {**rest, TWIN_DIFF_INPUT: diff, **{k: w.astype(_WEIGHT_DTYPES[k]) for k, w in weights.items()}}
        y = _forward(args)
    with _jax.named_scope("loss_head"):
        err = _jnp.square(y.astype(_jnp.float32) - loss_target)
        return 0.5 * _jnp.sum(_jnp.mean(err, axis=-1)) if err.ndim else 0.5 * err


def _adamw(w, g, m, v):
    m = ADAM_B1 * m + (1.0 - ADAM_B1) * g
    v = ADAM_B2 * v + (1.0 - ADAM_B2) * _jnp.square(g)
    m_hat = m / (1.0 - ADAM_B1 ** ADAM_STEP)
    v_hat = v / (1.0 - ADAM_B2 ** ADAM_STEP)
    delta = -ADAM_LR * (m_hat / (_jnp.sqrt(v_hat) + ADAM_EPS) + ADAM_WD * w)
    return delta, m, v


def reference(x, norm1_g, w_in, sgu_norm_g, sgu_w, sgu_b, pool_w, pool_scale, mix_norm_g, w_o, norm2_g, w_up, conv_w, conv_b, w_down, final_g, loss_target, m_norm1_g, m_w_in, m_sgu_norm_g, m_sgu_w, m_sgu_b, m_pool_w, m_pool_scale, m_mix_norm_g, m_w_o, m_norm2_g, m_w_up, m_conv_w, m_conv_b, m_w_down, m_final_g, v_norm1_g, v_w_in, v_sgu_norm_g, v_sgu_w, v_sgu_b, v_pool_w, v_pool_scale, v_mix_norm_g, v_w_o, v_norm2_g, v_w_up, v_conv_w, v_conv_b, v_w_down, v_final_g):
    given = dict(x=x, norm1_g=norm1_g, w_in=w_in, sgu_norm_g=sgu_norm_g, sgu_w=sgu_w, sgu_b=sgu_b, pool_w=pool_w, pool_scale=pool_scale, mix_norm_g=mix_norm_g, w_o=w_o, norm2_g=norm2_g, w_up=w_up, conv_w=conv_w, conv_b=conv_b, w_down=w_down, final_g=final_g, loss_target=loss_target, m_norm1_g=m_norm1_g, m_w_in=m_w_in, m_sgu_norm_g=m_sgu_norm_g, m_sgu_w=m_sgu_w, m_sgu_b=m_sgu_b, m_pool_w=m_pool_w, m_pool_scale=m_pool_scale, m_mix_norm_g=m_mix_norm_g, m_w_o=m_w_o, m_norm2_g=m_norm2_g, m_w_up=m_w_up, m_conv_w=m_conv_w, m_conv_b=m_conv_b, m_w_down=m_w_down, m_final_g=m_final_g, v_norm1_g=v_norm1_g, v_w_in=v_w_in, v_sgu_norm_g=v_sgu_norm_g, v_sgu_w=v_sgu_w, v_sgu_b=v_sgu_b, v_pool_w=v_pool_w, v_pool_scale=v_pool_scale, v_mix_norm_g=v_mix_norm_g, v_w_o=v_w_o, v_norm2_g=v_norm2_g, v_w_up=v_w_up, v_conv_w=v_conv_w, v_conv_b=v_conv_b, v_w_down=v_w_down, v_final_g=v_final_g)
    weights = {n: given[n] for n in TWIN_WEIGHTS}
    shared = {n: given[n] for n in SHARED_INPUTS}
    per_example = {n: given[n] for n in ['x']}
    grad_fn = _jax.value_and_grad(_loss, argnums=(0, 1))

    def one_microbatch(ex, loss_target):
        ex = dict(ex)
        diff = ex.pop(TWIN_DIFF_INPUT)
        return grad_fn(weights, diff, {**shared, **ex}, loss_target)

    if N_MICROBATCH == 1:
        loss, (grad_w, grad_x) = one_microbatch(per_example, given["loss_target"])
    else:
        def body(carry, xs):
            loss_sum, grad_sum = carry
            l_k, (gw_k, gx_k) = one_microbatch(xs[0], xs[1])
            with _jax.named_scope("update"):
                return (loss_sum + l_k, _jax.tree.map(_jnp.add, grad_sum, gw_k)), gx_k

        init = (_jnp.zeros((), _jnp.float32), _jax.tree.map(_jnp.zeros_like, weights))
        (loss, grad_w), grad_x = _jax.lax.scan(body, init, (per_example, given["loss_target"]))
    with _jax.named_scope("update"):
        delta_w, new_m, new_v = {}, {}, {}
        for n in TWIN_WEIGHTS:
            delta_w[n], new_m[n], new_v[n] = _adamw(weights[n], grad_w[n], given["m_" + n], given["v_" + n])
    return (loss, grad_x, *[grad_w[n] for n in TWIN_WEIGHTS], *[delta_w[n] for n in TWIN_WEIGHTS],
            *[new_m[n] for n in TWIN_WEIGHTS], *[new_v[n] for n in TWIN_WEIGHTS])
```

```python
import functools
import math

import jax
import jax.numpy as jnp
import numpy as np
from jax import lax
from jax.experimental import pallas as pl
from jax.experimental.pallas import tpu as pltpu

F32 = jnp.float32
BF16 = jnp.bfloat16
_MXU = jnp.bfloat16

D_MODEL = 1024
W_A = 256
W_B = 512
W_C = 256
HEAD_DIM = 64
IN_COLS = 2 * W_A + 3 * W_B + W_C
D_FF = 2816
CHUNK = 128
POOL_WINDOWS = (2, 4, 8, 16)
EPS = 1e-6
N_DEV = 8
LANES = 128
SUBLANES = 8
VMEM_LIMIT = 48 * 1024 * 1024

ADAM_LR = 0.001
ADAM_B1 = 0.9
ADAM_B2 = 0.999
ADAM_EPS = 1e-08
ADAM_WD = 0.01
ADAM_STEP = 10

INV_SQRT2 = 1.0 / math.sqrt(2.0)
INV_SQRT_2PI = 1.0 / math.sqrt(2.0 * math.pi)


def _pcall(body, **kw):
    return pl.pallas_call(body, **kw)


def _params(dims=None):
    return pltpu.CompilerParams(dimension_semantics=dims, vmem_limit_bytes=VMEM_LIMIT)


def _tile(n, pref, mult=SUBLANES):
    t = min(n, pref) // mult * mult
    while t >= mult:
        if n % t == 0:
            return t
        t -= mult
    return n


def _iota(shape, dim):
    return lax.broadcasted_iota(jnp.int32, shape, dim)


def _dot(a, b):
    return jnp.dot(a.astype(_MXU), b.astype(_MXU), preferred_element_type=F32)


def _dot_nt(a, b):
    return lax.dot_general(a.astype(_MXU), b.astype(_MXU), (((1,), (1,)), ((), ())), preferred_element_type=F32)


def _dot_tn(a, b):
    return lax.dot_general(a.astype(_MXU), b.astype(_MXU), (((0,), (0,)), ((), ())), preferred_element_type=F32)


def _split(x):
    hi = x.astype(BF16)
    lo = (x - hi.astype(F32)).astype(BF16)
    return hi, lo


def _dot_sel(x, sel):
    hi, lo = _split(x)
    return jnp.dot(hi, sel, preferred_element_type=F32) + jnp.dot(lo, sel, preferred_element_type=F32)


def _sel_dot(sel, x):
    hi, lo = _split(x)
    return jnp.dot(sel, hi, preferred_element_type=F32) + jnp.dot(sel, lo, preferred_element_type=F32)


def _group_mat(n):
    r = jnp.right_shift(_iota((n, n), 0), 6)
    c = jnp.right_shift(_iota((n, n), 1), 6)
    return (r == c).astype(BF16)


def _lane_group_mask(n, g):
    lane = _iota((1, n), 1)
    return (lane >= g * HEAD_DIM) & (lane < (g + 1) * HEAD_DIM)


def _gelu(a):
    return 0.5 * a * (1.0 + lax.erf(a * INV_SQRT2))


def _gelu_grad(a):
    return 0.5 * (1.0 + lax.erf(a * INV_SQRT2)) + a * jnp.exp(-0.5 * a * a) * INV_SQRT_2PI


def _rms(x):
    return lax.rsqrt(jnp.mean(x * x, axis=-1, keepdims=True) + EPS)


def _rms_bwd(x, g, dy):
    r = _rms(x)
    dyg = dy * g
    m2 = jnp.mean(dyg * x, axis=-1, keepdims=True)
    dx = r * dyg - x * (r * r * r) * m2
    dg = jnp.sum(dy * x * r, axis=0, keepdims=True)
    return dx, dg


def _shift_down(cur, prev8, k):
    rolled = pltpu.roll(cur, k, 0)
    row8 = _iota(prev8.shape, 0)
    top = jnp.where(row8 < k, pltpu.roll(prev8, k, 0), rolled[0:SUBLANES])
    return jnp.concatenate([top, rolled[SUBLANES:]], axis=0)


def _shift_up(cur, next8, k):
    n = cur.shape[0]
    rolled = pltpu.roll(cur, n - k, 0)
    row8 = _iota(next8.shape, 0)
    bot = jnp.where(row8 >= SUBLANES - k, pltpu.roll(next8, SUBLANES - k, 0), rolled[n - SUBLANES:])
    return jnp.concatenate([rolled[:n - SUBLANES], bot], axis=0)


def _mesh_pos():
    return lax.axis_index("x"), lax.axis_index("y"), lax.axis_index("c")


def _peer(x, y, c, k):
    px = 1 - x if k & 4 else x
    py = 1 - y if k & 2 else y
    pc = 1 - c if k & 1 else c
    return px, py, pc


def _exchange(arrs, name, scatter):
    n = len(arrs)
    out_shapes = [jax.ShapeDtypeStruct(a.shape if scatter else (N_DEV,) + a.shape, a.dtype) for a in arrs]

    def body(*refs):
        ins, outs = refs[:n], refs[n:2 * n]
        send, recv, loc = refs[2 * n:]
        x, y, c = _mesh_pos()
        me = 4 * x + 2 * y + c

        def src(a, idx):
            return ins[a].at[idx] if scatter else ins[a]

        locals_ = [pltpu.make_async_copy(src(a, me), outs[a].at[me], loc.at[a]) for a in range(n)]
        for cp in locals_:
            cp.start()
        copies = []
        for k in range(1, N_DEV):
            px, py, pc = _peer(x, y, c, k)
            pidx = 4 * px + 2 * py + pc
            for a in range(n):
                s = a * (N_DEV - 1) + k - 1
                pltpu.make_async_remote_copy(
                    src_ref=src(a, pidx), dst_ref=outs[a].at[me], send_sem=send.at[s], recv_sem=recv.at[s],
                    device_id=(px, py, pc), device_id_type=pl.DeviceIdType.MESH).start()
                copies.append(pltpu.make_async_remote_copy(
                    src_ref=src(a, pidx), dst_ref=outs[a].at[pidx], send_sem=send.at[s], recv_sem=recv.at[s],
                    device_id=(px, py, pc), device_id_type=pl.DeviceIdType.MESH))
        for cp in copies:
            cp.wait()
        for cp in locals_:
            cp.wait()

    hbm = pl.BlockSpec(memory_space=pl.ANY)
    return _pcall(
        body, name=name, out_shape=out_shapes, in_specs=[hbm] * n, out_specs=[hbm] * n,
        scratch_shapes=[pltpu.SemaphoreType.DMA((n * (N_DEV - 1),)), pltpu.SemaphoreType.DMA((n * (N_DEV - 1),)),
                        pltpu.SemaphoreType.DMA((n,))],
        compiler_params=pltpu.CompilerParams(has_side_effects=True),
    )(*arrs)


def _mm_tn(a, b, name):
    s, m = a.shape
    n = b.shape[1]
    tm, tn, tk = _tile(m, 1024, LANES), _tile(n, 512, LANES), _tile(s, 1024)

    def body(a_ref, b_ref, o_ref):
        @pl.when(pl.program_id(2) == 0)
        def _():
            o_ref[...] = jnp.zeros_like(o_ref)
        o_ref[...] += _dot_tn(a_ref[...], b_ref[...])

    return _pcall(
        body, name=name, grid=(m // tm, n // tn, s // tk),
        in_specs=[pl.BlockSpec((tk, tm), lambda i, j, k: (k, i)), pl.BlockSpec((tk, tn), lambda i, j, k: (k, j))],
        out_specs=pl.BlockSpec((tm, tn), lambda i, j, k: (i, j)),
        out_shape=jax.ShapeDtypeStruct((m, n), F32),
        compiler_params=_params(("parallel", "parallel", "arbitrary")),
    )(a, b)


def _inproj_fwd(x, g, w, name):
    s = x.shape[0]
    tm = _tile(s, 256)

    def body(x_ref, g_ref, w_ref, a_ref, qkv_ref, p_ref, h_ref):
        xv = x_ref[...]
        h = (xv * _rms(xv) * g_ref[...]).astype(_MXU)
        h_ref[...] = h
        a_ref[...] = jnp.dot(h, w_ref[:, 0:2 * W_A], preferred_element_type=F32)
        q = jnp.dot(h, w_ref[:, 2 * W_A:2 * W_A + W_B], preferred_element_type=F32)
        qkv_ref[:, 0:W_B] = (q * 0.125).astype(_MXU)
        kv = jnp.dot(h, w_ref[:, 2 * W_A + W_B:2 * W_A + 3 * W_B], preferred_element_type=F32)
        qkv_ref[:, W_B:3 * W_B] = kv.astype(_MXU)
        p_ref[...] = jnp.dot(h, w_ref[:, 2 * W_A + 3 * W_B:IN_COLS], preferred_element_type=F32)

    row = lambda n: pl.BlockSpec((tm, n), lambda i: (i, 0))
    return _pcall(
        body, name=name, grid=(s // tm,),
        in_specs=[row(D_MODEL), pl.BlockSpec((1, D_MODEL), lambda i: (0, 0)),
                  pl.BlockSpec((D_MODEL, IN_COLS), lambda i: (0, 0))],
        out_specs=[row(2 * W_A), row(3 * W_B), row(W_C), row(D_MODEL)],
        out_shape=[jax.ShapeDtypeStruct((s, 2 * W_A), F32), jax.ShapeDtypeStruct((s, 3 * W_B), _MXU),
                   jax.ShapeDtypeStruct((s, W_C), F32), jax.ShapeDtypeStruct((s, D_MODEL), _MXU)],
        compiler_params=_params(("parallel",)),
    )(x, g, w)


def _inproj_bwd(da, dq, dk, dv, dp, wt, x, g, dres, name):
    s = x.shape[0]
    tm = _tile(s, 256)

    def body(da_ref, dq_ref, dk_ref, dv_ref, dp_ref, wt_ref, x_ref, g_ref, dres_ref, dx_ref, dg_ref):
        dh = _dot(da_ref[...], wt_ref[0:2 * W_A, :])
        dh += _dot(dq_ref[...], wt_ref[2 * W_A:2 * W_A + W_B, :])
        dh += _dot(dk_ref[...], wt_ref[2 * W_A + W_B:2 * W_A + 2 * W_B, :])
        dh += _dot(dv_ref[...], wt_ref[2 * W_A + 2 * W_B:2 * W_A + 3 * W_B, :])
        dh += _dot(dp_ref[...], wt_ref[2 * W_A + 3 * W_B:IN_COLS, :])
        dx, dg = _rms_bwd(x_ref[...], g_ref[...], dh)
        dx_ref[...] = dres_ref[...] + dx

        @pl.when(pl.program_id(0) == 0)
        def _():
            dg_ref[...] = jnp.zeros_like(dg_ref)
        dg_ref[...] += dg

    row = lambda n: pl.BlockSpec((tm, n), lambda i: (i, 0))
    vec = pl.BlockSpec((1, D_MODEL), lambda i: (0, 0))
    return _pcall(
        body, name=name, grid=(s // tm,),
        in_specs=[row(2 * W_A), row(W_B), row(W_B), row(W_B), row(W_C),
                  pl.BlockSpec((IN_COLS, D_MODEL), lambda i: (0, 0)), row(D_MODEL), vec, row(D_MODEL)],
        out_specs=[row(D_MODEL), vec],
        out_shape=[jax.ShapeDtypeStruct((s, D_MODEL), F32), jax.ShapeDtypeStruct((1, D_MODEL), F32)],
        compiler_params=_params(("arbitrary",)),
    )(da, dq, dk, dv, dp, wt, x, g, dres)


def _sgu_core(a, gn, wm_ref, bias):
    ga = _gelu(a)
    u, v0 = ga[:, 0:W_A], ga[:, W_A:2 * W_A]
    r = lax.rsqrt(_dot_sel(v0 * v0, _group_mat(W_A)) * (1.0 / HEAD_DIM) + EPS)
    vn = v0 * r * gn
    sv = bias
    for h in range(W_A // HEAD_DIM):
        sv = sv + _dot(wm_ref[h], jnp.where(_lane_group_mask(W_A, h), vn, 0.0))
    return u, v0, r, vn, sv


def _sgu_fwd(a, gn, wm, bias, name):
    s = a.shape[0]

    def body(a_ref, gn_ref, wm_ref, b_ref, y_ref):
        u, _, _, _, sv = _sgu_core(a_ref[...], gn_ref[...], wm_ref, b_ref[...])
        y_ref[...] = u * sv

    return _pcall(
        body, name=name, grid=(s // CHUNK,),
        in_specs=[pl.BlockSpec((CHUNK, 2 * W_A), lambda i: (i, 0)), pl.BlockSpec((1, W_A), lambda i: (0, 0)),
                  pl.BlockSpec((4, CHUNK, CHUNK), lambda i: (0, 0, 0)), pl.BlockSpec((CHUNK, W_A), lambda i: (0, 0))],
        out_specs=pl.BlockSpec((CHUNK, W_A), lambda i: (i, 0)),
        out_shape=jax.ShapeDtypeStruct((s, W_A), F32),
        compiler_params=_params(("parallel",)),
    )(a, gn, wm, bias)


def _sgu_bwd(a, dy, gn, wm, bias, name):
    s = a.shape[0]

    def body(a_ref, dy_ref, gn_ref, wm_ref, b_ref, da_ref, dwm_ref, db_ref, dgn_ref):
        @pl.when(pl.program_id(0) == 0)
        def _():
            dwm_ref[...] = jnp.zeros_like(dwm_ref)
            db_ref[...] = jnp.zeros_like(db_ref)
            dgn_ref[...] = jnp.zeros_like(dgn_ref)

        av, gnv, dyv = a_ref[...], gn_ref[...], dy_ref[...]
        u, v0, r, vn, sv = _sgu_core(av, gnv, wm_ref, b_ref[...])
        du = dyv * sv
        ds = dyv * u
        db_ref[...] += ds
        tril = _iota((CHUNK, CHUNK), 1) <= _iota((CHUNK, CHUNK), 0)
        dvn = jnp.zeros_like(vn)
        for h in range(W_A // HEAD_DIM):
            dsm = jnp.where(_lane_group_mask(W_A, h), ds, 0.0)
            dwm_ref[h] += jnp.where(tril, _dot_nt(dsm, vn), 0.0)
            dvn = dvn + _dot_tn(wm_ref[h], dsm)
        dgn_ref[...] += jnp.sum(dvn * v0 * r, axis=0, keepdims=True)
        dvg = dvn * gnv
        m2 = _dot_sel(dvg * v0, _group_mat(W_A)) * (1.0 / HEAD_DIM)
        dv0 = r * dvg - v0 * (r * r * r) * m2
        gp = _gelu_grad(av)
        da_ref[:, 0:W_A] = du * gp[:, 0:W_A]
        da_ref[:, W_A:2 * W_A] = dv0 * gp[:, W_A:2 * W_A]

    return _pcall(
        body, name=name, grid=(s // CHUNK,),
        in_specs=[pl.BlockSpec((CHUNK, 2 * W_A), lambda i: (i, 0)), pl.BlockSpec((CHUNK, W_A), lambda i: (i, 0)),
                  pl.BlockSpec((1, W_A), lambda i: (0, 0)), pl.BlockSpec((4, CHUNK, CHUNK), lambda i: (0, 0, 0)),
                  pl.BlockSpec((CHUNK, W_A), lambda i: (0, 0))],
        out_specs=[pl.BlockSpec((CHUNK, 2 * W_A), lambda i: (i, 0)), pl.BlockSpec((4, CHUNK, CHUNK), lambda i: (0, 0, 0)),
                   pl.BlockSpec((CHUNK, W_A), lambda i: (0, 0)), pl.BlockSpec((1, W_A), lambda i: (0, 0))],
        out_shape=[jax.ShapeDtypeStruct((s, 2 * W_A), F32), jax.ShapeDtypeStruct((4, CHUNK, CHUNK), F32),
                   jax.ShapeDtypeStruct((CHUNK, W_A), F32), jax.ShapeDtypeStruct((1, W_A), F32)],
        compiler_params=_params(("arbitrary",)),
    )(a, dy, gn, wm, bias)


KBLK = 128


def _sb_scores(qh, kb, mask, tri, carry):
    z = _dot_nt(qh, kb)
    e = jnp.exp(-jnp.abs(z))
    l = jnp.where(mask, -(jnp.maximum(z, 0.0) + jnp.log(1.0 + e)), 0.0)
    a = jnp.where(mask, jnp.exp(z + carry + _dot_sel(l, tri)), 0.0)
    return z, e, l, a


def _attn_fwd(qkv, name):
    s = qkv.shape[0]
    tq = _tile(s, 256, KBLK)
    npairs = W_B // LANES
    assert s // KBLK <= LANES

    def body(q_ref, k_ref, v_ref, o_ref, cm_ref):
        i = pl.program_id(1)
        q = q_ref[...]
        lane = _iota((1, LANES), 1)
        lane_lo = lane < HEAD_DIM
        hmask = (lane_lo, jnp.logical_not(lane_lo))
        qh = [jnp.where(m, q, jnp.zeros_like(q)) for m in hmask]
        tri = (_iota((KBLK, KBLK), 0) >= _iota((KBLK, KBLK), 1)).astype(BF16)
        row = _iota((tq, KBLK), 0) + i * tq
        col0 = _iota((tq, KBLK), 1)
        nk = (i + 1) * (tq // KBLK)

        cm_ref[...] = jnp.zeros_like(cm_ref)

        def step(it, carry):
            c0, c1, acc = carry
            jj = nk - 1 - it
            ks = pl.multiple_of(jj * KBLK, KBLK)
            kb = k_ref[pl.ds(ks, KBLK), :]
            vb = v_ref[pl.ds(ks, KBLK), :]
            mask = (col0 + jj * KBLK) < row
            onehot = jnp.where(lane == jj, 1.0, 0.0)
            cs = [c0, c1]
            for hh in range(2):
                _, _, l, a = _sb_scores(qh[hh], kb, mask, tri, cs[hh])
                acc = acc + _dot(a, jnp.where(hmask[hh], vb, jnp.zeros_like(vb)))
                cm_ref[:, hh * LANES:(hh + 1) * LANES] += onehot * cs[hh]
                cs[hh] = cs[hh] + jnp.sum(l, axis=1, keepdims=True)
            return cs[0], cs[1], acc

        zc = jnp.zeros((tq, 1), F32)
        res = lax.fori_loop(0, nk, step, (zc, zc, jnp.zeros((tq, LANES), F32)))
        o_ref[...] = res[2]

    return _pcall(
        body, name=name, grid=(npairs, s // tq),
        in_specs=[pl.BlockSpec((tq, LANES), lambda p, i: (i, p)),
                  pl.BlockSpec((s, LANES), lambda p, i: (0, npairs + p)),
                  pl.BlockSpec((s, LANES), lambda p, i: (0, 2 * npairs + p))],
        out_specs=[pl.BlockSpec((tq, LANES), lambda p, i: (i, p)), pl.BlockSpec((tq, 2 * LANES), lambda p, i: (i, p))],
        out_shape=[jax.ShapeDtypeStruct((s, W_B), F32), jax.ShapeDtypeStruct((s, 2 * W_B), F32)],
        compiler_params=_params(("parallel", "parallel")),
    )(qkv, qkv, qkv)


def _attn_bwd(qkv, cm, do, name):
    s = qkv.shape[0]
    tq = _tile(s, 256, KBLK)
    npairs = W_B // LANES

    def body(q_ref, k_ref, v_ref, cm_ref, do_ref, dq_ref, dk_ref, dv_ref):
        i = pl.program_id(1)

        @pl.when(i == 0)
        def _():
            dk_ref[...] = jnp.zeros_like(dk_ref)
            dv_ref[...] = jnp.zeros_like(dv_ref)

        q = q_ref[...]
        dov = do_ref[...]
        lane = _iota((1, LANES), 1)
        lane_lo = lane < HEAD_DIM
        hmask = (lane_lo, jnp.logical_not(lane_lo))
        qh = [jnp.where(m, q, jnp.zeros_like(q)) for m in hmask]
        doh = [jnp.where(m, dov, 0.0).astype(_MXU) for m in hmask]
        cms = [cm_ref[:, 0:LANES], cm_ref[:, LANES:2 * LANES]]
        tri = (_iota((KBLK, KBLK), 0) >= _iota((KBLK, KBLK), 1)).astype(BF16)
        tri_prefix = (_iota((KBLK, KBLK), 0) <= _iota((KBLK, KBLK), 1)).astype(BF16)
        row = _iota((tq, KBLK), 0) + i * tq
        col0 = _iota((tq, KBLK), 1)
        nk = (i + 1) * (tq // KBLK)

        def step(jj, carry):
            g0, g1, dq = carry
            ks = pl.multiple_of(jj * KBLK, KBLK)
            kb = k_ref[pl.ds(ks, KBLK), :]
            vb = v_ref[pl.ds(ks, KBLK), :]
            mask = (col0 + jj * KBLK) < row
            gs = [g0, g1]
            for hh in range(2):
                c = jnp.sum(jnp.where(lane == jj, cms[hh], 0.0), axis=1, keepdims=True)
                z, e, _, a = _sb_scores(qh[hh], kb, mask, tri, c)
                g = a * _dot_nt(doh[hh], vb)
                p = gs[hh] + _dot_sel(g, tri_prefix)
                inv = 1.0 / (1.0 + e)
                sig = jnp.where(z >= 0.0, inv, e * inv)
                dz = jnp.where(mask, g - sig * p, 0.0).astype(_MXU)
                dq = dq + _dot(dz, jnp.where(hmask[hh], kb, jnp.zeros_like(kb)))
                dk_ref[pl.ds(ks, KBLK), :] += _dot_tn(dz, qh[hh])
                dv_ref[pl.ds(ks, KBLK), :] += _dot_tn(a, doh[hh])
                gs[hh] = gs[hh] + jnp.sum(g, axis=1, keepdims=True)
            return gs[0], gs[1], dq

        zc = jnp.zeros((tq, 1), F32)
        res = lax.fori_loop(0, nk, step, (zc, zc, jnp.zeros((tq, LANES), F32)))
        dq_ref[...] = res[2] * 0.125

    qspec = pl.BlockSpec((tq, LANES), lambda p, i: (i, p))
    full = pl.BlockSpec((s, LANES), lambda p, i: (0, p))
    return _pcall(
        body, name=name, grid=(npairs, s // tq),
        in_specs=[qspec, pl.BlockSpec((s, LANES), lambda p, i: (0, npairs + p)),
                  pl.BlockSpec((s, LANES), lambda p, i: (0, 2 * npairs + p)),
                  pl.BlockSpec((tq, 2 * LANES), lambda p, i: (i, p)), qspec],
        out_specs=[qspec, full, full],
        out_shape=[jax.ShapeDtypeStruct((s, W_B), F32)] * 3,
        compiler_params=_params(("parallel", "arbitrary")),
    )(qkv, qkv, qkv, cm, do)


POOL_HALO = 128


def _pool_window_lane():
    lane = _iota((1, W_C), 1)
    w = jnp.where(lane < 64, POOL_WINDOWS[0], jnp.where(lane < 128, POOL_WINDOWS[1],
                  jnp.where(lane < 192, POOL_WINDOWS[2], POOL_WINDOWS[3])))
    return w.astype(F32)


def _pool_count(tm, i):
    pos = (_iota((tm, W_C), 0) + (i * tm + 1)).astype(F32)
    return jnp.minimum(pos, _pool_window_lane())


def _pool_centered(prev, cur, cnt):
    tm = cur.shape[0]
    xx = jnp.concatenate([prev, cur], axis=0)
    hi, lo = _split(xx)
    t = _iota((tm, tm + POOL_HALO), 0)
    cc = _iota((tm, tm + POOL_HALO), 1) - POOL_HALO
    wsum = jnp.zeros_like(cur)
    for g, w in enumerate(POOL_WINDOWS):
        band = ((cc <= t) & (cc > t - w)).astype(BF16)
        mg = _lane_group_mask(W_C, g)
        wsum += jnp.dot(band, jnp.where(mg, hi, jnp.zeros_like(hi)), preferred_element_type=F32)
        wsum += jnp.dot(band, jnp.where(mg, lo, jnp.zeros_like(lo)), preferred_element_type=F32)
    return wsum / cnt - cur


def _pool_fwd(p, wbd, sc, name):
    s = p.shape[0]
    tm = _tile(s, 256, POOL_HALO)
    r = tm // POOL_HALO

    def body(pp_ref, p_ref, w_ref, sc_ref, y_ref):
        i = pl.program_id(0)
        prev = jnp.where(i > 0, pp_ref[...], 0.0)
        d = _pool_centered(prev, p_ref[...], _pool_count(tm, i))
        y_ref[...] = _dot(d, w_ref[...]) * sc_ref[...]

    return _pcall(
        body, name=name, grid=(s // tm,),
        in_specs=[pl.BlockSpec((POOL_HALO, W_C), lambda i: (jnp.maximum(i * r - 1, 0), 0)),
                  pl.BlockSpec((tm, W_C), lambda i: (i, 0)), pl.BlockSpec((W_C, W_C), lambda i: (0, 0)),
                  pl.BlockSpec((1, W_C), lambda i: (0, 0))],
        out_specs=pl.BlockSpec((tm, W_C), lambda i: (i, 0)),
        out_shape=jax.ShapeDtypeStruct((s, W_C), F32),
        compiler_params=_params(("parallel",)),
    )(p, p, wbd, sc)


def _pool_bwd(p, dy, wbd, wbdt, sc, name):
    s = p.shape[0]
    tm = _tile(s, 256, POOL_HALO)
    r = tm // POOL_HALO
    nt = s // tm

    def body(pp_ref, p_ref, dy_ref, dyn_ref, w_ref, wt_ref, sc_ref, dp_ref, dw_ref, dsc_ref):
        i = pl.program_id(0)

        @pl.when(i == 0)
        def _():
            dw_ref[...] = jnp.zeros_like(dw_ref)
            dsc_ref[...] = jnp.zeros_like(dsc_ref)

        scv = sc_ref[...]
        cnt = _pool_count(tm, i)
        prev = jnp.where(i > 0, pp_ref[...], 0.0)
        d = _pool_centered(prev, p_ref[...], cnt)
        e = _dot(d, w_ref[...])
        dyv = dy_ref[...]
        de = dyv * scv
        dsc_ref[...] += jnp.sum(dyv * e, axis=0, keepdims=True)
        dw_ref[...] += _dot_tn(d, de)
        dd = _dot(de, wt_ref[...])
        ddn = jnp.where(i < nt - 1, _dot(dyn_ref[...] * scv, wt_ref[...]), 0.0)
        yy = jnp.concatenate([dd / cnt, ddn / _pool_window_lane()], axis=0)
        hi, lo = _split(yy)
        t = _iota((tm, tm + POOL_HALO), 0)
        cc = _iota((tm, tm + POOL_HALO), 1)
        acc = jnp.zeros_like(dd)
        for g, w in enumerate(POOL_WINDOWS):
            band = ((cc >= t) & (cc < t + w)).astype(BF16)
            mg = _lane_group_mask(W_C, g)
            acc += jnp.dot(band, jnp.where(mg, hi, jnp.zeros_like(hi)), preferred_element_type=F32)
            acc += jnp.dot(band, jnp.where(mg, lo, jnp.zeros_like(lo)), preferred_element_type=F32)
        dp_ref[...] = acc - dd

    tile = pl.BlockSpec((tm, W_C), lambda i: (i, 0))
    mat = pl.BlockSpec((W_C, W_C), lambda i: (0, 0))
    vec = pl.BlockSpec((1, W_C), lambda i: (0, 0))
    return _pcall(
        body, name=name, grid=(nt,),
        in_specs=[pl.BlockSpec((POOL_HALO, W_C), lambda i: (jnp.maximum(i * r - 1, 0), 0)), tile, tile,
                  pl.BlockSpec((POOL_HALO, W_C), lambda i: (jnp.minimum((i + 1) * r, s // POOL_HALO - 1), 0)),
                  mat, mat, vec],
        out_specs=[tile, mat, vec],
        out_shape=[jax.ShapeDtypeStruct((s, W_C), F32), jax.ShapeDtypeStruct((W_C, W_C), F32),
                   jax.ShapeDtypeStruct((1, W_C), F32)],
        compiler_params=_params(("arbitrary",)),
    )(p, p, dy, dy, wbd, wbdt, sc)


def _mix_cols(ya_ref, yb_ref, yc_ref, cb):
    if cb < 2:
        return ya_ref[:, cb * LANES:(cb + 1) * LANES]
    if cb < 6:
        return yb_ref[:, (cb - 2) * LANES:(cb - 1) * LANES]
    return yc_ref[:, (cb - 6) * LANES:(cb - 5) * LANES]


def _mix_fwd(ya, yb, yc, g, wo, x, name):
    s = x.shape[0]
    tm = _tile(s, 256)

    def body(ya_ref, yb_ref, yc_ref, g_ref, w_ref, x_ref, o_ref, yn_ref):
        sel = _group_mat(LANES)
        for cb in range(D_MODEL // LANES):
            y = _mix_cols(ya_ref, yb_ref, yc_ref, cb)
            r = lax.rsqrt(_dot_sel(y * y, sel) * (1.0 / HEAD_DIM) + EPS)
            yn_ref[:, cb * LANES:(cb + 1) * LANES] = (y * r * g_ref[:, cb * LANES:(cb + 1) * LANES]).astype(_MXU)
        o_ref[...] = x_ref[...] + jnp.dot(yn_ref[...], w_ref[...], preferred_element_type=F32)

    row = lambda n: pl.BlockSpec((tm, n), lambda i: (i, 0))
    return _pcall(
        body, name=name, grid=(s // tm,),
        in_specs=[row(W_A), row(W_B), row(W_C), pl.BlockSpec((1, D_MODEL), lambda i: (0, 0)),
                  pl.BlockSpec((D_MODEL, D_MODEL), lambda i: (0, 0)), row(D_MODEL)],
        out_specs=[row(D_MODEL), row(D_MODEL)],
        out_shape=[jax.ShapeDtypeStruct((s, D_MODEL), F32), jax.ShapeDtypeStruct((s, D_MODEL), _MXU)],
        compiler_params=_params(("parallel",)),
    )(ya, yb, yc, g, wo, x)


def _mix_bwd(dx, ya, yb, yc, g, wot, name):
    s = dx.shape[0]
    tm = _tile(s, 256)

    def body(dx_ref, ya_ref, yb_ref, yc_ref, g_ref, wt_ref, dya_ref, dyb_ref, dyc_ref, dg_ref):
        @pl.when(pl.program_id(0) == 0)
        def _():
            dg_ref[...] = jnp.zeros_like(dg_ref)

        dyn = _dot(dx_ref[...], wt_ref[...])
        sel = _group_mat(LANES)
        for cb in range(D_MODEL // LANES):
            cols = slice(cb * LANES, (cb + 1) * LANES)
            y = _mix_cols(ya_ref, yb_ref, yc_ref, cb)
            r = lax.rsqrt(_dot_sel(y * y, sel) * (1.0 / HEAD_DIM) + EPS)
            dyc_ = dyn[:, cols]
            dyg = dyc_ * g_ref[:, cols]
            m2 = _dot_sel(dyg * y, sel) * (1.0 / HEAD_DIM)
            dy = r * dyg - y * (r * r * r) * m2
            dg_ref[:, cols] += jnp.sum(dyc_ * y * r, axis=0, keepdims=True)
            if cb < 2:
                dya_ref[:, cb * LANES:(cb + 1) * LANES] = dy
            elif cb < 6:
                dyb_ref[:, (cb - 2) * LANES:(cb - 1) * LANES] = dy
            else:
                dyc_ref[:, (cb - 6) * LANES:(cb - 5) * LANES] = dy

    row = lambda n: pl.BlockSpec((tm, n), lambda i: (i, 0))
    vec = pl.BlockSpec((1, D_MODEL), lambda i: (0, 0))
    return _pcall(
        body, name=name, grid=(s // tm,),
        in_specs=[row(D_MODEL), row(W_A), row(W_B), row(W_C), vec, pl.BlockSpec((D_MODEL, D_MODEL), lambda i: (0, 0))],
        out_specs=[row(W_A), row(W_B), row(W_C), vec],
        out_shape=[jax.ShapeDtypeStruct((s, W_A), F32), jax.ShapeDtypeStruct((s, W_B), F32),
                   jax.ShapeDtypeStruct((s, W_C), F32), jax.ShapeDtypeStruct((1, D_MODEL), F32)],
        compiler_params=_params(("arbitrary",)),
    )(dx, ya, yb, yc, g, wot)


FFN_CHUNK = 256
N_CHUNKS = D_FF // FFN_CHUNK
CW_ROWS = 8


def _ffn_up_fwd(x, g, w, name):
    s = x.shape[0]
    n = w.shape[1]
    tm, tn = _tile(s, 512), _tile(n, 512, LANES)

    def body(x_ref, g_ref, w_ref, z_ref, h_ref):
        @pl.when(pl.program_id(1) == 0)
        def _():
            xv = x_ref[...]
            h_ref[...] = (xv * _rms(xv) * g_ref[...]).astype(_MXU)
        z_ref[...] = jnp.dot(h_ref[...], w_ref[...], preferred_element_type=F32)

    return _pcall(
        body, name=name, grid=(s // tm, n // tn),
        in_specs=[pl.BlockSpec((tm, D_MODEL), lambda i, j: (i, 0)), pl.BlockSpec((1, D_MODEL), lambda i, j: (0, 0)),
                  pl.BlockSpec((D_MODEL, tn), lambda i, j: (0, j))],
        out_specs=[pl.BlockSpec((tm, tn), lambda i, j: (i, j)), pl.BlockSpec((tm, D_MODEL), lambda i, j: (i, 0))],
        out_shape=[jax.ShapeDtypeStruct((s, n), F32), jax.ShapeDtypeStruct((s, D_MODEL), _MXU)],
        compiler_params=_params(("parallel", "arbitrary")),
    )(x, g, w)


def _conv(cur, prev8, cw_ref):
    s1 = _shift_down(cur, prev8, 1)
    s2 = _shift_down(cur, prev8, 2)
    zc = cw_ref[3:4, :] + s2 * cw_ref[0:1, :]
    zc = zc + s1 * cw_ref[1:2, :]
    zc = zc + cur * cw_ref[2:3, :]
    return zc, s1, s2


def _halo_specs(tm, s):
    r = tm // SUBLANES
    prev = lambda off: pl.BlockSpec((SUBLANES, FFN_CHUNK), lambda i, j: (jnp.maximum(i * r - 1, 0), j + off))
    nxt = lambda off: pl.BlockSpec((SUBLANES, FFN_CHUNK), lambda i, j: (jnp.minimum((i + 1) * r, s // SUBLANES - 1), j + off))
    return prev, nxt


def _ffn_down_fwd(z, cw, wd, x, name):
    s = x.shape[0]
    tm = _tile(s, 512)
    prev, _ = _halo_specs(tm, s)

    def body(zg_ref, zu_ref, pg_ref, pu_ref, cg_ref, cu_ref, w_ref, x_ref, o_ref, act_ref, acc_ref):
        i, j = pl.program_id(0), pl.program_id(1)
        first = i > 0
        zg, _, _ = _conv(zg_ref[...], jnp.where(first, pg_ref[...], 0.0), cg_ref)
        zu, _, _ = _conv(zu_ref[...], jnp.where(first, pu_ref[...], 0.0), cu_ref)
        act = (zg * jax.nn.sigmoid(zg) * zu).astype(_MXU)
        act_ref[...] = act

        @pl.when(j == 0)
        def _():
            acc_ref[...] = x_ref[...]
        acc_ref[...] += jnp.dot(act, w_ref[...], preferred_element_type=F32)

        @pl.when(j == N_CHUNKS - 1)
        def _():
            o_ref[...] = acc_ref[...]

    zt = lambda off: pl.BlockSpec((tm, FFN_CHUNK), lambda i, j: (i, j + off))
    cwt = lambda off: pl.BlockSpec((CW_ROWS, FFN_CHUNK), lambda i, j: (0, j + off))
    return _pcall(
        body, name=name, grid=(s // tm, N_CHUNKS),
        in_specs=[zt(0), zt(N_CHUNKS), prev(0), prev(N_CHUNKS), cwt(0), cwt(N_CHUNKS),
                  pl.BlockSpec((FFN_CHUNK, D_MODEL), lambda i, j: (j, 0)), pl.BlockSpec((tm, D_MODEL), lambda i, j: (i, 0))],
        out_specs=[pl.BlockSpec((tm, D_MODEL), lambda i, j: (i, 0)), pl.BlockSpec((tm, FFN_CHUNK), lambda i, j: (i, j))],
        out_shape=[jax.ShapeDtypeStruct((s, D_MODEL), F32), jax.ShapeDtypeStruct((s, D_FF), _MXU)],
        scratch_shapes=[pltpu.VMEM((tm, D_MODEL), F32)],
        compiler_params=_params(("parallel", "arbitrary")),
    )(z, z, z, z, cw, cw, wd, x)


def _ffn_down_bwd(dx, z, cw, wdt, name):
    s = dx.shape[0]
    tm = _tile(s, 512)

    def body(dx_ref, zg_ref, zu_ref, pg_ref, pu_ref, cg_ref, cu_ref, wt_ref, dg_ref, du_ref, dcg_ref, dcu_ref):
        i = pl.program_id(1)
        first = i > 0

        @pl.when(i == 0)
        def _():
            dcg_ref[...] = jnp.zeros_like(dcg_ref)
            dcu_ref[...] = jnp.zeros_like(dcu_ref)

        dact = _dot(dx_ref[...], wt_ref[...])
        zg, g1, g2 = _conv(zg_ref[...], jnp.where(first, pg_ref[...], 0.0), cg_ref)
        zu, u1, u2 = _conv(zu_ref[...], jnp.where(first, pu_ref[...], 0.0), cu_ref)
        sg = jax.nn.sigmoid(zg)
        silu = zg * sg
        dzu = dact * silu
        dzg = dact * zu * (sg * (1.0 + zg * (1.0 - sg)))
        dg_ref[...] = dzg
        du_ref[...] = dzu
        for ref, dzc, cur, s1, s2 in ((dcg_ref, dzg, zg_ref[...], g1, g2), (dcu_ref, dzu, zu_ref[...], u1, u2)):
            ref[0:1, :] += jnp.sum(dzc * s2, axis=0, keepdims=True)
            ref[1:2, :] += jnp.sum(dzc * s1, axis=0, keepdims=True)
            ref[2:3, :] += jnp.sum(dzc * cur, axis=0, keepdims=True)
            ref[3:4, :] += jnp.sum(dzc, axis=0, keepdims=True)

    zt = lambda off: pl.BlockSpec((tm, FFN_CHUNK), lambda j, i: (i, j + off))
    r = tm // SUBLANES
    pv = lambda off: pl.BlockSpec((SUBLANES, FFN_CHUNK), lambda j, i: (jnp.maximum(i * r - 1, 0), j + off))
    cwt = lambda off: pl.BlockSpec((CW_ROWS, FFN_CHUNK), lambda j, i: (0, j + off))
    out_t = pl.BlockSpec((tm, FFN_CHUNK), lambda j, i: (i, j))
    dc_t = pl.BlockSpec((CW_ROWS, FFN_CHUNK), lambda j, i: (0, j))
    dzg, dzu, dcg, dcu = _pcall(
        body, name=name, grid=(N_CHUNKS, s // tm),
        in_specs=[pl.BlockSpec((tm, D_MODEL), lambda j, i: (i, 0)), zt(0), zt(N_CHUNKS), pv(0), pv(N_CHUNKS),
                  cwt(0), cwt(N_CHUNKS), pl.BlockSpec((D_MODEL, FFN_CHUNK), lambda j, i: (0, j))],
        out_specs=[out_t, out_t, dc_t, dc_t],
        out_shape=[jax.ShapeDtypeStruct((s, D_FF), F32), jax.ShapeDtypeStruct((s, D_FF), F32),
                   jax.ShapeDtypeStruct((CW_ROWS, D_FF), F32), jax.ShapeDtypeStruct((CW_ROWS, D_FF), F32)],
        compiler_params=_params(("parallel", "arbitrary")),
    )(dx, z, z, z, z, cw, cw, wdt)
    return dzg, dzu, jnp.concatenate([dcg, dcu], axis=1)


def _ffn_up_bwd(dzg, dzu, cw, wut, x, g, dres, name):
    s = x.shape[0]
    tm = _tile(s, 512)
    _, nxt = _halo_specs(tm, s)
    nt = s // tm

    def body(dg_ref, du_ref, ng_ref, nu_ref, cg_ref, cu_ref, wg_ref, wu_ref, x_ref, g_ref, dres_ref,
             dzg_ref, dzu_ref, dx_ref, dgn_ref, acc_ref):
        i, j = pl.program_id(0), pl.program_id(1)
        last = i < nt - 1

        def conv_bwd(cur, nxt8, cw_ref):
            up1 = _shift_up(cur, nxt8, 1)
            up2 = _shift_up(cur, nxt8, 2)
            return cur * cw_ref[2:3, :] + up1 * cw_ref[1:2, :] + up2 * cw_ref[0:1, :]

        dzg_ = conv_bwd(dg_ref[...], jnp.where(last, ng_ref[...], 0.0), cg_ref).astype(_MXU)
        dzu_ = conv_bwd(du_ref[...], jnp.where(last, nu_ref[...], 0.0), cu_ref).astype(_MXU)
        dzg_ref[...] = dzg_
        dzu_ref[...] = dzu_

        @pl.when(j == 0)
        def _():
            acc_ref[...] = jnp.zeros_like(acc_ref)
        acc_ref[...] += (jnp.dot(dzg_, wg_ref[...], preferred_element_type=F32)
                         + jnp.dot(dzu_, wu_ref[...], preferred_element_type=F32))

        @pl.when((i == 0) & (j == 0))
        def _():
            dgn_ref[...] = jnp.zeros_like(dgn_ref)

        @pl.when(j == N_CHUNKS - 1)
        def _():
            dx, dgn = _rms_bwd(x_ref[...], g_ref[...], acc_ref[...])
            dx_ref[...] = dres_ref[...] + dx
            dgn_ref[...] += dgn

    zt = pl.BlockSpec((tm, FFN_CHUNK), lambda i, j: (i, j))
    cwt = lambda off: pl.BlockSpec((CW_ROWS, FFN_CHUNK), lambda i, j: (0, j + off))
    wt = lambda off: pl.BlockSpec((FFN_CHUNK, D_MODEL), lambda i, j: (j + off, 0))
    row = pl.BlockSpec((tm, D_MODEL), lambda i, j: (i, 0))
    vec = pl.BlockSpec((1, D_MODEL), lambda i, j: (0, 0))
    return _pcall(
        body, name=name, grid=(nt, N_CHUNKS),
        in_specs=[zt, zt, nxt(0), nxt(0), cwt(0), cwt(N_CHUNKS), wt(0), wt(N_CHUNKS), row, vec, row],
        out_specs=[zt, zt, row, vec],
        out_shape=[jax.ShapeDtypeStruct((s, D_FF), _MXU), jax.ShapeDtypeStruct((s, D_FF), _MXU),
                   jax.ShapeDtypeStruct((s, D_MODEL), F32), jax.ShapeDtypeStruct((1, D_MODEL), F32)],
        scratch_shapes=[pltpu.VMEM((tm, D_MODEL), F32)],
        compiler_params=_params(("arbitrary", "arbitrary")),
    )(dzg, dzu, dzg, dzu, cw, cw, wut, wut, x, g, dres)


def _final_loss(x, g, tgt, name):
    s = x.shape[0]
    tm = _tile(s, 256)

    def body(x_ref, g_ref, t_ref, loss_ref, dx_ref, dg_ref):
        @pl.when(pl.program_id(0) == 0)
        def _():
            loss_ref[...] = jnp.zeros_like(loss_ref)
            dg_ref[...] = jnp.zeros_like(dg_ref)

        xv, gv = x_ref[...], g_ref[...]
        err = xv * _rms(xv) * gv - t_ref[...]
        per_tok = jnp.mean(err * err, axis=-1, keepdims=True)
        loss_ref[...] += 0.5 * jnp.sum(per_tok, axis=0, keepdims=True)
        dx, dg = _rms_bwd(xv, gv, err * (1.0 / D_MODEL))
        dx_ref[...] = dx
        dg_ref[...] += dg

    row = pl.BlockSpec((tm, D_MODEL), lambda i: (i, 0))
    vec = pl.BlockSpec((1, D_MODEL), lambda i: (0, 0))
    return _pcall(
        body, name=name, grid=(s // tm,),
        in_specs=[row, vec, row], out_specs=[pl.BlockSpec((1, 1), lambda i: (0, 0)), row, vec],
        out_shape=[jax.ShapeDtypeStruct((1, 1), F32), jax.ShapeDtypeStruct((s, D_MODEL), F32),
                   jax.ShapeDtypeStruct((1, D_MODEL), F32)],
        compiler_params=_params(("arbitrary",)),
    )(x, g, tgt)


def _adamw(parts, w, m, v, name):
    r, c = w.shape
    tr = _tile(r, 256)
    c1 = 1.0 - ADAM_B1 ** ADAM_STEP
    c2 = 1.0 - ADAM_B2 ** ADAM_STEP

    def body(p_ref, w_ref, m_ref, v_ref, g_ref, d_ref, mo_ref, vo_ref):
        g = p_ref[0]
        for i in range(1, N_DEV):
            g = g + p_ref[i]
        mn = ADAM_B1 * m_ref[...] + (1.0 - ADAM_B1) * g
        vn = ADAM_B2 * v_ref[...] + (1.0 - ADAM_B2) * (g * g)
        g_ref[...] = g
        mo_ref[...] = mn
        vo_ref[...] = vn
        d_ref[...] = -ADAM_LR * ((mn / c1) / (jnp.sqrt(vn / c2) + ADAM_EPS) + ADAM_WD * w_ref[...])

    t2 = pl.BlockSpec((tr, c), lambda i: (i, 0))
    return _pcall(
        body, name=name, grid=(r // tr,),
        in_specs=[pl.BlockSpec((N_DEV, tr, c), lambda i: (0, i, 0)), t2, t2, t2],
        out_specs=[t2] * 4, out_shape=[jax.ShapeDtypeStruct((r, c), F32)] * 4,
        compiler_params=_params(("parallel",)),
    )(parts, w, m, v)


SMALL = ("norm1_g", "sgu_norm_g", "sgu_w", "sgu_b", "pool_w", "pool_scale", "mix_norm_g", "norm2_g", "conv_b", "final_g")
SHARDED = ("w_in", "w_o", "w_up", "conv_w", "w_down")
ORDER = ("norm1_g", "w_in", "sgu_norm_g", "sgu_w", "sgu_b", "pool_w", "pool_scale", "mix_norm_g", "w_o", "norm2_g",
         "w_up", "conv_w", "conv_b", "w_down", "final_g")


def _pack(tree):
    return jnp.concatenate([tree[n].reshape(-1) for n in SMALL]).reshape(-1, LANES)


def _unpack(flat, like):
    out, off = {}, 0
    flat = flat.reshape(-1)
    for n in SMALL:
        size = math.prod(like[n].shape)
        out[n] = flat[off:off + size].reshape(like[n].shape)
        off += size
    return out


def _block_diag(pw):
    z = jnp.zeros((W_C, W_C), pw.dtype)
    for g in range(4):
        z = z.at[g * 64:(g + 1) * 64, g * 64:(g + 1) * 64].set(pw[g])
    return z


def kernel(x, norm1_g, w_in, sgu_norm_g, sgu_w, sgu_b, pool_w, pool_scale, mix_norm_g, w_o, norm2_g, w_up, conv_w, conv_b, w_down, final_g, loss_target, m_norm1_g, m_w_in, m_sgu_norm_g, m_sgu_w, m_sgu_b, m_pool_w, m_pool_scale, m_mix_norm_g, m_w_o, m_norm2_g, m_w_up, m_conv_w, m_conv_b, m_w_down, m_final_g, v_norm1_g, v_w_in, v_sgu_norm_g, v_sgu_w, v_sgu_b, v_pool_w, v_pool_scale, v_mix_norm_g, v_w_o, v_norm2_g, v_w_up, v_conv_w, v_conv_b, v_w_down, v_final_g):
    weights = dict(norm1_g=norm1_g, w_in=w_in, sgu_norm_g=sgu_norm_g, sgu_w=sgu_w, sgu_b=sgu_b, pool_w=pool_w,
                   pool_scale=pool_scale, mix_norm_g=mix_norm_g, w_o=w_o, norm2_g=norm2_g, w_up=w_up, conv_w=conv_w,
                   conv_b=conv_b, w_down=w_down, final_g=final_g)
    mom = dict(norm1_g=m_norm1_g, w_in=m_w_in, sgu_norm_g=m_sgu_norm_g, sgu_w=m_sgu_w, sgu_b=m_sgu_b, pool_w=m_pool_w,
               pool_scale=m_pool_scale, mix_norm_g=m_mix_norm_g, w_o=m_w_o, norm2_g=m_norm2_g, w_up=m_w_up,
               conv_w=m_conv_w, conv_b=m_conv_b, w_down=m_w_down, final_g=m_final_g)
    var = dict(norm1_g=v_norm1_g, w_in=v_w_in, sgu_norm_g=v_sgu_norm_g, sgu_w=v_sgu_w, sgu_b=v_sgu_b, pool_w=v_pool_w,
               pool_scale=v_pool_scale, mix_norm_g=v_mix_norm_g, w_o=v_w_o, norm2_g=v_norm2_g, w_up=v_w_up,
               conv_w=v_conv_w, conv_b=v_conv_b, w_down=v_w_down, final_g=v_final_g)
    depth = w_in.shape[0]
    s = x.shape[1]
    xs = x.reshape(s, D_MODEL)
    tgt = loss_target.reshape(s, D_MODEL)

    g_in, g_o, g_up, g_cw, g_dn = _exchange(
        [w_in.astype(_MXU), w_o.astype(_MXU), w_up.astype(_MXU), conv_w, w_down.astype(_MXU)], "gather_weights", False)
    full_in = jnp.transpose(g_in, (1, 2, 0, 3)).reshape(depth, D_MODEL, IN_COLS)
    full_o = jnp.transpose(g_o, (1, 0, 2, 3)).reshape(depth, D_MODEL, D_MODEL)
    full_up = jnp.transpose(g_up, (1, 2, 0, 3)).reshape(depth, D_MODEL, 2 * D_FF)
    full_cw = jnp.transpose(g_cw, (1, 2, 0, 3)).reshape(depth, 3, 2 * D_FF)
    full_dn = jnp.transpose(g_dn, (1, 0, 2, 3)).reshape(depth, D_FF, D_MODEL)

    tril = jnp.tril(jnp.ones((CHUNK, CHUNK), bool))
    layers = []
    for l in range(depth):
        wbd = _block_diag(pool_w[l])
        layers.append(dict(
            g1=norm1_g[l][None], w_in=full_in[l], w_in_t=full_in[l].T,
            gn=sgu_norm_g[l][None], wm=jnp.where(tril[None], sgu_w[l], 0.0).astype(_MXU),
            bias=jnp.repeat(sgu_b[l].T, HEAD_DIM, axis=1),
            wbd=wbd.astype(_MXU), wbd_t=wbd.T.astype(_MXU), sc=pool_scale[l][None],
            gmix=mix_norm_g[l][None], w_o=full_o[l], w_o_t=full_o[l].T,
            g2=norm2_g[l][None], w_up=full_up[l], w_up_t=full_up[l].T,
            cw=jnp.concatenate([full_cw[l], conv_b[l][None], jnp.zeros((CW_ROWS - 4, 2 * D_FF), F32)], axis=0),
            w_dn=full_dn[l], w_dn_t=full_dn[l].T))

    saved = []
    cur = xs
    for l, p in enumerate(layers):
        a_in, qkv, p_in, h1 = _inproj_fwd(cur, p["g1"], p["w_in"], f"inproj_fwd{l}")
        y_a = _sgu_fwd(a_in, p["gn"], p["wm"], p["bias"], f"sgu_fwd{l}")
        y_b, cm = _attn_fwd(qkv, f"attn_fwd{l}")
        y_c = _pool_fwd(p_in, p["wbd"], p["sc"], f"pool_fwd{l}")
        x_mid, yn = _mix_fwd(y_a, y_b, y_c, p["gmix"], p["w_o"], cur, f"mix_fwd{l}")
        z, h2 = _ffn_up_fwd(x_mid, p["g2"], p["w_up"], f"ffn_up_fwd{l}")
        x_out, act = _ffn_down_fwd(z, p["cw"], p["w_dn"], x_mid, f"ffn_down_fwd{l}")
        saved.append(dict(x_in=cur, a_in=a_in, qkv=qkv, p_in=p_in, h1=h1, y_a=y_a, y_b=y_b, cm=cm, y_c=y_c, x_mid=x_mid,
                          yn=yn, z=z, h2=h2, act=act))
        cur = x_out
    loss_part, dx, dg_final = _final_loss(cur, final_g[None], tgt, "final_loss")

    small = {n: [None] * depth for n in SMALL if n != "final_g"}
    big = {n: [None] * depth for n in SHARDED}
    for l in reversed(range(depth)):
        p, sv = layers[l], saved[l]
        dzg, dzu, dcw = _ffn_down_bwd(dx, sv["z"], p["cw"], p["w_dn_t"], f"ffn_down_bwd{l}")
        big["w_down"][l] = _mm_tn(sv["act"], dx, f"dw_down{l}").reshape(N_DEV, D_FF // N_DEV, D_MODEL)
        dzg_b, dzu_b, dx_mid, dg2 = _ffn_up_bwd(dzg, dzu, p["cw"], p["w_up_t"], sv["x_mid"], p["g2"], dx, f"ffn_up_bwd{l}")
        dw_up = jnp.stack([_mm_tn(sv["h2"], dzg_b, f"dw_up_g{l}"), _mm_tn(sv["h2"], dzu_b, f"dw_up_u{l}")])
        big["w_up"][l] = jnp.transpose(dw_up.reshape(2, D_MODEL, N_DEV // 2, 2 * D_FF // N_DEV), (0, 2, 1, 3)).reshape(
            N_DEV, D_MODEL, 2 * D_FF // N_DEV)
        big["conv_w"][l] = jnp.transpose(dcw[0:3].reshape(3, N_DEV, 2 * D_FF // N_DEV), (1, 0, 2))
        small["conv_b"][l] = dcw[3]
        small["norm2_g"][l] = dg2[0]
        dya, dyb, dyc, dgmix = _mix_bwd(dx_mid, sv["y_a"], sv["y_b"], sv["y_c"], p["gmix"], p["w_o_t"], f"mix_bwd{l}")
        small["mix_norm_g"][l] = dgmix[0]
        big["w_o"][l] = _mm_tn(sv["yn"], dx_mid, f"dw_o{l}").reshape(N_DEV, D_MODEL // N_DEV, D_MODEL)
        dp, dwbd, dsc = _pool_bwd(sv["p_in"], dyc, p["wbd"], p["wbd_t"], p["sc"], f"pool_bwd{l}")
        small["pool_w"][l] = jnp.stack([dwbd[g * 64:(g + 1) * 64, g * 64:(g + 1) * 64] for g in range(4)])
        small["pool_scale"][l] = dsc[0]
        dq, dk, dv = _attn_bwd(sv["qkv"], sv["cm"], dyb, f"attn_bwd{l}")
        da, dwm, dbias, dgn = _sgu_bwd(sv["a_in"], dya, p["gn"], p["wm"], p["bias"], f"sgu_bwd{l}")
        small["sgu_w"][l] = dwm
        small["sgu_b"][l] = jnp.sum(dbias.reshape(CHUNK, 4, HEAD_DIM), axis=-1).T
        small["sgu_norm_g"][l] = dgn[0]
        dx, dg1 = _inproj_bwd(da, dq, dk, dv, dp, p["w_in_t"], sv["x_in"], p["g1"], dx_mid, f"inproj_bwd{l}")
        small["norm1_g"][l] = dg1[0]
        pieces = (da, dq, dk, dv, dp)
        dw_in = jnp.concatenate([_mm_tn(sv["h1"], t, f"dw_in{i}_{l}") for i, t in enumerate(pieces)], axis=1)
        big["w_in"][l] = jnp.transpose(dw_in.reshape(D_MODEL, N_DEV, IN_COLS // N_DEV), (1, 0, 2))

    names = [(n, l) for n in SHARDED for l in range(depth)]
    recv = _exchange([big[n][l] for n, l in names], "scatter_grads", True)
    out_g, out_d, out_m, out_v = {}, {}, {}, {}
    for n in SHARDED:
        res = [_adamw(recv[names.index((n, l))], weights[n][l], mom[n][l], var[n][l], f"adamw_{n}{l}") for l in range(depth)]
        out_g[n], out_d[n], out_m[n], out_v[n] = (jnp.stack([r[i] for r in res]) for i in range(4))

    part = {n: jnp.stack(small[n]) for n in small}
    part["final_g"] = dg_final[0]
    (gathered,) = _exchange([_pack(part)], "gather_small_grads", False)
    sg, sd, sm, sv_ = _adamw(gathered, _pack(weights), _pack(mom), _pack(var), "adamw_small")
    for tree, flat in ((out_g, sg), (out_d, sd), (out_m, sm), (out_v, sv_)):
        tree.update(_unpack(flat, weights))

    loss = lax.psum(loss_part[0, 0], ("x", "y", "c"))
    grad_x = dx.reshape(1, s, D_MODEL)
    return (loss, grad_x, *[out_g[n] for n in ORDER], *[out_d[n] for n in ORDER], *[out_m[n] for n in ORDER],
            *[out_v[n] for n in ORDER])
```

```python
import functools
import math

import jax
import jax.numpy as jnp
import numpy as np
from jax import lax
from jax.experimental import pallas as pl
from jax.experimental.pallas import tpu as pltpu

F32 = jnp.float32
BF16 = jnp.bfloat16
_MXU = jnp.bfloat16

D_MODEL = 1024
W_A = 256
W_B = 512
W_C = 256
HEAD_DIM = 64
IN_COLS = 2 * W_A + 3 * W_B + W_C
D_FF = 2816
CHUNK = 128
POOL_WINDOWS = (2, 4, 8, 16)
EPS = 1e-6
N_DEV = 8
LANES = 128
SUBLANES = 8
VMEM_LIMIT = 48 * 1024 * 1024

ADAM_LR = 0.001
ADAM_B1 = 0.9
ADAM_B2 = 0.999
ADAM_EPS = 1e-08
ADAM_WD = 0.01
ADAM_STEP = 10

INV_SQRT2 = 1.0 / math.sqrt(2.0)
INV_SQRT_2PI = 1.0 / math.sqrt(2.0 * math.pi)


def _pcall(body, **kw):
    return pl.pallas_call(body, **kw)


def _params(dims=None):
    return pltpu.CompilerParams(dimension_semantics=dims, vmem_limit_bytes=VMEM_LIMIT)


def _tile(n, pref, mult=SUBLANES):
    t = min(n, pref) // mult * mult
    while t >= mult:
        if n % t == 0:
            return t
        t -= mult
    return n


def _iota(shape, dim):
    return lax.broadcasted_iota(jnp.int32, shape, dim)


def _dot(a, b):
    return jnp.dot(a.astype(_MXU), b.astype(_MXU), preferred_element_type=F32)


def _dot_nt(a, b):
    return lax.dot_general(a.astype(_MXU), b.astype(_MXU), (((1,), (1,)), ((), ())), preferred_element_type=F32)


def _dot_tn(a, b):
    return lax.dot_general(a.astype(_MXU), b.astype(_MXU), (((0,), (0,)), ((), ())), preferred_element_type=F32)


def _split(x):
    hi = x.astype(BF16)
    lo = (x - hi.astype(F32)).astype(BF16)
    return hi, lo


def _dot_sel(x, sel):
    hi, lo = _split(x)
    return jnp.dot(hi, sel, preferred_element_type=F32) + jnp.dot(lo, sel, preferred_element_type=F32)


def _sel_dot(sel, x):
    hi, lo = _split(x)
    return jnp.dot(sel, hi, preferred_element_type=F32) + jnp.dot(sel, lo, preferred_element_type=F32)


def _group_mat(n):
    r = jnp.right_shift(_iota((n, n), 0), 6)
    c = jnp.right_shift(_iota((n, n), 1), 6)
    return (r == c).astype(BF16)


def _lane_group_mask(n, g):
    lane = _iota((1, n), 1)
    return (lane >= g * HEAD_DIM) & (lane < (g + 1) * HEAD_DIM)


def _gelu(a):
    return 0.5 * a * (1.0 + lax.erf(a * INV_SQRT2))


def _gelu_grad(a):
    return 0.5 * (1.0 + lax.erf(a * INV_SQRT2)) + a * jnp.exp(-0.5 * a * a) * INV_SQRT_2PI


def _rms(x):
    return lax.rsqrt(jnp.mean(x * x, axis=-1, keepdims=True) + EPS)


def _rms_bwd(x, g, dy):
    r = _rms(x)
    dyg = dy * g
    m2 = jnp.mean(dyg * x, axis=-1, keepdims=True)
    dx = r * dyg - x * (r * r * r) * m2
    dg = jnp.sum(dy * x * r, axis=0, keepdims=True)
    return dx, dg


def _shift_down(cur, prev8, k):
    rolled = pltpu.roll(cur, k, 0)
    row8 = _iota(prev8.shape, 0)
    top = jnp.where(row8 < k, pltpu.roll(prev8, k, 0), rolled[0:SUBLANES])
    return jnp.concatenate([top, rolled[SUBLANES:]], axis=0)


def _shift_up(cur, next8, k):
    n = cur.shape[0]
    rolled = pltpu.roll(cur, n - k, 0)
    row8 = _iota(next8.shape, 0)
    bot = jnp.where(row8 >= SUBLANES - k, pltpu.roll(next8, SUBLANES - k, 0), rolled[n - SUBLANES:])
    return jnp.concatenate([rolled[:n - SUBLANES], bot], axis=0)


def _mesh_pos():
    return lax.axis_index("x"), lax.axis_index("y"), lax.axis_index("c")


def _peer(x, y, c, k):
    px = 1 - x if k & 4 else x
    py = 1 - y if k & 2 else y
    pc = 1 - c if k & 1 else c
    return px, py, pc


def _exchange(arrs, name, scatter):
    n = len(arrs)
    out_shapes = [jax.ShapeDtypeStruct(a.shape if scatter else (N_DEV,) + a.shape, a.dtype) for a in arrs]

    def body(*refs):
        ins, outs = refs[:n], refs[n:2 * n]
        send, recv, loc = refs[2 * n:]
        x, y, c = _mesh_pos()
        me = 4 * x + 2 * y + c

        def src(a, idx):
            return ins[a].at[idx] if scatter else ins[a]

        locals_ = [pltpu.make_async_copy(src(a, me), outs[a].at[me], loc.at[a]) for a in range(n)]
        for cp in locals_:
            cp.start()
        copies = []
        for k in range(1, N_DEV):
            px, py, pc = _peer(x, y, c, k)
            pidx = 4 * px + 2 * py + pc
            for a in range(n):
                s = a * (N_DEV - 1) + k - 1
                pltpu.make_async_remote_copy(
                    src_ref=src(a, pidx), dst_ref=outs[a].at[me], send_sem=send.at[s], recv_sem=recv.at[s],
                    device_id=(px, py, pc), device_id_type=pl.DeviceIdType.MESH).start()
                copies.append(pltpu.make_async_remote_copy(
                    src_ref=src(a, pidx), dst_ref=outs[a].at[pidx], send_sem=send.at[s], recv_sem=recv.at[s],
                    device_id=(px, py, pc), device_id_type=pl.DeviceIdType.MESH))
        for cp in copies:
            cp.wait()
        for cp in locals_:
            cp.wait()

    hbm = pl.BlockSpec(memory_space=pl.ANY)
    return _pcall(
        body, name=name, out_shape=out_shapes, in_specs=[hbm] * n, out_specs=[hbm] * n,
        scratch_shapes=[pltpu.SemaphoreType.DMA((n * (N_DEV - 1),)), pltpu.SemaphoreType.DMA((n * (N_DEV - 1),)),
                        pltpu.SemaphoreType.DMA((n,))],
        compiler_params=pltpu.CompilerParams(has_side_effects=True),
    )(*arrs)


def _mm_tn(a, b, name):
    s, m = a.shape
    n = b.shape[1]
    tm, tn, tk = _tile(m, 1024, LANES), _tile(n, 512, LANES), _tile(s, 1024)

    def body(a_ref, b_ref, o_ref):
        @pl.when(pl.program_id(2) == 0)
        def _():
            o_ref[...] = jnp.zeros_like(o_ref)
        o_ref[...] += _dot_tn(a_ref[...], b_ref[...])

    return _pcall(
        body, name=name, grid=(m // tm, n // tn, s // tk),
        in_specs=[pl.BlockSpec((tk, tm), lambda i, j, k: (k, i)), pl.BlockSpec((tk, tn), lambda i, j, k: (k, j))],
        out_specs=pl.BlockSpec((tm, tn), lambda i, j, k: (i, j)),
        out_shape=jax.ShapeDtypeStruct((m, n), F32),
        compiler_params=_params(("parallel", "parallel", "arbitrary")),
    )(a, b)


def _inproj_fwd(x, g, w, name):
    s = x.shape[0]
    tm = _tile(s, 256)

    def body(x_ref, g_ref, w_ref, a_ref, qkv_ref, p_ref, h_ref):
        xv = x_ref[...]
        h = (xv * _rms(xv) * g_ref[...]).astype(_MXU)
        h_ref[...] = h
        a_ref[...] = jnp.dot(h, w_ref[:, 0:2 * W_A], preferred_element_type=F32)
        q = jnp.dot(h, w_ref[:, 2 * W_A:2 * W_A + W_B], preferred_element_type=F32)
        qkv_ref[:, 0:W_B] = (q * 0.125).astype(_MXU)
        kv = jnp.dot(h, w_ref[:, 2 * W_A + W_B:2 * W_A + 3 * W_B], preferred_element_type=F32)
        qkv_ref[:, W_B:3 * W_B] = kv.astype(_MXU)
        p_ref[...] = jnp.dot(h, w_ref[:, 2 * W_A + 3 * W_B:IN_COLS], preferred_element_type=F32)

    row = lambda n: pl.BlockSpec((tm, n), lambda i: (i, 0))
    return _pcall(
        body, name=name, grid=(s // tm,),
        in_specs=[row(D_MODEL), pl.BlockSpec((1, D_MODEL), lambda i: (0, 0)),
                  pl.BlockSpec((D_MODEL, IN_COLS), lambda i: (0, 0))],
        out_specs=[row(2 * W_A), row(3 * W_B), row(W_C), row(D_MODEL)],
        out_shape=[jax.ShapeDtypeStruct((s, 2 * W_A), F32), jax.ShapeDtypeStruct((s, 3 * W_B), _MXU),
                   jax.ShapeDtypeStruct((s, W_C), F32), jax.ShapeDtypeStruct((s, D_MODEL), _MXU)],
        compiler_params=_params(("parallel",)),
    )(x, g, w)


def _inproj_bwd(da, dq, dk, dv, dp, wt, x, g, dres, name):
    s = x.shape[0]
    tm = _tile(s, 256)

    def body(da_ref, dq_ref, dk_ref, dv_ref, dp_ref, wt_ref, x_ref, g_ref, dres_ref, dx_ref, dg_ref):
        dh = _dot(da_ref[...], wt_ref[0:2 * W_A, :])
        dh += _dot(dq_ref[...], wt_ref[2 * W_A:2 * W_A + W_B, :])
        dh += _dot(dk_ref[...], wt_ref[2 * W_A + W_B:2 * W_A + 2 * W_B, :])
        dh += _dot(dv_ref[...], wt_ref[2 * W_A + 2 * W_B:2 * W_A + 3 * W_B, :])
        dh += _dot(dp_ref[...], wt_ref[2 * W_A + 3 * W_B:IN_COLS, :])
        dx, dg = _rms_bwd(x_ref[...], g_ref[...], dh)
        dx_ref[...] = dres_ref[...] + dx

        @pl.when(pl.program_id(0) == 0)
        def _():
            dg_ref[...] = jnp.zeros_like(dg_ref)
        dg_ref[...] += dg

    row = lambda n: pl.BlockSpec((tm, n), lambda i: (i, 0))
    vec = pl.BlockSpec((1, D_MODEL), lambda i: (0, 0))
    return _pcall(
        body, name=name, grid=(s // tm,),
        in_specs=[row(2 * W_A), row(W_B), row(W_B), row(W_B), row(W_C),
                  pl.BlockSpec((IN_COLS, D_MODEL), lambda i: (0, 0)), row(D_MODEL), vec, row(D_MODEL)],
        out_specs=[row(D_MODEL), vec],
        out_shape=[jax.ShapeDtypeStruct((s, D_MODEL), F32), jax.ShapeDtypeStruct((1, D_MODEL), F32)],
        compiler_params=_params(("arbitrary",)),
    )(da, dq, dk, dv, dp, wt, x, g, dres)


def _sgu_core(a, gn, wm_ref, bias):
    ga = _gelu(a)
    u, v0 = ga[:, 0:W_A], ga[:, W_A:2 * W_A]
    r = lax.rsqrt(_dot_sel(v0 * v0, _group_mat(W_A)) * (1.0 / HEAD_DIM) + EPS)
    vn = v0 * r * gn
    sv = bias
    for h in range(W_A // HEAD_DIM):
        sv = sv + _dot(wm_ref[h], jnp.where(_lane_group_mask(W_A, h), vn, 0.0))
    return u, v0, r, vn, sv


def _sgu_fwd(a, gn, wm, bias, name):
    s = a.shape[0]

    def body(a_ref, gn_ref, wm_ref, b_ref, y_ref):
        u, _, _, _, sv = _sgu_core(a_ref[...], gn_ref[...], wm_ref, b_ref[...])
        y_ref[...] = u * sv

    return _pcall(
        body, name=name, grid=(s // CHUNK,),
        in_specs=[pl.BlockSpec((CHUNK, 2 * W_A), lambda i: (i, 0)), pl.BlockSpec((1, W_A), lambda i: (0, 0)),
                  pl.BlockSpec((4, CHUNK, CHUNK), lambda i: (0, 0, 0)), pl.BlockSpec((CHUNK, W_A), lambda i: (0, 0))],
        out_specs=pl.BlockSpec((CHUNK, W_A), lambda i: (i, 0)),
        out_shape=jax.ShapeDtypeStruct((s, W_A), F32),
        compiler_params=_params(("parallel",)),
    )(a, gn, wm, bias)


def _sgu_bwd(a, dy, gn, wm, bias, name):
    s = a.shape[0]

    def body(a_ref, dy_ref, gn_ref, wm_ref, b_ref, da_ref, dwm_ref, db_ref, dgn_ref):
        @pl.when(pl.program_id(0) == 0)
        def _():
            dwm_ref[...] = jnp.zeros_like(dwm_ref)
            db_ref[...] = jnp.zeros_like(db_ref)
            dgn_ref[...] = jnp.zeros_like(dgn_ref)

        av, gnv, dyv = a_ref[...], gn_ref[...], dy_ref[...]
        u, v0, r, vn, sv = _sgu_core(av, gnv, wm_ref, b_ref[...])
        du = dyv * sv
        ds = dyv * u
        db_ref[...] += ds
        tril = _iota((CHUNK, CHUNK), 1) <= _iota((CHUNK, CHUNK), 0)
        dvn = jnp.zeros_like(vn)
        for h in range(W_A // HEAD_DIM):
            dsm = jnp.where(_lane_group_mask(W_A, h), ds, 0.0)
            dwm_ref[h] += jnp.where(tril, _dot_nt(dsm, vn), 0.0)
            dvn = dvn + _dot_tn(wm_ref[h], dsm)
        dgn_ref[...] += jnp.sum(dvn * v0 * r, axis=0, keepdims=True)
        dvg = dvn * gnv
        m2 = _dot_sel(dvg * v0, _group_mat(W_A)) * (1.0 / HEAD_DIM)
        dv0 = r * dvg - v0 * (r * r * r) * m2
        gp = _gelu_grad(av)
        da_ref[:, 0:W_A] = du * gp[:, 0:W_A]
        da_ref[:, W_A:2 * W_A] = dv0 * gp[:, W_A:2 * W_A]

    return _pcall(
        body, name=name, grid=(s // CHUNK,),
        in_specs=[pl.BlockSpec((CHUNK, 2 * W_A), lambda i: (i, 0)), pl.BlockSpec((CHUNK, W_A), lambda i: (i, 0)),
                  pl.BlockSpec((1, W_A), lambda i: (0, 0)), pl.BlockSpec((4, CHUNK, CHUNK), lambda i: (0, 0, 0)),
                  pl.BlockSpec((CHUNK, W_A), lambda i: (0, 0))],
        out_specs=[pl.BlockSpec((CHUNK, 2 * W_A), lambda i: (i, 0)), pl.BlockSpec((4, CHUNK, CHUNK), lambda i: (0, 0, 0)),
                   pl.BlockSpec((CHUNK, W_A), lambda i: (0, 0)), pl.BlockSpec((1, W_A), lambda i: (0, 0))],
        out_shape=[jax.ShapeDtypeStruct((s, 2 * W_A), F32), jax.ShapeDtypeStruct((4, CHUNK, CHUNK), F32),
                   jax.ShapeDtypeStruct((CHUNK, W_A), F32), jax.ShapeDtypeStruct((1, W_A), F32)],
        compiler_params=_params(("arbitrary",)),
    )(a, dy, gn, wm, bias)


KBLK = 128
ROW_CHUNK = 64


def _sb_scores(qh, kb, mask, tri, carry):
    z = _dot_nt(qh, kb)
    e = jnp.exp(-jnp.abs(z))
    l = jnp.where(mask, -(jnp.maximum(z, 0.0) + jnp.log(1.0 + e)), 0.0)
    a = jnp.where(mask, jnp.exp(z + carry + _dot_sel(l, tri)), 0.0)
    return z, e, l, a


def _attn_fwd(qkv, name):
    s = qkv.shape[0]
    tq = _tile(s, 256, KBLK)
    npairs = W_B // LANES
    assert s // KBLK <= LANES

    def body(q_ref, k_ref, v_ref, o_ref, cm_ref, z_ref, zw_ref, sums_ref, carry_ref, hi_ref, lo_ref, a_ref):
        i = pl.program_id(1)
        q = q_ref[...]
        lane = _iota((1, LANES), 1)
        lane_lo = lane < HEAD_DIM
        hmask = (lane_lo, jnp.logical_not(lane_lo))
        qh = [jnp.where(m, q, jnp.zeros_like(q)) for m in hmask]
        tri2 = ((_iota((KBLK, 2 * KBLK), 0) >= _iota((KBLK, 2 * KBLK), 1))
                | (_iota((KBLK, 2 * KBLK), 1) >= KBLK)).astype(BF16)
        dmat = _iota((tq, KBLK), 1) - (_iota((tq, KBLK), 0) + i * tq)
        chunks = [slice(r, r + ROW_CHUNK) for r in range(0, tq, ROW_CHUNK)]
        nk = (i + 1) * (tq // KBLK)

        cm_ref[...] = jnp.zeros_like(cm_ref)

        def before(b, valid):
            return jnp.where(valid, -b * KBLK, jnp.iinfo(jnp.int32).min)

        def scores(b, p):
            ks = pl.multiple_of(jnp.maximum(b, 0) * KBLK, KBLK)
            kb = k_ref[pl.ds(ks, KBLK), :]
            for hh in range(2):
                z_ref[p, hh] = _dot_nt(qh[hh], kb)

        def logs(b, p):
            t = before(b, (b >= 0) & (b < nk))
            for hh in range(2):
                for rows in chunks:
                    z = z_ref[p, hh, rows, :]
                    zw_ref[p, hh, rows, :] = z
                    e = jnp.exp(-jnp.abs(z))
                    l = jnp.where(dmat[rows] < t, -(jnp.maximum(z, 0.0) + jnp.log(1.0 + e)), 0.0)
                    hi, lo = _split(l)
                    hi_ref[p, hh, rows, :] = hi
                    lo_ref[p, hh, rows, :] = lo

        def sums(b, p):
            for hh in range(2):
                sums_ref[p, hh] = (jnp.dot(hi_ref[p, hh], tri2, preferred_element_type=F32)
                                   + jnp.dot(lo_ref[p, hh], tri2, preferred_element_type=F32))

        def weights(b, p):
            t = before(b, (b >= 0) & (b < nk))
            onehot = jnp.where(lane == b, 1.0, 0.0)
            for hh in range(2):
                for rows in chunks:
                    c = carry_ref[hh, rows, :]
                    arg = zw_ref[p, hh, rows, :] + c + sums_ref[p, hh, rows, 0:KBLK]
                    a_ref[p, hh, rows, :] = jnp.where(dmat[rows] < t, jnp.exp(arg), 0.0).astype(_MXU)
                    cm_ref[rows, hh * LANES:(hh + 1) * LANES] += onehot * c
                    carry_ref[hh, rows, :] = c + sums_ref[p, hh, rows, KBLK:2 * KBLK]

        def out(b, p, acc):
            ks = pl.multiple_of(jnp.minimum(b, nk - 1) * KBLK, KBLK)
            vb = v_ref[pl.ds(ks, KBLK), :]
            for hh in range(2):
                acc = acc + jnp.dot(a_ref[p, hh], jnp.where(hmask[hh], vb, jnp.zeros_like(vb)),
                                    preferred_element_type=F32)
            return acc

        def step(it, acc):
            for u in range(2):
                b = nk + 3 - (2 * it + u)
                p = 1 - u
                acc = out(b, p, acc)
                weights(b - 1, 1 - p)
                sums(b - 2, p)
                logs(b - 3, 1 - p)
                scores(b - 4, p)
            return acc

        for ref in (z_ref, zw_ref, sums_ref, carry_ref, hi_ref, lo_ref, a_ref):
            ref[...] = jnp.zeros_like(ref)
        o_ref[...] = lax.fori_loop(0, (nk + 4) // 2, step, q.astype(F32) * 0.0)

    return _pcall(
        body, name=name, grid=(npairs, s // tq),
        in_specs=[pl.BlockSpec((tq, LANES), lambda p, i: (i, p)),
                  pl.BlockSpec((s, LANES), lambda p, i: (0, npairs + p)),
                  pl.BlockSpec((s, LANES), lambda p, i: (0, 2 * npairs + p))],
        out_specs=[pl.BlockSpec((tq, LANES), lambda p, i: (i, p)), pl.BlockSpec((tq, 2 * LANES), lambda p, i: (i, p))],
        out_shape=[jax.ShapeDtypeStruct((s, W_B), F32), jax.ShapeDtypeStruct((s, 2 * W_B), F32)],
        scratch_shapes=[pltpu.VMEM((2, 2, tq, KBLK), F32), pltpu.VMEM((2, 2, tq, KBLK), F32),
                        pltpu.VMEM((2, 2, tq, 2 * KBLK), F32),
                        pltpu.VMEM((2, tq, KBLK), F32), pltpu.VMEM((2, 2, tq, KBLK), BF16),
                        pltpu.VMEM((2, 2, tq, KBLK), BF16), pltpu.VMEM((2, 2, tq, KBLK), _MXU)],
        compiler_params=_params(("parallel", "parallel")),
    )(qkv, qkv, qkv)


def _attn_bwd(qkv, cm, do, name):
    s = qkv.shape[0]
    tq = _tile(s, 256, KBLK)
    npairs = W_B // LANES

    def body(q_ref, k_ref, v_ref, cm_ref, do_ref, dq_ref, dk_ref, dv_ref,
             z_ref, zw_ref, cum_ref, da_ref, g_ref, sig_ref, gp_ref, gcarry_ref, hi_ref, lo_ref, ghi_ref, glo_ref, a_ref, dz_ref):
        i = pl.program_id(1)

        @pl.when(i == 0)
        def _():
            dk_ref[...] = jnp.zeros_like(dk_ref)
            dv_ref[...] = jnp.zeros_like(dv_ref)

        q = q_ref[...]
        dov = do_ref[...]
        lane = _iota((1, LANES), 1)
        lane_lo = lane < HEAD_DIM
        hmask = (lane_lo, jnp.logical_not(lane_lo))
        qh = [jnp.where(m, q, jnp.zeros_like(q)) for m in hmask]
        doh = [jnp.where(m, dov, 0.0).astype(_MXU) for m in hmask]
        tri = (_iota((KBLK, KBLK), 0) >= _iota((KBLK, KBLK), 1)).astype(BF16)
        prefix2 = ((_iota((KBLK, 2 * KBLK), 0) <= _iota((KBLK, 2 * KBLK), 1))
                   | (_iota((KBLK, 2 * KBLK), 1) >= KBLK)).astype(BF16)
        dmat = _iota((tq, KBLK), 1) - (_iota((tq, KBLK), 0) + i * tq)
        chunks = [slice(r, r + ROW_CHUNK) for r in range(0, tq, ROW_CHUNK)]
        nk = (i + 1) * (tq // KBLK)

        def before(b):
            valid = (b >= 0) & (b < nk)
            return jnp.where(valid, -b * KBLK, jnp.iinfo(jnp.int32).min)

        def block_rows(b):
            return pl.ds(pl.multiple_of(jnp.clip(b, 0, nk - 1) * KBLK, KBLK), KBLK)

        def m1(b, p):
            kb = k_ref[block_rows(b), :]
            for hh in range(2):
                z_ref[p, hh] = _dot_nt(qh[hh], kb)

        def v1(b, p):
            t = before(b)
            for hh in range(2):
                for rows in chunks:
                    z = z_ref[p, hh, rows, :]
                    zw_ref[p, hh, rows, :] = z
                    e = jnp.exp(-jnp.abs(z))
                    l = jnp.where(dmat[rows] < t, -(jnp.maximum(z, 0.0) + jnp.log(1.0 + e)), 0.0)
                    hi, lo = _split(l)
                    hi_ref[p, hh, rows, :] = hi
                    lo_ref[p, hh, rows, :] = lo

        def m2(b, p):
            vb = v_ref[block_rows(b), :]
            for hh in range(2):
                cum_ref[p, hh] = (jnp.dot(hi_ref[p, hh], tri, preferred_element_type=F32)
                                  + jnp.dot(lo_ref[p, hh], tri, preferred_element_type=F32))
                da_ref[p, hh] = _dot_nt(doh[hh], vb)

        def v2(b, p):
            t = before(b)
            pick = lane == b
            for hh in range(2):
                for rows in chunks:
                    c = jnp.sum(jnp.where(pick, cm_ref[rows, hh * LANES:(hh + 1) * LANES], 0.0), axis=1, keepdims=True)
                    z = zw_ref[p, hh, rows, :]
                    a = jnp.where(dmat[rows] < t, jnp.exp(z + c + cum_ref[p, hh, rows, :]), 0.0)
                    g = a * da_ref[p, hh, rows, :]
                    a_ref[p, hh, rows, :] = a.astype(_MXU)
                    g_ref[p, hh, rows, :] = g
                    hi, lo = _split(g)
                    ghi_ref[p, hh, rows, :] = hi
                    glo_ref[p, hh, rows, :] = lo
                    sig_ref[p, hh, rows, :] = jax.nn.sigmoid(z)

        def m3(b, p):
            for hh in range(2):
                gp_ref[p, hh] = (jnp.dot(ghi_ref[p, hh], prefix2, preferred_element_type=F32)
                                 + jnp.dot(glo_ref[p, hh], prefix2, preferred_element_type=F32))
                dv_ref[block_rows(b), :] += _dot_tn(a_ref[p, hh], doh[hh])

        def v3(b, p):
            t = before(b)
            for hh in range(2):
                for rows in chunks:
                    gc = gcarry_ref[hh, rows, :]
                    upto = gc + gp_ref[p, hh, rows, 0:KBLK]
                    dz = jnp.where(dmat[rows] < t, g_ref[p, hh, rows, :] - sig_ref[p, hh, rows, :] * upto, 0.0)
                    dz_ref[p, hh, rows, :] = dz.astype(_MXU)
                    gcarry_ref[hh, rows, :] = gc + gp_ref[p, hh, rows, KBLK:2 * KBLK]

        def m4(b, p, dq):
            kb = k_ref[block_rows(b), :]
            for hh in range(2):
                dq = dq + jnp.dot(dz_ref[p, hh], jnp.where(hmask[hh], kb, jnp.zeros_like(kb)), preferred_element_type=F32)
                dk_ref[block_rows(b), :] += _dot_tn(dz_ref[p, hh], qh[hh])
            return dq

        def step(it, dq):
            for u in range(2):
                j = 2 * it + u
                dq = m4(j - 6, u, dq)
                m3(j - 4, u)
                m2(j - 2, u)
                m1(j, u)
                v3(j - 5, 1 - u)
                v2(j - 3, 1 - u)
                v1(j - 1, 1 - u)
            return dq

        for ref in (z_ref, zw_ref, hi_ref, lo_ref, cum_ref, da_ref, a_ref, g_ref, ghi_ref, glo_ref, sig_ref, gp_ref,
                    dz_ref, gcarry_ref):
            ref[...] = jnp.zeros_like(ref)
        dq_ref[...] = lax.fori_loop(0, (nk + 6) // 2, step, dov * 0.0) * 0.125

    qspec = pl.BlockSpec((tq, LANES), lambda p, i: (i, p))
    full = pl.BlockSpec((s, LANES), lambda p, i: (0, p))
    return _pcall(
        body, name=name, grid=(npairs, s // tq),
        in_specs=[qspec, pl.BlockSpec((s, LANES), lambda p, i: (0, npairs + p)),
                  pl.BlockSpec((s, LANES), lambda p, i: (0, 2 * npairs + p)),
                  pl.BlockSpec((tq, 2 * LANES), lambda p, i: (i, p)), qspec],
        out_specs=[qspec, full, full],
        out_shape=[jax.ShapeDtypeStruct((s, W_B), F32)] * 3,
        scratch_shapes=[pltpu.VMEM((2, 2, tq, KBLK), F32)] * 6 + [pltpu.VMEM((2, 2, tq, 2 * KBLK), F32),
                        pltpu.VMEM((2, tq, KBLK), F32)] + [pltpu.VMEM((2, 2, tq, KBLK), BF16)] * 4
                       + [pltpu.VMEM((2, 2, tq, KBLK), _MXU)] * 2,
        compiler_params=_params(("parallel", "arbitrary")),
    )(qkv, qkv, qkv, cm, do)


POOL_HALO = 128


def _pool_window_lane():
    lane = _iota((1, W_C), 1)
    w = jnp.where(lane < 64, POOL_WINDOWS[0], jnp.where(lane < 128, POOL_WINDOWS[1],
                  jnp.where(lane < 192, POOL_WINDOWS[2], POOL_WINDOWS[3])))
    return w.astype(F32)


def _pool_count(tm, i):
    pos = (_iota((tm, W_C), 0) + (i * tm + 1)).astype(F32)
    return jnp.minimum(pos, _pool_window_lane())


def _pool_centered(prev, cur, cnt):
    tm = cur.shape[0]
    xx = jnp.concatenate([prev, cur], axis=0)
    hi, lo = _split(xx)
    t = _iota((tm, tm + POOL_HALO), 0)
    cc = _iota((tm, tm + POOL_HALO), 1) - POOL_HALO
    wsum = jnp.zeros_like(cur)
    for g, w in enumerate(POOL_WINDOWS):
        band = ((cc <= t) & (cc > t - w)).astype(BF16)
        mg = _lane_group_mask(W_C, g)
        wsum += jnp.dot(band, jnp.where(mg, hi, jnp.zeros_like(hi)), preferred_element_type=F32)
        wsum += jnp.dot(band, jnp.where(mg, lo, jnp.zeros_like(lo)), preferred_element_type=F32)
    return wsum / cnt - cur


def _pool_fwd(p, wbd, sc, name):
    s = p.shape[0]
    tm = _tile(s, 256, POOL_HALO)
    r = tm // POOL_HALO

    def body(pp_ref, p_ref, w_ref, sc_ref, y_ref):
        i = pl.program_id(0)
        prev = jnp.where(i > 0, pp_ref[...], 0.0)
        d = _pool_centered(prev, p_ref[...], _pool_count(tm, i))
        y_ref[...] = _dot(d, w_ref[...]) * sc_ref[...]

    return _pcall(
        body, name=name, grid=(s // tm,),
        in_specs=[pl.BlockSpec((POOL_HALO, W_C), lambda i: (jnp.maximum(i * r - 1, 0), 0)),
                  pl.BlockSpec((tm, W_C), lambda i: (i, 0)), pl.BlockSpec((W_C, W_C), lambda i: (0, 0)),
                  pl.BlockSpec((1, W_C), lambda i: (0, 0))],
        out_specs=pl.BlockSpec((tm, W_C), lambda i: (i, 0)),
        out_shape=jax.ShapeDtypeStruct((s, W_C), F32),
        compiler_params=_params(("parallel",)),
    )(p, p, wbd, sc)


def _pool_bwd(p, dy, wbd, wbdt, sc, name):
    s = p.shape[0]
    tm = _tile(s, 256, POOL_HALO)
    r = tm // POOL_HALO
    nt = s // tm

    def body(pp_ref, p_ref, dy_ref, dyn_ref, w_ref, wt_ref, sc_ref, dp_ref, dw_ref, dsc_ref):
        i = pl.program_id(0)

        @pl.when(i == 0)
        def _():
            dw_ref[...] = jnp.zeros_like(dw_ref)
            dsc_ref[...] = jnp.zeros_like(dsc_ref)

        scv = sc_ref[...]
        cnt = _pool_count(tm, i)
        prev = jnp.where(i > 0, pp_ref[...], 0.0)
        d = _pool_centered(prev, p_ref[...], cnt)
        e = _dot(d, w_ref[...])
        dyv = dy_ref[...]
        de = dyv * scv
        dsc_ref[...] += jnp.sum(dyv * e, axis=0, keepdims=True)
        dw_ref[...] += _dot_tn(d, de)
        dd = _dot(de, wt_ref[...])
        ddn = jnp.where(i < nt - 1, _dot(dyn_ref[...] * scv, wt_ref[...]), 0.0)
        yy = jnp.concatenate([dd / cnt, ddn / _pool_window_lane()], axis=0)
        hi, lo = _split(yy)
        t = _iota((tm, tm + POOL_HALO), 0)
        cc = _iota((tm, tm + POOL_HALO), 1)
        acc = jnp.zeros_like(dd)
        for g, w in enumerate(POOL_WINDOWS):
            band = ((cc >= t) & (cc < t + w)).astype(BF16)
            mg = _lane_group_mask(W_C, g)
            acc += jnp.dot(band, jnp.where(mg, hi, jnp.zeros_like(hi)), preferred_element_type=F32)
            acc += jnp.dot(band, jnp.where(mg, lo, jnp.zeros_like(lo)), preferred_element_type=F32)
        dp_ref[...] = acc - dd

    tile = pl.BlockSpec((tm, W_C), lambda i: (i, 0))
    mat = pl.BlockSpec((W_C, W_C), lambda i: (0, 0))
    vec = pl.BlockSpec((1, W_C), lambda i: (0, 0))
    return _pcall(
        body, name=name, grid=(nt,),
        in_specs=[pl.BlockSpec((POOL_HALO, W_C), lambda i: (jnp.maximum(i * r - 1, 0), 0)), tile, tile,
                  pl.BlockSpec((POOL_HALO, W_C), lambda i: (jnp.minimum((i + 1) * r, s // POOL_HALO - 1), 0)),
                  mat, mat, vec],
        out_specs=[tile, mat, vec],
        out_shape=[jax.ShapeDtypeStruct((s, W_C), F32), jax.ShapeDtypeStruct((W_C, W_C), F32),
                   jax.ShapeDtypeStruct((1, W_C), F32)],
        compiler_params=_params(("arbitrary",)),
    )(p, p, dy, dy, wbd, wbdt, sc)


def _mix_cols(ya_ref, yb_ref, yc_ref, cb):
    if cb < 2:
        return ya_ref[:, cb * LANES:(cb + 1) * LANES]
    if cb < 6:
        return yb_ref[:, (cb - 2) * LANES:(cb - 1) * LANES]
    return yc_ref[:, (cb - 6) * LANES:(cb - 5) * LANES]


def _mix_fwd(ya, yb, yc, g, wo, x, name):
    s = x.shape[0]
    tm = _tile(s, 256)

    def body(ya_ref, yb_ref, yc_ref, g_ref, w_ref, x_ref, o_ref, yn_ref):
        sel = _group_mat(LANES)
        for cb in range(D_MODEL // LANES):
            y = _mix_cols(ya_ref, yb_ref, yc_ref, cb)
            r = lax.rsqrt(_dot_sel(y * y, sel) * (1.0 / HEAD_DIM) + EPS)
            yn_ref[:, cb * LANES:(cb + 1) * LANES] = (y * r * g_ref[:, cb * LANES:(cb + 1) * LANES]).astype(_MXU)
        o_ref[...] = x_ref[...] + jnp.dot(yn_ref[...], w_ref[...], preferred_element_type=F32)

    row = lambda n: pl.BlockSpec((tm, n), lambda i: (i, 0))
    return _pcall(
        body, name=name, grid=(s // tm,),
        in_specs=[row(W_A), row(W_B), row(W_C), pl.BlockSpec((1, D_MODEL), lambda i: (0, 0)),
                  pl.BlockSpec((D_MODEL, D_MODEL), lambda i: (0, 0)), row(D_MODEL)],
        out_specs=[row(D_MODEL), row(D_MODEL)],
        out_shape=[jax.ShapeDtypeStruct((s, D_MODEL), F32), jax.ShapeDtypeStruct((s, D_MODEL), _MXU)],
        compiler_params=_params(("parallel",)),
    )(ya, yb, yc, g, wo, x)


def _mix_bwd(dx, ya, yb, yc, g, wot, name):
    s = dx.shape[0]
    tm = _tile(s, 256)

    def body(dx_ref, ya_ref, yb_ref, yc_ref, g_ref, wt_ref, dya_ref, dyb_ref, dyc_ref, dg_ref):
        @pl.when(pl.program_id(0) == 0)
        def _():
            dg_ref[...] = jnp.zeros_like(dg_ref)

        dyn = _dot(dx_ref[...], wt_ref[...])
        sel = _group_mat(LANES)
        for cb in range(D_MODEL // LANES):
            cols = slice(cb * LANES, (cb + 1) * LANES)
            y = _mix_cols(ya_ref, yb_ref, yc_ref, cb)
            r = lax.rsqrt(_dot_sel(y * y, sel) * (1.0 / HEAD_DIM) + EPS)
            dyc_ = dyn[:, cols]
            dyg = dyc_ * g_ref[:, cols]
            m2 = _dot_sel(dyg * y, sel) * (1.0 / HEAD_DIM)
            dy = r * dyg - y * (r * r * r) * m2
            dg_ref[:, cols] += jnp.sum(dyc_ * y * r, axis=0, keepdims=True)
            if cb < 2:
                dya_ref[:, cb * LANES:(cb + 1) * LANES] = dy
            elif cb < 6:
                dyb_ref[:, (cb - 2) * LANES:(cb - 1) * LANES] = dy
            else:
                dyc_ref[:, (cb - 6) * LANES:(cb - 5) * LANES] = dy

    row = lambda n: pl.BlockSpec((tm, n), lambda i: (i, 0))
    vec = pl.BlockSpec((1, D_MODEL), lambda i: (0, 0))
    return _pcall(
        body, name=name, grid=(s // tm,),
        in_specs=[row(D_MODEL), row(W_A), row(W_B), row(W_C), vec, pl.BlockSpec((D_MODEL, D_MODEL), lambda i: (0, 0))],
        out_specs=[row(W_A), row(W_B), row(W_C), vec],
        out_shape=[jax.ShapeDtypeStruct((s, W_A), F32), jax.ShapeDtypeStruct((s, W_B), F32),
                   jax.ShapeDtypeStruct((s, W_C), F32), jax.ShapeDtypeStruct((1, D_MODEL), F32)],
        compiler_params=_params(("arbitrary",)),
    )(dx, ya, yb, yc, g, wot)


FFN_CHUNK = 256
N_CHUNKS = D_FF // FFN_CHUNK
CW_ROWS = 8


def _ffn_up_fwd(x, g, w, name):
    s = x.shape[0]
    n = w.shape[1]
    tm, tn = _tile(s, 512), _tile(n, 512, LANES)

    def body(x_ref, g_ref, w_ref, z_ref, h_ref):
        @pl.when(pl.program_id(1) == 0)
        def _():
            xv = x_ref[...]
            h_ref[...] = (xv * _rms(xv) * g_ref[...]).astype(_MXU)
        z_ref[...] = jnp.dot(h_ref[...], w_ref[...], preferred_element_type=F32)

    return _pcall(
        body, name=name, grid=(s // tm, n // tn),
        in_specs=[pl.BlockSpec((tm, D_MODEL), lambda i, j: (i, 0)), pl.BlockSpec((1, D_MODEL), lambda i, j: (0, 0)),
                  pl.BlockSpec((D_MODEL, tn), lambda i, j: (0, j))],
        out_specs=[pl.BlockSpec((tm, tn), lambda i, j: (i, j)), pl.BlockSpec((tm, D_MODEL), lambda i, j: (i, 0))],
        out_shape=[jax.ShapeDtypeStruct((s, n), F32), jax.ShapeDtypeStruct((s, D_MODEL), _MXU)],
        compiler_params=_params(("parallel", "arbitrary")),
    )(x, g, w)


def _conv(cur, prev8, cw_ref):
    s1 = _shift_down(cur, prev8, 1)
    s2 = _shift_down(cur, prev8, 2)
    zc = cw_ref[3:4, :] + s2 * cw_ref[0:1, :]
    zc = zc + s1 * cw_ref[1:2, :]
    zc = zc + cur * cw_ref[2:3, :]
    return zc, s1, s2


def _halo_specs(tm, s):
    r = tm // SUBLANES
    prev = lambda off: pl.BlockSpec((SUBLANES, FFN_CHUNK), lambda i, j: (jnp.maximum(i * r - 1, 0), j + off))
    nxt = lambda off: pl.BlockSpec((SUBLANES, FFN_CHUNK), lambda i, j: (jnp.minimum((i + 1) * r, s // SUBLANES - 1), j + off))
    return prev, nxt


def _ffn_down_fwd(z, cw, wd, x, name):
    s = x.shape[0]
    tm = _tile(s, 512)
    prev, _ = _halo_specs(tm, s)

    def body(zg_ref, zu_ref, pg_ref, pu_ref, cg_ref, cu_ref, w_ref, x_ref, o_ref, act_ref, acc_ref):
        i, j = pl.program_id(0), pl.program_id(1)
        first = i > 0
        zg, _, _ = _conv(zg_ref[...], jnp.where(first, pg_ref[...], 0.0), cg_ref)
        zu, _, _ = _conv(zu_ref[...], jnp.where(first, pu_ref[...], 0.0), cu_ref)
        act = (zg * jax.nn.sigmoid(zg) * zu).astype(_MXU)
        act_ref[...] = act

        @pl.when(j == 0)
        def _():
            acc_ref[...] = x_ref[...]
        acc_ref[...] += jnp.dot(act, w_ref[...], preferred_element_type=F32)

        @pl.when(j == N_CHUNKS - 1)
        def _():
            o_ref[...] = acc_ref[...]

    zt = lambda off: pl.BlockSpec((tm, FFN_CHUNK), lambda i, j: (i, j + off))
    cwt = lambda off: pl.BlockSpec((CW_ROWS, FFN_CHUNK), lambda i, j: (0, j + off))
    return _pcall(
        body, name=name, grid=(s // tm, N_CHUNKS),
        in_specs=[zt(0), zt(N_CHUNKS), prev(0), prev(N_CHUNKS), cwt(0), cwt(N_CHUNKS),
                  pl.BlockSpec((FFN_CHUNK, D_MODEL), lambda i, j: (j, 0)), pl.BlockSpec((tm, D_MODEL), lambda i, j: (i, 0))],
        out_specs=[pl.BlockSpec((tm, D_MODEL), lambda i, j: (i, 0)), pl.BlockSpec((tm, FFN_CHUNK), lambda i, j: (i, j))],
        out_shape=[jax.ShapeDtypeStruct((s, D_MODEL), F32), jax.ShapeDtypeStruct((s, D_FF), _MXU)],
        scratch_shapes=[pltpu.VMEM((tm, D_MODEL), F32)],
        compiler_params=_params(("parallel", "arbitrary")),
    )(z, z, z, z, cw, cw, wd, x)


def _ffn_down_bwd(dx, z, cw, wdt, name):
    s = dx.shape[0]
    tm = _tile(s, 512)

    def body(dx_ref, zg_ref, zu_ref, pg_ref, pu_ref, cg_ref, cu_ref, wt_ref, dg_ref, du_ref, dcg_ref, dcu_ref):
        i = pl.program_id(1)
        first = i > 0

        @pl.when(i == 0)
        def _():
            dcg_ref[...] = jnp.zeros_like(dcg_ref)
            dcu_ref[...] = jnp.zeros_like(dcu_ref)

        dact = _dot(dx_ref[...], wt_ref[...])
        zg, g1, g2 = _conv(zg_ref[...], jnp.where(first, pg_ref[...], 0.0), cg_ref)
        zu, u1, u2 = _conv(zu_ref[...], jnp.where(first, pu_ref[...], 0.0), cu_ref)
        sg = jax.nn.sigmoid(zg)
        silu = zg * sg
        dzu = dact * silu
        dzg = dact * zu * (sg * (1.0 + zg * (1.0 - sg)))
        dg_ref[...] = dzg
        du_ref[...] = dzu
        for ref, dzc, cur, s1, s2 in ((dcg_ref, dzg, zg_ref[...], g1, g2), (dcu_ref, dzu, zu_ref[...], u1, u2)):
            ref[0:1, :] += jnp.sum(dzc * s2, axis=0, keepdims=True)
            ref[1:2, :] += jnp.sum(dzc * s1, axis=0, keepdims=True)
            ref[2:3, :] += jnp.sum(dzc * cur, axis=0, keepdims=True)
            ref[3:4, :] += jnp.sum(dzc, axis=0, keepdims=True)

    zt = lambda off: pl.BlockSpec((tm, FFN_CHUNK), lambda j, i: (i, j + off))
    r = tm // SUBLANES
    pv = lambda off: pl.BlockSpec((SUBLANES, FFN_CHUNK), lambda j, i: (jnp.maximum(i * r - 1, 0), j + off))
    cwt = lambda off: pl.BlockSpec((CW_ROWS, FFN_CHUNK), lambda j, i: (0, j + off))
    out_t = pl.BlockSpec((tm, FFN_CHUNK), lambda j, i: (i, j))
    dc_t = pl.BlockSpec((CW_ROWS, FFN_CHUNK), lambda j, i: (0, j))
    dzg, dzu, dcg, dcu = _pcall(
        body, name=name, grid=(N_CHUNKS, s // tm),
        in_specs=[pl.BlockSpec((tm, D_MODEL), lambda j, i: (i, 0)), zt(0), zt(N_CHUNKS), pv(0), pv(N_CHUNKS),
                  cwt(0), cwt(N_CHUNKS), pl.BlockSpec((D_MODEL, FFN_CHUNK), lambda j, i: (0, j))],
        out_specs=[out_t, out_t, dc_t, dc_t],
        out_shape=[jax.ShapeDtypeStruct((s, D_FF), F32), jax.ShapeDtypeStruct((s, D_FF), F32),
                   jax.ShapeDtypeStruct((CW_ROWS, D_FF), F32), jax.ShapeDtypeStruct((CW_ROWS, D_FF), F32)],
        compiler_params=_params(("parallel", "arbitrary")),
    )(dx, z, z, z, z, cw, cw, wdt)
    return dzg, dzu, jnp.concatenate([dcg, dcu], axis=1)


def _ffn_up_bwd(dzg, dzu, cw, wut, x, g, dres, name):
    s = x.shape[0]
    tm = _tile(s, 512)
    _, nxt = _halo_specs(tm, s)
    nt = s // tm

    def body(dg_ref, du_ref, ng_ref, nu_ref, cg_ref, cu_ref, wg_ref, wu_ref, x_ref, g_ref, dres_ref,
             dzg_ref, dzu_ref, dx_ref, dgn_ref, acc_ref):
        i, j = pl.program_id(0), pl.program_id(1)
        last = i < nt - 1

        def conv_bwd(cur, nxt8, cw_ref):
            up1 = _shift_up(cur, nxt8, 1)
            up2 = _shift_up(cur, nxt8, 2)
            return cur * cw_ref[2:3, :] + up1 * cw_ref[1:2, :] + up2 * cw_ref[0:1, :]

        dzg_ = conv_bwd(dg_ref[...], jnp.where(last, ng_ref[...], 0.0), cg_ref).astype(_MXU)
        dzu_ = conv_bwd(du_ref[...], jnp.where(last, nu_ref[...], 0.0), cu_ref).astype(_MXU)
        dzg_ref[...] = dzg_
        dzu_ref[...] = dzu_

        @pl.when(j == 0)
        def _():
            acc_ref[...] = jnp.zeros_like(acc_ref)
        acc_ref[...] += (jnp.dot(dzg_, wg_ref[...], preferred_element_type=F32)
                         + jnp.dot(dzu_, wu_ref[...], preferred_element_type=F32))

        @pl.when((i == 0) & (j == 0))
        def _():
            dgn_ref[...] = jnp.zeros_like(dgn_ref)

        @pl.when(j == N_CHUNKS - 1)
        def _():
            dx, dgn = _rms_bwd(x_ref[...], g_ref[...], acc_ref[...])
            dx_ref[...] = dres_ref[...] + dx
            dgn_ref[...] += dgn

    zt = pl.BlockSpec((tm, FFN_CHUNK), lambda i, j: (i, j))
    cwt = lambda off: pl.BlockSpec((CW_ROWS, FFN_CHUNK), lambda i, j: (0, j + off))
    wt = lambda off: pl.BlockSpec((FFN_CHUNK, D_MODEL), lambda i, j: (j + off, 0))
    row = pl.BlockSpec((tm, D_MODEL), lambda i, j: (i, 0))
    vec = pl.BlockSpec((1, D_MODEL), lambda i, j: (0, 0))
    return _pcall(
        body, name=name, grid=(nt, N_CHUNKS),
        in_specs=[zt, zt, nxt(0), nxt(0), cwt(0), cwt(N_CHUNKS), wt(0), wt(N_CHUNKS), row, vec, row],
        out_specs=[zt, zt, row, vec],
        out_shape=[jax.ShapeDtypeStruct((s, D_FF), _MXU), jax.ShapeDtypeStruct((s, D_FF), _MXU),
                   jax.ShapeDtypeStruct((s, D_MODEL), F32), jax.ShapeDtypeStruct((1, D_MODEL), F32)],
        scratch_shapes=[pltpu.VMEM((tm, D_MODEL), F32)],
        compiler_params=_params(("arbitrary", "arbitrary")),
    )(dzg, dzu, dzg, dzu, cw, cw, wut, wut, x, g, dres)


def _final_loss(x, g, tgt, name):
    s = x.shape[0]
    tm = _tile(s, 256)

    def body(x_ref, g_ref, t_ref, loss_ref, dx_ref, dg_ref):
        @pl.when(pl.program_id(0) == 0)
        def _():
            loss_ref[...] = jnp.zeros_like(loss_ref)
            dg_ref[...] = jnp.zeros_like(dg_ref)

        xv, gv = x_ref[...], g_ref[...]
        err = xv * _rms(xv) * gv - t_ref[...]
        per_tok = jnp.mean(err * err, axis=-1, keepdims=True)
        loss_ref[...] += 0.5 * jnp.sum(per_tok, axis=0, keepdims=True)
        dx, dg = _rms_bwd(xv, gv, err * (1.0 / D_MODEL))
        dx_ref[...] = dx
        dg_ref[...] += dg

    row = pl.BlockSpec((tm, D_MODEL), lambda i: (i, 0))
    vec = pl.BlockSpec((1, D_MODEL), lambda i: (0, 0))
    return _pcall(
        body, name=name, grid=(s // tm,),
        in_specs=[row, vec, row], out_specs=[pl.BlockSpec((1, 1), lambda i: (0, 0)), row, vec],
        out_shape=[jax.ShapeDtypeStruct((1, 1), F32), jax.ShapeDtypeStruct((s, D_MODEL), F32),
                   jax.ShapeDtypeStruct((1, D_MODEL), F32)],
        compiler_params=_params(("arbitrary",)),
    )(x, g, tgt)


def _adamw(parts, w, m, v, name):
    r, c = w.shape
    tr = _tile(r, 256)
    c1 = 1.0 - ADAM_B1 ** ADAM_STEP
    c2 = 1.0 - ADAM_B2 ** ADAM_STEP

    def body(p_ref, w_ref, m_ref, v_ref, g_ref, d_ref, mo_ref, vo_ref):
        g = p_ref[0]
        for i in range(1, N_DEV):
            g = g + p_ref[i]
        mn = ADAM_B1 * m_ref[...] + (1.0 - ADAM_B1) * g
        vn = ADAM_B2 * v_ref[...] + (1.0 - ADAM_B2) * (g * g)
        g_ref[...] = g
        mo_ref[...] = mn
        vo_ref[...] = vn
        d_ref[...] = -ADAM_LR * ((mn / c1) / (jnp.sqrt(vn / c2) + ADAM_EPS) + ADAM_WD * w_ref[...])

    t2 = pl.BlockSpec((tr, c), lambda i: (i, 0))
    return _pcall(
        body, name=name, grid=(r // tr,),
        in_specs=[pl.BlockSpec((N_DEV, tr, c), lambda i: (0, i, 0)), t2, t2, t2],
        out_specs=[t2] * 4, out_shape=[jax.ShapeDtypeStruct((r, c), F32)] * 4,
        compiler_params=_params(("parallel",)),
    )(parts, w, m, v)


SMALL = ("norm1_g", "sgu_norm_g", "sgu_w", "sgu_b", "pool_w", "pool_scale", "mix_norm_g", "norm2_g", "conv_b", "final_g")
SHARDED = ("w_in", "w_o", "w_up", "conv_w", "w_down")
ORDER = ("norm1_g", "w_in", "sgu_norm_g", "sgu_w", "sgu_b", "pool_w", "pool_scale", "mix_norm_g", "w_o", "norm2_g",
         "w_up", "conv_w", "conv_b", "w_down", "final_g")


def _pack(tree):
    return jnp.concatenate([tree[n].reshape(-1) for n in SMALL]).reshape(-1, LANES)


def _unpack(flat, like):
    out, off = {}, 0
    flat = flat.reshape(-1)
    for n in SMALL:
        size = math.prod(like[n].shape)
        out[n] = flat[off:off + size].reshape(like[n].shape)
        off += size
    return out


def _block_diag(pw):
    z = jnp.zeros((W_C, W_C), pw.dtype)
    for g in range(4):
        z = z.at[g * 64:(g + 1) * 64, g * 64:(g + 1) * 64].set(pw[g])
    return z


def kernel(x, norm1_g, w_in, sgu_norm_g, sgu_w, sgu_b, pool_w, pool_scale, mix_norm_g, w_o, norm2_g, w_up, conv_w, conv_b, w_down, final_g, loss_target, m_norm1_g, m_w_in, m_sgu_norm_g, m_sgu_w, m_sgu_b, m_pool_w, m_pool_scale, m_mix_norm_g, m_w_o, m_norm2_g, m_w_up, m_conv_w, m_conv_b, m_w_down, m_final_g, v_norm1_g, v_w_in, v_sgu_norm_g, v_sgu_w, v_sgu_b, v_pool_w, v_pool_scale, v_mix_norm_g, v_w_o, v_norm2_g, v_w_up, v_conv_w, v_conv_b, v_w_down, v_final_g):
    weights = dict(norm1_g=norm1_g, w_in=w_in, sgu_norm_g=sgu_norm_g, sgu_w=sgu_w, sgu_b=sgu_b, pool_w=pool_w,
                   pool_scale=pool_scale, mix_norm_g=mix_norm_g, w_o=w_o, norm2_g=norm2_g, w_up=w_up, conv_w=conv_w,
                   conv_b=conv_b, w_down=w_down, final_g=final_g)
    mom = dict(norm1_g=m_norm1_g, w_in=m_w_in, sgu_norm_g=m_sgu_norm_g, sgu_w=m_sgu_w, sgu_b=m_sgu_b, pool_w=m_pool_w,
               pool_scale=m_pool_scale, mix_norm_g=m_mix_norm_g, w_o=m_w_o, norm2_g=m_norm2_g, w_up=m_w_up,
               conv_w=m_conv_w, conv_b=m_conv_b, w_down=m_w_down, final_g=m_final_g)
    var = dict(norm1_g=v_norm1_g, w_in=v_w_in, sgu_norm_g=v_sgu_norm_g, sgu_w=v_sgu_w, sgu_b=v_sgu_b, pool_w=v_pool_w,
               pool_scale=v_pool_scale, mix_norm_g=v_mix_norm_g, w_o=v_w_o, norm2_g=v_norm2_g, w_up=v_w_up,
               conv_w=v_conv_w, conv_b=v_conv_b, w_down=v_w_down, final_g=v_final_g)
    depth = w_in.shape[0]
    s = x.shape[1]
    xs = x.reshape(s, D_MODEL)
    tgt = loss_target.reshape(s, D_MODEL)

    g_in, g_o, g_up, g_cw, g_dn = _exchange(
        [w_in.astype(_MXU), w_o.astype(_MXU), w_up.astype(_MXU), conv_w, w_down.astype(_MXU)], "gather_weights", False)
    full_in = jnp.transpose(g_in, (1, 2, 0, 3)).reshape(depth, D_MODEL, IN_COLS)
    full_o = jnp.transpose(g_o, (1, 0, 2, 3)).reshape(depth, D_MODEL, D_MODEL)
    full_up = jnp.transpose(g_up, (1, 2, 0, 3)).reshape(depth, D_MODEL, 2 * D_FF)
    full_cw = jnp.transpose(g_cw, (1, 2, 0, 3)).reshape(depth, 3, 2 * D_FF)
    full_dn = jnp.transpose(g_dn, (1, 0, 2, 3)).reshape(depth, D_FF, D_MODEL)

    tril = jnp.tril(jnp.ones((CHUNK, CHUNK), bool))
    layers = []
    for l in range(depth):
        wbd = _block_diag(pool_w[l])
        layers.append(dict(
            g1=norm1_g[l][None], w_in=full_in[l], w_in_t=full_in[l].T,
            gn=sgu_norm_g[l][None], wm=jnp.where(tril[None], sgu_w[l], 0.0).astype(_MXU),
            bias=jnp.repeat(sgu_b[l].T, HEAD_DIM, axis=1),
            wbd=wbd.astype(_MXU), wbd_t=wbd.T.astype(_MXU), sc=pool_scale[l][None],
            gmix=mix_norm_g[l][None], w_o=full_o[l], w_o_t=full_o[l].T,
            g2=norm2_g[l][None], w_up=full_up[l], w_up_t=full_up[l].T,
            cw=jnp.concatenate([full_cw[l], conv_b[l][None], jnp.zeros((CW_ROWS - 4, 2 * D_FF), F32)], axis=0),
            w_dn=full_dn[l], w_dn_t=full_dn[l].T))

    saved = []
    cur = xs
    for l, p in enumerate(layers):
        a_in, qkv, p_in, h1 = _inproj_fwd(cur, p["g1"], p["w_in"], f"inproj_fwd{l}")
        y_a = _sgu_fwd(a_in, p["gn"], p["wm"], p["bias"], f"sgu_fwd{l}")
        y_b, cm = _attn_fwd(qkv, f"attn_fwd{l}")
        y_c = _pool_fwd(p_in, p["wbd"], p["sc"], f"pool_fwd{l}")
        x_mid, yn = _mix_fwd(y_a, y_b, y_c, p["gmix"], p["w_o"], cur, f"mix_fwd{l}")
        z, h2 = _ffn_up_fwd(x_mid, p["g2"], p["w_up"], f"ffn_up_fwd{l}")
        x_out, act = _ffn_down_fwd(z, p["cw"], p["w_dn"], x_mid, f"ffn_down_fwd{l}")
        saved.append(dict(x_in=cur, a_in=a_in, qkv=qkv, p_in=p_in, h1=h1, y_a=y_a, y_b=y_b, cm=cm, y_c=y_c, x_mid=x_mid,
                          yn=yn, z=z, h2=h2, act=act))
        cur = x_out
    loss_part, dx, dg_final = _final_loss(cur, final_g[None], tgt, "final_loss")

    small = {n: [None] * depth for n in SMALL if n != "final_g"}
    big = {n: [None] * depth for n in SHARDED}
    for l in reversed(range(depth)):
        p, sv = layers[l], saved[l]
        dzg, dzu, dcw = _ffn_down_bwd(dx, sv["z"], p["cw"], p["w_dn_t"], f"ffn_down_bwd{l}")
        big["w_down"][l] = _mm_tn(sv["act"], dx, f"dw_down{l}").reshape(N_DEV, D_FF // N_DEV, D_MODEL)
        dzg_b, dzu_b, dx_mid, dg2 = _ffn_up_bwd(dzg, dzu, p["cw"], p["w_up_t"], sv["x_mid"], p["g2"], dx, f"ffn_up_bwd{l}")
        dw_up = jnp.stack([_mm_tn(sv["h2"], dzg_b, f"dw_up_g{l}"), _mm_tn(sv["h2"], dzu_b, f"dw_up_u{l}")])
        big["w_up"][l] = jnp.transpose(dw_up.reshape(2, D_MODEL, N_DEV // 2, 2 * D_FF // N_DEV), (0, 2, 1, 3)).reshape(
            N_DEV, D_MODEL, 2 * D_FF // N_DEV)
        big["conv_w"][l] = jnp.transpose(dcw[0:3].reshape(3, N_DEV, 2 * D_FF // N_DEV), (1, 0, 2))
        small["conv_b"][l] = dcw[3]
        small["norm2_g"][l] = dg2[0]
        dya, dyb, dyc, dgmix = _mix_bwd(dx_mid, sv["y_a"], sv["y_b"], sv["y_c"], p["gmix"], p["w_o_t"], f"mix_bwd{l}")
        small["mix_norm_g"][l] = dgmix[0]
        big["w_o"][l] = _mm_tn(sv["yn"], dx_mid, f"dw_o{l}").reshape(N_DEV, D_MODEL // N_DEV, D_MODEL)
        dp, dwbd, dsc = _pool_bwd(sv["p_in"], dyc, p["wbd"], p["wbd_t"], p["sc"], f"pool_bwd{l}")
        small["pool_w"][l] = jnp.stack([dwbd[g * 64:(g + 1) * 64, g * 64:(g + 1) * 64] for g in range(4)])
        small["pool_scale"][l] = dsc[0]
        dq, dk, dv = _attn_bwd(sv["qkv"], sv["cm"], dyb, f"attn_bwd{l}")
        da, dwm, dbias, dgn = _sgu_bwd(sv["a_in"], dya, p["gn"], p["wm"], p["bias"], f"sgu_bwd{l}")
        small["sgu_w"][l] = dwm
        small["sgu_b"][l] = jnp.sum(dbias.reshape(CHUNK, 4, HEAD_DIM), axis=-1).T
        small["sgu_norm_g"][l] = dgn[0]
        dx, dg1 = _inproj_bwd(da, dq, dk, dv, dp, p["w_in_t"], sv["x_in"], p["g1"], dx_mid, f"inproj_bwd{l}")
        small["norm1_g"][l] = dg1[0]
        pieces = (da, dq, dk, dv, dp)
        dw_in = jnp.concatenate([_mm_tn(sv["h1"], t, f"dw_in{i}_{l}") for i, t in enumerate(pieces)], axis=1)
        big["w_in"][l] = jnp.transpose(dw_in.reshape(D_MODEL, N_DEV, IN_COLS // N_DEV), (1, 0, 2))

    names = [(n, l) for n in SHARDED for l in range(depth)]
    recv = _exchange([big[n][l] for n, l in names], "scatter_grads", True)
    out_g, out_d, out_m, out_v = {}, {}, {}, {}
    for n in SHARDED:
        res = [_adamw(recv[names.index((n, l))], weights[n][l], mom[n][l], var[n][l], f"adamw_{n}{l}") for l in range(depth)]
        out_g[n], out_d[n], out_m[n], out_v[n] = (jnp.stack([r[i] for r in res]) for i in range(4))

    part = {n: jnp.stack(small[n]) for n in small}
    part["final_g"] = dg_final[0]
    (gathered,) = _exchange([_pack(part)], "gather_small_grads", False)
    sg, sd, sm, sv_ = _adamw(gathered, _pack(weights), _pack(mom), _pack(var), "adamw_small")
    for tree, flat in ((out_g, sg), (out_d, sd), (out_m, sm), (out_v, sv_)):
        tree.update(_unpack(flat, weights))

    loss = lax.psum(loss_part[0, 0], ("x", "y", "c"))
    grad_x = dx.reshape(1, s, D_MODEL)
    return (loss, grad_x, *[out_g[n] for n in ORDER], *[out_d[n] for n in ORDER], *[out_m[n] for n in ORDER],
            *[out_v[n] for n in ORDER])
```

```python
import functools
import math

import jax
import jax.numpy as jnp
import numpy as np
from jax import lax
from jax.experimental import pallas as pl
from jax.experimental.pallas import tpu as pltpu

F32 = jnp.float32
BF16 = jnp.bfloat16
_MXU = jnp.bfloat16
GRAD_WIRE = jnp.bfloat16

D_MODEL = 1024
W_A = 256
W_B = 512
W_C = 256
HEAD_DIM = 64
IN_COLS = 2 * W_A + 3 * W_B + W_C
D_FF = 2816
CHUNK = 128
POOL_WINDOWS = (2, 4, 8, 16)
EPS = 1e-6
N_DEV = 8
LANES = 128
SUBLANES = 8
VMEM_LIMIT = 48 * 1024 * 1024

ADAM_LR = 0.001
ADAM_B1 = 0.9
ADAM_B2 = 0.999
ADAM_EPS = 1e-08
ADAM_WD = 0.01
ADAM_STEP = 10

INV_SQRT2 = 1.0 / math.sqrt(2.0)
INV_SQRT_2PI = 1.0 / math.sqrt(2.0 * math.pi)


def _pcall(body, **kw):
    return pl.pallas_call(body, **kw)


def _params(dims=None):
    return pltpu.CompilerParams(dimension_semantics=dims, vmem_limit_bytes=VMEM_LIMIT)


def _tile(n, pref, mult=SUBLANES):
    t = min(n, pref) // mult * mult
    while t >= mult:
        if n % t == 0:
            return t
        t -= mult
    return n


def _iota(shape, dim):
    return lax.broadcasted_iota(jnp.int32, shape, dim)


def _dot(a, b):
    return jnp.dot(a.astype(_MXU), b.astype(_MXU), preferred_element_type=F32)


def _dot_nt(a, b):
    return lax.dot_general(a.astype(_MXU), b.astype(_MXU), (((1,), (1,)), ((), ())), preferred_element_type=F32)


def _dot_tn(a, b):
    return lax.dot_general(a.astype(_MXU), b.astype(_MXU), (((0,), (0,)), ((), ())), preferred_element_type=F32)


def _split(x):
    hi = x.astype(BF16)
    lo = (x - hi.astype(F32)).astype(BF16)
    return hi, lo


def _dot_sel(x, sel):
    hi, lo = _split(x)
    return jnp.dot(hi, sel, preferred_element_type=F32) + jnp.dot(lo, sel, preferred_element_type=F32)


def _sel_dot(sel, x):
    hi, lo = _split(x)
    return jnp.dot(sel, hi, preferred_element_type=F32) + jnp.dot(sel, lo, preferred_element_type=F32)


def _group_mat(n):
    r = jnp.right_shift(_iota((n, n), 0), 6)
    c = jnp.right_shift(_iota((n, n), 1), 6)
    return (r == c).astype(BF16)


def _lane_group_mask(n, g):
    lane = _iota((1, n), 1)
    return (lane >= g * HEAD_DIM) & (lane < (g + 1) * HEAD_DIM)


def _gelu(a):
    return 0.5 * a * (1.0 + lax.erf(a * INV_SQRT2))


def _gelu_grad(a):
    return 0.5 * (1.0 + lax.erf(a * INV_SQRT2)) + a * jnp.exp(-0.5 * a * a) * INV_SQRT_2PI


def _rms(x):
    return lax.rsqrt(jnp.mean(x * x, axis=-1, keepdims=True) + EPS)


def _rms_bwd(x, g, dy):
    r = _rms(x)
    dyg = dy * g
    m2 = jnp.mean(dyg * x, axis=-1, keepdims=True)
    dx = r * dyg - x * (r * r * r) * m2
    dg = jnp.sum(dy * x * r, axis=0, keepdims=True)
    return dx, dg


def _shift_down(cur, prev8, k):
    rolled = pltpu.roll(cur, k, 0)
    row8 = _iota(prev8.shape, 0)
    top = jnp.where(row8 < k, pltpu.roll(prev8, k, 0), rolled[0:SUBLANES])
    return jnp.concatenate([top, rolled[SUBLANES:]], axis=0)


def _shift_up(cur, next8, k):
    n = cur.shape[0]
    rolled = pltpu.roll(cur, n - k, 0)
    row8 = _iota(next8.shape, 0)
    bot = jnp.where(row8 >= SUBLANES - k, pltpu.roll(next8, SUBLANES - k, 0), rolled[n - SUBLANES:])
    return jnp.concatenate([rolled[:n - SUBLANES], bot], axis=0)


def _mesh_pos():
    return lax.axis_index("x"), lax.axis_index("y"), lax.axis_index("c")


def _peer(x, y, c, k):
    px = 1 - x if k & 4 else x
    py = 1 - y if k & 2 else y
    pc = 1 - c if k & 1 else c
    return px, py, pc


def _exchange(arrs, name, scatter):
    n = len(arrs)
    out_shapes = [jax.ShapeDtypeStruct(a.shape if scatter else (N_DEV,) + a.shape, a.dtype) for a in arrs]

    def body(*refs):
        ins, outs = refs[:n], refs[n:2 * n]
        send, recv, loc = refs[2 * n:]
        x, y, c = _mesh_pos()
        me = 4 * x + 2 * y + c

        def src(a, idx):
            return ins[a].at[idx] if scatter else ins[a]

        locals_ = [pltpu.make_async_copy(src(a, me), outs[a].at[me], loc.at[a]) for a in range(n)]
        for cp in locals_:
            cp.start()
        copies = []
        for k in range(1, N_DEV):
            px, py, pc = _peer(x, y, c, k)
            pidx = 4 * px + 2 * py + pc
            for a in range(n):
                s = a * (N_DEV - 1) + k - 1
                pltpu.make_async_remote_copy(
                    src_ref=src(a, pidx), dst_ref=outs[a].at[me], send_sem=send.at[s], recv_sem=recv.at[s],
                    device_id=(px, py, pc), device_id_type=pl.DeviceIdType.MESH).start()
                copies.append(pltpu.make_async_remote_copy(
                    src_ref=src(a, pidx), dst_ref=outs[a].at[pidx], send_sem=send.at[s], recv_sem=recv.at[s],
                    device_id=(px, py, pc), device_id_type=pl.DeviceIdType.MESH))
        for cp in copies:
            cp.wait()
        for cp in locals_:
            cp.wait()

    hbm = pl.BlockSpec(memory_space=pl.ANY)
    return _pcall(
        body, name=name, out_shape=out_shapes, in_specs=[hbm] * n, out_specs=[hbm] * n,
        scratch_shapes=[pltpu.SemaphoreType.DMA((n * (N_DEV - 1),)), pltpu.SemaphoreType.DMA((n * (N_DEV - 1),)),
                        pltpu.SemaphoreType.DMA((n,))],
        compiler_params=pltpu.CompilerParams(has_side_effects=True),
    )(*arrs)


def _mm_tn(a, b, name):
    s, m = a.shape
    n = b.shape[1]
    tm, tn, tk = _tile(m, 1024, LANES), _tile(n, 512, LANES), _tile(s, 1024)

    def body(a_ref, b_ref, o_ref):
        @pl.when(pl.program_id(2) == 0)
        def _():
            o_ref[...] = jnp.zeros_like(o_ref)
        o_ref[...] += _dot_tn(a_ref[...], b_ref[...])

    return _pcall(
        body, name=name, grid=(m // tm, n // tn, s // tk),
        in_specs=[pl.BlockSpec((tk, tm), lambda i, j, k: (k, i)), pl.BlockSpec((tk, tn), lambda i, j, k: (k, j))],
        out_specs=pl.BlockSpec((tm, tn), lambda i, j, k: (i, j)),
        out_shape=jax.ShapeDtypeStruct((m, n), F32),
        compiler_params=_params(("parallel", "parallel", "arbitrary")),
    )(a, b)


def _inproj_fwd(x, g, w, name):
    s = x.shape[0]
    tm = _tile(s, 256)

    def body(x_ref, g_ref, w_ref, a_ref, qkv_ref, p_ref, h_ref):
        xv = x_ref[...]
        h = (xv * _rms(xv) * g_ref[...]).astype(_MXU)
        h_ref[...] = h
        a_ref[...] = jnp.dot(h, w_ref[:, 0:2 * W_A], preferred_element_type=F32)
        q = jnp.dot(h, w_ref[:, 2 * W_A:2 * W_A + W_B], preferred_element_type=F32)
        qkv_ref[:, 0:W_B] = (q * 0.125).astype(_MXU)
        kv = jnp.dot(h, w_ref[:, 2 * W_A + W_B:2 * W_A + 3 * W_B], preferred_element_type=F32)
        qkv_ref[:, W_B:3 * W_B] = kv.astype(_MXU)
        p_ref[...] = jnp.dot(h, w_ref[:, 2 * W_A + 3 * W_B:IN_COLS], preferred_element_type=F32)

    row = lambda n: pl.BlockSpec((tm, n), lambda i: (i, 0))
    return _pcall(
        body, name=name, grid=(s // tm,),
        in_specs=[row(D_MODEL), pl.BlockSpec((1, D_MODEL), lambda i: (0, 0)),
                  pl.BlockSpec((D_MODEL, IN_COLS), lambda i: (0, 0))],
        out_specs=[row(2 * W_A), row(3 * W_B), row(W_C), row(D_MODEL)],
        out_shape=[jax.ShapeDtypeStruct((s, 2 * W_A), F32), jax.ShapeDtypeStruct((s, 3 * W_B), _MXU),
                   jax.ShapeDtypeStruct((s, W_C), F32), jax.ShapeDtypeStruct((s, D_MODEL), _MXU)],
        compiler_params=_params(("parallel",)),
    )(x, g, w)


def _inproj_bwd(da, dq, dk, dv, dp, wt, x, g, dres, name):
    s = x.shape[0]
    tm = _tile(s, 256)

    def body(da_ref, dq_ref, dk_ref, dv_ref, dp_ref, wt_ref, x_ref, g_ref, dres_ref, dx_ref, dg_ref):
        dh = _dot(da_ref[...], wt_ref[0:2 * W_A, :])
        dh += _dot(dq_ref[...], wt_ref[2 * W_A:2 * W_A + W_B, :])
        dh += _dot(dk_ref[...], wt_ref[2 * W_A + W_B:2 * W_A + 2 * W_B, :])
        dh += _dot(dv_ref[...], wt_ref[2 * W_A + 2 * W_B:2 * W_A + 3 * W_B, :])
        dh += _dot(dp_ref[...], wt_ref[2 * W_A + 3 * W_B:IN_COLS, :])
        dx, dg = _rms_bwd(x_ref[...], g_ref[...], dh)
        dx_ref[...] = dres_ref[...] + dx

        @pl.when(pl.program_id(0) == 0)
        def _():
            dg_ref[...] = jnp.zeros_like(dg_ref)
        dg_ref[...] += dg

    row = lambda n: pl.BlockSpec((tm, n), lambda i: (i, 0))
    vec = pl.BlockSpec((1, D_MODEL), lambda i: (0, 0))
    return _pcall(
        body, name=name, grid=(s // tm,),
        in_specs=[row(2 * W_A), row(W_B), row(W_B), row(W_B), row(W_C),
                  pl.BlockSpec((IN_COLS, D_MODEL), lambda i: (0, 0)), row(D_MODEL), vec, row(D_MODEL)],
        out_specs=[row(D_MODEL), vec],
        out_shape=[jax.ShapeDtypeStruct((s, D_MODEL), F32), jax.ShapeDtypeStruct((1, D_MODEL), F32)],
        compiler_params=_params(("arbitrary",)),
    )(da, dq, dk, dv, dp, wt, x, g, dres)


def _sgu_core(a, gn, wm_ref, bias):
    ga = _gelu(a)
    u, v0 = ga[:, 0:W_A], ga[:, W_A:2 * W_A]
    r = lax.rsqrt(_dot_sel(v0 * v0, _group_mat(W_A)) * (1.0 / HEAD_DIM) + EPS)
    vn = v0 * r * gn
    sv = bias
    for h in range(W_A // HEAD_DIM):
        sv = sv + _dot(wm_ref[h], jnp.where(_lane_group_mask(W_A, h), vn, 0.0))
    return u, v0, r, vn, sv


def _sgu_fwd(a, gn, wm, bias, name):
    s = a.shape[0]

    def body(a_ref, gn_ref, wm_ref, b_ref, y_ref):
        u, _, _, _, sv = _sgu_core(a_ref[...], gn_ref[...], wm_ref, b_ref[...])
        y_ref[...] = u * sv

    return _pcall(
        body, name=name, grid=(s // CHUNK,),
        in_specs=[pl.BlockSpec((CHUNK, 2 * W_A), lambda i: (i, 0)), pl.BlockSpec((1, W_A), lambda i: (0, 0)),
                  pl.BlockSpec((4, CHUNK, CHUNK), lambda i: (0, 0, 0)), pl.BlockSpec((CHUNK, W_A), lambda i: (0, 0))],
        out_specs=pl.BlockSpec((CHUNK, W_A), lambda i: (i, 0)),
        out_shape=jax.ShapeDtypeStruct((s, W_A), F32),
        compiler_params=_params(("parallel",)),
    )(a, gn, wm, bias)


def _sgu_bwd(a, dy, gn, wm, bias, name):
    s = a.shape[0]

    def body(a_ref, dy_ref, gn_ref, wm_ref, b_ref, da_ref, dwm_ref, db_ref, dgn_ref):
        @pl.when(pl.program_id(0) == 0)
        def _():
            dwm_ref[...] = jnp.zeros_like(dwm_ref)
            db_ref[...] = jnp.zeros_like(db_ref)
            dgn_ref[...] = jnp.zeros_like(dgn_ref)

        av, gnv, dyv = a_ref[...], gn_ref[...], dy_ref[...]
        u, v0, r, vn, sv = _sgu_core(av, gnv, wm_ref, b_ref[...])
        du = dyv * sv
        ds = dyv * u
        db_ref[...] += ds
        tril = _iota((CHUNK, CHUNK), 1) <= _iota((CHUNK, CHUNK), 0)
        dvn = jnp.zeros_like(vn)
        for h in range(W_A // HEAD_DIM):
            dsm = jnp.where(_lane_group_mask(W_A, h), ds, 0.0)
            dwm_ref[h] += jnp.where(tril, _dot_nt(dsm, vn), 0.0)
            dvn = dvn + _dot_tn(wm_ref[h], dsm)
        dgn_ref[...] += jnp.sum(dvn * v0 * r, axis=0, keepdims=True)
        dvg = dvn * gnv
        m2 = _dot_sel(dvg * v0, _group_mat(W_A)) * (1.0 / HEAD_DIM)
        dv0 = r * dvg - v0 * (r * r * r) * m2
        gp = _gelu_grad(av)
        da_ref[:, 0:W_A] = du * gp[:, 0:W_A]
        da_ref[:, W_A:2 * W_A] = dv0 * gp[:, W_A:2 * W_A]

    return _pcall(
        body, name=name, grid=(s // CHUNK,),
        in_specs=[pl.BlockSpec((CHUNK, 2 * W_A), lambda i: (i, 0)), pl.BlockSpec((CHUNK, W_A), lambda i: (i, 0)),
                  pl.BlockSpec((1, W_A), lambda i: (0, 0)), pl.BlockSpec((4, CHUNK, CHUNK), lambda i: (0, 0, 0)),
                  pl.BlockSpec((CHUNK, W_A), lambda i: (0, 0))],
        out_specs=[pl.BlockSpec((CHUNK, 2 * W_A), lambda i: (i, 0)), pl.BlockSpec((4, CHUNK, CHUNK), lambda i: (0, 0, 0)),
                   pl.BlockSpec((CHUNK, W_A), lambda i: (0, 0)), pl.BlockSpec((1, W_A), lambda i: (0, 0))],
        out_shape=[jax.ShapeDtypeStruct((s, 2 * W_A), F32), jax.ShapeDtypeStruct((4, CHUNK, CHUNK), F32),
                   jax.ShapeDtypeStruct((CHUNK, W_A), F32), jax.ShapeDtypeStruct((1, W_A), F32)],
        compiler_params=_params(("arbitrary",)),
    )(a, dy, gn, wm, bias)


KBLK = 128
ROW_CHUNK = 64
MASKED_SCORE = -1e30


def _attn_fwd(qkv, name):
    s = qkv.shape[0]
    tq = _tile(s, 256, KBLK)
    npairs = W_B // LANES
    assert s // KBLK <= LANES

    def body(q_ref, k_ref, v_ref, o_ref, cm_ref, z_ref, zw_ref, sums_ref, carry_ref, hl_ref, a_ref):
        i = pl.program_id(1)
        q = q_ref[...]
        lane = _iota((1, LANES), 1)
        lane_lo = lane < HEAD_DIM
        hmask = (lane_lo, jnp.logical_not(lane_lo))
        tri2 = ((_iota((KBLK, 2 * KBLK), 0) >= _iota((KBLK, 2 * KBLK), 1))
                | (_iota((KBLK, 2 * KBLK), 1) >= KBLK)).astype(BF16)
        tri2s = jnp.concatenate([tri2, tri2], axis=0)
        dmat = _iota((tq, KBLK), 1) - (_iota((tq, KBLK), 0) + i * tq)
        chunks = [slice(r, r + ROW_CHUNK) for r in range(0, tq, ROW_CHUNK)]
        heads = [slice(hh * LANES, (hh + 1) * LANES) for hh in range(2)]
        nk = (i + 1) * (tq // KBLK)

        cm_ref[...] = jnp.zeros_like(cm_ref)

        def before(b):
            return jnp.where((b >= 0) & (b < nk), -b * KBLK, jnp.iinfo(jnp.int32).min)

        def per_head(block):
            return jnp.concatenate([jnp.where(m, block, jnp.zeros_like(block)) for m in hmask], axis=0)

        def scores(b, p):
            ks = pl.multiple_of(jnp.maximum(b, 0) * KBLK, KBLK)
            z_ref[p] = _dot_nt(q, per_head(k_ref[pl.ds(ks, KBLK), :]))

        def logs(b, p):
            t = before(b)
            for hh in range(2):
                for rows in chunks:
                    z = jnp.where(dmat[rows] < t, z_ref[p, rows, heads[hh]], MASKED_SCORE)
                    zw_ref[p, rows, heads[hh]] = z
                    l = -(jnp.maximum(z, 0.0) + jnp.log(1.0 + jnp.exp(-jnp.abs(z))))
                    hi, lo = _split(l)
                    hl_ref[p, hh, rows, 0:KBLK] = hi
                    hl_ref[p, hh, rows, KBLK:2 * KBLK] = lo

        def sums(b, p):
            for hh in range(2):
                sums_ref[p, hh] = jnp.dot(hl_ref[p, hh], tri2s, preferred_element_type=F32)

        def weights(b, p):
            onehot = jnp.where(lane == b, 1.0, 0.0)
            for hh in range(2):
                for rows in chunks:
                    c = carry_ref[hh, rows, :]
                    arg = zw_ref[p, rows, heads[hh]] + c + sums_ref[p, hh, rows, 0:KBLK]
                    a_ref[p, rows, heads[hh]] = jnp.exp(arg).astype(_MXU)
                    cm_ref[rows, heads[hh]] += onehot * c
                    carry_ref[hh, rows, :] = c + sums_ref[p, hh, rows, KBLK:2 * KBLK]

        def out(b, p, acc):
            ks = pl.multiple_of(jnp.minimum(b, nk - 1) * KBLK, KBLK)
            return acc + jnp.dot(a_ref[p], per_head(v_ref[pl.ds(ks, KBLK), :]), preferred_element_type=F32)

        def step(it, acc):
            for u in range(2):
                b = nk + 3 - (2 * it + u)
                p = 1 - u
                acc = out(b, p, acc)
                weights(b - 1, 1 - p)
                sums(b - 2, p)
                logs(b - 3, 1 - p)
                scores(b - 4, p)
            return acc

        z_ref[...] = jnp.zeros_like(z_ref)
        zw_ref[...] = jnp.full_like(zw_ref, MASKED_SCORE)
        for ref in (sums_ref, carry_ref, hl_ref, a_ref):
            ref[...] = jnp.zeros_like(ref)
        o_ref[...] = lax.fori_loop(0, (nk + 4) // 2, step, q.astype(F32) * 0.0)

    return _pcall(
        body, name=name, grid=(npairs, s // tq),
        in_specs=[pl.BlockSpec((tq, LANES), lambda p, i: (i, p)),
                  pl.BlockSpec((s, LANES), lambda p, i: (0, npairs + p)),
                  pl.BlockSpec((s, LANES), lambda p, i: (0, 2 * npairs + p))],
        out_specs=[pl.BlockSpec((tq, LANES), lambda p, i: (i, p)), pl.BlockSpec((tq, 2 * LANES), lambda p, i: (i, p))],
        out_shape=[jax.ShapeDtypeStruct((s, W_B), F32), jax.ShapeDtypeStruct((s, 2 * W_B), F32)],
        scratch_shapes=[pltpu.VMEM((2, tq, 2 * KBLK), F32), pltpu.VMEM((2, tq, 2 * KBLK), F32),
                        pltpu.VMEM((2, 2, tq, 2 * KBLK), F32), pltpu.VMEM((2, tq, KBLK), F32),
                        pltpu.VMEM((2, 2, tq, 2 * KBLK), BF16), pltpu.VMEM((2, tq, 2 * KBLK), _MXU)],
        compiler_params=_params(("parallel", "parallel")),
    )(qkv, qkv, qkv)


def _attn_bwd(qkv, cm, do, name):
    s = qkv.shape[0]
    tq = _tile(s, 256, KBLK)
    npairs = W_B // LANES

    def body(q_ref, k_ref, v_ref, cm_ref, do_ref, dq_ref, dk_ref, dv_ref,
             z_ref, zw_ref, da_ref, g_ref, sig_ref, cum_ref, gp_ref, gcarry_ref, hl_ref, ghl_ref, a_ref, dz_ref,
             dkt_ref, dvt_ref):
        i = pl.program_id(1)

        @pl.when(i == 0)
        def _():
            dkt_ref[...] = jnp.zeros_like(dkt_ref)
            dvt_ref[...] = jnp.zeros_like(dvt_ref)

        q = q_ref[...]
        dov = do_ref[...].astype(_MXU)
        lane = _iota((1, LANES), 1)
        lane_lo = lane < HEAD_DIM
        hmask = (lane_lo, jnp.logical_not(lane_lo))
        tri = (_iota((KBLK, KBLK), 0) >= _iota((KBLK, KBLK), 1)).astype(BF16)
        tris = jnp.concatenate([tri, tri], axis=0)
        prefix2 = ((_iota((KBLK, 2 * KBLK), 0) <= _iota((KBLK, 2 * KBLK), 1))
                   | (_iota((KBLK, 2 * KBLK), 1) >= KBLK)).astype(BF16)
        prefix2s = jnp.concatenate([prefix2, prefix2], axis=0)
        dmat = _iota((tq, KBLK), 1) - (_iota((tq, KBLK), 0) + i * tq)
        chunks = [slice(r, r + ROW_CHUNK) for r in range(0, tq, ROW_CHUNK)]
        heads = [slice(hh * LANES, (hh + 1) * LANES) for hh in range(2)]
        nk = (i + 1) * (tq // KBLK)

        def before(b):
            return jnp.where((b >= 0) & (b < nk), -b * KBLK, jnp.iinfo(jnp.int32).min)

        def block_rows(b):
            return pl.ds(pl.multiple_of(jnp.clip(b, 0, nk - 1) * KBLK, KBLK), KBLK)

        def per_head(block):
            return jnp.concatenate([jnp.where(m, block, jnp.zeros_like(block)) for m in hmask], axis=0)

        q_t = q.astype(F32).T.astype(_MXU)
        do_t = do_ref[...].T.astype(_MXU)
        feature_lo = _iota((LANES, KBLK), 0) < HEAD_DIM

        def own_features(side_by_side):
            return jnp.where(feature_lo, side_by_side[:, 0:KBLK], side_by_side[:, KBLK:2 * KBLK])

        def m1(b, p):
            z_ref[p] = _dot_nt(q, per_head(k_ref[block_rows(b), :]))

        def v1(b, p):
            t = before(b)
            for hh in range(2):
                for rows in chunks:
                    z = jnp.where(dmat[rows] < t, z_ref[p, rows, heads[hh]], MASKED_SCORE)
                    zw_ref[p, rows, heads[hh]] = z
                    l = -(jnp.maximum(z, 0.0) + jnp.log(1.0 + jnp.exp(-jnp.abs(z))))
                    hi, lo = _split(l)
                    hl_ref[p, hh, rows, 0:KBLK] = hi
                    hl_ref[p, hh, rows, KBLK:2 * KBLK] = lo

        def m2(b, p):
            for hh in range(2):
                cum_ref[p, hh] = jnp.dot(hl_ref[p, hh], tris, preferred_element_type=F32)
            da_ref[p] = _dot_nt(dov, per_head(v_ref[block_rows(b), :]))

        def v2(b, p):
            pick = lane == b
            for hh in range(2):
                for rows in chunks:
                    c = jnp.sum(jnp.where(pick, cm_ref[rows, heads[hh]], 0.0), axis=1, keepdims=True)
                    z = zw_ref[p, rows, heads[hh]]
                    a = jnp.exp(z + c + cum_ref[p, hh, rows, :])
                    g = a * da_ref[p, rows, heads[hh]]
                    a_ref[p, rows, heads[hh]] = a.astype(_MXU)
                    g_ref[p, rows, heads[hh]] = g
                    hi, lo = _split(g)
                    ghl_ref[p, hh, rows, 0:KBLK] = hi
                    ghl_ref[p, hh, rows, KBLK:2 * KBLK] = lo
                    sig_ref[p, rows, heads[hh]] = jax.nn.sigmoid(z)

        def m3(b, p):
            for hh in range(2):
                gp_ref[p, hh] = jnp.dot(ghl_ref[p, hh], prefix2s, preferred_element_type=F32)
            dvt_ref[jnp.clip(b, 0, nk - 1)] += own_features(jnp.dot(do_t, a_ref[p], preferred_element_type=F32))

        def v3(b, p):
            for hh in range(2):
                for rows in chunks:
                    gc = gcarry_ref[hh, rows, :]
                    upto = gc + gp_ref[p, hh, rows, 0:KBLK]
                    dz = g_ref[p, rows, heads[hh]] - sig_ref[p, rows, heads[hh]] * upto
                    dz_ref[p, rows, heads[hh]] = dz.astype(_MXU)
                    gcarry_ref[hh, rows, :] = gc + gp_ref[p, hh, rows, KBLK:2 * KBLK]

        def m4(b, p, dq):
            dq = dq + jnp.dot(dz_ref[p], per_head(k_ref[block_rows(b), :]), preferred_element_type=F32)
            dkt_ref[jnp.clip(b, 0, nk - 1)] += own_features(jnp.dot(q_t, dz_ref[p], preferred_element_type=F32))
            return dq

        def step(it, dq):
            for u in range(2):
                j = 2 * it + u
                dq = m4(j - 6, u, dq)
                m3(j - 4, u)
                m2(j - 2, u)
                m1(j, u)
                v3(j - 5, 1 - u)
                v2(j - 3, 1 - u)
                v1(j - 1, 1 - u)
            return dq

        zw_ref[...] = jnp.full_like(zw_ref, MASKED_SCORE)
        for ref in (z_ref, da_ref, g_ref, sig_ref, cum_ref, gp_ref, gcarry_ref, hl_ref, ghl_ref, a_ref, dz_ref):
            ref[...] = jnp.zeros_like(ref)
        dq_ref[...] = lax.fori_loop(0, (nk + 6) // 2, step, do_ref[...] * 0.0) * 0.125

        @pl.when(i == s // tq - 1)
        def _():
            def untranspose(blk, carry):
                rows = pl.ds(pl.multiple_of(blk * KBLK, KBLK), KBLK)
                dk_ref[rows, :] = dkt_ref[blk].T
                dv_ref[rows, :] = dvt_ref[blk].T
                return carry
            lax.fori_loop(0, nblk, untranspose, 0)

    qspec = pl.BlockSpec((tq, LANES), lambda p, i: (i, p))
    nblk = s // KBLK
    full = pl.BlockSpec((s, LANES), lambda p, i: (0, p))
    return _pcall(
        body, name=name, grid=(npairs, s // tq),
        in_specs=[qspec, pl.BlockSpec((s, LANES), lambda p, i: (0, npairs + p)),
                  pl.BlockSpec((s, LANES), lambda p, i: (0, 2 * npairs + p)),
                  pl.BlockSpec((tq, 2 * LANES), lambda p, i: (i, p)), qspec],
        out_specs=[qspec, full, full],
        out_shape=[jax.ShapeDtypeStruct((s, W_B), F32)] * 3,
        scratch_shapes=[pltpu.VMEM((2, tq, 2 * KBLK), F32)] * 5 + [pltpu.VMEM((2, 2, tq, KBLK), F32),
                        pltpu.VMEM((2, 2, tq, 2 * KBLK), F32), pltpu.VMEM((2, tq, KBLK), F32)]
                       + [pltpu.VMEM((2, 2, tq, 2 * KBLK), BF16)] * 2 + [pltpu.VMEM((2, tq, 2 * KBLK), _MXU)] * 2
                       + [pltpu.VMEM((nblk, LANES, KBLK), F32)] * 2,
        compiler_params=_params(("parallel", "arbitrary")),
    )(qkv, qkv, qkv, cm, do)


POOL_HALO = 128


def _pool_window_lane():
    lane = _iota((1, W_C), 1)
    w = jnp.where(lane < 64, POOL_WINDOWS[0], jnp.where(lane < 128, POOL_WINDOWS[1],
                  jnp.where(lane < 192, POOL_WINDOWS[2], POOL_WINDOWS[3])))
    return w.astype(F32)


def _pool_count(tm, i):
    pos = (_iota((tm, W_C), 0) + (i * tm + 1)).astype(F32)
    return jnp.minimum(pos, _pool_window_lane())


def _pool_centered(prev, cur, cnt):
    tm = cur.shape[0]
    xx = jnp.concatenate([prev, cur], axis=0)
    hi, lo = _split(xx)
    t = _iota((tm, tm + POOL_HALO), 0)
    cc = _iota((tm, tm + POOL_HALO), 1) - POOL_HALO
    wsum = jnp.zeros_like(cur)
    for g, w in enumerate(POOL_WINDOWS):
        band = ((cc <= t) & (cc > t - w)).astype(BF16)
        mg = _lane_group_mask(W_C, g)
        wsum += jnp.dot(band, jnp.where(mg, hi, jnp.zeros_like(hi)), preferred_element_type=F32)
        wsum += jnp.dot(band, jnp.where(mg, lo, jnp.zeros_like(lo)), preferred_element_type=F32)
    return wsum / cnt - cur


def _pool_fwd(p, wbd, sc, name):
    s = p.shape[0]
    tm = _tile(s, 256, POOL_HALO)
    r = tm // POOL_HALO

    def body(pp_ref, p_ref, w_ref, sc_ref, y_ref):
        i = pl.program_id(0)
        prev = jnp.where(i > 0, pp_ref[...], 0.0)
        d = _pool_centered(prev, p_ref[...], _pool_count(tm, i))
        y_ref[...] = _dot(d, w_ref[...]) * sc_ref[...]

    return _pcall(
        body, name=name, grid=(s // tm,),
        in_specs=[pl.BlockSpec((POOL_HALO, W_C), lambda i: (jnp.maximum(i * r - 1, 0), 0)),
                  pl.BlockSpec((tm, W_C), lambda i: (i, 0)), pl.BlockSpec((W_C, W_C), lambda i: (0, 0)),
                  pl.BlockSpec((1, W_C), lambda i: (0, 0))],
        out_specs=pl.BlockSpec((tm, W_C), lambda i: (i, 0)),
        out_shape=jax.ShapeDtypeStruct((s, W_C), F32),
        compiler_params=_params(("parallel",)),
    )(p, p, wbd, sc)


def _pool_bwd(p, dy, wbd, wbdt, sc, name):
    s = p.shape[0]
    tm = _tile(s, 256, POOL_HALO)
    r = tm // POOL_HALO
    nt = s // tm

    def body(pp_ref, p_ref, dy_ref, dyn_ref, w_ref, wt_ref, sc_ref, dp_ref, dw_ref, dsc_ref):
        i = pl.program_id(0)

        @pl.when(i == 0)
        def _():
            dw_ref[...] = jnp.zeros_like(dw_ref)
            dsc_ref[...] = jnp.zeros_like(dsc_ref)

        scv = sc_ref[...]
        cnt = _pool_count(tm, i)
        prev = jnp.where(i > 0, pp_ref[...], 0.0)
        d = _pool_centered(prev, p_ref[...], cnt)
        e = _dot(d, w_ref[...])
        dyv = dy_ref[...]
        de = dyv * scv
        dsc_ref[...] += jnp.sum(dyv * e, axis=0, keepdims=True)
        dw_ref[...] += _dot_tn(d, de)
        dd = _dot(de, wt_ref[...])
        ddn = jnp.where(i < nt - 1, _dot(dyn_ref[...] * scv, wt_ref[...]), 0.0)
        yy = jnp.concatenate([dd / cnt, ddn / _pool_window_lane()], axis=0)
        hi, lo = _split(yy)
        t = _iota((tm, tm + POOL_HALO), 0)
        cc = _iota((tm, tm + POOL_HALO), 1)
        acc = jnp.zeros_like(dd)
        for g, w in enumerate(POOL_WINDOWS):
            band = ((cc >= t) & (cc < t + w)).astype(BF16)
            mg = _lane_group_mask(W_C, g)
            acc += jnp.dot(band, jnp.where(mg, hi, jnp.zeros_like(hi)), preferred_element_type=F32)
            acc += jnp.dot(band, jnp.where(mg, lo, jnp.zeros_like(lo)), preferred_element_type=F32)
        dp_ref[...] = acc - dd

    tile = pl.BlockSpec((tm, W_C), lambda i: (i, 0))
    mat = pl.BlockSpec((W_C, W_C), lambda i: (0, 0))
    vec = pl.BlockSpec((1, W_C), lambda i: (0, 0))
    return _pcall(
        body, name=name, grid=(nt,),
        in_specs=[pl.BlockSpec((POOL_HALO, W_C), lambda i: (jnp.maximum(i * r - 1, 0), 0)), tile, tile,
                  pl.BlockSpec((POOL_HALO, W_C), lambda i: (jnp.minimum((i + 1) * r, s // POOL_HALO - 1), 0)),
                  mat, mat, vec],
        out_specs=[tile, mat, vec],
        out_shape=[jax.ShapeDtypeStruct((s, W_C), F32), jax.ShapeDtypeStruct((W_C, W_C), F32),
                   jax.ShapeDtypeStruct((1, W_C), F32)],
        compiler_params=_params(("arbitrary",)),
    )(p, p, dy, dy, wbd, wbdt, sc)


def _mix_cols(ya_ref, yb_ref, yc_ref, cb):
    if cb < 2:
        return ya_ref[:, cb * LANES:(cb + 1) * LANES]
    if cb < 6:
        return yb_ref[:, (cb - 2) * LANES:(cb - 1) * LANES]
    return yc_ref[:, (cb - 6) * LANES:(cb - 5) * LANES]


def _mix_fwd(ya, yb, yc, g, wo, x, name):
    s = x.shape[0]
    tm = _tile(s, 256)

    def body(ya_ref, yb_ref, yc_ref, g_ref, w_ref, x_ref, o_ref, yn_ref):
        sel = _group_mat(LANES)
        for cb in range(D_MODEL // LANES):
            y = _mix_cols(ya_ref, yb_ref, yc_ref, cb)
            r = lax.rsqrt(_dot_sel(y * y, sel) * (1.0 / HEAD_DIM) + EPS)
            yn_ref[:, cb * LANES:(cb + 1) * LANES] = (y * r * g_ref[:, cb * LANES:(cb + 1) * LANES]).astype(_MXU)
        o_ref[...] = x_ref[...] + jnp.dot(yn_ref[...], w_ref[...], preferred_element_type=F32)

    row = lambda n: pl.BlockSpec((tm, n), lambda i: (i, 0))
    return _pcall(
        body, name=name, grid=(s // tm,),
        in_specs=[row(W_A), row(W_B), row(W_C), pl.BlockSpec((1, D_MODEL), lambda i: (0, 0)),
                  pl.BlockSpec((D_MODEL, D_MODEL), lambda i: (0, 0)), row(D_MODEL)],
        out_specs=[row(D_MODEL), row(D_MODEL)],
        out_shape=[jax.ShapeDtypeStruct((s, D_MODEL), F32), jax.ShapeDtypeStruct((s, D_MODEL), _MXU)],
        compiler_params=_params(("parallel",)),
    )(ya, yb, yc, g, wo, x)


def _mix_bwd(dx, ya, yb, yc, g, wot, name):
    s = dx.shape[0]
    tm = _tile(s, 256)

    def body(dx_ref, ya_ref, yb_ref, yc_ref, g_ref, wt_ref, dya_ref, dyb_ref, dyc_ref, dg_ref):
        @pl.when(pl.program_id(0) == 0)
        def _():
            dg_ref[...] = jnp.zeros_like(dg_ref)

        dyn = _dot(dx_ref[...], wt_ref[...])
        sel = _group_mat(LANES)
        for cb in range(D_MODEL // LANES):
            cols = slice(cb * LANES, (cb + 1) * LANES)
            y = _mix_cols(ya_ref, yb_ref, yc_ref, cb)
            r = lax.rsqrt(_dot_sel(y * y, sel) * (1.0 / HEAD_DIM) + EPS)
            dyc_ = dyn[:, cols]
            dyg = dyc_ * g_ref[:, cols]
            m2 = _dot_sel(dyg * y, sel) * (1.0 / HEAD_DIM)
            dy = r * dyg - y * (r * r * r) * m2
            dg_ref[:, cols] += jnp.sum(dyc_ * y * r, axis=0, keepdims=True)
            if cb < 2:
                dya_ref[:, cb * LANES:(cb + 1) * LANES] = dy
            elif cb < 6:
                dyb_ref[:, (cb - 2) * LANES:(cb - 1) * LANES] = dy
            else:
                dyc_ref[:, (cb - 6) * LANES:(cb - 5) * LANES] = dy

    row = lambda n: pl.BlockSpec((tm, n), lambda i: (i, 0))
    vec = pl.BlockSpec((1, D_MODEL), lambda i: (0, 0))
    return _pcall(
        body, name=name, grid=(s // tm,),
        in_specs=[row(D_MODEL), row(W_A), row(W_B), row(W_C), vec, pl.BlockSpec((D_MODEL, D_MODEL), lambda i: (0, 0))],
        out_specs=[row(W_A), row(W_B), row(W_C), vec],
        out_shape=[jax.ShapeDtypeStruct((s, W_A), F32), jax.ShapeDtypeStruct((s, W_B), F32),
                   jax.ShapeDtypeStruct((s, W_C), F32), jax.ShapeDtypeStruct((1, D_MODEL), F32)],
        compiler_params=_params(("arbitrary",)),
    )(dx, ya, yb, yc, g, wot)


FFN_CHUNK = 256
N_CHUNKS = D_FF // FFN_CHUNK
CW_ROWS = 8


def _ffn_up_fwd(x, g, w, name):
    s = x.shape[0]
    n = w.shape[1]
    tm, tn = _tile(s, 512), _tile(n, 512, LANES)

    def body(x_ref, g_ref, w_ref, z_ref, h_ref):
        @pl.when(pl.program_id(1) == 0)
        def _():
            xv = x_ref[...]
            h_ref[...] = (xv * _rms(xv) * g_ref[...]).astype(_MXU)
        z_ref[...] = jnp.dot(h_ref[...], w_ref[...], preferred_element_type=F32)

    return _pcall(
        body, name=name, grid=(s // tm, n // tn),
        in_specs=[pl.BlockSpec((tm, D_MODEL), lambda i, j: (i, 0)), pl.BlockSpec((1, D_MODEL), lambda i, j: (0, 0)),
                  pl.BlockSpec((D_MODEL, tn), lambda i, j: (0, j))],
        out_specs=[pl.BlockSpec((tm, tn), lambda i, j: (i, j)), pl.BlockSpec((tm, D_MODEL), lambda i, j: (i, 0))],
        out_shape=[jax.ShapeDtypeStruct((s, n), F32), jax.ShapeDtypeStruct((s, D_MODEL), _MXU)],
        compiler_params=_params(("parallel", "arbitrary")),
    )(x, g, w)


def _conv(cur, prev8, cw_ref):
    s1 = _shift_down(cur, prev8, 1)
    s2 = _shift_down(cur, prev8, 2)
    zc = cw_ref[3:4, :] + s2 * cw_ref[0:1, :]
    zc = zc + s1 * cw_ref[1:2, :]
    zc = zc + cur * cw_ref[2:3, :]
    return zc, s1, s2


def _halo_specs(tm, s):
    r = tm // SUBLANES
    prev = lambda off: pl.BlockSpec((SUBLANES, FFN_CHUNK), lambda i, j: (jnp.maximum(i * r - 1, 0), j + off))
    nxt = lambda off: pl.BlockSpec((SUBLANES, FFN_CHUNK), lambda i, j: (jnp.minimum((i + 1) * r, s // SUBLANES - 1), j + off))
    return prev, nxt


def _ffn_down_fwd(z, cw, wd, x, name):
    s = x.shape[0]
    tm = _tile(s, 512)
    prev, _ = _halo_specs(tm, s)

    def body(zg_ref, zu_ref, pg_ref, pu_ref, cg_ref, cu_ref, w_ref, x_ref, o_ref, act_ref, acc_ref):
        i, j = pl.program_id(0), pl.program_id(1)
        first = i > 0
        zg, _, _ = _conv(zg_ref[...], jnp.where(first, pg_ref[...], 0.0), cg_ref)
        zu, _, _ = _conv(zu_ref[...], jnp.where(first, pu_ref[...], 0.0), cu_ref)
        act = (zg * jax.nn.sigmoid(zg) * zu).astype(_MXU)
        act_ref[...] = act

        @pl.when(j == 0)
        def _():
            acc_ref[...] = x_ref[...]
        acc_ref[...] += jnp.dot(act, w_ref[...], preferred_element_type=F32)

        @pl.when(j == N_CHUNKS - 1)
        def _():
            o_ref[...] = acc_ref[...]

    zt = lambda off: pl.BlockSpec((tm, FFN_CHUNK), lambda i, j: (i, j + off))
    cwt = lambda off: pl.BlockSpec((CW_ROWS, FFN_CHUNK), lambda i, j: (0, j + off))
    return _pcall(
        body, name=name, grid=(s // tm, N_CHUNKS),
        in_specs=[zt(0), zt(N_CHUNKS), prev(0), prev(N_CHUNKS), cwt(0), cwt(N_CHUNKS),
                  pl.BlockSpec((FFN_CHUNK, D_MODEL), lambda i, j: (j, 0)), pl.BlockSpec((tm, D_MODEL), lambda i, j: (i, 0))],
        out_specs=[pl.BlockSpec((tm, D_MODEL), lambda i, j: (i, 0)), pl.BlockSpec((tm, FFN_CHUNK), lambda i, j: (i, j))],
        out_shape=[jax.ShapeDtypeStruct((s, D_MODEL), F32), jax.ShapeDtypeStruct((s, D_FF), _MXU)],
        scratch_shapes=[pltpu.VMEM((tm, D_MODEL), F32)],
        compiler_params=_params(("parallel", "arbitrary")),
    )(z, z, z, z, cw, cw, wd, x)


def _ffn_down_bwd(dx, z, cw, wdt, name):
    s = dx.shape[0]
    tm = _tile(s, 512)

    def body(dx_ref, zg_ref, zu_ref, pg_ref, pu_ref, cg_ref, cu_ref, wt_ref, dg_ref, du_ref, dcg_ref, dcu_ref):
        i = pl.program_id(1)
        first = i > 0

        @pl.when(i == 0)
        def _():
            dcg_ref[...] = jnp.zeros_like(dcg_ref)
            dcu_ref[...] = jnp.zeros_like(dcu_ref)

        dact = _dot(dx_ref[...], wt_ref[...])
        zg, g1, g2 = _conv(zg_ref[...], jnp.where(first, pg_ref[...], 0.0), cg_ref)
        zu, u1, u2 = _conv(zu_ref[...], jnp.where(first, pu_ref[...], 0.0), cu_ref)
        sg = jax.nn.sigmoid(zg)
        silu = zg * sg
        dzu = dact * silu
        dzg = dact * zu * (sg * (1.0 + zg * (1.0 - sg)))
        dg_ref[...] = dzg
        du_ref[...] = dzu
        for ref, dzc, cur, s1, s2 in ((dcg_ref, dzg, zg_ref[...], g1, g2), (dcu_ref, dzu, zu_ref[...], u1, u2)):
            ref[0:1, :] += jnp.sum(dzc * s2, axis=0, keepdims=True)
            ref[1:2, :] += jnp.sum(dzc * s1, axis=0, keepdims=True)
            ref[2:3, :] += jnp.sum(dzc * cur, axis=0, keepdims=True)
            ref[3:4, :] += jnp.sum(dzc, axis=0, keepdims=True)

    zt = lambda off: pl.BlockSpec((tm, FFN_CHUNK), lambda j, i: (i, j + off))
    r = tm // SUBLANES
    pv = lambda off: pl.BlockSpec((SUBLANES, FFN_CHUNK), lambda j, i: (jnp.maximum(i * r - 1, 0), j + off))
    cwt = lambda off: pl.BlockSpec((CW_ROWS, FFN_CHUNK), lambda j, i: (0, j + off))
    out_t = pl.BlockSpec((tm, FFN_CHUNK), lambda j, i: (i, j))
    dc_t = pl.BlockSpec((CW_ROWS, FFN_CHUNK), lambda j, i: (0, j))
    dzg, dzu, dcg, dcu = _pcall(
        body, name=name, grid=(N_CHUNKS, s // tm),
        in_specs=[pl.BlockSpec((tm, D_MODEL), lambda j, i: (i, 0)), zt(0), zt(N_CHUNKS), pv(0), pv(N_CHUNKS),
                  cwt(0), cwt(N_CHUNKS), pl.BlockSpec((D_MODEL, FFN_CHUNK), lambda j, i: (0, j))],
        out_specs=[out_t, out_t, dc_t, dc_t],
        out_shape=[jax.ShapeDtypeStruct((s, D_FF), F32), jax.ShapeDtypeStruct((s, D_FF), F32),
                   jax.ShapeDtypeStruct((CW_ROWS, D_FF), F32), jax.ShapeDtypeStruct((CW_ROWS, D_FF), F32)],
        compiler_params=_params(("parallel", "arbitrary")),
    )(dx, z, z, z, z, cw, cw, wdt)
    return dzg, dzu, jnp.concatenate([dcg, dcu], axis=1)


def _ffn_up_bwd(dzg, dzu, cw, wut, x, g, dres, name):
    s = x.shape[0]
    tm = _tile(s, 512)
    _, nxt = _halo_specs(tm, s)
    nt = s // tm

    def body(dg_ref, du_ref, ng_ref, nu_ref, cg_ref, cu_ref, wg_ref, wu_ref, x_ref, g_ref, dres_ref,
             dzg_ref, dzu_ref, dx_ref, dgn_ref, acc_ref):
        i, j = pl.program_id(0), pl.program_id(1)
        last = i < nt - 1

        def conv_bwd(cur, nxt8, cw_ref):
            up1 = _shift_up(cur, nxt8, 1)
            up2 = _shift_up(cur, nxt8, 2)
            return cur * cw_ref[2:3, :] + up1 * cw_ref[1:2, :] + up2 * cw_ref[0:1, :]

        dzg_ = conv_bwd(dg_ref[...], jnp.where(last, ng_ref[...], 0.0), cg_ref).astype(_MXU)
        dzu_ = conv_bwd(du_ref[...], jnp.where(last, nu_ref[...], 0.0), cu_ref).astype(_MXU)
        dzg_ref[...] = dzg_
        dzu_ref[...] = dzu_

        @pl.when(j == 0)
        def _():
            acc_ref[...] = jnp.zeros_like(acc_ref)
        acc_ref[...] += (jnp.dot(dzg_, wg_ref[...], preferred_element_type=F32)
                         + jnp.dot(dzu_, wu_ref[...], preferred_element_type=F32))

        @pl.when((i == 0) & (j == 0))
        def _():
            dgn_ref[...] = jnp.zeros_like(dgn_ref)

        @pl.when(j == N_CHUNKS - 1)
        def _():
            dx, dgn = _rms_bwd(x_ref[...], g_ref[...], acc_ref[...])
            dx_ref[...] = dres_ref[...] + dx
            dgn_ref[...] += dgn

    zt = pl.BlockSpec((tm, FFN_CHUNK), lambda i, j: (i, j))
    cwt = lambda off: pl.BlockSpec((CW_ROWS, FFN_CHUNK), lambda i, j: (0, j + off))
    wt = lambda off: pl.BlockSpec((FFN_CHUNK, D_MODEL), lambda i, j: (j + off, 0))
    row = pl.BlockSpec((tm, D_MODEL), lambda i, j: (i, 0))
    vec = pl.BlockSpec((1, D_MODEL), lambda i, j: (0, 0))
    return _pcall(
        body, name=name, grid=(nt, N_CHUNKS),
        in_specs=[zt, zt, nxt(0), nxt(0), cwt(0), cwt(N_CHUNKS), wt(0), wt(N_CHUNKS), row, vec, row],
        out_specs=[zt, zt, row, vec],
        out_shape=[jax.ShapeDtypeStruct((s, D_FF), _MXU), jax.ShapeDtypeStruct((s, D_FF), _MXU),
                   jax.ShapeDtypeStruct((s, D_MODEL), F32), jax.ShapeDtypeStruct((1, D_MODEL), F32)],
        scratch_shapes=[pltpu.VMEM((tm, D_MODEL), F32)],
        compiler_params=_params(("arbitrary", "arbitrary")),
    )(dzg, dzu, dzg, dzu, cw, cw, wut, wut, x, g, dres)


def _final_loss(x, g, tgt, name):
    s = x.shape[0]
    tm = _tile(s, 256)

    def body(x_ref, g_ref, t_ref, loss_ref, dx_ref, dg_ref):
        @pl.when(pl.program_id(0) == 0)
        def _():
            loss_ref[...] = jnp.zeros_like(loss_ref)
            dg_ref[...] = jnp.zeros_like(dg_ref)

        xv, gv = x_ref[...], g_ref[...]
        err = xv * _rms(xv) * gv - t_ref[...]
        per_tok = jnp.mean(err * err, axis=-1, keepdims=True)
        loss_ref[...] += 0.5 * jnp.sum(per_tok, axis=0, keepdims=True)
        dx, dg = _rms_bwd(xv, gv, err * (1.0 / D_MODEL))
        dx_ref[...] = dx
        dg_ref[...] += dg

    row = pl.BlockSpec((tm, D_MODEL), lambda i: (i, 0))
    vec = pl.BlockSpec((1, D_MODEL), lambda i: (0, 0))
    return _pcall(
        body, name=name, grid=(s // tm,),
        in_specs=[row, vec, row], out_specs=[pl.BlockSpec((1, 1), lambda i: (0, 0)), row, vec],
        out_shape=[jax.ShapeDtypeStruct((1, 1), F32), jax.ShapeDtypeStruct((s, D_MODEL), F32),
                   jax.ShapeDtypeStruct((1, D_MODEL), F32)],
        compiler_params=_params(("arbitrary",)),
    )(x, g, tgt)


def _adamw(parts, w, m, v, name):
    r, c = w.shape
    tr = _tile(r, 256)
    c1 = 1.0 - ADAM_B1 ** ADAM_STEP
    c2 = 1.0 - ADAM_B2 ** ADAM_STEP

    def body(p_ref, w_ref, m_ref, v_ref, g_ref, d_ref, mo_ref, vo_ref):
        g = p_ref[0].astype(F32)
        for i in range(1, N_DEV):
            g = g + p_ref[i].astype(F32)
        mn = ADAM_B1 * m_ref[...] + (1.0 - ADAM_B1) * g
        vn = ADAM_B2 * v_ref[...] + (1.0 - ADAM_B2) * (g * g)
        g_ref[...] = g
        mo_ref[...] = mn
        vo_ref[...] = vn
        d_ref[...] = -ADAM_LR * ((mn / c1) / (jnp.sqrt(vn / c2) + ADAM_EPS) + ADAM_WD * w_ref[...])

    t2 = pl.BlockSpec((tr, c), lambda i: (i, 0))
    return _pcall(
        body, name=name, grid=(r // tr,),
        in_specs=[pl.BlockSpec((N_DEV, tr, c), lambda i: (0, i, 0)), t2, t2, t2],
        out_specs=[t2] * 4, out_shape=[jax.ShapeDtypeStruct((r, c), F32)] * 4,
        compiler_params=_params(("parallel",)),
    )(parts, w, m, v)


SMALL = ("norm1_g", "sgu_norm_g", "sgu_w", "sgu_b", "pool_w", "pool_scale", "mix_norm_g", "norm2_g", "conv_b", "final_g")
SHARDED = ("w_in", "w_o", "w_up", "conv_w", "w_down")
ORDER = ("norm1_g", "w_in", "sgu_norm_g", "sgu_w", "sgu_b", "pool_w", "pool_scale", "mix_norm_g", "w_o", "norm2_g",
         "w_up", "conv_w", "conv_b", "w_down", "final_g")


def _pack(tree):
    return jnp.concatenate([tree[n].reshape(-1) for n in SMALL]).reshape(-1, LANES)


def _unpack(flat, like):
    out, off = {}, 0
    flat = flat.reshape(-1)
    for n in SMALL:
        size = math.prod(like[n].shape)
        out[n] = flat[off:off + size].reshape(like[n].shape)
        off += size
    return out


def _block_diag(pw):
    z = jnp.zeros((W_C, W_C), pw.dtype)
    for g in range(4):
        z = z.at[g * 64:(g + 1) * 64, g * 64:(g + 1) * 64].set(pw[g])
    return z


def kernel(x, norm1_g, w_in, sgu_norm_g, sgu_w, sgu_b, pool_w, pool_scale, mix_norm_g, w_o, norm2_g, w_up, conv_w, conv_b, w_down, final_g, loss_target, m_norm1_g, m_w_in, m_sgu_norm_g, m_sgu_w, m_sgu_b, m_pool_w, m_pool_scale, m_mix_norm_g, m_w_o, m_norm2_g, m_w_up, m_conv_w, m_conv_b, m_w_down, m_final_g, v_norm1_g, v_w_in, v_sgu_norm_g, v_sgu_w, v_sgu_b, v_pool_w, v_pool_scale, v_mix_norm_g, v_w_o, v_norm2_g, v_w_up, v_conv_w, v_conv_b, v_w_down, v_final_g):
    weights = dict(norm1_g=norm1_g, w_in=w_in, sgu_norm_g=sgu_norm_g, sgu_w=sgu_w, sgu_b=sgu_b, pool_w=pool_w,
                   pool_scale=pool_scale, mix_norm_g=mix_norm_g, w_o=w_o, norm2_g=norm2_g, w_up=w_up, conv_w=conv_w,
                   conv_b=conv_b, w_down=w_down, final_g=final_g)
    mom = dict(norm1_g=m_norm1_g, w_in=m_w_in, sgu_norm_g=m_sgu_norm_g, sgu_w=m_sgu_w, sgu_b=m_sgu_b, pool_w=m_pool_w,
               pool_scale=m_pool_scale, mix_norm_g=m_mix_norm_g, w_o=m_w_o, norm2_g=m_norm2_g, w_up=m_w_up,
               conv_w=m_conv_w, conv_b=m_conv_b, w_down=m_w_down, final_g=m_final_g)
    var = dict(norm1_g=v_norm1_g, w_in=v_w_in, sgu_norm_g=v_sgu_norm_g, sgu_w=v_sgu_w, sgu_b=v_sgu_b, pool_w=v_pool_w,
               pool_scale=v_pool_scale, mix_norm_g=v_mix_norm_g, w_o=v_w_o, norm2_g=v_norm2_g, w_up=v_w_up,
               conv_w=v_conv_w, conv_b=v_conv_b, w_down=v_w_down, final_g=v_final_g)
    depth = w_in.shape[0]
    s = x.shape[1]
    xs = x.reshape(s, D_MODEL)
    tgt = loss_target.reshape(s, D_MODEL)

    g_in, g_o, g_up, g_cw, g_dn = _exchange(
        [w_in.astype(_MXU), w_o.astype(_MXU), w_up.astype(_MXU), conv_w, w_down.astype(_MXU)], "gather_weights", False)
    full_in = jnp.transpose(g_in, (1, 2, 0, 3)).reshape(depth, D_MODEL, IN_COLS)
    full_o = jnp.transpose(g_o, (1, 0, 2, 3)).reshape(depth, D_MODEL, D_MODEL)
    full_up = jnp.transpose(g_up, (1, 2, 0, 3)).reshape(depth, D_MODEL, 2 * D_FF)
    full_cw = jnp.transpose(g_cw, (1, 2, 0, 3)).reshape(depth, 3, 2 * D_FF)
    full_dn = jnp.transpose(g_dn, (1, 0, 2, 3)).reshape(depth, D_FF, D_MODEL)

    tril = jnp.tril(jnp.ones((CHUNK, CHUNK), bool))
    layers = []
    for l in range(depth):
        wbd = _block_diag(pool_w[l])
        layers.append(dict(
            g1=norm1_g[l][None], w_in=full_in[l], w_in_t=full_in[l].T,
            gn=sgu_norm_g[l][None], wm=jnp.where(tril[None], sgu_w[l], 0.0).astype(_MXU),
            bias=jnp.repeat(sgu_b[l].T, HEAD_DIM, axis=1),
            wbd=wbd.astype(_MXU), wbd_t=wbd.T.astype(_MXU), sc=pool_scale[l][None],
            gmix=mix_norm_g[l][None], w_o=full_o[l], w_o_t=full_o[l].T,
            g2=norm2_g[l][None], w_up=full_up[l], w_up_t=full_up[l].T,
            cw=jnp.concatenate([full_cw[l], conv_b[l][None], jnp.zeros((CW_ROWS - 4, 2 * D_FF), F32)], axis=0),
            w_dn=full_dn[l], w_dn_t=full_dn[l].T))

    saved = []
    cur = xs
    for l, p in enumerate(layers):
        a_in, qkv, p_in, h1 = _inproj_fwd(cur, p["g1"], p["w_in"], f"inproj_fwd{l}")
        y_a = _sgu_fwd(a_in, p["gn"], p["wm"], p["bias"], f"sgu_fwd{l}")
        y_b, cm = _attn_fwd(qkv, f"attn_fwd{l}")
        y_c = _pool_fwd(p_in, p["wbd"], p["sc"], f"pool_fwd{l}")
        x_mid, yn = _mix_fwd(y_a, y_b, y_c, p["gmix"], p["w_o"], cur, f"mix_fwd{l}")
        z, h2 = _ffn_up_fwd(x_mid, p["g2"], p["w_up"], f"ffn_up_fwd{l}")
        x_out, act = _ffn_down_fwd(z, p["cw"], p["w_dn"], x_mid, f"ffn_down_fwd{l}")
        saved.append(dict(x_in=cur, a_in=a_in, qkv=qkv, p_in=p_in, h1=h1, y_a=y_a, y_b=y_b, cm=cm, y_c=y_c, x_mid=x_mid,
                          yn=yn, z=z, h2=h2, act=act))
        cur = x_out
    loss_part, dx, dg_final = _final_loss(cur, final_g[None], tgt, "final_loss")

    small = {n: [None] * depth for n in SMALL if n != "final_g"}
    big = {n: [None] * depth for n in SHARDED}
    for l in reversed(range(depth)):
        p, sv = layers[l], saved[l]
        dzg, dzu, dcw = _ffn_down_bwd(dx, sv["z"], p["cw"], p["w_dn_t"], f"ffn_down_bwd{l}")
        big["w_down"][l] = _mm_tn(sv["act"], dx, f"dw_down{l}").reshape(N_DEV, D_FF // N_DEV, D_MODEL)
        dzg_b, dzu_b, dx_mid, dg2 = _ffn_up_bwd(dzg, dzu, p["cw"], p["w_up_t"], sv["x_mid"], p["g2"], dx, f"ffn_up_bwd{l}")
        dw_up = jnp.stack([_mm_tn(sv["h2"], dzg_b, f"dw_up_g{l}"), _mm_tn(sv["h2"], dzu_b, f"dw_up_u{l}")])
        big["w_up"][l] = jnp.transpose(dw_up.reshape(2, D_MODEL, N_DEV // 2, 2 * D_FF // N_DEV), (0, 2, 1, 3)).reshape(
            N_DEV, D_MODEL, 2 * D_FF // N_DEV)
        big["conv_w"][l] = jnp.transpose(dcw[0:3].reshape(3, N_DEV, 2 * D_FF // N_DEV), (1, 0, 2))
        small["conv_b"][l] = dcw[3]
        small["norm2_g"][l] = dg2[0]
        dya, dyb, dyc, dgmix = _mix_bwd(dx_mid, sv["y_a"], sv["y_b"], sv["y_c"], p["gmix"], p["w_o_t"], f"mix_bwd{l}")
        small["mix_norm_g"][l] = dgmix[0]
        big["w_o"][l] = _mm_tn(sv["yn"], dx_mid, f"dw_o{l}").reshape(N_DEV, D_MODEL // N_DEV, D_MODEL)
        dp, dwbd, dsc = _pool_bwd(sv["p_in"], dyc, p["wbd"], p["wbd_t"], p["sc"], f"pool_bwd{l}")
        small["pool_w"][l] = jnp.stack([dwbd[g * 64:(g + 1) * 64, g * 64:(g + 1) * 64] for g in range(4)])
        small["pool_scale"][l] = dsc[0]
        dq, dk, dv = _attn_bwd(sv["qkv"], sv["cm"], dyb, f"attn_bwd{l}")
        da, dwm, dbias, dgn = _sgu_bwd(sv["a_in"], dya, p["gn"], p["wm"], p["bias"], f"sgu_bwd{l}")
        small["sgu_w"][l] = dwm
        small["sgu_b"][l] = jnp.sum(dbias.reshape(CHUNK, 4, HEAD_DIM), axis=-1).T
        small["sgu_norm_g"][l] = dgn[0]
        dx, dg1 = _inproj_bwd(da, dq, dk, dv, dp, p["w_in_t"], sv["x_in"], p["g1"], dx_mid, f"inproj_bwd{l}")
        small["norm1_g"][l] = dg1[0]
        pieces = (da, dq, dk, dv, dp)
        dw_in = jnp.concatenate([_mm_tn(sv["h1"], t, f"dw_in{i}_{l}") for i, t in enumerate(pieces)], axis=1)
        big["w_in"][l] = jnp.transpose(dw_in.reshape(D_MODEL, N_DEV, IN_COLS // N_DEV), (1, 0, 2))

    names = [(n, l) for n in SHARDED for l in range(depth)]
    wire = lambda n, t: t if n == "conv_w" else t.astype(GRAD_WIRE)
    recv = _exchange([wire(n, big[n][l]) for n, l in names], "scatter_grads", True)
    out_g, out_d, out_m, out_v = {}, {}, {}, {}
    for n in SHARDED:
        res = [_adamw(recv[names.index((n, l))], weights[n][l], mom[n][l], var[n][l], f"adamw_{n}{l}") for l in range(depth)]
        out_g[n], out_d[n], out_m[n], out_v[n] = (jnp.stack([r[i] for r in res]) for i in range(4))

    part = {n: jnp.stack(small[n]) for n in small}
    part["final_g"] = dg_final[0]
    (gathered,) = _exchange([_pack(part)], "gather_small_grads", False)
    sg, sd, sm, sv_ = _adamw(gathered, _pack(weights), _pack(mom), _pack(var), "adamw_small")
    for tree, flat in ((out_g, sg), (out_d, sd), (out_m, sm), (out_v, sv_)):
        tree.update(_unpack(flat, weights))

    loss = lax.psum(loss_part[0, 0], ("x", "y", "c"))
    grad_x = dx.reshape(1, s, D_MODEL)
    return (loss, grad_x, *[out_g[n] for n in ORDER], *[out_d[n] for n in ORDER], *[out_m[n] for n in ORDER],
            *[out_v[n] for n in ORDER])
```

```python
import functools
import math

import jax
import jax.numpy as jnp
import numpy as np
from jax import lax
from jax.experimental import pallas as pl
from jax.experimental.pallas import tpu as pltpu

F32 = jnp.float32
BF16 = jnp.bfloat16
_MXU = jnp.bfloat16
GRAD_WIRE = jnp.bfloat16

D_MODEL = 1024
W_A = 256
W_B = 512
W_C = 256
HEAD_DIM = 64
IN_COLS = 2 * W_A + 3 * W_B + W_C
D_FF = 2816
CHUNK = 128
POOL_WINDOWS = (2, 4, 8, 16)
EPS = 1e-6
N_DEV = 8
LANES = 128
SUBLANES = 8
VMEM_LIMIT = 48 * 1024 * 1024

ADAM_LR = 0.001
ADAM_B1 = 0.9
ADAM_B2 = 0.999
ADAM_EPS = 1e-08
ADAM_WD = 0.01
ADAM_STEP = 10

INV_SQRT2 = 1.0 / math.sqrt(2.0)
INV_SQRT_2PI = 1.0 / math.sqrt(2.0 * math.pi)


def _pcall(body, **kw):
    return pl.pallas_call(body, **kw)


def _params(dims=None):
    return pltpu.CompilerParams(dimension_semantics=dims, vmem_limit_bytes=VMEM_LIMIT)


def _tile(n, pref, mult=SUBLANES):
    t = min(n, pref) // mult * mult
    while t >= mult:
        if n % t == 0:
            return t
        t -= mult
    return n


def _iota(shape, dim):
    return lax.broadcasted_iota(jnp.int32, shape, dim)


def _dot(a, b):
    return jnp.dot(a.astype(_MXU), b.astype(_MXU), preferred_element_type=F32)


def _dot_nt(a, b):
    return lax.dot_general(a.astype(_MXU), b.astype(_MXU), (((1,), (1,)), ((), ())), preferred_element_type=F32)


def _dot_tn(a, b):
    return lax.dot_general(a.astype(_MXU), b.astype(_MXU), (((0,), (0,)), ((), ())), preferred_element_type=F32)


def _split(x):
    hi = x.astype(BF16)
    lo = (x - hi.astype(F32)).astype(BF16)
    return hi, lo


def _dot_sel(x, sel):
    hi, lo = _split(x)
    return jnp.dot(hi, sel, preferred_element_type=F32) + jnp.dot(lo, sel, preferred_element_type=F32)


def _sel_dot(sel, x):
    hi, lo = _split(x)
    return jnp.dot(sel, hi, preferred_element_type=F32) + jnp.dot(sel, lo, preferred_element_type=F32)


def _group_mat(n):
    r = jnp.right_shift(_iota((n, n), 0), 6)
    c = jnp.right_shift(_iota((n, n), 1), 6)
    return (r == c).astype(BF16)


def _lane_group_mask(n, g):
    lane = _iota((1, n), 1)
    return (lane >= g * HEAD_DIM) & (lane < (g + 1) * HEAD_DIM)


def _gelu(a):
    return 0.5 * a * (1.0 + lax.erf(a * INV_SQRT2))


def _gelu_grad(a):
    return 0.5 * (1.0 + lax.erf(a * INV_SQRT2)) + a * jnp.exp(-0.5 * a * a) * INV_SQRT_2PI


def _rms(x):
    return lax.rsqrt(jnp.mean(x * x, axis=-1, keepdims=True) + EPS)


def _rms_bwd(x, g, dy):
    r = _rms(x)
    dyg = dy * g
    m2 = jnp.mean(dyg * x, axis=-1, keepdims=True)
    dx = r * dyg - x * (r * r * r) * m2
    dg = jnp.sum(dy * x * r, axis=0, keepdims=True)
    return dx, dg


def _shift_down(cur, prev8, k):
    rolled = pltpu.roll(cur, k, 0)
    row8 = _iota(prev8.shape, 0)
    top = jnp.where(row8 < k, pltpu.roll(prev8, k, 0), rolled[0:SUBLANES])
    return jnp.concatenate([top, rolled[SUBLANES:]], axis=0)


def _shift_up(cur, next8, k):
    n = cur.shape[0]
    rolled = pltpu.roll(cur, n - k, 0)
    row8 = _iota(next8.shape, 0)
    bot = jnp.where(row8 >= SUBLANES - k, pltpu.roll(next8, SUBLANES - k, 0), rolled[n - SUBLANES:])
    return jnp.concatenate([rolled[:n - SUBLANES], bot], axis=0)


def _mesh_pos():
    return lax.axis_index("x"), lax.axis_index("y"), lax.axis_index("c")


def _peer(x, y, c, k):
    px = 1 - x if k & 4 else x
    py = 1 - y if k & 2 else y
    pc = 1 - c if k & 1 else c
    return px, py, pc


class _Exchange:
    def __init__(self, arrs, scatter):
        self.arrs, self.scatter, self.n = list(arrs), scatter, len(arrs)
        self.out_shapes = [jax.ShapeDtypeStruct(a.shape if scatter else (N_DEV,) + a.shape, a.dtype) for a in arrs]
        self.specs = [pl.BlockSpec(memory_space=pl.ANY)] * self.n
        self.semaphores = [pltpu.SemaphoreType.DMA((self.n * (N_DEV - 1),)),
                           pltpu.SemaphoreType.DMA((self.n * (N_DEV - 1),)), pltpu.SemaphoreType.DMA((self.n,))]

    def _copies(self, ins, outs, sems):
        send, recv, loc = sems
        x, y, c = _mesh_pos()
        me = 4 * x + 2 * y + c
        src = (lambda a, idx: ins[a].at[idx]) if self.scatter else (lambda a, idx: ins[a])
        starts = [pltpu.make_async_copy(src(a, me), outs[a].at[me], loc.at[a]) for a in range(self.n)]
        waits = list(starts)
        for k in range(1, N_DEV):
            px, py, pc = _peer(x, y, c, k)
            pidx = 4 * px + 2 * py + pc
            for a in range(self.n):
                s = a * (N_DEV - 1) + k - 1
                common = dict(src_ref=src(a, pidx), send_sem=send.at[s], recv_sem=recv.at[s],
                              device_id=(px, py, pc), device_id_type=pl.DeviceIdType.MESH)
                starts.append(pltpu.make_async_remote_copy(dst_ref=outs[a].at[me], **common))
                waits.append(pltpu.make_async_remote_copy(dst_ref=outs[a].at[pidx], **common))
        return starts, waits

    def start(self, ins, outs, sems):
        for cp in self._copies(ins, outs, sems)[0]:
            cp.start()

    def wait(self, ins, outs, sems):
        for cp in self._copies(ins, outs, sems)[1]:
            cp.wait()


def _with_exchange(compute, ride, n_in, n_out, n_scratch, last_step):
    if ride is None:
        return compute
    nx = ride.n

    def body(*refs):
        ins, ride_in = refs[:n_in], refs[n_in:n_in + nx]
        outs = refs[n_in + nx:n_in + nx + n_out]
        ride_out = refs[n_in + nx + n_out:n_in + 2 * nx + n_out]
        scratch = refs[n_in + 2 * nx + n_out:n_in + 2 * nx + n_out + n_scratch]
        sems = refs[n_in + 2 * nx + n_out + n_scratch:]
        step = (pl.program_id(0), pl.program_id(1))

        @pl.when((step[0] == 0) & (step[1] == 0))
        def _():
            ride.start(ride_in, ride_out, sems)

        compute(*ins, *outs, *scratch)

        @pl.when((step[0] == last_step[0]) & (step[1] == last_step[1]))
        def _():
            ride.wait(ride_in, ride_out, sems)

    return body


def _exchange(arrs, name, scatter):
    ex = _Exchange(arrs, scatter)
    n = ex.n

    def body(*refs):
        ins, outs, sems = refs[:n], refs[n:2 * n], refs[2 * n:]
        ex.start(ins, outs, sems)
        ex.wait(ins, outs, sems)

    return _pcall(body, name=name, out_shape=ex.out_shapes, in_specs=ex.specs, out_specs=ex.specs,
                  scratch_shapes=ex.semaphores, compiler_params=pltpu.CompilerParams(has_side_effects=True))(*arrs)


def _mm_tn(a, b, name):
    s, m = a.shape
    n = b.shape[1]
    tm, tn, tk = _tile(m, 1408, LANES), _tile(n, 512, LANES), _tile(s, 1024)

    def body(a_ref, b_ref, o_ref):
        @pl.when(pl.program_id(2) == 0)
        def _():
            o_ref[...] = jnp.zeros_like(o_ref)
        o_ref[...] += _dot_tn(a_ref[...], b_ref[...])

    return _pcall(
        body, name=name, grid=(m // tm, n // tn, s // tk),
        in_specs=[pl.BlockSpec((tk, tm), lambda i, j, k: (k, i)), pl.BlockSpec((tk, tn), lambda i, j, k: (k, j))],
        out_specs=pl.BlockSpec((tm, tn), lambda i, j, k: (i, j)),
        out_shape=jax.ShapeDtypeStruct((m, n), F32),
        compiler_params=_params(("parallel", "parallel", "arbitrary")),
    )(a, b)


def _inproj_fwd(x, g, w, name):
    s = x.shape[0]
    tm = _tile(s, 256)

    def body(x_ref, g_ref, w_ref, a_ref, qkv_ref, p_ref, h_ref):
        xv = x_ref[...]
        h = (xv * _rms(xv) * g_ref[...]).astype(_MXU)
        h_ref[...] = h
        a_ref[...] = jnp.dot(h, w_ref[:, 0:2 * W_A], preferred_element_type=F32)
        q = jnp.dot(h, w_ref[:, 2 * W_A:2 * W_A + W_B], preferred_element_type=F32)
        qkv_ref[:, 0:W_B] = (q * 0.125).astype(_MXU)
        kv = jnp.dot(h, w_ref[:, 2 * W_A + W_B:2 * W_A + 3 * W_B], preferred_element_type=F32)
        qkv_ref[:, W_B:3 * W_B] = kv.astype(_MXU)
        p_ref[...] = jnp.dot(h, w_ref[:, 2 * W_A + 3 * W_B:IN_COLS], preferred_element_type=F32)

    row = lambda n: pl.BlockSpec((tm, n), lambda i: (i, 0))
    return _pcall(
        body, name=name, grid=(s // tm,),
        in_specs=[row(D_MODEL), pl.BlockSpec((1, D_MODEL), lambda i: (0, 0)),
                  pl.BlockSpec((D_MODEL, IN_COLS), lambda i: (0, 0))],
        out_specs=[row(2 * W_A), row(3 * W_B), row(W_C), row(D_MODEL)],
        out_shape=[jax.ShapeDtypeStruct((s, 2 * W_A), F32), jax.ShapeDtypeStruct((s, 3 * W_B), _MXU),
                   jax.ShapeDtypeStruct((s, W_C), F32), jax.ShapeDtypeStruct((s, D_MODEL), _MXU)],
        compiler_params=_params(("parallel",)),
    )(x, g, w)


def _inproj_bwd(da, dq, dk, dv, dp, wt, x, g, dres, name):
    s = x.shape[0]
    tm = _tile(s, 256)

    def body(da_ref, dq_ref, dk_ref, dv_ref, dp_ref, wt_ref, x_ref, g_ref, dres_ref, dx_ref, dg_ref):
        dh = _dot(da_ref[...], wt_ref[0:2 * W_A, :])
        dh += _dot(dq_ref[...], wt_ref[2 * W_A:2 * W_A + W_B, :])
        dh += _dot(dk_ref[...], wt_ref[2 * W_A + W_B:2 * W_A + 2 * W_B, :])
        dh += _dot(dv_ref[...], wt_ref[2 * W_A + 2 * W_B:2 * W_A + 3 * W_B, :])
        dh += _dot(dp_ref[...], wt_ref[2 * W_A + 3 * W_B:IN_COLS, :])
        dx, dg = _rms_bwd(x_ref[...], g_ref[...], dh)
        dx_ref[...] = dres_ref[...] + dx

        @pl.when(pl.program_id(0) == 0)
        def _():
            dg_ref[...] = jnp.zeros_like(dg_ref)
        dg_ref[...] += dg

    row = lambda n: pl.BlockSpec((tm, n), lambda i: (i, 0))
    vec = pl.BlockSpec((1, D_MODEL), lambda i: (0, 0))
    return _pcall(
        body, name=name, grid=(s // tm,),
        in_specs=[row(2 * W_A), row(W_B), row(W_B), row(W_B), row(W_C),
                  pl.BlockSpec((IN_COLS, D_MODEL), lambda i: (0, 0)), row(D_MODEL), vec, row(D_MODEL)],
        out_specs=[row(D_MODEL), vec],
        out_shape=[jax.ShapeDtypeStruct((s, D_MODEL), F32), jax.ShapeDtypeStruct((1, D_MODEL), F32)],
        compiler_params=_params(("arbitrary",)),
    )(da, dq, dk, dv, dp, wt, x, g, dres)


def _sgu_core(a, gn, wm_ref, bias):
    ga = _gelu(a)
    u, v0 = ga[:, 0:W_A], ga[:, W_A:2 * W_A]
    r = lax.rsqrt(_dot_sel(v0 * v0, _group_mat(W_A)) * (1.0 / HEAD_DIM) + EPS)
    vn = v0 * r * gn
    sv = bias
    for h in range(W_A // HEAD_DIM):
        sv = sv + _dot(wm_ref[h], jnp.where(_lane_group_mask(W_A, h), vn, 0.0))
    return u, v0, r, vn, sv


def _sgu_fwd(a, gn, wm, bias, name):
    s = a.shape[0]

    def body(a_ref, gn_ref, wm_ref, b_ref, y_ref):
        u, _, _, _, sv = _sgu_core(a_ref[...], gn_ref[...], wm_ref, b_ref[...])
        y_ref[...] = u * sv

    return _pcall(
        body, name=name, grid=(s // CHUNK,),
        in_specs=[pl.BlockSpec((CHUNK, 2 * W_A), lambda i: (i, 0)), pl.BlockSpec((1, W_A), lambda i: (0, 0)),
                  pl.BlockSpec((4, CHUNK, CHUNK), lambda i: (0, 0, 0)), pl.BlockSpec((CHUNK, W_A), lambda i: (0, 0))],
        out_specs=pl.BlockSpec((CHUNK, W_A), lambda i: (i, 0)),
        out_shape=jax.ShapeDtypeStruct((s, W_A), F32),
        compiler_params=_params(("parallel",)),
    )(a, gn, wm, bias)


def _sgu_bwd(a, dy, gn, wm, bias, name):
    s = a.shape[0]

    def body(a_ref, dy_ref, gn_ref, wm_ref, b_ref, da_ref, dwm_ref, db_ref, dgn_ref):
        @pl.when(pl.program_id(0) == 0)
        def _():
            dwm_ref[...] = jnp.zeros_like(dwm_ref)
            db_ref[...] = jnp.zeros_like(db_ref)
            dgn_ref[...] = jnp.zeros_like(dgn_ref)

        av, gnv, dyv = a_ref[...], gn_ref[...], dy_ref[...]
        u, v0, r, vn, sv = _sgu_core(av, gnv, wm_ref, b_ref[...])
        du = dyv * sv
        ds = dyv * u
        db_ref[...] += ds
        tril = _iota((CHUNK, CHUNK), 1) <= _iota((CHUNK, CHUNK), 0)
        dvn = jnp.zeros_like(vn)
        for h in range(W_A // HEAD_DIM):
            dsm = jnp.where(_lane_group_mask(W_A, h), ds, 0.0)
            dwm_ref[h] += jnp.where(tril, _dot_nt(dsm, vn), 0.0)
            dvn = dvn + _dot_tn(wm_ref[h], dsm)
        dgn_ref[...] += jnp.sum(dvn * v0 * r, axis=0, keepdims=True)
        dvg = dvn * gnv
        m2 = _dot_sel(dvg * v0, _group_mat(W_A)) * (1.0 / HEAD_DIM)
        dv0 = r * dvg - v0 * (r * r * r) * m2
        gp = _gelu_grad(av)
        da_ref[:, 0:W_A] = du * gp[:, 0:W_A]
        da_ref[:, W_A:2 * W_A] = dv0 * gp[:, W_A:2 * W_A]

    return _pcall(
        body, name=name, grid=(s // CHUNK,),
        in_specs=[pl.BlockSpec((CHUNK, 2 * W_A), lambda i: (i, 0)), pl.BlockSpec((CHUNK, W_A), lambda i: (i, 0)),
                  pl.BlockSpec((1, W_A), lambda i: (0, 0)), pl.BlockSpec((4, CHUNK, CHUNK), lambda i: (0, 0, 0)),
                  pl.BlockSpec((CHUNK, W_A), lambda i: (0, 0))],
        out_specs=[pl.BlockSpec((CHUNK, 2 * W_A), lambda i: (i, 0)), pl.BlockSpec((4, CHUNK, CHUNK), lambda i: (0, 0, 0)),
                   pl.BlockSpec((CHUNK, W_A), lambda i: (0, 0)), pl.BlockSpec((1, W_A), lambda i: (0, 0))],
        out_shape=[jax.ShapeDtypeStruct((s, 2 * W_A), F32), jax.ShapeDtypeStruct((4, CHUNK, CHUNK), F32),
                   jax.ShapeDtypeStruct((CHUNK, W_A), F32), jax.ShapeDtypeStruct((1, W_A), F32)],
        compiler_params=_params(("arbitrary",)),
    )(a, dy, gn, wm, bias)


KBLK = 128
ROW_CHUNK = 64
MASKED_SCORE = -1e30


def _attn_fwd(qkv, name, ride=None):
    s = qkv.shape[0]
    tq = _tile(s, 256, KBLK)
    npairs = W_B // LANES
    assert s // KBLK <= LANES
    rides = [] if ride is None else [ride]

    def body(q_ref, k_ref, v_ref, o_ref, cm_ref, z_ref, zw_ref, sums_ref, carry_ref, hl_ref, a_ref):
        i = pl.program_id(1)
        q = q_ref[...]
        lane = _iota((1, LANES), 1)
        lane_lo = lane < HEAD_DIM
        hmask = (lane_lo, jnp.logical_not(lane_lo))
        tri2 = ((_iota((KBLK, 2 * KBLK), 0) >= _iota((KBLK, 2 * KBLK), 1))
                | (_iota((KBLK, 2 * KBLK), 1) >= KBLK)).astype(BF16)
        dmat = _iota((tq, KBLK), 1) - (_iota((tq, KBLK), 0) + i * tq)
        chunks = [slice(r, r + ROW_CHUNK) for r in range(0, tq, ROW_CHUNK)]
        heads = [slice(hh * LANES, (hh + 1) * LANES) for hh in range(2)]
        nk = (i + 1) * (tq // KBLK)

        cm_ref[...] = jnp.zeros_like(cm_ref)

        def before(b):
            return jnp.where((b >= 0) & (b < nk), -b * KBLK, jnp.iinfo(jnp.int32).min)

        def per_head(block):
            return jnp.concatenate([jnp.where(m, block, jnp.zeros_like(block)) for m in hmask], axis=0)

        def scores(b, p):
            ks = pl.multiple_of(jnp.maximum(b, 0) * KBLK, KBLK)
            z_ref[p] = _dot_nt(q, per_head(k_ref[pl.ds(ks, KBLK), :]))

        def logs(b, p):
            t = before(b)
            for hh in range(2):
                for rows in chunks:
                    z = jnp.where(dmat[rows] < t, z_ref[p, rows, heads[hh]], MASKED_SCORE)
                    zw_ref[p, rows, heads[hh]] = z
                    l = -(jnp.maximum(z, 0.0) + jnp.log(1.0 + jnp.exp(-jnp.abs(z))))
                    hi, lo = _split(l)
                    hl_ref[p, hh, rows, 0:KBLK] = hi
                    hl_ref[p, hh, rows, KBLK:2 * KBLK] = lo

        def sums(b, p):
            for hh in range(2):
                sums_ref[p, hh] = (jnp.dot(hl_ref[p, hh, :, 0:KBLK], tri2, preferred_element_type=F32)
                                   + jnp.dot(hl_ref[p, hh, :, KBLK:2 * KBLK], tri2, preferred_element_type=F32))

        def weights(b, p):
            onehot = jnp.where(lane == b, 1.0, 0.0)
            for hh in range(2):
                for rows in chunks:
                    c = carry_ref[hh, rows, :]
                    arg = zw_ref[p, rows, heads[hh]] + c + sums_ref[p, hh, rows, 0:KBLK]
                    a_ref[p, rows, heads[hh]] = jnp.exp(arg).astype(_MXU)
                    cm_ref[rows, heads[hh]] += onehot * c
                    carry_ref[hh, rows, :] = c + sums_ref[p, hh, rows, KBLK:2 * KBLK]

        def out(b, p, acc):
            ks = pl.multiple_of(jnp.minimum(b, nk - 1) * KBLK, KBLK)
            vb = v_ref[pl.ds(ks, KBLK), :]
            for hh in range(2):
                acc = acc + jnp.dot(a_ref[p, :, heads[hh]], jnp.where(hmask[hh], vb, jnp.zeros_like(vb)),
                                    preferred_element_type=F32)
            return acc

        def step(it, acc):
            for u in range(2):
                b = nk + 3 - (2 * it + u)
                p = 1 - u
                acc = out(b, p, acc)
                weights(b - 1, 1 - p)
                sums(b - 2, p)
                logs(b - 3, 1 - p)
                scores(b - 4, p)
            return acc

        z_ref[...] = jnp.zeros_like(z_ref)
        zw_ref[...] = jnp.full_like(zw_ref, MASKED_SCORE)
        for ref in (sums_ref, carry_ref, hl_ref, a_ref):
            ref[...] = jnp.zeros_like(ref)
        o_ref[...] = lax.fori_loop(0, (nk + 4) // 2, step, q.astype(F32) * 0.0)

    scratch = [pltpu.VMEM((2, tq, 2 * KBLK), F32), pltpu.VMEM((2, tq, 2 * KBLK), F32),
               pltpu.VMEM((2, 2, tq, 2 * KBLK), F32), pltpu.VMEM((2, tq, KBLK), F32),
               pltpu.VMEM((2, 2, tq, 2 * KBLK), BF16), pltpu.VMEM((2, tq, 2 * KBLK), _MXU)]
    res = _pcall(
        _with_exchange(body, ride, 3, 2, len(scratch), (npairs - 1, s // tq - 1)), name=name, grid=(npairs, s // tq),
        in_specs=[pl.BlockSpec((tq, LANES), lambda p, i: (i, p)),
                  pl.BlockSpec((s, LANES), lambda p, i: (0, npairs + p)),
                  pl.BlockSpec((s, LANES), lambda p, i: (0, 2 * npairs + p))] + [sp for r in rides for sp in r.specs],
        out_specs=[pl.BlockSpec((tq, LANES), lambda p, i: (i, p)), pl.BlockSpec((tq, 2 * LANES), lambda p, i: (i, p))]
                  + [sp for r in rides for sp in r.specs],
        out_shape=[jax.ShapeDtypeStruct((s, W_B), F32), jax.ShapeDtypeStruct((s, 2 * W_B), F32)]
                  + [sh for r in rides for sh in r.out_shapes],
        scratch_shapes=scratch + [sem for r in rides for sem in r.semaphores],
        compiler_params=_params(("arbitrary", "arbitrary")),
    )(qkv, qkv, qkv, *[a for r in rides for a in r.arrs])
    return res[0], res[1], list(res[2:])


def _attn_bwd(qkv, cm, do, name, ride=None):
    s = qkv.shape[0]
    tq = _tile(s, 256, KBLK)
    npairs = W_B // LANES
    rides = [] if ride is None else [ride]

    def body(q_ref, k_ref, v_ref, cm_ref, do_ref, dq_ref, dk_ref, dv_ref,
             z_ref, zw_ref, da_ref, g_ref, sig_ref, cum_ref, gp_ref, gcarry_ref, hl_ref, ghl_ref, a_ref, dz_ref,
             dkt_ref, dvt_ref):
        i = pl.program_id(1)

        @pl.when(i == 0)
        def _():
            dkt_ref[...] = jnp.zeros_like(dkt_ref)
            dvt_ref[...] = jnp.zeros_like(dvt_ref)

        q = q_ref[...]
        dov = do_ref[...].astype(_MXU)
        lane = _iota((1, LANES), 1)
        lane_lo = lane < HEAD_DIM
        hmask = (lane_lo, jnp.logical_not(lane_lo))
        tri = (_iota((KBLK, KBLK), 0) >= _iota((KBLK, KBLK), 1)).astype(BF16)
        tris = jnp.concatenate([tri, tri], axis=0)
        prefix2 = ((_iota((KBLK, 2 * KBLK), 0) <= _iota((KBLK, 2 * KBLK), 1))
                   | (_iota((KBLK, 2 * KBLK), 1) >= KBLK)).astype(BF16)
        prefix2s = jnp.concatenate([prefix2, prefix2], axis=0)
        dmat = _iota((tq, KBLK), 1) - (_iota((tq, KBLK), 0) + i * tq)
        chunks = [slice(r, r + ROW_CHUNK) for r in range(0, tq, ROW_CHUNK)]
        heads = [slice(hh * LANES, (hh + 1) * LANES) for hh in range(2)]
        nk = (i + 1) * (tq // KBLK)

        def before(b):
            return jnp.where((b >= 0) & (b < nk), -b * KBLK, jnp.iinfo(jnp.int32).min)

        def block_rows(b):
            return pl.ds(pl.multiple_of(jnp.clip(b, 0, nk - 1) * KBLK, KBLK), KBLK)

        def per_head(block):
            return jnp.concatenate([jnp.where(m, block, jnp.zeros_like(block)) for m in hmask], axis=0)

        q_t = q.astype(F32).T.astype(_MXU)
        do_t = do_ref[...].T.astype(_MXU)
        feature_lo = _iota((LANES, KBLK), 0) < HEAD_DIM

        def own_features(side_by_side):
            return jnp.where(feature_lo, side_by_side[:, 0:KBLK], side_by_side[:, KBLK:2 * KBLK])

        def m1(b, p):
            z_ref[p] = _dot_nt(q, per_head(k_ref[block_rows(b), :]))

        def v1(b, p):
            t = before(b)
            for hh in range(2):
                for rows in chunks:
                    z = jnp.where(dmat[rows] < t, z_ref[p, rows, heads[hh]], MASKED_SCORE)
                    zw_ref[p, rows, heads[hh]] = z
                    l = -(jnp.maximum(z, 0.0) + jnp.log(1.0 + jnp.exp(-jnp.abs(z))))
                    hi, lo = _split(l)
                    hl_ref[p, hh, rows, 0:KBLK] = hi
                    hl_ref[p, hh, rows, KBLK:2 * KBLK] = lo

        def m2(b, p):
            for hh in range(2):
                cum_ref[p, hh] = jnp.dot(hl_ref[p, hh], tris, preferred_element_type=F32)
            da_ref[p] = _dot_nt(dov, per_head(v_ref[block_rows(b), :]))

        def v2(b, p):
            pick = lane == b
            for hh in range(2):
                for rows in chunks:
                    c = jnp.sum(jnp.where(pick, cm_ref[rows, heads[hh]], 0.0), axis=1, keepdims=True)
                    z = zw_ref[p, rows, heads[hh]]
                    a = jnp.exp(z + c + cum_ref[p, hh, rows, :])
                    g = a * da_ref[p, rows, heads[hh]]
                    a_ref[p, rows, heads[hh]] = a.astype(_MXU)
                    g_ref[p, rows, heads[hh]] = g
                    hi, lo = _split(g)
                    ghl_ref[p, hh, rows, 0:KBLK] = hi
                    ghl_ref[p, hh, rows, KBLK:2 * KBLK] = lo
                    sig_ref[p, rows, heads[hh]] = jax.nn.sigmoid(z)

        def m3(b, p):
            for hh in range(2):
                gp_ref[p, hh] = jnp.dot(ghl_ref[p, hh], prefix2s, preferred_element_type=F32)
            dvt_ref[jnp.clip(b, 0, nk - 1)] += own_features(jnp.dot(do_t, a_ref[p], preferred_element_type=F32))

        def v3(b, p):
            for hh in range(2):
                for rows in chunks:
                    gc = gcarry_ref[hh, rows, :]
                    upto = gc + gp_ref[p, hh, rows, 0:KBLK]
                    dz = g_ref[p, rows, heads[hh]] - sig_ref[p, rows, heads[hh]] * upto
                    dz_ref[p, rows, heads[hh]] = dz.astype(_MXU)
                    gcarry_ref[hh, rows, :] = gc + gp_ref[p, hh, rows, KBLK:2 * KBLK]

        def m4(b, p, dq):
            dq = dq + jnp.dot(dz_ref[p], per_head(k_ref[block_rows(b), :]), preferred_element_type=F32)
            dkt_ref[jnp.clip(b, 0, nk - 1)] += own_features(jnp.dot(q_t, dz_ref[p], preferred_element_type=F32))
            return dq

        def step(it, dq):
            for u in range(2):
                j = 2 * it + u
                dq = m4(j - 6, u, dq)
                m3(j - 4, u)
                m2(j - 2, u)
                m1(j, u)
                v3(j - 5, 1 - u)
                v2(j - 3, 1 - u)
                v1(j - 1, 1 - u)
            return dq

        zw_ref[...] = jnp.full_like(zw_ref, MASKED_SCORE)
        for ref in (z_ref, da_ref, g_ref, sig_ref, cum_ref, gp_ref, gcarry_ref, hl_ref, ghl_ref, a_ref, dz_ref):
            ref[...] = jnp.zeros_like(ref)
        dq_ref[...] = lax.fori_loop(0, (nk + 6) // 2, step, do_ref[...] * 0.0) * 0.125

        @pl.when(i == s // tq - 1)
        def _():
            def untranspose(blk, carry):
                rows = pl.ds(pl.multiple_of(blk * KBLK, KBLK), KBLK)
                dk_ref[rows, :] = dkt_ref[blk].T
                dv_ref[rows, :] = dvt_ref[blk].T
                return carry
            lax.fori_loop(0, nblk, untranspose, 0)

    qspec = pl.BlockSpec((tq, LANES), lambda p, i: (i, p))
    nblk = s // KBLK
    full = pl.BlockSpec((s, LANES), lambda p, i: (0, p))
    scratch = ([pltpu.VMEM((2, tq, 2 * KBLK), F32)] * 5 + [pltpu.VMEM((2, 2, tq, KBLK), F32),
               pltpu.VMEM((2, 2, tq, 2 * KBLK), F32), pltpu.VMEM((2, tq, KBLK), F32)]
               + [pltpu.VMEM((2, 2, tq, 2 * KBLK), BF16)] * 2 + [pltpu.VMEM((2, tq, 2 * KBLK), _MXU)] * 2
               + [pltpu.VMEM((nblk, LANES, KBLK), F32)] * 2)
    res = _pcall(
        _with_exchange(body, ride, 5, 3, len(scratch), (npairs - 1, s // tq - 1)), name=name, grid=(npairs, s // tq),
        in_specs=[qspec, pl.BlockSpec((s, LANES), lambda p, i: (0, npairs + p)),
                  pl.BlockSpec((s, LANES), lambda p, i: (0, 2 * npairs + p)),
                  pl.BlockSpec((tq, 2 * LANES), lambda p, i: (i, p)), qspec] + [sp for r in rides for sp in r.specs],
        out_specs=[qspec, full, full] + [sp for r in rides for sp in r.specs],
        out_shape=[jax.ShapeDtypeStruct((s, W_B), F32)] * 3 + [sh for r in rides for sh in r.out_shapes],
        scratch_shapes=scratch + [sem for r in rides for sem in r.semaphores],
        compiler_params=_params(("arbitrary", "arbitrary")),
    )(qkv, qkv, qkv, cm, do, *[a for r in rides for a in r.arrs])
    return res[0], res[1], res[2], list(res[3:])


POOL_HALO = 128


def _pool_window_lane():
    lane = _iota((1, W_C), 1)
    w = jnp.where(lane < 64, POOL_WINDOWS[0], jnp.where(lane < 128, POOL_WINDOWS[1],
                  jnp.where(lane < 192, POOL_WINDOWS[2], POOL_WINDOWS[3])))
    return w.astype(F32)


def _pool_count(tm, i):
    pos = (_iota((tm, W_C), 0) + (i * tm + 1)).astype(F32)
    return jnp.minimum(pos, _pool_window_lane())


def _pool_centered(prev, cur, cnt):
    tm = cur.shape[0]
    xx = jnp.concatenate([prev, cur], axis=0)
    hi, lo = _split(xx)
    t = _iota((tm, tm + POOL_HALO), 0)
    cc = _iota((tm, tm + POOL_HALO), 1) - POOL_HALO
    wsum = jnp.zeros_like(cur)
    for g, w in enumerate(POOL_WINDOWS):
        band = ((cc <= t) & (cc > t - w)).astype(BF16)
        mg = _lane_group_mask(W_C, g)
        wsum += jnp.dot(band, jnp.where(mg, hi, jnp.zeros_like(hi)), preferred_element_type=F32)
        wsum += jnp.dot(band, jnp.where(mg, lo, jnp.zeros_like(lo)), preferred_element_type=F32)
    return wsum / cnt - cur


def _pool_fwd(p, wbd, sc, name):
    s = p.shape[0]
    tm = _tile(s, 256, POOL_HALO)
    r = tm // POOL_HALO

    def body(pp_ref, p_ref, w_ref, sc_ref, y_ref):
        i = pl.program_id(0)
        prev = jnp.where(i > 0, pp_ref[...], 0.0)
        d = _pool_centered(prev, p_ref[...], _pool_count(tm, i))
        y_ref[...] = _dot(d, w_ref[...]) * sc_ref[...]

    return _pcall(
        body, name=name, grid=(s // tm,),
        in_specs=[pl.BlockSpec((POOL_HALO, W_C), lambda i: (jnp.maximum(i * r - 1, 0), 0)),
                  pl.BlockSpec((tm, W_C), lambda i: (i, 0)), pl.BlockSpec((W_C, W_C), lambda i: (0, 0)),
                  pl.BlockSpec((1, W_C), lambda i: (0, 0))],
        out_specs=pl.BlockSpec((tm, W_C), lambda i: (i, 0)),
        out_shape=jax.ShapeDtypeStruct((s, W_C), F32),
        compiler_params=_params(("parallel",)),
    )(p, p, wbd, sc)


def _pool_bwd(p, dy, wbd, wbdt, sc, name):
    s = p.shape[0]
    tm = _tile(s, 256, POOL_HALO)
    r = tm // POOL_HALO
    nt = s // tm

    def body(pp_ref, p_ref, dy_ref, dyn_ref, w_ref, wt_ref, sc_ref, dp_ref, dw_ref, dsc_ref):
        i = pl.program_id(0)

        @pl.when(i == 0)
        def _():
            dw_ref[...] = jnp.zeros_like(dw_ref)
            dsc_ref[...] = jnp.zeros_like(dsc_ref)

        scv = sc_ref[...]
        cnt = _pool_count(tm, i)
        prev = jnp.where(i > 0, pp_ref[...], 0.0)
        d = _pool_centered(prev, p_ref[...], cnt)
        e = _dot(d, w_ref[...])
        dyv = dy_ref[...]
        de = dyv * scv
        dsc_ref[...] += jnp.sum(dyv * e, axis=0, keepdims=True)
        dw_ref[...] += _dot_tn(d, de)
        dd = _dot(de, wt_ref[...])
        ddn = jnp.where(i < nt - 1, _dot(dyn_ref[...] * scv, wt_ref[...]), 0.0)
        yy = jnp.concatenate([dd / cnt, ddn / _pool_window_lane()], axis=0)
        hi, lo = _split(yy)
        t = _iota((tm, tm + POOL_HALO), 0)
        cc = _iota((tm, tm + POOL_HALO), 1)
        acc = jnp.zeros_like(dd)
        for g, w in enumerate(POOL_WINDOWS):
            band = ((cc >= t) & (cc < t + w)).astype(BF16)
            mg = _lane_group_mask(W_C, g)
            acc += jnp.dot(band, jnp.where(mg, hi, jnp.zeros_like(hi)), preferred_element_type=F32)
            acc += jnp.dot(band, jnp.where(mg, lo, jnp.zeros_like(lo)), preferred_element_type=F32)
        dp_ref[...] = acc - dd

    tile = pl.BlockSpec((tm, W_C), lambda i: (i, 0))
    mat = pl.BlockSpec((W_C, W_C), lambda i: (0, 0))
    vec = pl.BlockSpec((1, W_C), lambda i: (0, 0))
    return _pcall(
        body, name=name, grid=(nt,),
        in_specs=[pl.BlockSpec((POOL_HALO, W_C), lambda i: (jnp.maximum(i * r - 1, 0), 0)), tile, tile,
                  pl.BlockSpec((POOL_HALO, W_C), lambda i: (jnp.minimum((i + 1) * r, s // POOL_HALO - 1), 0)),
                  mat, mat, vec],
        out_specs=[tile, mat, vec],
        out_shape=[jax.ShapeDtypeStruct((s, W_C), F32), jax.ShapeDtypeStruct((W_C, W_C), F32),
                   jax.ShapeDtypeStruct((1, W_C), F32)],
        compiler_params=_params(("arbitrary",)),
    )(p, p, dy, dy, wbd, wbdt, sc)


def _mix_cols(ya_ref, yb_ref, yc_ref, cb):
    if cb < 2:
        return ya_ref[:, cb * LANES:(cb + 1) * LANES]
    if cb < 6:
        return yb_ref[:, (cb - 2) * LANES:(cb - 1) * LANES]
    return yc_ref[:, (cb - 6) * LANES:(cb - 5) * LANES]


def _mix_fwd(ya, yb, yc, g, wo, x, name):
    s = x.shape[0]
    tm = _tile(s, 256)

    def body(ya_ref, yb_ref, yc_ref, g_ref, w_ref, x_ref, o_ref, yn_ref):
        sel = _group_mat(LANES)
        for cb in range(D_MODEL // LANES):
            y = _mix_cols(ya_ref, yb_ref, yc_ref, cb)
            r = lax.rsqrt(_dot_sel(y * y, sel) * (1.0 / HEAD_DIM) + EPS)
            yn_ref[:, cb * LANES:(cb + 1) * LANES] = (y * r * g_ref[:, cb * LANES:(cb + 1) * LANES]).astype(_MXU)
        o_ref[...] = x_ref[...] + jnp.dot(yn_ref[...], w_ref[...], preferred_element_type=F32)

    row = lambda n: pl.BlockSpec((tm, n), lambda i: (i, 0))
    return _pcall(
        body, name=name, grid=(s // tm,),
        in_specs=[row(W_A), row(W_B), row(W_C), pl.BlockSpec((1, D_MODEL), lambda i: (0, 0)),
                  pl.BlockSpec((D_MODEL, D_MODEL), lambda i: (0, 0)), row(D_MODEL)],
        out_specs=[row(D_MODEL), row(D_MODEL)],
        out_shape=[jax.ShapeDtypeStruct((s, D_MODEL), F32), jax.ShapeDtypeStruct((s, D_MODEL), _MXU)],
        compiler_params=_params(("parallel",)),
    )(ya, yb, yc, g, wo, x)


def _mix_bwd(dx, ya, yb, yc, g, wot, name):
    s = dx.shape[0]
    tm = _tile(s, 256)

    def body(dx_ref, ya_ref, yb_ref, yc_ref, g_ref, wt_ref, dya_ref, dyb_ref, dyc_ref, dg_ref):
        @pl.when(pl.program_id(0) == 0)
        def _():
            dg_ref[...] = jnp.zeros_like(dg_ref)

        dyn = _dot(dx_ref[...], wt_ref[...])
        sel = _group_mat(LANES)
        for cb in range(D_MODEL // LANES):
            cols = slice(cb * LANES, (cb + 1) * LANES)
            y = _mix_cols(ya_ref, yb_ref, yc_ref, cb)
            r = lax.rsqrt(_dot_sel(y * y, sel) * (1.0 / HEAD_DIM) + EPS)
            dyc_ = dyn[:, cols]
            dyg = dyc_ * g_ref[:, cols]
            m2 = _dot_sel(dyg * y, sel) * (1.0 / HEAD_DIM)
            dy = r * dyg - y * (r * r * r) * m2
            dg_ref[:, cols] += jnp.sum(dyc_ * y * r, axis=0, keepdims=True)
            if cb < 2:
                dya_ref[:, cb * LANES:(cb + 1) * LANES] = dy
            elif cb < 6:
                dyb_ref[:, (cb - 2) * LANES:(cb - 1) * LANES] = dy
            else:
                dyc_ref[:, (cb - 6) * LANES:(cb - 5) * LANES] = dy

    row = lambda n: pl.BlockSpec((tm, n), lambda i: (i, 0))
    vec = pl.BlockSpec((1, D_MODEL), lambda i: (0, 0))
    return _pcall(
        body, name=name, grid=(s // tm,),
        in_specs=[row(D_MODEL), row(W_A), row(W_B), row(W_C), vec, pl.BlockSpec((D_MODEL, D_MODEL), lambda i: (0, 0))],
        out_specs=[row(W_A), row(W_B), row(W_C), vec],
        out_shape=[jax.ShapeDtypeStruct((s, W_A), F32), jax.ShapeDtypeStruct((s, W_B), F32),
                   jax.ShapeDtypeStruct((s, W_C), F32), jax.ShapeDtypeStruct((1, D_MODEL), F32)],
        compiler_params=_params(("arbitrary",)),
    )(dx, ya, yb, yc, g, wot)


FFN_CHUNK = 256
N_CHUNKS = D_FF // FFN_CHUNK
CW_ROWS = 8


def _ffn_up_fwd(x, g, w, name):
    s = x.shape[0]
    n = w.shape[1]
    tm, tn = _tile(s, 512), _tile(n, 2816, LANES)

    def body(x_ref, g_ref, w_ref, z_ref, h_ref):
        @pl.when(pl.program_id(1) == 0)
        def _():
            xv = x_ref[...]
            h_ref[...] = (xv * _rms(xv) * g_ref[...]).astype(_MXU)
        z_ref[...] = jnp.dot(h_ref[...], w_ref[...], preferred_element_type=F32)

    return _pcall(
        body, name=name, grid=(s // tm, n // tn),
        in_specs=[pl.BlockSpec((tm, D_MODEL), lambda i, j: (i, 0)), pl.BlockSpec((1, D_MODEL), lambda i, j: (0, 0)),
                  pl.BlockSpec((D_MODEL, tn), lambda i, j: (0, j))],
        out_specs=[pl.BlockSpec((tm, tn), lambda i, j: (i, j)), pl.BlockSpec((tm, D_MODEL), lambda i, j: (i, 0))],
        out_shape=[jax.ShapeDtypeStruct((s, n), F32), jax.ShapeDtypeStruct((s, D_MODEL), _MXU)],
        compiler_params=_params(("parallel", "arbitrary")),
    )(x, g, w)


def _conv(cur, prev8, cw_ref):
    s1 = _shift_down(cur, prev8, 1)
    s2 = _shift_down(cur, prev8, 2)
    zc = cw_ref[3:4, :] + s2 * cw_ref[0:1, :]
    zc = zc + s1 * cw_ref[1:2, :]
    zc = zc + cur * cw_ref[2:3, :]
    return zc, s1, s2


def _halo_specs(tm, s):
    r = tm // SUBLANES
    prev = lambda off: pl.BlockSpec((SUBLANES, FFN_CHUNK), lambda i, j: (jnp.maximum(i * r - 1, 0), j + off))
    nxt = lambda off: pl.BlockSpec((SUBLANES, FFN_CHUNK), lambda i, j: (jnp.minimum((i + 1) * r, s // SUBLANES - 1), j + off))
    return prev, nxt


def _ffn_down_fwd(z, cw, wd, x, name):
    s = x.shape[0]
    tm = _tile(s, 512)
    prev, _ = _halo_specs(tm, s)

    def body(zg_ref, zu_ref, pg_ref, pu_ref, cg_ref, cu_ref, w_ref, x_ref, o_ref, act_ref, acc_ref):
        i, j = pl.program_id(0), pl.program_id(1)
        first = i > 0
        zg, _, _ = _conv(zg_ref[...], jnp.where(first, pg_ref[...], 0.0), cg_ref)
        zu, _, _ = _conv(zu_ref[...], jnp.where(first, pu_ref[...], 0.0), cu_ref)
        act = (zg * jax.nn.sigmoid(zg) * zu).astype(_MXU)
        act_ref[...] = act

        @pl.when(j == 0)
        def _():
            acc_ref[...] = x_ref[...]
        acc_ref[...] += jnp.dot(act, w_ref[...], preferred_element_type=F32)

        @pl.when(j == N_CHUNKS - 1)
        def _():
            o_ref[...] = acc_ref[...]

    zt = lambda off: pl.BlockSpec((tm, FFN_CHUNK), lambda i, j: (i, j + off))
    cwt = lambda off: pl.BlockSpec((CW_ROWS, FFN_CHUNK), lambda i, j: (0, j + off))
    return _pcall(
        body, name=name, grid=(s // tm, N_CHUNKS),
        in_specs=[zt(0), zt(N_CHUNKS), prev(0), prev(N_CHUNKS), cwt(0), cwt(N_CHUNKS),
                  pl.BlockSpec((FFN_CHUNK, D_MODEL), lambda i, j: (j, 0)), pl.BlockSpec((tm, D_MODEL), lambda i, j: (i, 0))],
        out_specs=[pl.BlockSpec((tm, D_MODEL), lambda i, j: (i, 0)), pl.BlockSpec((tm, FFN_CHUNK), lambda i, j: (i, j))],
        out_shape=[jax.ShapeDtypeStruct((s, D_MODEL), F32), jax.ShapeDtypeStruct((s, D_FF), _MXU)],
        scratch_shapes=[pltpu.VMEM((tm, D_MODEL), F32)],
        compiler_params=_params(("parallel", "arbitrary")),
    )(z, z, z, z, cw, cw, wd, x)


def _ffn_down_bwd(dx, z, cw, wdt, name):
    s = dx.shape[0]
    tm = _tile(s, 512)

    def body(dx_ref, zg_ref, zu_ref, pg_ref, pu_ref, cg_ref, cu_ref, wt_ref, dg_ref, du_ref, dcg_ref, dcu_ref):
        i = pl.program_id(1)
        first = i > 0

        @pl.when(i == 0)
        def _():
            dcg_ref[...] = jnp.zeros_like(dcg_ref)
            dcu_ref[...] = jnp.zeros_like(dcu_ref)

        dact = _dot(dx_ref[...], wt_ref[...])
        zg, g1, g2 = _conv(zg_ref[...], jnp.where(first, pg_ref[...], 0.0), cg_ref)
        zu, u1, u2 = _conv(zu_ref[...], jnp.where(first, pu_ref[...], 0.0), cu_ref)
        sg = jax.nn.sigmoid(zg)
        silu = zg * sg
        dzu = dact * silu
        dzg = dact * zu * (sg * (1.0 + zg * (1.0 - sg)))
        dg_ref[...] = dzg
        du_ref[...] = dzu
        for ref, dzc, cur, s1, s2 in ((dcg_ref, dzg, zg_ref[...], g1, g2), (dcu_ref, dzu, zu_ref[...], u1, u2)):
            ref[0:1, :] += jnp.sum(dzc * s2, axis=0, keepdims=True)
            ref[1:2, :] += jnp.sum(dzc * s1, axis=0, keepdims=True)
            ref[2:3, :] += jnp.sum(dzc * cur, axis=0, keepdims=True)
            ref[3:4, :] += jnp.sum(dzc, axis=0, keepdims=True)

    zt = lambda off: pl.BlockSpec((tm, FFN_CHUNK), lambda j, i: (i, j + off))
    r = tm // SUBLANES
    pv = lambda off: pl.BlockSpec((SUBLANES, FFN_CHUNK), lambda j, i: (jnp.maximum(i * r - 1, 0), j + off))
    cwt = lambda off: pl.BlockSpec((CW_ROWS, FFN_CHUNK), lambda j, i: (0, j + off))
    out_t = pl.BlockSpec((tm, FFN_CHUNK), lambda j, i: (i, j))
    dc_t = pl.BlockSpec((CW_ROWS, FFN_CHUNK), lambda j, i: (0, j))
    dzg, dzu, dcg, dcu = _pcall(
        body, name=name, grid=(N_CHUNKS, s // tm),
        in_specs=[pl.BlockSpec((tm, D_MODEL), lambda j, i: (i, 0)), zt(0), zt(N_CHUNKS), pv(0), pv(N_CHUNKS),
                  cwt(0), cwt(N_CHUNKS), pl.BlockSpec((D_MODEL, FFN_CHUNK), lambda j, i: (0, j))],
        out_specs=[out_t, out_t, dc_t, dc_t],
        out_shape=[jax.ShapeDtypeStruct((s, D_FF), F32), jax.ShapeDtypeStruct((s, D_FF), F32),
                   jax.ShapeDtypeStruct((CW_ROWS, D_FF), F32), jax.ShapeDtypeStruct((CW_ROWS, D_FF), F32)],
        compiler_params=_params(("parallel", "arbitrary")),
    )(dx, z, z, z, z, cw, cw, wdt)
    return dzg, dzu, jnp.concatenate([dcg, dcu], axis=1)


def _ffn_up_bwd(dzg, dzu, cw, wut, x, g, dres, name):
    s = x.shape[0]
    tm = _tile(s, 512)
    _, nxt = _halo_specs(tm, s)
    nt = s // tm

    def body(dg_ref, du_ref, ng_ref, nu_ref, cg_ref, cu_ref, wg_ref, wu_ref, x_ref, g_ref, dres_ref,
             dzg_ref, dzu_ref, dx_ref, dgn_ref, acc_ref):
        i, j = pl.program_id(0), pl.program_id(1)
        last = i < nt - 1

        def conv_bwd(cur, nxt8, cw_ref):
            up1 = _shift_up(cur, nxt8, 1)
            up2 = _shift_up(cur, nxt8, 2)
            return cur * cw_ref[2:3, :] + up1 * cw_ref[1:2, :] + up2 * cw_ref[0:1, :]

        dzg_ = conv_bwd(dg_ref[...], jnp.where(last, ng_ref[...], 0.0), cg_ref).astype(_MXU)
        dzu_ = conv_bwd(du_ref[...], jnp.where(last, nu_ref[...], 0.0), cu_ref).astype(_MXU)
        dzg_ref[...] = dzg_
        dzu_ref[...] = dzu_

        @pl.when(j == 0)
        def _():
            acc_ref[...] = jnp.zeros_like(acc_ref)
        acc_ref[...] += (jnp.dot(dzg_, wg_ref[...], preferred_element_type=F32)
                         + jnp.dot(dzu_, wu_ref[...], preferred_element_type=F32))

        @pl.when((i == 0) & (j == 0))
        def _():
            dgn_ref[...] = jnp.zeros_like(dgn_ref)

        @pl.when(j == N_CHUNKS - 1)
        def _():
            dx, dgn = _rms_bwd(x_ref[...], g_ref[...], acc_ref[...])
            dx_ref[...] = dres_ref[...] + dx
            dgn_ref[...] += dgn

    zt = pl.BlockSpec((tm, FFN_CHUNK), lambda i, j: (i, j))
    cwt = lambda off: pl.BlockSpec((CW_ROWS, FFN_CHUNK), lambda i, j: (0, j + off))
    wt = lambda off: pl.BlockSpec((FFN_CHUNK, D_MODEL), lambda i, j: (j + off, 0))
    row = pl.BlockSpec((tm, D_MODEL), lambda i, j: (i, 0))
    vec = pl.BlockSpec((1, D_MODEL), lambda i, j: (0, 0))
    return _pcall(
        body, name=name, grid=(nt, N_CHUNKS),
        in_specs=[zt, zt, nxt(0), nxt(0), cwt(0), cwt(N_CHUNKS), wt(0), wt(N_CHUNKS), row, vec, row],
        out_specs=[zt, zt, row, vec],
        out_shape=[jax.ShapeDtypeStruct((s, D_FF), _MXU), jax.ShapeDtypeStruct((s, D_FF), _MXU),
                   jax.ShapeDtypeStruct((s, D_MODEL), F32), jax.ShapeDtypeStruct((1, D_MODEL), F32)],
        scratch_shapes=[pltpu.VMEM((tm, D_MODEL), F32)],
        compiler_params=_params(("arbitrary", "arbitrary")),
    )(dzg, dzu, dzg, dzu, cw, cw, wut, wut, x, g, dres)


def _final_loss(x, g, tgt, name):
    s = x.shape[0]
    tm = _tile(s, 256)

    def body(x_ref, g_ref, t_ref, loss_ref, dx_ref, dg_ref):
        @pl.when(pl.program_id(0) == 0)
        def _():
            loss_ref[...] = jnp.zeros_like(loss_ref)
            dg_ref[...] = jnp.zeros_like(dg_ref)

        xv, gv = x_ref[...], g_ref[...]
        err = xv * _rms(xv) * gv - t_ref[...]
        per_tok = jnp.mean(err * err, axis=-1, keepdims=True)
        loss_ref[...] += 0.5 * jnp.sum(per_tok, axis=0, keepdims=True)
        dx, dg = _rms_bwd(xv, gv, err * (1.0 / D_MODEL))
        dx_ref[...] = dx
        dg_ref[...] += dg

    row = pl.BlockSpec((tm, D_MODEL), lambda i: (i, 0))
    vec = pl.BlockSpec((1, D_MODEL), lambda i: (0, 0))
    return _pcall(
        body, name=name, grid=(s // tm,),
        in_specs=[row, vec, row], out_specs=[pl.BlockSpec((1, 1), lambda i: (0, 0)), row, vec],
        out_shape=[jax.ShapeDtypeStruct((1, 1), F32), jax.ShapeDtypeStruct((s, D_MODEL), F32),
                   jax.ShapeDtypeStruct((1, D_MODEL), F32)],
        compiler_params=_params(("arbitrary",)),
    )(x, g, tgt)


def _adamw(parts, w, m, v, name):
    r, c = w.shape
    tr = _tile(r, 256)
    c1 = 1.0 - ADAM_B1 ** ADAM_STEP
    c2 = 1.0 - ADAM_B2 ** ADAM_STEP

    def body(p_ref, w_ref, m_ref, v_ref, g_ref, d_ref, mo_ref, vo_ref):
        g = p_ref[0].astype(F32)
        for i in range(1, N_DEV):
            g = g + p_ref[i].astype(F32)
        mn = ADAM_B1 * m_ref[...] + (1.0 - ADAM_B1) * g
        vn = ADAM_B2 * v_ref[...] + (1.0 - ADAM_B2) * (g * g)
        g_ref[...] = g
        mo_ref[...] = mn
        vo_ref[...] = vn
        d_ref[...] = -ADAM_LR * ((mn / c1) / (jnp.sqrt(vn / c2) + ADAM_EPS) + ADAM_WD * w_ref[...])

    t2 = pl.BlockSpec((tr, c), lambda i: (i, 0))
    return _pcall(
        body, name=name, grid=(r // tr,),
        in_specs=[pl.BlockSpec((N_DEV, tr, c), lambda i: (0, i, 0)), t2, t2, t2],
        out_specs=[t2] * 4, out_shape=[jax.ShapeDtypeStruct((r, c), F32)] * 4,
        compiler_params=_params(("parallel",)),
    )(parts, w, m, v)


SMALL = ("norm1_g", "sgu_norm_g", "sgu_w", "sgu_b", "pool_w", "pool_scale", "mix_norm_g", "norm2_g", "conv_b", "final_g")
SHARDED = ("w_in", "w_o", "w_up", "conv_w", "w_down")
ORDER = ("norm1_g", "w_in", "sgu_norm_g", "sgu_w", "sgu_b", "pool_w", "pool_scale", "mix_norm_g", "w_o", "norm2_g",
         "w_up", "conv_w", "conv_b", "w_down", "final_g")


def _pack(tree):
    return jnp.concatenate([tree[n].reshape(-1) for n in SMALL]).reshape(-1, LANES)


def _unpack(flat, like):
    out, off = {}, 0
    flat = flat.reshape(-1)
    for n in SMALL:
        size = math.prod(like[n].shape)
        out[n] = flat[off:off + size].reshape(like[n].shape)
        off += size
    return out


def _block_diag(pw):
    z = jnp.zeros((W_C, W_C), pw.dtype)
    for g in range(4):
        z = z.at[g * 64:(g + 1) * 64, g * 64:(g + 1) * 64].set(pw[g])
    return z


def kernel(x, norm1_g, w_in, sgu_norm_g, sgu_w, sgu_b, pool_w, pool_scale, mix_norm_g, w_o, norm2_g, w_up, conv_w, conv_b, w_down, final_g, loss_target, m_norm1_g, m_w_in, m_sgu_norm_g, m_sgu_w, m_sgu_b, m_pool_w, m_pool_scale, m_mix_norm_g, m_w_o, m_norm2_g, m_w_up, m_conv_w, m_conv_b, m_w_down, m_final_g, v_norm1_g, v_w_in, v_sgu_norm_g, v_sgu_w, v_sgu_b, v_pool_w, v_pool_scale, v_mix_norm_g, v_w_o, v_norm2_g, v_w_up, v_conv_w, v_conv_b, v_w_down, v_final_g):
    weights = dict(norm1_g=norm1_g, w_in=w_in, sgu_norm_g=sgu_norm_g, sgu_w=sgu_w, sgu_b=sgu_b, pool_w=pool_w,
                   pool_scale=pool_scale, mix_norm_g=mix_norm_g, w_o=w_o, norm2_g=norm2_g, w_up=w_up, conv_w=conv_w,
                   conv_b=conv_b, w_down=w_down, final_g=final_g)
    mom = dict(norm1_g=m_norm1_g, w_in=m_w_in, sgu_norm_g=m_sgu_norm_g, sgu_w=m_sgu_w, sgu_b=m_sgu_b, pool_w=m_pool_w,
               pool_scale=m_pool_scale, mix_norm_g=m_mix_norm_g, w_o=m_w_o, norm2_g=m_norm2_g, w_up=m_w_up,
               conv_w=m_conv_w, conv_b=m_conv_b, w_down=m_w_down, final_g=m_final_g)
    var = dict(norm1_g=v_norm1_g, w_in=v_w_in, sgu_norm_g=v_sgu_norm_g, sgu_w=v_sgu_w, sgu_b=v_sgu_b, pool_w=v_pool_w,
               pool_scale=v_pool_scale, mix_norm_g=v_mix_norm_g, w_o=v_w_o, norm2_g=v_norm2_g, w_up=v_w_up,
               conv_w=v_conv_w, conv_b=v_conv_b, w_down=v_w_down, final_g=v_final_g)
    depth = w_in.shape[0]
    s = x.shape[1]
    xs = x.reshape(s, D_MODEL)
    tgt = loss_target.reshape(s, D_MODEL)

    assert depth >= 2
    (g_in0,) = _exchange([w_in[0].astype(_MXU)], "gather_w_in0", False)
    w_in0 = jnp.transpose(g_in0, (1, 0, 2)).reshape(D_MODEL, IN_COLS)
    gather_rest = _Exchange([w_in[1:].astype(_MXU), w_o.astype(_MXU), w_up.astype(_MXU), conv_w, w_down.astype(_MXU)], False)

    tril = jnp.tril(jnp.ones((CHUNK, CHUNK), bool))
    layers = []
    for l in range(depth):
        wbd = _block_diag(pool_w[l])
        layers.append(dict(
            g1=norm1_g[l][None], gn=sgu_norm_g[l][None], wm=jnp.where(tril[None], sgu_w[l], 0.0).astype(_MXU),
            bias=jnp.repeat(sgu_b[l].T, HEAD_DIM, axis=1),
            wbd=wbd.astype(_MXU), wbd_t=wbd.T.astype(_MXU), sc=pool_scale[l][None],
            gmix=mix_norm_g[l][None], g2=norm2_g[l][None]))
    layers[0].update(w_in=w_in0, w_in_t=w_in0.T)

    def place_gathered(g_in, g_o, g_up, g_cw, g_dn):
        full_in = jnp.transpose(g_in, (1, 2, 0, 3)).reshape(depth - 1, D_MODEL, IN_COLS)
        full_o = jnp.transpose(g_o, (1, 0, 2, 3)).reshape(depth, D_MODEL, D_MODEL)
        full_up = jnp.transpose(g_up, (1, 2, 0, 3)).reshape(depth, D_MODEL, 2 * D_FF)
        full_cw = jnp.transpose(g_cw, (1, 2, 0, 3)).reshape(depth, 3, 2 * D_FF)
        full_dn = jnp.transpose(g_dn, (1, 0, 2, 3)).reshape(depth, D_FF, D_MODEL)
        for l in range(depth):
            if l > 0:
                layers[l].update(w_in=full_in[l - 1], w_in_t=full_in[l - 1].T)
            layers[l].update(
                w_o=full_o[l], w_o_t=full_o[l].T, w_up=full_up[l], w_up_t=full_up[l].T,
                cw=jnp.concatenate([full_cw[l], conv_b[l][None], jnp.zeros((CW_ROWS - 4, 2 * D_FF), F32)], axis=0),
                w_dn=full_dn[l], w_dn_t=full_dn[l].T)

    saved = []
    cur = xs
    for l, p in enumerate(layers):
        a_in, qkv, p_in, h1 = _inproj_fwd(cur, p["g1"], p["w_in"], f"inproj_fwd{l}")
        y_a = _sgu_fwd(a_in, p["gn"], p["wm"], p["bias"], f"sgu_fwd{l}")
        y_b, cm, gathered_w = _attn_fwd(qkv, f"attn_fwd{l}", gather_rest if l == 0 else None)
        if l == 0:
            place_gathered(*gathered_w)
        y_c = _pool_fwd(p_in, p["wbd"], p["sc"], f"pool_fwd{l}")
        x_mid, yn = _mix_fwd(y_a, y_b, y_c, p["gmix"], p["w_o"], cur, f"mix_fwd{l}")
        z, h2 = _ffn_up_fwd(x_mid, p["g2"], p["w_up"], f"ffn_up_fwd{l}")
        x_out, act = _ffn_down_fwd(z, p["cw"], p["w_dn"], x_mid, f"ffn_down_fwd{l}")
        saved.append(dict(x_in=cur, a_in=a_in, qkv=qkv, p_in=p_in, h1=h1, y_a=y_a, y_b=y_b, cm=cm, y_c=y_c, x_mid=x_mid,
                          yn=yn, z=z, h2=h2, act=act))
        cur = x_out
    loss_part, dx, dg_final = _final_loss(cur, final_g[None], tgt, "final_loss")

    small = {n: [None] * depth for n in SMALL if n != "final_g"}
    big = {n: [None] * depth for n in SHARDED}
    early = [(n, l) for n in SHARDED for l in range(depth) if (n, l) != ("w_in", 0)]
    wire = lambda n, t: t if n == "conv_w" else t.astype(GRAD_WIRE)
    for l in reversed(range(depth)):
        p, sv = layers[l], saved[l]
        dzg, dzu, dcw = _ffn_down_bwd(dx, sv["z"], p["cw"], p["w_dn_t"], f"ffn_down_bwd{l}")
        big["w_down"][l] = _mm_tn(sv["act"], dx, f"dw_down{l}").reshape(N_DEV, D_FF // N_DEV, D_MODEL)
        dzg_b, dzu_b, dx_mid, dg2 = _ffn_up_bwd(dzg, dzu, p["cw"], p["w_up_t"], sv["x_mid"], p["g2"], dx, f"ffn_up_bwd{l}")
        dw_up = jnp.stack([_mm_tn(sv["h2"], dzg_b, f"dw_up_g{l}"), _mm_tn(sv["h2"], dzu_b, f"dw_up_u{l}")])
        big["w_up"][l] = jnp.transpose(dw_up.reshape(2, D_MODEL, N_DEV // 2, 2 * D_FF // N_DEV), (0, 2, 1, 3)).reshape(
            N_DEV, D_MODEL, 2 * D_FF // N_DEV)
        big["conv_w"][l] = jnp.transpose(dcw[0:3].reshape(3, N_DEV, 2 * D_FF // N_DEV), (1, 0, 2))
        small["conv_b"][l] = dcw[3]
        small["norm2_g"][l] = dg2[0]
        dya, dyb, dyc, dgmix = _mix_bwd(dx_mid, sv["y_a"], sv["y_b"], sv["y_c"], p["gmix"], p["w_o_t"], f"mix_bwd{l}")
        small["mix_norm_g"][l] = dgmix[0]
        big["w_o"][l] = _mm_tn(sv["yn"], dx_mid, f"dw_o{l}").reshape(N_DEV, D_MODEL // N_DEV, D_MODEL)
        dp, dwbd, dsc = _pool_bwd(sv["p_in"], dyc, p["wbd"], p["wbd_t"], p["sc"], f"pool_bwd{l}")
        small["pool_w"][l] = jnp.stack([dwbd[g * 64:(g + 1) * 64, g * 64:(g + 1) * 64] for g in range(4)])
        small["pool_scale"][l] = dsc[0]
        scatter_early = _Exchange([wire(n, big[n][ll]) for n, ll in early], True) if l == 0 else None
        dq, dk, dv, recv_early = _attn_bwd(sv["qkv"], sv["cm"], dyb, f"attn_bwd{l}", scatter_early)
        if l == 0:
            recv_early_all = recv_early
        da, dwm, dbias, dgn = _sgu_bwd(sv["a_in"], dya, p["gn"], p["wm"], p["bias"], f"sgu_bwd{l}")
        small["sgu_w"][l] = dwm
        small["sgu_b"][l] = jnp.sum(dbias.reshape(CHUNK, 4, HEAD_DIM), axis=-1).T
        small["sgu_norm_g"][l] = dgn[0]
        dx, dg1 = _inproj_bwd(da, dq, dk, dv, dp, p["w_in_t"], sv["x_in"], p["g1"], dx_mid, f"inproj_bwd{l}")
        small["norm1_g"][l] = dg1[0]
        pieces = (da, dq, dk, dv, dp)
        dw_in = jnp.concatenate([_mm_tn(sv["h1"], t, f"dw_in{i}_{l}") for i, t in enumerate(pieces)], axis=1)
        big["w_in"][l] = jnp.transpose(dw_in.reshape(D_MODEL, N_DEV, IN_COLS // N_DEV), (1, 0, 2))

    part = {n: jnp.stack(small[n]) for n in small}
    part["final_g"] = dg_final[0]
    packed = _pack(part)
    recv_in0, gathered = _exchange([wire("w_in", big["w_in"][0]), jnp.broadcast_to(packed, (N_DEV,) + packed.shape)],
                                   "scatter_last", True)
    recv = dict(zip(early, recv_early_all))
    recv[("w_in", 0)] = recv_in0

    out_g, out_d, out_m, out_v = {}, {}, {}, {}
    for n in SHARDED:
        res = [_adamw(recv[(n, l)], weights[n][l], mom[n][l], var[n][l], f"adamw_{n}{l}") for l in range(depth)]
        out_g[n], out_d[n], out_m[n], out_v[n] = (jnp.stack([r[i] for r in res]) for i in range(4))

    sg, sd, sm, sv_ = _adamw(gathered, _pack(weights), _pack(mom), _pack(var), "adamw_small")
    for tree, flat in ((out_g, sg), (out_d, sd), (out_m, sm), (out_v, sv_)):
        tree.update(_unpack(flat, weights))

    loss = lax.psum(loss_part[0, 0], ("x", "y", "c"))
    grad_x = dx.reshape(1, s, D_MODEL)
    return (loss, grad_x, *[out_g[n] for n in ORDER], *[out_d[n] for n in ORDER], *[out_m[n] for n in ORDER],
            *[out_v[n] for n in ORDER])
```

```python
import functools
import math

import jax
import jax.numpy as jnp
import numpy as np
from jax import lax
from jax.experimental import pallas as pl
from jax.experimental.pallas import tpu as pltpu

F32 = jnp.float32
BF16 = jnp.bfloat16
_MXU = jnp.bfloat16
GRAD_WIRE = jnp.bfloat16

D_MODEL = 1024
W_A = 256
W_B = 512
W_C = 256
HEAD_DIM = 64
IN_COLS = 2 * W_A + 3 * W_B + W_C
D_FF = 2816
CHUNK = 128
POOL_WINDOWS = (2, 4, 8, 16)
EPS = 1e-6
N_DEV = 8
LANES = 128
SUBLANES = 8
VMEM_LIMIT = 48 * 1024 * 1024

ADAM_LR = 0.001
ADAM_B1 = 0.9
ADAM_B2 = 0.999
ADAM_EPS = 1e-08
ADAM_WD = 0.01
ADAM_STEP = 10

INV_SQRT2 = 1.0 / math.sqrt(2.0)
INV_SQRT_2PI = 1.0 / math.sqrt(2.0 * math.pi)


def _pcall(body, **kw):
    return pl.pallas_call(body, **kw)


def _params(dims=None):
    return pltpu.CompilerParams(dimension_semantics=dims, vmem_limit_bytes=VMEM_LIMIT)


def _tile(n, pref, mult=SUBLANES):
    t = min(n, pref) // mult * mult
    while t >= mult:
        if n % t == 0:
            return t
        t -= mult
    return n


def _iota(shape, dim):
    return lax.broadcasted_iota(jnp.int32, shape, dim)


def _dot(a, b):
    return jnp.dot(a.astype(_MXU), b.astype(_MXU), preferred_element_type=F32)


def _dot_nt(a, b):
    return lax.dot_general(a.astype(_MXU), b.astype(_MXU), (((1,), (1,)), ((), ())), preferred_element_type=F32)


def _dot_tn(a, b):
    return lax.dot_general(a.astype(_MXU), b.astype(_MXU), (((0,), (0,)), ((), ())), preferred_element_type=F32)


def _split(x):
    hi = x.astype(BF16)
    lo = (x - hi.astype(F32)).astype(BF16)
    return hi, lo


def _dot_sel(x, sel):
    hi, lo = _split(x)
    return jnp.dot(hi, sel, preferred_element_type=F32) + jnp.dot(lo, sel, preferred_element_type=F32)


def _sel_dot(sel, x):
    hi, lo = _split(x)
    return jnp.dot(sel, hi, preferred_element_type=F32) + jnp.dot(sel, lo, preferred_element_type=F32)


def _group_mat(n):
    r = jnp.right_shift(_iota((n, n), 0), 6)
    c = jnp.right_shift(_iota((n, n), 1), 6)
    return (r == c).astype(BF16)


def _lane_group_mask(n, g):
    lane = _iota((1, n), 1)
    return (lane >= g * HEAD_DIM) & (lane < (g + 1) * HEAD_DIM)


def _gelu(a):
    return 0.5 * a * (1.0 + lax.erf(a * INV_SQRT2))


def _gelu_grad(a):
    return 0.5 * (1.0 + lax.erf(a * INV_SQRT2)) + a * jnp.exp(-0.5 * a * a) * INV_SQRT_2PI


def _rms(x):
    return lax.rsqrt(jnp.mean(x * x, axis=-1, keepdims=True) + EPS)


def _rms_bwd(x, g, dy):
    r = _rms(x)
    dyg = dy * g
    m2 = jnp.mean(dyg * x, axis=-1, keepdims=True)
    dx = r * dyg - x * (r * r * r) * m2
    dg = jnp.sum(dy * x * r, axis=0, keepdims=True)
    return dx, dg


def _shift_down(cur, prev8, k):
    rolled = pltpu.roll(cur, k, 0)
    row8 = _iota(prev8.shape, 0)
    top = jnp.where(row8 < k, pltpu.roll(prev8, k, 0), rolled[0:SUBLANES])
    return jnp.concatenate([top, rolled[SUBLANES:]], axis=0)


def _shift_up(cur, next8, k):
    n = cur.shape[0]
    rolled = pltpu.roll(cur, n - k, 0)
    row8 = _iota(next8.shape, 0)
    bot = jnp.where(row8 >= SUBLANES - k, pltpu.roll(next8, SUBLANES - k, 0), rolled[n - SUBLANES:])
    return jnp.concatenate([rolled[:n - SUBLANES], bot], axis=0)


def _mesh_pos():
    return lax.axis_index("x"), lax.axis_index("y"), lax.axis_index("c")


def _peer(x, y, c, k):
    px = 1 - x if k & 4 else x
    py = 1 - y if k & 2 else y
    pc = 1 - c if k & 1 else c
    return px, py, pc


class _Exchange:
    def __init__(self, arrs, scatter):
        self.arrs, self.scatter, self.n = list(arrs), scatter, len(arrs)
        self.out_shapes = [jax.ShapeDtypeStruct(a.shape if scatter else (N_DEV,) + a.shape, a.dtype) for a in arrs]
        self.specs = [pl.BlockSpec(memory_space=pl.ANY)] * self.n
        self.semaphores = [pltpu.SemaphoreType.DMA((self.n * (N_DEV - 1),)),
                           pltpu.SemaphoreType.DMA((self.n * (N_DEV - 1),)), pltpu.SemaphoreType.DMA((self.n,))]

    def _copies(self, ins, outs, sems):
        send, recv, loc = sems
        x, y, c = _mesh_pos()
        me = 4 * x + 2 * y + c
        src = (lambda a, idx: ins[a].at[idx]) if self.scatter else (lambda a, idx: ins[a])
        starts = [pltpu.make_async_copy(src(a, me), outs[a].at[me], loc.at[a]) for a in range(self.n)]
        waits = list(starts)
        for k in range(1, N_DEV):
            px, py, pc = _peer(x, y, c, k)
            pidx = 4 * px + 2 * py + pc
            for a in range(self.n):
                s = a * (N_DEV - 1) + k - 1
                common = dict(src_ref=src(a, pidx), send_sem=send.at[s], recv_sem=recv.at[s],
                              device_id=(px, py, pc), device_id_type=pl.DeviceIdType.MESH)
                starts.append(pltpu.make_async_remote_copy(dst_ref=outs[a].at[me], **common))
                waits.append(pltpu.make_async_remote_copy(dst_ref=outs[a].at[pidx], **common))
        return starts, waits

    def start(self, ins, outs, sems):
        for cp in self._copies(ins, outs, sems)[0]:
            cp.start()

    def wait(self, ins, outs, sems):
        for cp in self._copies(ins, outs, sems)[1]:
            cp.wait()


def _with_exchange(compute, ride, n_in, n_out, n_scratch, last_step):
    if ride is None:
        return compute
    nx = ride.n

    def body(*refs):
        ins, ride_in = refs[:n_in], refs[n_in:n_in + nx]
        outs = refs[n_in + nx:n_in + nx + n_out]
        ride_out = refs[n_in + nx + n_out:n_in + 2 * nx + n_out]
        scratch = refs[n_in + 2 * nx + n_out:n_in + 2 * nx + n_out + n_scratch]
        sems = refs[n_in + 2 * nx + n_out + n_scratch:]
        step = (pl.program_id(0), pl.program_id(1))

        @pl.when((step[0] == 0) & (step[1] == 0))
        def _():
            ride.start(ride_in, ride_out, sems)

        compute(*ins, *outs, *scratch)

        @pl.when((step[0] == last_step[0]) & (step[1] == last_step[1]))
        def _():
            ride.wait(ride_in, ride_out, sems)

    return body


def _exchange(arrs, name, scatter):
    ex = _Exchange(arrs, scatter)
    n = ex.n

    def body(*refs):
        ins, outs, sems = refs[:n], refs[n:2 * n], refs[2 * n:]
        ex.start(ins, outs, sems)
        ex.wait(ins, outs, sems)

    return _pcall(body, name=name, out_shape=ex.out_shapes, in_specs=ex.specs, out_specs=ex.specs,
                  scratch_shapes=ex.semaphores, compiler_params=pltpu.CompilerParams(has_side_effects=True))(*arrs)


def _mm_tn(a, b, name):
    s, m = a.shape
    n = b.shape[1]
    tm, tn, tk = _tile(m, 1408, LANES), _tile(n, 1408, LANES), _tile(s, 1024)

    def body(a_ref, b_ref, o_ref):
        @pl.when(pl.program_id(2) == 0)
        def _():
            o_ref[...] = jnp.zeros_like(o_ref)
        o_ref[...] += _dot_tn(a_ref[...], b_ref[...])

    return _pcall(
        body, name=name, grid=(m // tm, n // tn, s // tk),
        in_specs=[pl.BlockSpec((tk, tm), lambda i, j, k: (k, i)), pl.BlockSpec((tk, tn), lambda i, j, k: (k, j))],
        out_specs=pl.BlockSpec((tm, tn), lambda i, j, k: (i, j)),
        out_shape=jax.ShapeDtypeStruct((m, n), F32),
        compiler_params=_params(("parallel", "parallel", "arbitrary")),
    )(a, b)


def _inproj_fwd(x, g, w, name):
    s = x.shape[0]
    tm = _tile(s, 256)

    def body(x_ref, g_ref, w_ref, a_ref, qkv_ref, p_ref, h_ref):
        xv = x_ref[...]
        h = (xv * _rms(xv) * g_ref[...]).astype(_MXU)
        h_ref[...] = h
        a_ref[...] = jnp.dot(h, w_ref[:, 0:2 * W_A], preferred_element_type=F32)
        q = jnp.dot(h, w_ref[:, 2 * W_A:2 * W_A + W_B], preferred_element_type=F32)
        qkv_ref[:, 0:W_B] = (q * 0.125).astype(_MXU)
        kv = jnp.dot(h, w_ref[:, 2 * W_A + W_B:2 * W_A + 3 * W_B], preferred_element_type=F32)
        qkv_ref[:, W_B:3 * W_B] = kv.astype(_MXU)
        p_ref[...] = jnp.dot(h, w_ref[:, 2 * W_A + 3 * W_B:IN_COLS], preferred_element_type=F32)

    row = lambda n: pl.BlockSpec((tm, n), lambda i: (i, 0))
    return _pcall(
        body, name=name, grid=(s // tm,),
        in_specs=[row(D_MODEL), pl.BlockSpec((1, D_MODEL), lambda i: (0, 0)),
                  pl.BlockSpec((D_MODEL, IN_COLS), lambda i: (0, 0))],
        out_specs=[row(2 * W_A), row(3 * W_B), row(W_C), row(D_MODEL)],
        out_shape=[jax.ShapeDtypeStruct((s, 2 * W_A), F32), jax.ShapeDtypeStruct((s, 3 * W_B), _MXU),
                   jax.ShapeDtypeStruct((s, W_C), F32), jax.ShapeDtypeStruct((s, D_MODEL), _MXU)],
        compiler_params=_params(("parallel",)),
    )(x, g, w)


def _inproj_bwd(da, dq, dk, dv, dp, wt, x, g, dres, name):
    s = x.shape[0]
    tm = _tile(s, 256)

    def body(da_ref, dq_ref, dk_ref, dv_ref, dp_ref, wt_ref, x_ref, g_ref, dres_ref, dx_ref, dg_ref):
        dh = _dot(da_ref[...], wt_ref[0:2 * W_A, :])
        dh += _dot(dq_ref[...], wt_ref[2 * W_A:2 * W_A + W_B, :])
        dh += _dot(dk_ref[...], wt_ref[2 * W_A + W_B:2 * W_A + 2 * W_B, :])
        dh += _dot(dv_ref[...], wt_ref[2 * W_A + 2 * W_B:2 * W_A + 3 * W_B, :])
        dh += _dot(dp_ref[...], wt_ref[2 * W_A + 3 * W_B:IN_COLS, :])
        dx, dg = _rms_bwd(x_ref[...], g_ref[...], dh)
        dx_ref[...] = dres_ref[...] + dx

        @pl.when(pl.program_id(0) == 0)
        def _():
            dg_ref[...] = jnp.zeros_like(dg_ref)
        dg_ref[...] += dg

    row = lambda n: pl.BlockSpec((tm, n), lambda i: (i, 0))
    vec = pl.BlockSpec((1, D_MODEL), lambda i: (0, 0))
    return _pcall(
        body, name=name, grid=(s // tm,),
        in_specs=[row(2 * W_A), row(W_B), row(W_B), row(W_B), row(W_C),
                  pl.BlockSpec((IN_COLS, D_MODEL), lambda i: (0, 0)), row(D_MODEL), vec, row(D_MODEL)],
        out_specs=[row(D_MODEL), vec],
        out_shape=[jax.ShapeDtypeStruct((s, D_MODEL), F32), jax.ShapeDtypeStruct((1, D_MODEL), F32)],
        compiler_params=_params(("arbitrary",)),
    )(da, dq, dk, dv, dp, wt, x, g, dres)


def _sgu_core(a, gn, wm_ref, bias):
    ga = _gelu(a)
    u, v0 = ga[:, 0:W_A], ga[:, W_A:2 * W_A]
    r = lax.rsqrt(_dot_sel(v0 * v0, _group_mat(W_A)) * (1.0 / HEAD_DIM) + EPS)
    vn = v0 * r * gn
    sv = bias
    for h in range(W_A // HEAD_DIM):
        sv = sv + _dot(wm_ref[h], jnp.where(_lane_group_mask(W_A, h), vn, 0.0))
    return u, v0, r, vn, sv


def _sgu_fwd(a, gn, wm, bias, name):
    s = a.shape[0]

    def body(a_ref, gn_ref, wm_ref, b_ref, y_ref):
        u, _, _, _, sv = _sgu_core(a_ref[...], gn_ref[...], wm_ref, b_ref[...])
        y_ref[...] = u * sv

    return _pcall(
        body, name=name, grid=(s // CHUNK,),
        in_specs=[pl.BlockSpec((CHUNK, 2 * W_A), lambda i: (i, 0)), pl.BlockSpec((1, W_A), lambda i: (0, 0)),
                  pl.BlockSpec((4, CHUNK, CHUNK), lambda i: (0, 0, 0)), pl.BlockSpec((CHUNK, W_A), lambda i: (0, 0))],
        out_specs=pl.BlockSpec((CHUNK, W_A), lambda i: (i, 0)),
        out_shape=jax.ShapeDtypeStruct((s, W_A), F32),
        compiler_params=_params(("parallel",)),
    )(a, gn, wm, bias)


def _sgu_bwd(a, dy, gn, wm, bias, name):
    s = a.shape[0]

    def body(a_ref, dy_ref, gn_ref, wm_ref, b_ref, da_ref, dwm_ref, db_ref, dgn_ref):
        @pl.when(pl.program_id(0) == 0)
        def _():
            dwm_ref[...] = jnp.zeros_like(dwm_ref)
            db_ref[...] = jnp.zeros_like(db_ref)
            dgn_ref[...] = jnp.zeros_like(dgn_ref)

        av, gnv, dyv = a_ref[...], gn_ref[...], dy_ref[...]
        u, v0, r, vn, sv = _sgu_core(av, gnv, wm_ref, b_ref[...])
        du = dyv * sv
        ds = dyv * u
        db_ref[...] += ds
        tril = _iota((CHUNK, CHUNK), 1) <= _iota((CHUNK, CHUNK), 0)
        dvn = jnp.zeros_like(vn)
        for h in range(W_A // HEAD_DIM):
            dsm = jnp.where(_lane_group_mask(W_A, h), ds, 0.0)
            dwm_ref[h] += jnp.where(tril, _dot_nt(dsm, vn), 0.0)
            dvn = dvn + _dot_tn(wm_ref[h], dsm)
        dgn_ref[...] += jnp.sum(dvn * v0 * r, axis=0, keepdims=True)
        dvg = dvn * gnv
        m2 = _dot_sel(dvg * v0, _group_mat(W_A)) * (1.0 / HEAD_DIM)
        dv0 = r * dvg - v0 * (r * r * r) * m2
        gp = _gelu_grad(av)
        da_ref[:, 0:W_A] = du * gp[:, 0:W_A]
        da_ref[:, W_A:2 * W_A] = dv0 * gp[:, W_A:2 * W_A]

    return _pcall(
        body, name=name, grid=(s // CHUNK,),
        in_specs=[pl.BlockSpec((CHUNK, 2 * W_A), lambda i: (i, 0)), pl.BlockSpec((CHUNK, W_A), lambda i: (i, 0)),
                  pl.BlockSpec((1, W_A), lambda i: (0, 0)), pl.BlockSpec((4, CHUNK, CHUNK), lambda i: (0, 0, 0)),
                  pl.BlockSpec((CHUNK, W_A), lambda i: (0, 0))],
        out_specs=[pl.BlockSpec((CHUNK, 2 * W_A), lambda i: (i, 0)), pl.BlockSpec((4, CHUNK, CHUNK), lambda i: (0, 0, 0)),
                   pl.BlockSpec((CHUNK, W_A), lambda i: (0, 0)), pl.BlockSpec((1, W_A), lambda i: (0, 0))],
        out_shape=[jax.ShapeDtypeStruct((s, 2 * W_A), F32), jax.ShapeDtypeStruct((4, CHUNK, CHUNK), F32),
                   jax.ShapeDtypeStruct((CHUNK, W_A), F32), jax.ShapeDtypeStruct((1, W_A), F32)],
        compiler_params=_params(("arbitrary",)),
    )(a, dy, gn, wm, bias)


KBLK = 128
ROW_CHUNK = 64
MASKED_SCORE = -1e30


def _attn_fwd(qkv, name, ride=None):
    s = qkv.shape[0]
    tq = _tile(s, 256, KBLK)
    npairs = W_B // LANES
    assert s // KBLK <= LANES
    rides = [] if ride is None else [ride]

    def body(q_ref, k_ref, v_ref, o_ref, cm_ref, z_ref, zw_ref, sums_ref, carry_ref, hl_ref, a_ref):
        i = pl.program_id(1)
        q = q_ref[...]
        lane = _iota((1, LANES), 1)
        lane_lo = lane < HEAD_DIM
        hmask = (lane_lo, jnp.logical_not(lane_lo))
        tri2 = ((_iota((KBLK, 2 * KBLK), 0) >= _iota((KBLK, 2 * KBLK), 1))
                | (_iota((KBLK, 2 * KBLK), 1) >= KBLK)).astype(BF16)
        dmat = _iota((tq, KBLK), 1) - (_iota((tq, KBLK), 0) + i * tq)
        chunks = [slice(r, r + ROW_CHUNK) for r in range(0, tq, ROW_CHUNK)]
        heads = [slice(hh * LANES, (hh + 1) * LANES) for hh in range(2)]
        nk = (i + 1) * (tq // KBLK)

        cm_ref[...] = jnp.zeros_like(cm_ref)

        def before(b):
            return jnp.where((b >= 0) & (b < nk), -b * KBLK, jnp.iinfo(jnp.int32).min)

        def per_head(block):
            return jnp.concatenate([jnp.where(m, block, jnp.zeros_like(block)) for m in hmask], axis=0)

        def scores(b, p):
            ks = pl.multiple_of(jnp.maximum(b, 0) * KBLK, KBLK)
            z_ref[p] = _dot_nt(q, per_head(k_ref[pl.ds(ks, KBLK), :]))

        def logs(b, p):
            t = before(b)
            for hh in range(2):
                for rows in chunks:
                    z = jnp.where(dmat[rows] < t, z_ref[p, rows, heads[hh]], MASKED_SCORE)
                    zw_ref[p, rows, heads[hh]] = z
                    l = -(jnp.maximum(z, 0.0) + jnp.log(1.0 + jnp.exp(-jnp.abs(z))))
                    hi, lo = _split(l)
                    hl_ref[p, hh, rows, 0:KBLK] = hi
                    hl_ref[p, hh, rows, KBLK:2 * KBLK] = lo

        def sums(b, p):
            for hh in range(2):
                sums_ref[p, hh] = (jnp.dot(hl_ref[p, hh, :, 0:KBLK], tri2, preferred_element_type=F32)
                                   + jnp.dot(hl_ref[p, hh, :, KBLK:2 * KBLK], tri2, preferred_element_type=F32))

        def weights(b, p):
            pick = lane == b
            for hh in range(2):
                for rows in chunks:
                    c = carry_ref[hh, rows, :]
                    arg = zw_ref[p, rows, heads[hh]] + c + sums_ref[p, hh, rows, 0:KBLK]
                    a_ref[p, rows, heads[hh]] = jnp.exp(arg).astype(_MXU)
                    cm_ref[rows, heads[hh]] = jnp.where(pick, c, cm_ref[rows, heads[hh]])
                    carry_ref[hh, rows, :] = c + sums_ref[p, hh, rows, KBLK:2 * KBLK]

        def out(b, p, acc):
            ks = pl.multiple_of(jnp.minimum(b, nk - 1) * KBLK, KBLK)
            vb = v_ref[pl.ds(ks, KBLK), :]
            for hh in range(2):
                acc = acc + jnp.dot(a_ref[p, :, heads[hh]], jnp.where(hmask[hh], vb, jnp.zeros_like(vb)),
                                    preferred_element_type=F32)
            return acc

        def step(it, acc):
            for u in range(2):
                b = nk + 3 - (2 * it + u)
                p = 1 - u
                acc = out(b, p, acc)
                weights(b - 1, 1 - p)
                sums(b - 2, p)
                logs(b - 3, 1 - p)
                scores(b - 4, p)
            return acc

        z_ref[...] = jnp.zeros_like(z_ref)
        zw_ref[...] = jnp.full_like(zw_ref, MASKED_SCORE)
        for ref in (sums_ref, carry_ref, hl_ref, a_ref):
            ref[...] = jnp.zeros_like(ref)
        o_ref[...] = lax.fori_loop(0, (nk + 4) // 2, step, q.astype(F32) * 0.0)

    scratch = [pltpu.VMEM((2, tq, 2 * KBLK), F32), pltpu.VMEM((2, tq, 2 * KBLK), F32),
               pltpu.VMEM((2, 2, tq, 2 * KBLK), F32), pltpu.VMEM((2, tq, KBLK), F32),
               pltpu.VMEM((2, 2, tq, 2 * KBLK), BF16), pltpu.VMEM((2, tq, 2 * KBLK), _MXU)]
    res = _pcall(
        _with_exchange(body, ride, 3, 2, len(scratch), (npairs - 1, s // tq - 1)), name=name, grid=(npairs, s // tq),
        in_specs=[pl.BlockSpec((tq, LANES), lambda p, i: (i, p)),
                  pl.BlockSpec((s, LANES), lambda p, i: (0, npairs + p)),
                  pl.BlockSpec((s, LANES), lambda p, i: (0, 2 * npairs + p))] + [sp for r in rides for sp in r.specs],
        out_specs=[pl.BlockSpec((tq, LANES), lambda p, i: (i, p)), pl.BlockSpec((tq, 2 * LANES), lambda p, i: (i, p))]
                  + [sp for r in rides for sp in r.specs],
        out_shape=[jax.ShapeDtypeStruct((s, W_B), F32), jax.ShapeDtypeStruct((s, 2 * W_B), F32)]
                  + [sh for r in rides for sh in r.out_shapes],
        scratch_shapes=scratch + [sem for r in rides for sem in r.semaphores],
        compiler_params=_params(("arbitrary", "arbitrary")),
    )(qkv, qkv, qkv, *[a for r in rides for a in r.arrs])
    return res[0], res[1], list(res[2:])


def _attn_bwd(qkv, cm, do, name, ride=None):
    s = qkv.shape[0]
    tq = _tile(s, 256, KBLK)
    npairs = W_B // LANES
    rides = [] if ride is None else [ride]

    def body(q_ref, k_ref, v_ref, cm_ref, do_ref, dq_ref, dk_ref, dv_ref,
             z_ref, zw_ref, da_ref, g_ref, sig_ref, cum_ref, gp_ref, gcarry_ref, hl_ref, ghl_ref, a_ref, dz_ref,
             dkt_ref, dvt_ref):
        i = pl.program_id(1)

        @pl.when(i == 0)
        def _():
            dkt_ref[...] = jnp.zeros_like(dkt_ref)
            dvt_ref[...] = jnp.zeros_like(dvt_ref)

        q = q_ref[...]
        dov = do_ref[...].astype(_MXU)
        lane = _iota((1, LANES), 1)
        lane_lo = lane < HEAD_DIM
        hmask = (lane_lo, jnp.logical_not(lane_lo))
        tri = (_iota((KBLK, KBLK), 0) >= _iota((KBLK, KBLK), 1)).astype(BF16)
        prefix2 = ((_iota((KBLK, 2 * KBLK), 0) <= _iota((KBLK, 2 * KBLK), 1))
                   | (_iota((KBLK, 2 * KBLK), 1) >= KBLK)).astype(BF16)
        dmat = _iota((tq, KBLK), 1) - (_iota((tq, KBLK), 0) + i * tq)
        chunks = [slice(r, r + ROW_CHUNK) for r in range(0, tq, ROW_CHUNK)]
        heads = [slice(hh * LANES, (hh + 1) * LANES) for hh in range(2)]
        nk = (i + 1) * (tq // KBLK)

        def before(b):
            return jnp.where((b >= 0) & (b < nk), -b * KBLK, jnp.iinfo(jnp.int32).min)

        def block_rows(b):
            return pl.ds(pl.multiple_of(jnp.clip(b, 0, nk - 1) * KBLK, KBLK), KBLK)

        def per_head(block):
            return jnp.concatenate([jnp.where(m, block, jnp.zeros_like(block)) for m in hmask], axis=0)

        q_t = q.astype(F32).T.astype(_MXU)
        do_t = do_ref[...].T.astype(_MXU)
        feature_lo = _iota((LANES, KBLK), 0) < HEAD_DIM

        def own_features(side_by_side):
            return jnp.where(feature_lo, side_by_side[:, 0:KBLK], side_by_side[:, KBLK:2 * KBLK])

        def m1(b, p):
            z_ref[p] = _dot_nt(q, per_head(k_ref[block_rows(b), :]))

        def v1(b, p):
            t = before(b)
            for hh in range(2):
                for rows in chunks:
                    z = jnp.where(dmat[rows] < t, z_ref[p, rows, heads[hh]], MASKED_SCORE)
                    zw_ref[p, rows, heads[hh]] = z
                    l = -(jnp.maximum(z, 0.0) + jnp.log(1.0 + jnp.exp(-jnp.abs(z))))
                    hi, lo = _split(l)
                    hl_ref[p, hh, rows, 0:KBLK] = hi
                    hl_ref[p, hh, rows, KBLK:2 * KBLK] = lo

        def m2(b, p):
            for hh in range(2):
                cum_ref[p, hh] = (jnp.dot(hl_ref[p, hh, :, 0:KBLK], tri, preferred_element_type=F32)
                                  + jnp.dot(hl_ref[p, hh, :, KBLK:2 * KBLK], tri, preferred_element_type=F32))
            da_ref[p] = _dot_nt(dov, per_head(v_ref[block_rows(b), :]))

        def v2(b, p):
            pick = lane == b
            for hh in range(2):
                for rows in chunks:
                    c = jnp.sum(jnp.where(pick, cm_ref[rows, heads[hh]], 0.0), axis=1, keepdims=True)
                    z = zw_ref[p, rows, heads[hh]]
                    a = jnp.exp(z + c + cum_ref[p, hh, rows, :])
                    g = a * da_ref[p, rows, heads[hh]]
                    a_ref[p, rows, heads[hh]] = a.astype(_MXU)
                    g_ref[p, rows, heads[hh]] = g
                    hi, lo = _split(g)
                    ghl_ref[p, hh, rows, 0:KBLK] = hi
                    ghl_ref[p, hh, rows, KBLK:2 * KBLK] = lo
                    sig_ref[p, rows, heads[hh]] = jax.nn.sigmoid(z)

        def m3(b, p):
            for hh in range(2):
                gp_ref[p, hh] = (jnp.dot(ghl_ref[p, hh, :, 0:KBLK], prefix2, preferred_element_type=F32)
                                 + jnp.dot(ghl_ref[p, hh, :, KBLK:2 * KBLK], prefix2, preferred_element_type=F32))
            dvt_ref[jnp.clip(b, 0, nk - 1)] += own_features(jnp.dot(do_t, a_ref[p], preferred_element_type=F32))

        def v3(b, p):
            for hh in range(2):
                for rows in chunks:
                    gc = gcarry_ref[hh, rows, :]
                    upto = gc + gp_ref[p, hh, rows, 0:KBLK]
                    dz = g_ref[p, rows, heads[hh]] - sig_ref[p, rows, heads[hh]] * upto
                    dz_ref[p, rows, heads[hh]] = dz.astype(_MXU)
                    gcarry_ref[hh, rows, :] = gc + gp_ref[p, hh, rows, KBLK:2 * KBLK]

        def m4(b, p, dq):
            kb = k_ref[block_rows(b), :]
            for hh in range(2):
                dq = dq + jnp.dot(dz_ref[p, :, heads[hh]], jnp.where(hmask[hh], kb, jnp.zeros_like(kb)),
                                  preferred_element_type=F32)
            dkt_ref[jnp.clip(b, 0, nk - 1)] += own_features(jnp.dot(q_t, dz_ref[p], preferred_element_type=F32))
            return dq

        def step(it, dq):
            for u in range(2):
                j = 2 * it + u
                dq = m4(j - 6, u, dq)
                m3(j - 4, u)
                m2(j - 2, u)
                m1(j, u)
                v3(j - 5, 1 - u)
                v2(j - 3, 1 - u)
                v1(j - 1, 1 - u)
            return dq

        zw_ref[...] = jnp.full_like(zw_ref, MASKED_SCORE)
        for ref in (z_ref, da_ref, g_ref, sig_ref, cum_ref, gp_ref, gcarry_ref, hl_ref, ghl_ref, a_ref, dz_ref):
            ref[...] = jnp.zeros_like(ref)
        dq_ref[...] = lax.fori_loop(0, (nk + 6) // 2, step, do_ref[...] * 0.0) * 0.125

        @pl.when(i == s // tq - 1)
        def _():
            def untranspose(blk, carry):
                rows = pl.ds(pl.multiple_of(blk * KBLK, KBLK), KBLK)
                dk_ref[rows, :] = dkt_ref[blk].T
                dv_ref[rows, :] = dvt_ref[blk].T
                return carry
            lax.fori_loop(0, nblk, untranspose, 0)

    qspec = pl.BlockSpec((tq, LANES), lambda p, i: (i, p))
    nblk = s // KBLK
    full = pl.BlockSpec((s, LANES), lambda p, i: (0, p))
    scratch = ([pltpu.VMEM((2, tq, 2 * KBLK), F32)] * 5 + [pltpu.VMEM((2, 2, tq, KBLK), F32),
               pltpu.VMEM((2, 2, tq, 2 * KBLK), F32), pltpu.VMEM((2, tq, KBLK), F32)]
               + [pltpu.VMEM((2, 2, tq, 2 * KBLK), BF16)] * 2 + [pltpu.VMEM((2, tq, 2 * KBLK), _MXU)] * 2
               + [pltpu.VMEM((nblk, LANES, KBLK), F32)] * 2)
    res = _pcall(
        _with_exchange(body, ride, 5, 3, len(scratch), (npairs - 1, s // tq - 1)), name=name, grid=(npairs, s // tq),
        in_specs=[qspec, pl.BlockSpec((s, LANES), lambda p, i: (0, npairs + p)),
                  pl.BlockSpec((s, LANES), lambda p, i: (0, 2 * npairs + p)),
                  pl.BlockSpec((tq, 2 * LANES), lambda p, i: (i, p)), qspec] + [sp for r in rides for sp in r.specs],
        out_specs=[qspec, full, full] + [sp for r in rides for sp in r.specs],
        out_shape=[jax.ShapeDtypeStruct((s, W_B), F32)] * 3 + [sh for r in rides for sh in r.out_shapes],
        scratch_shapes=scratch + [sem for r in rides for sem in r.semaphores],
        compiler_params=_params(("arbitrary", "arbitrary")),
    )(qkv, qkv, qkv, cm, do, *[a for r in rides for a in r.arrs])
    return res[0], res[1], res[2], list(res[3:])


POOL_HALO = 128


def _pool_window_lane():
    lane = _iota((1, W_C), 1)
    w = jnp.where(lane < 64, POOL_WINDOWS[0], jnp.where(lane < 128, POOL_WINDOWS[1],
                  jnp.where(lane < 192, POOL_WINDOWS[2], POOL_WINDOWS[3])))
    return w.astype(F32)


def _pool_count(tm, i):
    pos = (_iota((tm, W_C), 0) + (i * tm + 1)).astype(F32)
    return jnp.minimum(pos, _pool_window_lane())


def _pool_centered(prev, cur, cnt):
    tm = cur.shape[0]
    xx = jnp.concatenate([prev, cur], axis=0)
    hi, lo = _split(xx)
    t = _iota((tm, tm + POOL_HALO), 0)
    cc = _iota((tm, tm + POOL_HALO), 1) - POOL_HALO
    wsum = jnp.zeros_like(cur)
    for g, w in enumerate(POOL_WINDOWS):
        band = ((cc <= t) & (cc > t - w)).astype(BF16)
        mg = _lane_group_mask(W_C, g)
        wsum += jnp.dot(band, jnp.where(mg, hi, jnp.zeros_like(hi)), preferred_element_type=F32)
        wsum += jnp.dot(band, jnp.where(mg, lo, jnp.zeros_like(lo)), preferred_element_type=F32)
    return wsum / cnt - cur


def _pool_fwd(p, wbd, sc, name):
    s = p.shape[0]
    tm = _tile(s, 256, POOL_HALO)
    r = tm // POOL_HALO

    def body(pp_ref, p_ref, w_ref, sc_ref, y_ref):
        i = pl.program_id(0)
        prev = jnp.where(i > 0, pp_ref[...], 0.0)
        d = _pool_centered(prev, p_ref[...], _pool_count(tm, i))
        y_ref[...] = _dot(d, w_ref[...]) * sc_ref[...]

    return _pcall(
        body, name=name, grid=(s // tm,),
        in_specs=[pl.BlockSpec((POOL_HALO, W_C), lambda i: (jnp.maximum(i * r - 1, 0), 0)),
                  pl.BlockSpec((tm, W_C), lambda i: (i, 0)), pl.BlockSpec((W_C, W_C), lambda i: (0, 0)),
                  pl.BlockSpec((1, W_C), lambda i: (0, 0))],
        out_specs=pl.BlockSpec((tm, W_C), lambda i: (i, 0)),
        out_shape=jax.ShapeDtypeStruct((s, W_C), F32),
        compiler_params=_params(("parallel",)),
    )(p, p, wbd, sc)


def _pool_bwd(p, dy, wbd, wbdt, sc, name):
    s = p.shape[0]
    tm = _tile(s, 256, POOL_HALO)
    r = tm // POOL_HALO
    nt = s // tm

    def body(pp_ref, p_ref, dy_ref, dyn_ref, w_ref, wt_ref, sc_ref, dp_ref, dw_ref, dsc_ref):
        i = pl.program_id(0)

        @pl.when(i == 0)
        def _():
            dw_ref[...] = jnp.zeros_like(dw_ref)
            dsc_ref[...] = jnp.zeros_like(dsc_ref)

        scv = sc_ref[...]
        cnt = _pool_count(tm, i)
        prev = jnp.where(i > 0, pp_ref[...], 0.0)
        d = _pool_centered(prev, p_ref[...], cnt)
        e = _dot(d, w_ref[...])
        dyv = dy_ref[...]
        de = dyv * scv
        dsc_ref[...] += jnp.sum(dyv * e, axis=0, keepdims=True)
        dw_ref[...] += _dot_tn(d, de)
        dd = _dot(de, wt_ref[...])
        ddn = jnp.where(i < nt - 1, _dot(dyn_ref[...] * scv, wt_ref[...]), 0.0)
        yy = jnp.concatenate([dd / cnt, ddn / _pool_window_lane()], axis=0)
        hi, lo = _split(yy)
        t = _iota((tm, tm + POOL_HALO), 0)
        cc = _iota((tm, tm + POOL_HALO), 1)
        acc = jnp.zeros_like(dd)
        for g, w in enumerate(POOL_WINDOWS):
            band = ((cc >= t) & (cc < t + w)).astype(BF16)
            mg = _lane_group_mask(W_C, g)
            acc += jnp.dot(band, jnp.where(mg, hi, jnp.zeros_like(hi)), preferred_element_type=F32)
            acc += jnp.dot(band, jnp.where(mg, lo, jnp.zeros_like(lo)), preferred_element_type=F32)
        dp_ref[...] = acc - dd

    tile = pl.BlockSpec((tm, W_C), lambda i: (i, 0))
    mat = pl.BlockSpec((W_C, W_C), lambda i: (0, 0))
    vec = pl.BlockSpec((1, W_C), lambda i: (0, 0))
    return _pcall(
        body, name=name, grid=(nt,),
        in_specs=[pl.BlockSpec((POOL_HALO, W_C), lambda i: (jnp.maximum(i * r - 1, 0), 0)), tile, tile,
                  pl.BlockSpec((POOL_HALO, W_C), lambda i: (jnp.minimum((i + 1) * r, s // POOL_HALO - 1), 0)),
                  mat, mat, vec],
        out_specs=[tile, mat, vec],
        out_shape=[jax.ShapeDtypeStruct((s, W_C), F32), jax.ShapeDtypeStruct((W_C, W_C), F32),
                   jax.ShapeDtypeStruct((1, W_C), F32)],
        compiler_params=_params(("arbitrary",)),
    )(p, p, dy, dy, wbd, wbdt, sc)


def _mix_cols(ya_ref, yb_ref, yc_ref, cb):
    if cb < 2:
        return ya_ref[:, cb * LANES:(cb + 1) * LANES]
    if cb < 6:
        return yb_ref[:, (cb - 2) * LANES:(cb - 1) * LANES]
    return yc_ref[:, (cb - 6) * LANES:(cb - 5) * LANES]


def _mix_fwd(ya, yb, yc, g, wo, x, name):
    s = x.shape[0]
    tm = _tile(s, 256)

    def body(ya_ref, yb_ref, yc_ref, g_ref, w_ref, x_ref, o_ref, yn_ref):
        sel = _group_mat(LANES)
        for cb in range(D_MODEL // LANES):
            y = _mix_cols(ya_ref, yb_ref, yc_ref, cb)
            r = lax.rsqrt(_dot_sel(y * y, sel) * (1.0 / HEAD_DIM) + EPS)
            yn_ref[:, cb * LANES:(cb + 1) * LANES] = (y * r * g_ref[:, cb * LANES:(cb + 1) * LANES]).astype(_MXU)
        o_ref[...] = x_ref[...] + jnp.dot(yn_ref[...], w_ref[...], preferred_element_type=F32)

    row = lambda n: pl.BlockSpec((tm, n), lambda i: (i, 0))
    return _pcall(
        body, name=name, grid=(s // tm,),
        in_specs=[row(W_A), row(W_B), row(W_C), pl.BlockSpec((1, D_MODEL), lambda i: (0, 0)),
                  pl.BlockSpec((D_MODEL, D_MODEL), lambda i: (0, 0)), row(D_MODEL)],
        out_specs=[row(D_MODEL), row(D_MODEL)],
        out_shape=[jax.ShapeDtypeStruct((s, D_MODEL), F32), jax.ShapeDtypeStruct((s, D_MODEL), _MXU)],
        compiler_params=_params(("parallel",)),
    )(ya, yb, yc, g, wo, x)


def _mix_bwd(dx, ya, yb, yc, g, wot, name):
    s = dx.shape[0]
    tm = _tile(s, 256)

    def body(dx_ref, ya_ref, yb_ref, yc_ref, g_ref, wt_ref, dya_ref, dyb_ref, dyc_ref, dg_ref):
        @pl.when(pl.program_id(0) == 0)
        def _():
            dg_ref[...] = jnp.zeros_like(dg_ref)

        dyn = _dot(dx_ref[...], wt_ref[...])
        sel = _group_mat(LANES)
        for cb in range(D_MODEL // LANES):
            cols = slice(cb * LANES, (cb + 1) * LANES)
            y = _mix_cols(ya_ref, yb_ref, yc_ref, cb)
            r = lax.rsqrt(_dot_sel(y * y, sel) * (1.0 / HEAD_DIM) + EPS)
            dyc_ = dyn[:, cols]
            dyg = dyc_ * g_ref[:, cols]
            m2 = _dot_sel(dyg * y, sel) * (1.0 / HEAD_DIM)
            dy = r * dyg - y * (r * r * r) * m2
            dg_ref[:, cols] += jnp.sum(dyc_ * y * r, axis=0, keepdims=True)
            if cb < 2:
                dya_ref[:, cb * LANES:(cb + 1) * LANES] = dy
            elif cb < 6:
                dyb_ref[:, (cb - 2) * LANES:(cb - 1) * LANES] = dy
            else:
                dyc_ref[:, (cb - 6) * LANES:(cb - 5) * LANES] = dy

    row = lambda n: pl.BlockSpec((tm, n), lambda i: (i, 0))
    vec = pl.BlockSpec((1, D_MODEL), lambda i: (0, 0))
    return _pcall(
        body, name=name, grid=(s // tm,),
        in_specs=[row(D_MODEL), row(W_A), row(W_B), row(W_C), vec, pl.BlockSpec((D_MODEL, D_MODEL), lambda i: (0, 0))],
        out_specs=[row(W_A), row(W_B), row(W_C), vec],
        out_shape=[jax.ShapeDtypeStruct((s, W_A), F32), jax.ShapeDtypeStruct((s, W_B), F32),
                   jax.ShapeDtypeStruct((s, W_C), F32), jax.ShapeDtypeStruct((1, D_MODEL), F32)],
        compiler_params=_params(("arbitrary",)),
    )(dx, ya, yb, yc, g, wot)


FFN_CHUNK = 1408
FFN_ROWS = 256
N_CHUNKS = D_FF // FFN_CHUNK
CW_ROWS = 8


def _ffn_up_fwd(x, g, w, name):
    s = x.shape[0]
    n = w.shape[1]
    tm, tn = _tile(s, 512), _tile(n, 2816, LANES)

    def body(x_ref, g_ref, w_ref, z_ref, h_ref):
        @pl.when(pl.program_id(1) == 0)
        def _():
            xv = x_ref[...]
            h_ref[...] = (xv * _rms(xv) * g_ref[...]).astype(_MXU)
        z_ref[...] = jnp.dot(h_ref[...], w_ref[...], preferred_element_type=F32)

    return _pcall(
        body, name=name, grid=(s // tm, n // tn),
        in_specs=[pl.BlockSpec((tm, D_MODEL), lambda i, j: (i, 0)), pl.BlockSpec((1, D_MODEL), lambda i, j: (0, 0)),
                  pl.BlockSpec((D_MODEL, tn), lambda i, j: (0, j))],
        out_specs=[pl.BlockSpec((tm, tn), lambda i, j: (i, j)), pl.BlockSpec((tm, D_MODEL), lambda i, j: (i, 0))],
        out_shape=[jax.ShapeDtypeStruct((s, n), F32), jax.ShapeDtypeStruct((s, D_MODEL), _MXU)],
        compiler_params=_params(("parallel", "arbitrary")),
    )(x, g, w)


def _conv(cur, prev8, cw_ref):
    s1 = _shift_down(cur, prev8, 1)
    s2 = _shift_down(cur, prev8, 2)
    zc = cw_ref[3:4, :] + s2 * cw_ref[0:1, :]
    zc = zc + s1 * cw_ref[1:2, :]
    zc = zc + cur * cw_ref[2:3, :]
    return zc, s1, s2


def _halo_specs(tm, s):
    r = tm // SUBLANES
    prev = lambda off: pl.BlockSpec((SUBLANES, FFN_CHUNK), lambda i, j: (jnp.maximum(i * r - 1, 0), j + off))
    nxt = lambda off: pl.BlockSpec((SUBLANES, FFN_CHUNK), lambda i, j: (jnp.minimum((i + 1) * r, s // SUBLANES - 1), j + off))
    return prev, nxt


def _ffn_down_fwd(z, cw, wd, x, name):
    s = x.shape[0]
    tm = _tile(s, FFN_ROWS)
    prev, _ = _halo_specs(tm, s)

    def body(zg_ref, zu_ref, pg_ref, pu_ref, cg_ref, cu_ref, w_ref, x_ref, o_ref, act_ref, acc_ref):
        i, j = pl.program_id(0), pl.program_id(1)
        first = i > 0
        zg, _, _ = _conv(zg_ref[...], jnp.where(first, pg_ref[...], 0.0), cg_ref)
        zu, _, _ = _conv(zu_ref[...], jnp.where(first, pu_ref[...], 0.0), cu_ref)
        act = (zg * jax.nn.sigmoid(zg) * zu).astype(_MXU)
        act_ref[...] = act

        @pl.when(j == 0)
        def _():
            acc_ref[...] = x_ref[...]
        acc_ref[...] += jnp.dot(act, w_ref[...], preferred_element_type=F32)

        @pl.when(j == N_CHUNKS - 1)
        def _():
            o_ref[...] = acc_ref[...]

    zt = lambda off: pl.BlockSpec((tm, FFN_CHUNK), lambda i, j: (i, j + off))
    cwt = lambda off: pl.BlockSpec((CW_ROWS, FFN_CHUNK), lambda i, j: (0, j + off))
    return _pcall(
        body, name=name, grid=(s // tm, N_CHUNKS),
        in_specs=[zt(0), zt(N_CHUNKS), prev(0), prev(N_CHUNKS), cwt(0), cwt(N_CHUNKS),
                  pl.BlockSpec((FFN_CHUNK, D_MODEL), lambda i, j: (j, 0)), pl.BlockSpec((tm, D_MODEL), lambda i, j: (i, 0))],
        out_specs=[pl.BlockSpec((tm, D_MODEL), lambda i, j: (i, 0)), pl.BlockSpec((tm, FFN_CHUNK), lambda i, j: (i, j))],
        out_shape=[jax.ShapeDtypeStruct((s, D_MODEL), F32), jax.ShapeDtypeStruct((s, D_FF), _MXU)],
        scratch_shapes=[pltpu.VMEM((tm, D_MODEL), F32)],
        compiler_params=_params(("parallel", "arbitrary")),
    )(z, z, z, z, cw, cw, wd, x)


def _ffn_down_bwd(dx, z, cw, wdt, name):
    s = dx.shape[0]
    tm = _tile(s, FFN_ROWS)

    def body(dx_ref, zg_ref, zu_ref, pg_ref, pu_ref, cg_ref, cu_ref, wt_ref, dg_ref, du_ref, dcg_ref, dcu_ref):
        i = pl.program_id(1)
        first = i > 0

        @pl.when(i == 0)
        def _():
            dcg_ref[...] = jnp.zeros_like(dcg_ref)
            dcu_ref[...] = jnp.zeros_like(dcu_ref)

        dact = _dot(dx_ref[...], wt_ref[...])
        zg, g1, g2 = _conv(zg_ref[...], jnp.where(first, pg_ref[...], 0.0), cg_ref)
        zu, u1, u2 = _conv(zu_ref[...], jnp.where(first, pu_ref[...], 0.0), cu_ref)
        sg = jax.nn.sigmoid(zg)
        silu = zg * sg
        dzu = dact * silu
        dzg = dact * zu * (sg * (1.0 + zg * (1.0 - sg)))
        dg_ref[...] = dzg
        du_ref[...] = dzu
        for ref, dzc, cur, s1, s2 in ((dcg_ref, dzg, zg_ref[...], g1, g2), (dcu_ref, dzu, zu_ref[...], u1, u2)):
            ref[0:1, :] += jnp.sum(dzc * s2, axis=0, keepdims=True)
            ref[1:2, :] += jnp.sum(dzc * s1, axis=0, keepdims=True)
            ref[2:3, :] += jnp.sum(dzc * cur, axis=0, keepdims=True)
            ref[3:4, :] += jnp.sum(dzc, axis=0, keepdims=True)

    zt = lambda off: pl.BlockSpec((tm, FFN_CHUNK), lambda j, i: (i, j + off))
    r = tm // SUBLANES
    pv = lambda off: pl.BlockSpec((SUBLANES, FFN_CHUNK), lambda j, i: (jnp.maximum(i * r - 1, 0), j + off))
    cwt = lambda off: pl.BlockSpec((CW_ROWS, FFN_CHUNK), lambda j, i: (0, j + off))
    out_t = pl.BlockSpec((tm, FFN_CHUNK), lambda j, i: (i, j))
    dc_t = pl.BlockSpec((CW_ROWS, FFN_CHUNK), lambda j, i: (0, j))
    dzg, dzu, dcg, dcu = _pcall(
        body, name=name, grid=(N_CHUNKS, s // tm),
        in_specs=[pl.BlockSpec((tm, D_MODEL), lambda j, i: (i, 0)), zt(0), zt(N_CHUNKS), pv(0), pv(N_CHUNKS),
                  cwt(0), cwt(N_CHUNKS), pl.BlockSpec((D_MODEL, FFN_CHUNK), lambda j, i: (0, j))],
        out_specs=[out_t, out_t, dc_t, dc_t],
        out_shape=[jax.ShapeDtypeStruct((s, D_FF), F32), jax.ShapeDtypeStruct((s, D_FF), F32),
                   jax.ShapeDtypeStruct((CW_ROWS, D_FF), F32), jax.ShapeDtypeStruct((CW_ROWS, D_FF), F32)],
        compiler_params=_params(("parallel", "arbitrary")),
    )(dx, z, z, z, z, cw, cw, wdt)
    return dzg, dzu, jnp.concatenate([dcg, dcu], axis=1)


def _ffn_up_bwd(dzg, dzu, cw, wut, x, g, dres, name):
    s = x.shape[0]
    tm = _tile(s, FFN_ROWS)
    _, nxt = _halo_specs(tm, s)
    nt = s // tm

    def body(dg_ref, du_ref, ng_ref, nu_ref, cg_ref, cu_ref, wg_ref, wu_ref, x_ref, g_ref, dres_ref,
             dzg_ref, dzu_ref, dx_ref, dgn_ref, acc_ref):
        i, j = pl.program_id(0), pl.program_id(1)
        last = i < nt - 1

        def conv_bwd(cur, nxt8, cw_ref):
            up1 = _shift_up(cur, nxt8, 1)
            up2 = _shift_up(cur, nxt8, 2)
            return cur * cw_ref[2:3, :] + up1 * cw_ref[1:2, :] + up2 * cw_ref[0:1, :]

        dzg_ = conv_bwd(dg_ref[...], jnp.where(last, ng_ref[...], 0.0), cg_ref).astype(_MXU)
        dzu_ = conv_bwd(du_ref[...], jnp.where(last, nu_ref[...], 0.0), cu_ref).astype(_MXU)
        dzg_ref[...] = dzg_
        dzu_ref[...] = dzu_

        @pl.when(j == 0)
        def _():
            acc_ref[...] = jnp.zeros_like(acc_ref)
        acc_ref[...] += (jnp.dot(dzg_, wg_ref[...], preferred_element_type=F32)
                         + jnp.dot(dzu_, wu_ref[...], preferred_element_type=F32))

        @pl.when((i == 0) & (j == 0))
        def _():
            dgn_ref[...] = jnp.zeros_like(dgn_ref)

        @pl.when(j == N_CHUNKS - 1)
        def _():
            dx, dgn = _rms_bwd(x_ref[...], g_ref[...], acc_ref[...])
            dx_ref[...] = dres_ref[...] + dx
            dgn_ref[...] += dgn

    zt = pl.BlockSpec((tm, FFN_CHUNK), lambda i, j: (i, j))
    cwt = lambda off: pl.BlockSpec((CW_ROWS, FFN_CHUNK), lambda i, j: (0, j + off))
    wt = lambda off: pl.BlockSpec((FFN_CHUNK, D_MODEL), lambda i, j: (j + off, 0))
    row = pl.BlockSpec((tm, D_MODEL), lambda i, j: (i, 0))
    vec = pl.BlockSpec((1, D_MODEL), lambda i, j: (0, 0))
    return _pcall(
        body, name=name, grid=(nt, N_CHUNKS),
        in_specs=[zt, zt, nxt(0), nxt(0), cwt(0), cwt(N_CHUNKS), wt(0), wt(N_CHUNKS), row, vec, row],
        out_specs=[zt, zt, row, vec],
        out_shape=[jax.ShapeDtypeStruct((s, D_FF), _MXU), jax.ShapeDtypeStruct((s, D_FF), _MXU),
                   jax.ShapeDtypeStruct((s, D_MODEL), F32), jax.ShapeDtypeStruct((1, D_MODEL), F32)],
        scratch_shapes=[pltpu.VMEM((tm, D_MODEL), F32)],
        compiler_params=_params(("arbitrary", "arbitrary")),
    )(dzg, dzu, dzg, dzu, cw, cw, wut, wut, x, g, dres)


def _final_loss(x, g, tgt, name):
    s = x.shape[0]
    tm = _tile(s, 256)

    def body(x_ref, g_ref, t_ref, loss_ref, dx_ref, dg_ref):
        @pl.when(pl.program_id(0) == 0)
        def _():
            loss_ref[...] = jnp.zeros_like(loss_ref)
            dg_ref[...] = jnp.zeros_like(dg_ref)

        xv, gv = x_ref[...], g_ref[...]
        err = xv * _rms(xv) * gv - t_ref[...]
        per_tok = jnp.mean(err * err, axis=-1, keepdims=True)
        loss_ref[...] += 0.5 * jnp.sum(per_tok, axis=0, keepdims=True)
        dx, dg = _rms_bwd(xv, gv, err * (1.0 / D_MODEL))
        dx_ref[...] = dx
        dg_ref[...] += dg

    row = pl.BlockSpec((tm, D_MODEL), lambda i: (i, 0))
    vec = pl.BlockSpec((1, D_MODEL), lambda i: (0, 0))
    return _pcall(
        body, name=name, grid=(s // tm,),
        in_specs=[row, vec, row], out_specs=[pl.BlockSpec((1, 1), lambda i: (0, 0)), row, vec],
        out_shape=[jax.ShapeDtypeStruct((1, 1), F32), jax.ShapeDtypeStruct((s, D_MODEL), F32),
                   jax.ShapeDtypeStruct((1, D_MODEL), F32)],
        compiler_params=_params(("arbitrary",)),
    )(x, g, tgt)


def _adamw(parts, w, m, v, name):
    r, c = w.shape
    tr = _tile(r, 256)
    c1 = 1.0 - ADAM_B1 ** ADAM_STEP
    c2 = 1.0 - ADAM_B2 ** ADAM_STEP

    def body(p_ref, w_ref, m_ref, v_ref, g_ref, d_ref, mo_ref, vo_ref):
        g = p_ref[0].astype(F32)
        for i in range(1, N_DEV):
            g = g + p_ref[i].astype(F32)
        mn = ADAM_B1 * m_ref[...] + (1.0 - ADAM_B1) * g
        vn = ADAM_B2 * v_ref[...] + (1.0 - ADAM_B2) * (g * g)
        g_ref[...] = g
        mo_ref[...] = mn
        vo_ref[...] = vn
        d_ref[...] = -ADAM_LR * ((mn / c1) / (jnp.sqrt(vn / c2) + ADAM_EPS) + ADAM_WD * w_ref[...])

    t2 = pl.BlockSpec((tr, c), lambda i: (i, 0))
    return _pcall(
        body, name=name, grid=(r // tr,),
        in_specs=[pl.BlockSpec((N_DEV, tr, c), lambda i: (0, i, 0)), t2, t2, t2],
        out_specs=[t2] * 4, out_shape=[jax.ShapeDtypeStruct((r, c), F32)] * 4,
        compiler_params=_params(("parallel",)),
    )(parts, w, m, v)


SMALL = ("norm1_g", "sgu_norm_g", "sgu_w", "sgu_b", "pool_w", "pool_scale", "mix_norm_g", "norm2_g", "conv_b", "final_g")
SHARDED = ("w_in", "w_o", "w_up", "conv_w", "w_down")
ORDER = ("norm1_g", "w_in", "sgu_norm_g", "sgu_w", "sgu_b", "pool_w", "pool_scale", "mix_norm_g", "w_o", "norm2_g",
         "w_up", "conv_w", "conv_b", "w_down", "final_g")


def _pack(tree):
    return jnp.concatenate([tree[n].reshape(-1) for n in SMALL]).reshape(-1, LANES)


def _unpack(flat, like):
    out, off = {}, 0
    flat = flat.reshape(-1)
    for n in SMALL:
        size = math.prod(like[n].shape)
        out[n] = flat[off:off + size].reshape(like[n].shape)
        off += size
    return out


def _block_diag(pw):
    z = jnp.zeros((W_C, W_C), pw.dtype)
    for g in range(4):
        z = z.at[g * 64:(g + 1) * 64, g * 64:(g + 1) * 64].set(pw[g])
    return z


def kernel(x, norm1_g, w_in, sgu_norm_g, sgu_w, sgu_b, pool_w, pool_scale, mix_norm_g, w_o, norm2_g, w_up, conv_w, conv_b, w_down, final_g, loss_target, m_norm1_g, m_w_in, m_sgu_norm_g, m_sgu_w, m_sgu_b, m_pool_w, m_pool_scale, m_mix_norm_g, m_w_o, m_norm2_g, m_w_up, m_conv_w, m_conv_b, m_w_down, m_final_g, v_norm1_g, v_w_in, v_sgu_norm_g, v_sgu_w, v_sgu_b, v_pool_w, v_pool_scale, v_mix_norm_g, v_w_o, v_norm2_g, v_w_up, v_conv_w, v_conv_b, v_w_down, v_final_g):
    weights = dict(norm1_g=norm1_g, w_in=w_in, sgu_norm_g=sgu_norm_g, sgu_w=sgu_w, sgu_b=sgu_b, pool_w=pool_w,
                   pool_scale=pool_scale, mix_norm_g=mix_norm_g, w_o=w_o, norm2_g=norm2_g, w_up=w_up, conv_w=conv_w,
                   conv_b=conv_b, w_down=w_down, final_g=final_g)
    mom = dict(norm1_g=m_norm1_g, w_in=m_w_in, sgu_norm_g=m_sgu_norm_g, sgu_w=m_sgu_w, sgu_b=m_sgu_b, pool_w=m_pool_w,
               pool_scale=m_pool_scale, mix_norm_g=m_mix_norm_g, w_o=m_w_o, norm2_g=m_norm2_g, w_up=m_w_up,
               conv_w=m_conv_w, conv_b=m_conv_b, w_down=m_w_down, final_g=m_final_g)
    var = dict(norm1_g=v_norm1_g, w_in=v_w_in, sgu_norm_g=v_sgu_norm_g, sgu_w=v_sgu_w, sgu_b=v_sgu_b, pool_w=v_pool_w,
               pool_scale=v_pool_scale, mix_norm_g=v_mix_norm_g, w_o=v_w_o, norm2_g=v_norm2_g, w_up=v_w_up,
               conv_w=v_conv_w, conv_b=v_conv_b, w_down=v_w_down, final_g=v_final_g)
    depth = w_in.shape[0]
    s = x.shape[1]
    xs = x.reshape(s, D_MODEL)
    tgt = loss_target.reshape(s, D_MODEL)

    assert depth >= 2
    (g_in0,) = _exchange([w_in[0].astype(_MXU)], "gather_w_in0", False)
    w_in0 = jnp.transpose(g_in0, (1, 0, 2)).reshape(D_MODEL, IN_COLS)
    gather_rest = _Exchange([w_in[1:].astype(_MXU), w_o.astype(_MXU), w_up.astype(_MXU), conv_w, w_down.astype(_MXU)], False)

    tril = jnp.tril(jnp.ones((CHUNK, CHUNK), bool))
    layers = []
    for l in range(depth):
        wbd = _block_diag(pool_w[l])
        layers.append(dict(
            g1=norm1_g[l][None], gn=sgu_norm_g[l][None], wm=jnp.where(tril[None], sgu_w[l], 0.0).astype(_MXU),
            bias=jnp.repeat(sgu_b[l].T, HEAD_DIM, axis=1),
            wbd=wbd.astype(_MXU), wbd_t=wbd.T.astype(_MXU), sc=pool_scale[l][None],
            gmix=mix_norm_g[l][None], g2=norm2_g[l][None]))
    layers[0].update(w_in=w_in0, w_in_t=w_in0.T)

    def place_gathered(g_in, g_o, g_up, g_cw, g_dn):
        full_in = jnp.transpose(g_in, (1, 2, 0, 3)).reshape(depth - 1, D_MODEL, IN_COLS)
        full_o = jnp.transpose(g_o, (1, 0, 2, 3)).reshape(depth, D_MODEL, D_MODEL)
        full_up = jnp.transpose(g_up, (1, 2, 0, 3)).reshape(depth, D_MODEL, 2 * D_FF)
        full_cw = jnp.transpose(g_cw, (1, 2, 0, 3)).reshape(depth, 3, 2 * D_FF)
        full_dn = jnp.transpose(g_dn, (1, 0, 2, 3)).reshape(depth, D_FF, D_MODEL)
        for l in range(depth):
            if l > 0:
                layers[l].update(w_in=full_in[l - 1], w_in_t=full_in[l - 1].T)
            layers[l].update(
                w_o=full_o[l], w_o_t=full_o[l].T, w_up=full_up[l], w_up_t=full_up[l].T,
                cw=jnp.concatenate([full_cw[l], conv_b[l][None], jnp.zeros((CW_ROWS - 4, 2 * D_FF), F32)], axis=0),
                w_dn=full_dn[l], w_dn_t=full_dn[l].T)

    saved = []
    cur = xs
    for l, p in enumerate(layers):
        a_in, qkv, p_in, h1 = _inproj_fwd(cur, p["g1"], p["w_in"], f"inproj_fwd{l}")
        y_a = _sgu_fwd(a_in, p["gn"], p["wm"], p["bias"], f"sgu_fwd{l}")
        y_b, cm, gathered_w = _attn_fwd(qkv, f"attn_fwd{l}", gather_rest if l == 0 else None)
        if l == 0:
            place_gathered(*gathered_w)
        y_c = _pool_fwd(p_in, p["wbd"], p["sc"], f"pool_fwd{l}")
        x_mid, yn = _mix_fwd(y_a, y_b, y_c, p["gmix"], p["w_o"], cur, f"mix_fwd{l}")
        z, h2 = _ffn_up_fwd(x_mid, p["g2"], p["w_up"], f"ffn_up_fwd{l}")
        x_out, act = _ffn_down_fwd(z, p["cw"], p["w_dn"], x_mid, f"ffn_down_fwd{l}")
        saved.append(dict(x_in=cur, a_in=a_in, qkv=qkv, p_in=p_in, h1=h1, y_a=y_a, y_b=y_b, cm=cm, y_c=y_c, x_mid=x_mid,
                          yn=yn, z=z, h2=h2, act=act))
        cur = x_out
    loss_part, dx, dg_final = _final_loss(cur, final_g[None], tgt, "final_loss")

    small = {n: [None] * depth for n in SMALL if n != "final_g"}
    big = {n: [None] * depth for n in SHARDED}
    early = [(n, l) for n in SHARDED for l in range(depth) if (n, l) != ("w_in", 0)]
    wire = lambda n, t: t if n == "conv_w" else t.astype(GRAD_WIRE)
    for l in reversed(range(depth)):
        p, sv = layers[l], saved[l]
        dzg, dzu, dcw = _ffn_down_bwd(dx, sv["z"], p["cw"], p["w_dn_t"], f"ffn_down_bwd{l}")
        big["w_down"][l] = _mm_tn(sv["act"], dx, f"dw_down{l}").reshape(N_DEV, D_FF // N_DEV, D_MODEL)
        dzg_b, dzu_b, dx_mid, dg2 = _ffn_up_bwd(dzg, dzu, p["cw"], p["w_up_t"], sv["x_mid"], p["g2"], dx, f"ffn_up_bwd{l}")
        dw_up = jnp.stack([_mm_tn(sv["h2"], dzg_b, f"dw_up_g{l}"), _mm_tn(sv["h2"], dzu_b, f"dw_up_u{l}")])
        big["w_up"][l] = jnp.transpose(dw_up.reshape(2, D_MODEL, N_DEV // 2, 2 * D_FF // N_DEV), (0, 2, 1, 3)).reshape(
            N_DEV, D_MODEL, 2 * D_FF // N_DEV)
        big["conv_w"][l] = jnp.transpose(dcw[0:3].reshape(3, N_DEV, 2 * D_FF // N_DEV), (1, 0, 2))
        small["conv_b"][l] = dcw[3]
        small["norm2_g"][l] = dg2[0]
        dya, dyb, dyc, dgmix = _mix_bwd(dx_mid, sv["y_a"], sv["y_b"], sv["y_c"], p["gmix"], p["w_o_t"], f"mix_bwd{l}")
        small["mix_norm_g"][l] = dgmix[0]
        big["w_o"][l] = _mm_tn(sv["yn"], dx_mid, f"dw_o{l}").reshape(N_DEV, D_MODEL // N_DEV, D_MODEL)
        dp, dwbd, dsc = _pool_bwd(sv["p_in"], dyc, p["wbd"], p["wbd_t"], p["sc"], f"pool_bwd{l}")
        small["pool_w"][l] = jnp.stack([dwbd[g * 64:(g + 1) * 64, g * 64:(g + 1) * 64] for g in range(4)])
        small["pool_scale"][l] = dsc[0]
        scatter_early = _Exchange([wire(n, big[n][ll]) for n, ll in early], True) if l == 0 else None
        dq, dk, dv, recv_early = _attn_bwd(sv["qkv"], sv["cm"], dyb, f"attn_bwd{l}", scatter_early)
        if l == 0:
            recv_early_all = recv_early
        da, dwm, dbias, dgn = _sgu_bwd(sv["a_in"], dya, p["gn"], p["wm"], p["bias"], f"sgu_bwd{l}")
        small["sgu_w"][l] = dwm
        small["sgu_b"][l] = jnp.sum(dbias.reshape(CHUNK, 4, HEAD_DIM), axis=-1).T
        small["sgu_norm_g"][l] = dgn[0]
        dx, dg1 = _inproj_bwd(da, dq, dk, dv, dp, p["w_in_t"], sv["x_in"], p["g1"], dx_mid, f"inproj_bwd{l}")
        small["norm1_g"][l] = dg1[0]
        pieces = (da, dq, dk, dv, dp)
        dw_in = jnp.concatenate([_mm_tn(sv["h1"], t, f"dw_in{i}_{l}") for i, t in enumerate(pieces)], axis=1)
        big["w_in"][l] = jnp.transpose(dw_in.reshape(D_MODEL, N_DEV, IN_COLS // N_DEV), (1, 0, 2))

    part = {n: jnp.stack(small[n]) for n in small}
    part["final_g"] = dg_final[0]
    packed = _pack(part)
    recv_in0, gathered = _exchange([wire("w_in", big["w_in"][0]), jnp.broadcast_to(packed, (N_DEV,) + packed.shape)],
                                   "scatter_last", True)
    recv = dict(zip(early, recv_early_all))
    recv[("w_in", 0)] = recv_in0

    out_g, out_d, out_m, out_v = {}, {}, {}, {}
    for n in SHARDED:
        res = [_adamw(recv[(n, l)], weights[n][l], mom[n][l], var[n][l], f"adamw_{n}{l}") for l in range(depth)]
        out_g[n], out_d[n], out_m[n], out_v[n] = (jnp.stack([r[i] for r in res]) for i in range(4))

    sg, sd, sm, sv_ = _adamw(gathered, _pack(weights), _pack(mom), _pack(var), "adamw_small")
    for tree, flat in ((out_g, sg), (out_d, sd), (out_m, sm), (out_v, sv_)):
        tree.update(_unpack(flat, weights))

    loss = lax.psum(loss_part[0, 0], ("x", "y", "c"))
    grad_x = dx.reshape(1, s, D_MODEL)
    return (loss, grad_x, *[out_g[n] for n in ORDER], *[out_d[n] for n in ORDER], *[out_m[n] for n in ORDER],
            *[out_v[n] for n in ORDER])
```

```python
import functools
import math

import jax
import jax.numpy as jnp
import numpy as np
from jax import lax
from jax.experimental import pallas as pl
from jax.experimental.pallas import tpu as pltpu

F32 = jnp.float32
BF16 = jnp.bfloat16
_MXU = jnp.bfloat16
GRAD_WIRE = jnp.bfloat16

D_MODEL = 1024
W_A = 256
W_B = 512
W_C = 256
HEAD_DIM = 64
IN_COLS = 2 * W_A + 3 * W_B + W_C
D_FF = 2816
CHUNK = 128
POOL_WINDOWS = (2, 4, 8, 16)
EPS = 1e-6
N_DEV = 8
LANES = 128
SUBLANES = 8
VMEM_LIMIT = 48 * 1024 * 1024

ADAM_LR = 0.001
ADAM_B1 = 0.9
ADAM_B2 = 0.999
ADAM_EPS = 1e-08
ADAM_WD = 0.01
ADAM_STEP = 10

INV_SQRT2 = 1.0 / math.sqrt(2.0)
INV_SQRT_2PI = 1.0 / math.sqrt(2.0 * math.pi)


def _pcall(body, **kw):
    return pl.pallas_call(body, **kw)


def _params(dims=None):
    return pltpu.CompilerParams(dimension_semantics=dims, vmem_limit_bytes=VMEM_LIMIT)


def _tile(n, pref, mult=SUBLANES):
    t = min(n, pref) // mult * mult
    while t >= mult:
        if n % t == 0:
            return t
        t -= mult
    return n


def _iota(shape, dim):
    return lax.broadcasted_iota(jnp.int32, shape, dim)


def _dot(a, b):
    return jnp.dot(a.astype(_MXU), b.astype(_MXU), preferred_element_type=F32)


def _dot_nt(a, b):
    return lax.dot_general(a.astype(_MXU), b.astype(_MXU), (((1,), (1,)), ((), ())), preferred_element_type=F32)


def _dot_tn(a, b):
    return lax.dot_general(a.astype(_MXU), b.astype(_MXU), (((0,), (0,)), ((), ())), preferred_element_type=F32)


def _split(x):
    hi = x.astype(BF16)
    lo = (x - hi.astype(F32)).astype(BF16)
    return hi, lo


def _dot_sel(x, sel):
    hi, lo = _split(x)
    return jnp.dot(hi, sel, preferred_element_type=F32) + jnp.dot(lo, sel, preferred_element_type=F32)


def _sel_dot(sel, x):
    hi, lo = _split(x)
    return jnp.dot(sel, hi, preferred_element_type=F32) + jnp.dot(sel, lo, preferred_element_type=F32)


def _group_mat(n):
    r = jnp.right_shift(_iota((n, n), 0), 6)
    c = jnp.right_shift(_iota((n, n), 1), 6)
    return (r == c).astype(BF16)


def _lane_group_mask(n, g):
    lane = _iota((1, n), 1)
    return (lane >= g * HEAD_DIM) & (lane < (g + 1) * HEAD_DIM)


def _gelu(a):
    return 0.5 * a * (1.0 + lax.erf(a * INV_SQRT2))


def _gelu_grad(a):
    return 0.5 * (1.0 + lax.erf(a * INV_SQRT2)) + a * jnp.exp(-0.5 * a * a) * INV_SQRT_2PI


def _rms(x):
    return lax.rsqrt(jnp.mean(x * x, axis=-1, keepdims=True) + EPS)


def _rms_bwd(x, g, dy):
    r = _rms(x)
    dyg = dy * g
    m2 = jnp.mean(dyg * x, axis=-1, keepdims=True)
    dx = r * dyg - x * (r * r * r) * m2
    dg = jnp.sum(dy * x * r, axis=0, keepdims=True)
    return dx, dg


def _shift_down(cur, prev8, k):
    rolled = pltpu.roll(cur, k, 0)
    row8 = _iota(prev8.shape, 0)
    top = jnp.where(row8 < k, pltpu.roll(prev8, k, 0), rolled[0:SUBLANES])
    return jnp.concatenate([top, rolled[SUBLANES:]], axis=0)


def _shift_up(cur, next8, k):
    n = cur.shape[0]
    rolled = pltpu.roll(cur, n - k, 0)
    row8 = _iota(next8.shape, 0)
    bot = jnp.where(row8 >= SUBLANES - k, pltpu.roll(next8, SUBLANES - k, 0), rolled[n - SUBLANES:])
    return jnp.concatenate([rolled[:n - SUBLANES], bot], axis=0)


def _mesh_pos():
    return lax.axis_index("x"), lax.axis_index("y"), lax.axis_index("c")


def _peer(x, y, c, k):
    px = 1 - x if k & 4 else x
    py = 1 - y if k & 2 else y
    pc = 1 - c if k & 1 else c
    return px, py, pc


class _Exchange:
    def __init__(self, arrs, scatter):
        self.arrs, self.scatter, self.n = list(arrs), scatter, len(arrs)
        self.out_shapes = [jax.ShapeDtypeStruct(a.shape if scatter else (N_DEV,) + a.shape, a.dtype) for a in arrs]
        self.specs = [pl.BlockSpec(memory_space=pl.ANY)] * self.n
        self.semaphores = [pltpu.SemaphoreType.DMA((self.n * (N_DEV - 1),)),
                           pltpu.SemaphoreType.DMA((self.n * (N_DEV - 1),)), pltpu.SemaphoreType.DMA((self.n,))]

    def _copies(self, ins, outs, sems):
        send, recv, loc = sems
        x, y, c = _mesh_pos()
        me = 4 * x + 2 * y + c
        src = (lambda a, idx: ins[a].at[idx]) if self.scatter else (lambda a, idx: ins[a])
        starts = [pltpu.make_async_copy(src(a, me), outs[a].at[me], loc.at[a]) for a in range(self.n)]
        waits = list(starts)
        for k in range(1, N_DEV):
            px, py, pc = _peer(x, y, c, k)
            pidx = 4 * px + 2 * py + pc
            for a in range(self.n):
                s = a * (N_DEV - 1) + k - 1
                common = dict(src_ref=src(a, pidx), send_sem=send.at[s], recv_sem=recv.at[s],
                              device_id=(px, py, pc), device_id_type=pl.DeviceIdType.MESH)
                starts.append(pltpu.make_async_remote_copy(dst_ref=outs[a].at[me], **common))
                waits.append(pltpu.make_async_remote_copy(dst_ref=outs[a].at[pidx], **common))
        return starts, waits

    def start(self, ins, outs, sems):
        for cp in self._copies(ins, outs, sems)[0]:
            cp.start()

    def wait(self, ins, outs, sems):
        for cp in self._copies(ins, outs, sems)[1]:
            cp.wait()


def _with_exchange(compute, ride, n_in, n_out, n_scratch, last_step):
    if ride is None:
        return compute
    nx = ride.n

    def body(*refs):
        ins, ride_in = refs[:n_in], refs[n_in:n_in + nx]
        outs = refs[n_in + nx:n_in + nx + n_out]
        ride_out = refs[n_in + nx + n_out:n_in + 2 * nx + n_out]
        scratch = refs[n_in + 2 * nx + n_out:n_in + 2 * nx + n_out + n_scratch]
        sems = refs[n_in + 2 * nx + n_out + n_scratch:]
        step = (pl.program_id(0), pl.program_id(1))

        @pl.when((step[0] == 0) & (step[1] == 0))
        def _():
            ride.start(ride_in, ride_out, sems)

        compute(*ins, *outs, *scratch)

        @pl.when((step[0] == last_step[0]) & (step[1] == last_step[1]))
        def _():
            ride.wait(ride_in, ride_out, sems)

    return body


def _exchange(arrs, name, scatter):
    ex = _Exchange(arrs, scatter)
    n = ex.n

    def body(*refs):
        ins, outs, sems = refs[:n], refs[n:2 * n], refs[2 * n:]
        ex.start(ins, outs, sems)
        ex.wait(ins, outs, sems)

    return _pcall(body, name=name, out_shape=ex.out_shapes, in_specs=ex.specs, out_specs=ex.specs,
                  scratch_shapes=ex.semaphores, compiler_params=pltpu.CompilerParams(has_side_effects=True))(*arrs)


def _mm_tn(a, b, name):
    s, m = a.shape
    n = b.shape[1]
    tm, tn, tk = _tile(m, 1408, LANES), _tile(n, 1408, LANES), _tile(s, 1024)

    def body(a_ref, b_ref, o_ref):
        @pl.when(pl.program_id(2) == 0)
        def _():
            o_ref[...] = jnp.zeros_like(o_ref)
        o_ref[...] += _dot_tn(a_ref[...], b_ref[...])

    return _pcall(
        body, name=name, grid=(m // tm, n // tn, s // tk),
        in_specs=[pl.BlockSpec((tk, tm), lambda i, j, k: (k, i)), pl.BlockSpec((tk, tn), lambda i, j, k: (k, j))],
        out_specs=pl.BlockSpec((tm, tn), lambda i, j, k: (i, j)),
        out_shape=jax.ShapeDtypeStruct((m, n), F32),
        compiler_params=_params(("parallel", "parallel", "arbitrary")),
    )(a, b)


def _inproj_fwd(x, g, w, name):
    s = x.shape[0]
    tm = _tile(s, 256)

    def body(x_ref, g_ref, w_ref, a_ref, qkv_ref, p_ref, h_ref):
        xv = x_ref[...]
        h = (xv * _rms(xv) * g_ref[...]).astype(_MXU)
        h_ref[...] = h
        a_ref[...] = jnp.dot(h, w_ref[:, 0:2 * W_A], preferred_element_type=F32)
        q = jnp.dot(h, w_ref[:, 2 * W_A:2 * W_A + W_B], preferred_element_type=F32)
        qkv_ref[:, 0:W_B] = (q * 0.125).astype(_MXU)
        kv = jnp.dot(h, w_ref[:, 2 * W_A + W_B:2 * W_A + 3 * W_B], preferred_element_type=F32)
        qkv_ref[:, W_B:3 * W_B] = kv.astype(_MXU)
        p_ref[...] = jnp.dot(h, w_ref[:, 2 * W_A + 3 * W_B:IN_COLS], preferred_element_type=F32)

    row = lambda n: pl.BlockSpec((tm, n), lambda i: (i, 0))
    return _pcall(
        body, name=name, grid=(s // tm,),
        in_specs=[row(D_MODEL), pl.BlockSpec((1, D_MODEL), lambda i: (0, 0)),
                  pl.BlockSpec((D_MODEL, IN_COLS), lambda i: (0, 0))],
        out_specs=[row(2 * W_A), row(3 * W_B), row(W_C), row(D_MODEL)],
        out_shape=[jax.ShapeDtypeStruct((s, 2 * W_A), F32), jax.ShapeDtypeStruct((s, 3 * W_B), _MXU),
                   jax.ShapeDtypeStruct((s, W_C), F32), jax.ShapeDtypeStruct((s, D_MODEL), _MXU)],
        compiler_params=_params(("parallel",)),
    )(x, g, w)


def _inproj_bwd(da, dq, dk, dv, dp, wt, x, g, dres, name):
    s = x.shape[0]
    tm = _tile(s, 256)

    def body(da_ref, dq_ref, dk_ref, dv_ref, dp_ref, wt_ref, x_ref, g_ref, dres_ref, dx_ref, dg_ref):
        dh = _dot(da_ref[...], wt_ref[0:2 * W_A, :])
        dh += _dot(dq_ref[...], wt_ref[2 * W_A:2 * W_A + W_B, :])
        dh += _dot(dk_ref[...], wt_ref[2 * W_A + W_B:2 * W_A + 2 * W_B, :])
        dh += _dot(dv_ref[...], wt_ref[2 * W_A + 2 * W_B:2 * W_A + 3 * W_B, :])
        dh += _dot(dp_ref[...], wt_ref[2 * W_A + 3 * W_B:IN_COLS, :])
        dx, dg = _rms_bwd(x_ref[...], g_ref[...], dh)
        dx_ref[...] = dres_ref[...] + dx

        @pl.when(pl.program_id(0) == 0)
        def _():
            dg_ref[...] = jnp.zeros_like(dg_ref)
        dg_ref[...] += dg

    row = lambda n: pl.BlockSpec((tm, n), lambda i: (i, 0))
    vec = pl.BlockSpec((1, D_MODEL), lambda i: (0, 0))
    return _pcall(
        body, name=name, grid=(s // tm,),
        in_specs=[row(2 * W_A), row(W_B), row(W_B), row(W_B), row(W_C),
                  pl.BlockSpec((IN_COLS, D_MODEL), lambda i: (0, 0)), row(D_MODEL), vec, row(D_MODEL)],
        out_specs=[row(D_MODEL), vec],
        out_shape=[jax.ShapeDtypeStruct((s, D_MODEL), F32), jax.ShapeDtypeStruct((1, D_MODEL), F32)],
        compiler_params=_params(("arbitrary",)),
    )(da, dq, dk, dv, dp, wt, x, g, dres)


def _sgu_core(a, gn, wm_ref, bias):
    ga = _gelu(a)
    u, v0 = ga[:, 0:W_A], ga[:, W_A:2 * W_A]
    r = lax.rsqrt(_dot_sel(v0 * v0, _group_mat(W_A)) * (1.0 / HEAD_DIM) + EPS)
    vn = v0 * r * gn
    sv = bias
    for h in range(W_A // HEAD_DIM):
        sv = sv + _dot(wm_ref[h], jnp.where(_lane_group_mask(W_A, h), vn, 0.0))
    return u, v0, r, vn, sv


def _sgu_fwd(a, gn, wm, bias, name):
    s = a.shape[0]

    def body(a_ref, gn_ref, wm_ref, b_ref, y_ref):
        u, _, _, _, sv = _sgu_core(a_ref[...], gn_ref[...], wm_ref, b_ref[...])
        y_ref[...] = u * sv

    return _pcall(
        body, name=name, grid=(s // CHUNK,),
        in_specs=[pl.BlockSpec((CHUNK, 2 * W_A), lambda i: (i, 0)), pl.BlockSpec((1, W_A), lambda i: (0, 0)),
                  pl.BlockSpec((4, CHUNK, CHUNK), lambda i: (0, 0, 0)), pl.BlockSpec((CHUNK, W_A), lambda i: (0, 0))],
        out_specs=pl.BlockSpec((CHUNK, W_A), lambda i: (i, 0)),
        out_shape=jax.ShapeDtypeStruct((s, W_A), F32),
        compiler_params=_params(("parallel",)),
    )(a, gn, wm, bias)


def _sgu_bwd(a, dy, gn, wm, bias, name):
    s = a.shape[0]

    def body(a_ref, dy_ref, gn_ref, wm_ref, b_ref, da_ref, dwm_ref, db_ref, dgn_ref):
        @pl.when(pl.program_id(0) == 0)
        def _():
            dwm_ref[...] = jnp.zeros_like(dwm_ref)
            db_ref[...] = jnp.zeros_like(db_ref)
            dgn_ref[...] = jnp.zeros_like(dgn_ref)

        av, gnv, dyv = a_ref[...], gn_ref[...], dy_ref[...]
        u, v0, r, vn, sv = _sgu_core(av, gnv, wm_ref, b_ref[...])
        du = dyv * sv
        ds = dyv * u
        db_ref[...] += ds
        tril = _iota((CHUNK, CHUNK), 1) <= _iota((CHUNK, CHUNK), 0)
        dvn = jnp.zeros_like(vn)
        for h in range(W_A // HEAD_DIM):
            dsm = jnp.where(_lane_group_mask(W_A, h), ds, 0.0)
            dwm_ref[h] += jnp.where(tril, _dot_nt(dsm, vn), 0.0)
            dvn = dvn + _dot_tn(wm_ref[h], dsm)
        dgn_ref[...] += jnp.sum(dvn * v0 * r, axis=0, keepdims=True)
        dvg = dvn * gnv
        m2 = _dot_sel(dvg * v0, _group_mat(W_A)) * (1.0 / HEAD_DIM)
        dv0 = r * dvg - v0 * (r * r * r) * m2
        gp = _gelu_grad(av)
        da_ref[:, 0:W_A] = du * gp[:, 0:W_A]
        da_ref[:, W_A:2 * W_A] = dv0 * gp[:, W_A:2 * W_A]

    return _pcall(
        body, name=name, grid=(s // CHUNK,),
        in_specs=[pl.BlockSpec((CHUNK, 2 * W_A), lambda i: (i, 0)), pl.BlockSpec((CHUNK, W_A), lambda i: (i, 0)),
                  pl.BlockSpec((1, W_A), lambda i: (0, 0)), pl.BlockSpec((4, CHUNK, CHUNK), lambda i: (0, 0, 0)),
                  pl.BlockSpec((CHUNK, W_A), lambda i: (0, 0))],
        out_specs=[pl.BlockSpec((CHUNK, 2 * W_A), lambda i: (i, 0)), pl.BlockSpec((4, CHUNK, CHUNK), lambda i: (0, 0, 0)),
                   pl.BlockSpec((CHUNK, W_A), lambda i: (0, 0)), pl.BlockSpec((1, W_A), lambda i: (0, 0))],
        out_shape=[jax.ShapeDtypeStruct((s, 2 * W_A), F32), jax.ShapeDtypeStruct((4, CHUNK, CHUNK), F32),
                   jax.ShapeDtypeStruct((CHUNK, W_A), F32), jax.ShapeDtypeStruct((1, W_A), F32)],
        compiler_params=_params(("arbitrary",)),
    )(a, dy, gn, wm, bias)


KBLK = 128
ROW_CHUNK = 64
MASKED_SCORE = -1e30


def _attn_fwd(qkv, name, ride=None):
    s = qkv.shape[0]
    tq = _tile(s, 256, KBLK)
    npairs = W_B // LANES
    assert s // KBLK <= LANES
    rides = [] if ride is None else [ride]

    def body(q_ref, k_ref, v_ref, o_ref, cm_ref, z_ref, zw_ref, sums_ref, carry_ref, hl_ref, a_ref):
        i = pl.program_id(1)
        q = q_ref[...]
        lane = _iota((1, LANES), 1)
        lane_lo = lane < HEAD_DIM
        hmask = (lane_lo, jnp.logical_not(lane_lo))
        tri2 = ((_iota((KBLK, 2 * KBLK), 0) >= _iota((KBLK, 2 * KBLK), 1))
                | (_iota((KBLK, 2 * KBLK), 1) >= KBLK)).astype(BF16)
        dmat = _iota((tq, KBLK), 1) - (_iota((tq, KBLK), 0) + i * tq)
        chunks = [slice(r, r + ROW_CHUNK) for r in range(0, tq, ROW_CHUNK)]
        heads = [slice(hh * LANES, (hh + 1) * LANES) for hh in range(2)]
        nk = (i + 1) * (tq // KBLK)

        cm_ref[...] = jnp.zeros_like(cm_ref)

        def before(b):
            return jnp.where((b >= 0) & (b < nk), -b * KBLK, jnp.iinfo(jnp.int32).min)

        def per_head(block):
            return jnp.concatenate([jnp.where(m, block, jnp.zeros_like(block)) for m in hmask], axis=0)

        def scores(b, p):
            ks = pl.multiple_of(jnp.maximum(b, 0) * KBLK, KBLK)
            z_ref[p] = _dot_nt(q, per_head(k_ref[pl.ds(ks, KBLK), :]))

        def logs(b, p):
            t = before(b)
            for hh in range(2):
                for rows in chunks:
                    z = jnp.where(dmat[rows] < t, z_ref[p, rows, heads[hh]], MASKED_SCORE)
                    zw_ref[p, rows, heads[hh]] = z
                    l = -(jnp.maximum(z, 0.0) + jnp.log(1.0 + jnp.exp(-jnp.abs(z))))
                    hi, lo = _split(l)
                    hl_ref[p, hh, rows, 0:KBLK] = hi
                    hl_ref[p, hh, rows, KBLK:2 * KBLK] = lo

        def sums(b, p):
            for hh in range(2):
                sums_ref[p, hh] = (jnp.dot(hl_ref[p, hh, :, 0:KBLK], tri2, preferred_element_type=F32)
                                   + jnp.dot(hl_ref[p, hh, :, KBLK:2 * KBLK], tri2, preferred_element_type=F32))

        def weights(b, p):
            pick = lane == b
            for hh in range(2):
                for rows in chunks:
                    c = carry_ref[hh, rows, :]
                    arg = zw_ref[p, rows, heads[hh]] + c + sums_ref[p, hh, rows, 0:KBLK]
                    a_ref[p, rows, heads[hh]] = jnp.exp(arg).astype(_MXU)
                    cm_ref[rows, heads[hh]] = jnp.where(pick, c, cm_ref[rows, heads[hh]])
                    carry_ref[hh, rows, :] = c + sums_ref[p, hh, rows, KBLK:2 * KBLK]

        def out(b, p, acc):
            ks = pl.multiple_of(jnp.minimum(b, nk - 1) * KBLK, KBLK)
            vb = v_ref[pl.ds(ks, KBLK), :]
            for hh in range(2):
                acc = acc + jnp.dot(a_ref[p, :, heads[hh]], jnp.where(hmask[hh], vb, jnp.zeros_like(vb)),
                                    preferred_element_type=F32)
            return acc

        def step(it, acc):
            for u in range(2):
                b = nk + 3 - (2 * it + u)
                p = 1 - u
                acc = out(b, p, acc)
                weights(b - 1, 1 - p)
                sums(b - 2, p)
                logs(b - 3, 1 - p)
                scores(b - 4, p)
            return acc

        z_ref[...] = jnp.zeros_like(z_ref)
        zw_ref[...] = jnp.full_like(zw_ref, MASKED_SCORE)
        for ref in (sums_ref, carry_ref, hl_ref, a_ref):
            ref[...] = jnp.zeros_like(ref)
        o_ref[...] = lax.fori_loop(0, (nk + 4) // 2, step, q.astype(F32) * 0.0)

    scratch = [pltpu.VMEM((2, tq, 2 * KBLK), F32), pltpu.VMEM((2, tq, 2 * KBLK), F32),
               pltpu.VMEM((2, 2, tq, 2 * KBLK), F32), pltpu.VMEM((2, tq, KBLK), F32),
               pltpu.VMEM((2, 2, tq, 2 * KBLK), BF16), pltpu.VMEM((2, tq, 2 * KBLK), _MXU)]
    res = _pcall(
        _with_exchange(body, ride, 3, 2, len(scratch), (npairs - 1, s // tq - 1)), name=name, grid=(npairs, s // tq),
        in_specs=[pl.BlockSpec((tq, LANES), lambda p, i: (i, p)),
                  pl.BlockSpec((s, LANES), lambda p, i: (0, npairs + p)),
                  pl.BlockSpec((s, LANES), lambda p, i: (0, 2 * npairs + p))] + [sp for r in rides for sp in r.specs],
        out_specs=[pl.BlockSpec((tq, LANES), lambda p, i: (i, p)), pl.BlockSpec((tq, 2 * LANES), lambda p, i: (i, p))]
                  + [sp for r in rides for sp in r.specs],
        out_shape=[jax.ShapeDtypeStruct((s, W_B), F32), jax.ShapeDtypeStruct((s, 2 * W_B), F32)]
                  + [sh for r in rides for sh in r.out_shapes],
        scratch_shapes=scratch + [sem for r in rides for sem in r.semaphores],
        compiler_params=_params(("arbitrary", "arbitrary")),
    )(qkv, qkv, qkv, *[a for r in rides for a in r.arrs])
    return res[0], res[1], list(res[2:])


def _attn_bwd(qkv, cm, do, name, ride=None):
    s = qkv.shape[0]
    tq = _tile(s, 256, KBLK)
    npairs = W_B // LANES
    rides = [] if ride is None else [ride]

    nq = s // tq
    per_tile = tq // KBLK
    assert nq % 2 == 0 and per_tile % 2 == 0
    ntot = (nq + 1) * per_tile

    def body(qa_ref, k_ref, v_ref, cma_ref, doa_ref, qb_ref, cmb_ref, dob_ref, dqa_ref, dqb_ref, dk_ref, dv_ref,
             z_ref, zw_ref, da_ref, g_ref, sig_ref, cum_ref, gp_ref, gcarry_ref, hl_ref, ghl_ref, a_ref, dz_ref,
             dkt_ref, dvt_ref, q_st, do_st, qt_st, dot_st, cm_st):
        i = pl.program_id(1)

        @pl.when(i == 0)
        def _():
            dkt_ref[...] = jnp.zeros_like(dkt_ref)
            dvt_ref[...] = jnp.zeros_like(dvt_ref)

        tiles = (i, nq - 1 - i)
        nk_a = (i + 1) * per_tile
        for t, (q_in, do_in, cm_in) in enumerate(((qa_ref, doa_ref, cma_ref), (qb_ref, dob_ref, cmb_ref))):
            q_st[t] = q_in[...]
            do_st[t] = do_in[...].astype(_MXU)
            qt_st[t] = q_in[...].astype(F32).T.astype(_MXU)
            dot_st[t] = do_in[...].T.astype(_MXU)
            cm_st[t] = cm_in[...]

        lane = _iota((1, LANES), 1)
        lane_lo = lane < HEAD_DIM
        hmask = (lane_lo, jnp.logical_not(lane_lo))
        tri = (_iota((KBLK, KBLK), 0) >= _iota((KBLK, KBLK), 1)).astype(BF16)
        prefix2 = ((_iota((KBLK, 2 * KBLK), 0) <= _iota((KBLK, 2 * KBLK), 1))
                   | (_iota((KBLK, 2 * KBLK), 1) >= KBLK)).astype(BF16)
        dmat = _iota((tq, KBLK), 1) - _iota((tq, KBLK), 0)
        chunks = [slice(r, r + ROW_CHUNK) for r in range(0, tq, ROW_CHUNK)]
        heads = [slice(hh * LANES, (hh + 1) * LANES) for hh in range(2)]

        def locate(v):
            second = v >= nk_a
            return ((v >= 0) & (v < ntot), second.astype(jnp.int32), jnp.where(second, tiles[1], tiles[0]),
                    jnp.where(second, v - nk_a, v))

        def before(v):
            valid, _, tile, b = locate(v)
            return jnp.where(valid, tile * tq - b * KBLK, jnp.iinfo(jnp.int32).min)

        def key_block(v):
            return jnp.clip(locate(v)[3], 0, nblk - 1)

        def block_rows(v):
            return pl.ds(pl.multiple_of(key_block(v) * KBLK, KBLK), KBLK)

        def per_head(block):
            return jnp.concatenate([jnp.where(m, block, jnp.zeros_like(block)) for m in hmask], axis=0)

        feature_lo = _iota((LANES, KBLK), 0) < HEAD_DIM

        def own_features(side_by_side):
            return jnp.where(feature_lo, side_by_side[:, 0:KBLK], side_by_side[:, KBLK:2 * KBLK])

        def m1(b, p):
            z_ref[p] = _dot_nt(q_st[locate(b)[1]], per_head(k_ref[block_rows(b), :]))

        def v1(b, p):
            t = before(b)
            for hh in range(2):
                for rows in chunks:
                    z = jnp.where(dmat[rows] < t, z_ref[p, rows, heads[hh]], MASKED_SCORE)
                    zw_ref[p, rows, heads[hh]] = z
                    l = -(jnp.maximum(z, 0.0) + jnp.log(1.0 + jnp.exp(-jnp.abs(z))))
                    hi, lo = _split(l)
                    hl_ref[p, hh, rows, 0:KBLK] = hi
                    hl_ref[p, hh, rows, KBLK:2 * KBLK] = lo

        def m2(b, p):
            for hh in range(2):
                cum_ref[p, hh] = (jnp.dot(hl_ref[p, hh, :, 0:KBLK], tri, preferred_element_type=F32)
                                  + jnp.dot(hl_ref[p, hh, :, KBLK:2 * KBLK], tri, preferred_element_type=F32))
            da_ref[p] = _dot_nt(do_st[locate(b)[1]], per_head(v_ref[block_rows(b), :]))

        def v2(b, p):
            valid, which, _, blk = locate(b)
            pick = lane == jnp.where(valid, blk, -1)
            for hh in range(2):
                for rows in chunks:
                    c = jnp.sum(jnp.where(pick, cm_st[which, rows, heads[hh]], 0.0), axis=1, keepdims=True)
                    z = zw_ref[p, rows, heads[hh]]
                    a = jnp.exp(z + c + cum_ref[p, hh, rows, :])
                    g = a * da_ref[p, rows, heads[hh]]
                    a_ref[p, rows, heads[hh]] = a.astype(_MXU)
                    g_ref[p, rows, heads[hh]] = g
                    hi, lo = _split(g)
                    ghl_ref[p, hh, rows, 0:KBLK] = hi
                    ghl_ref[p, hh, rows, KBLK:2 * KBLK] = lo
                    sig_ref[p, rows, heads[hh]] = jax.nn.sigmoid(z)

        def m3(b, p):
            for hh in range(2):
                gp_ref[p, hh] = (jnp.dot(ghl_ref[p, hh, :, 0:KBLK], prefix2, preferred_element_type=F32)
                                 + jnp.dot(ghl_ref[p, hh, :, KBLK:2 * KBLK], prefix2, preferred_element_type=F32))
            dvt_ref[key_block(b)] += own_features(jnp.dot(dot_st[locate(b)[1]], a_ref[p], preferred_element_type=F32))

        def v3(b, p):
            restart = b == nk_a
            for hh in range(2):
                for rows in chunks:
                    gc = jnp.where(restart, 0.0, gcarry_ref[hh, rows, :])
                    upto = gc + gp_ref[p, hh, rows, 0:KBLK]
                    dz = g_ref[p, rows, heads[hh]] - sig_ref[p, rows, heads[hh]] * upto
                    dz_ref[p, rows, heads[hh]] = dz.astype(_MXU)
                    gcarry_ref[hh, rows, :] = gc + gp_ref[p, hh, rows, KBLK:2 * KBLK]

        def m4(b, p, dqs):
            which = locate(b)[1]
            kb = k_ref[block_rows(b), :]
            part = None
            for hh in range(2):
                d = jnp.dot(dz_ref[p, :, heads[hh]], jnp.where(hmask[hh], kb, jnp.zeros_like(kb)),
                            preferred_element_type=F32)
                part = d if part is None else part + d
            dkt_ref[key_block(b)] += own_features(jnp.dot(qt_st[which], dz_ref[p], preferred_element_type=F32))
            return dqs[0] + jnp.where(which == 0, part, 0.0), dqs[1] + jnp.where(which == 1, part, 0.0)

        def step(it, dqs):
            for u in range(2):
                j = 2 * it + u
                dqs = m4(j - 6, u, dqs)
                m3(j - 4, u)
                m2(j - 2, u)
                m1(j, u)
                v3(j - 5, 1 - u)
                v2(j - 3, 1 - u)
                v1(j - 1, 1 - u)
            return dqs

        zw_ref[...] = jnp.full_like(zw_ref, MASKED_SCORE)
        for ref in (z_ref, da_ref, g_ref, sig_ref, cum_ref, gp_ref, gcarry_ref, hl_ref, ghl_ref, a_ref, dz_ref):
            ref[...] = jnp.zeros_like(ref)
        zero = doa_ref[...] * 0.0
        dq_a, dq_b = lax.fori_loop(0, (ntot + 6) // 2, step, (zero, zero))
        dqa_ref[...] = dq_a * 0.125
        dqb_ref[...] = dq_b * 0.125

        @pl.when(i == nq // 2 - 1)
        def _():
            def untranspose(blk, carry):
                rows = pl.ds(pl.multiple_of(blk * KBLK, KBLK), KBLK)
                dk_ref[rows, :] = dkt_ref[blk].T
                dv_ref[rows, :] = dvt_ref[blk].T
                return carry
            lax.fori_loop(0, nblk, untranspose, 0)

    first = lambda width: pl.BlockSpec((tq, width), lambda p, i: (i, p))
    second = lambda width: pl.BlockSpec((tq, width), lambda p, i: (nq - 1 - i, p))
    second_out = pl.BlockSpec((tq, LANES), lambda p, i: (nq // 2 - 1 - i, p))
    nblk = s // KBLK
    full = pl.BlockSpec((s, LANES), lambda p, i: (0, p))
    scratch = ([pltpu.VMEM((2, tq, 2 * KBLK), F32)] * 5 + [pltpu.VMEM((2, 2, tq, KBLK), F32),
               pltpu.VMEM((2, 2, tq, 2 * KBLK), F32), pltpu.VMEM((2, tq, KBLK), F32)]
               + [pltpu.VMEM((2, 2, tq, 2 * KBLK), BF16)] * 2 + [pltpu.VMEM((2, tq, 2 * KBLK), _MXU)] * 2
               + [pltpu.VMEM((nblk, LANES, KBLK), F32)] * 2
               + [pltpu.VMEM((2, tq, LANES), _MXU)] * 2 + [pltpu.VMEM((2, LANES, tq), _MXU)] * 2
               + [pltpu.VMEM((2, tq, 2 * LANES), F32)])
    res = _pcall(
        _with_exchange(body, ride, 8, 4, len(scratch), (npairs - 1, nq // 2 - 1)), name=name, grid=(npairs, nq // 2),
        in_specs=[first(LANES), pl.BlockSpec((s, LANES), lambda p, i: (0, npairs + p)),
                  pl.BlockSpec((s, LANES), lambda p, i: (0, 2 * npairs + p)), first(2 * LANES), first(LANES),
                  second(LANES), second(2 * LANES), second(LANES)] + [sp for r in rides for sp in r.specs],
        out_specs=[first(LANES), second_out, full, full] + [sp for r in rides for sp in r.specs],
        out_shape=[jax.ShapeDtypeStruct((s // 2, W_B), F32)] * 2 + [jax.ShapeDtypeStruct((s, W_B), F32)] * 2
                  + [sh for r in rides for sh in r.out_shapes],
        scratch_shapes=scratch + [sem for r in rides for sem in r.semaphores],
        compiler_params=_params(("arbitrary", "arbitrary")),
    )(qkv, qkv, qkv, cm, do, qkv, cm, do, *[a for r in rides for a in r.arrs])
    return jnp.concatenate([res[0], res[1]], axis=0), res[2], res[3], list(res[4:])


POOL_HALO = 128


def _pool_window_lane():
    lane = _iota((1, W_C), 1)
    w = jnp.where(lane < 64, POOL_WINDOWS[0], jnp.where(lane < 128, POOL_WINDOWS[1],
                  jnp.where(lane < 192, POOL_WINDOWS[2], POOL_WINDOWS[3])))
    return w.astype(F32)


def _pool_count(tm, i):
    pos = (_iota((tm, W_C), 0) + (i * tm + 1)).astype(F32)
    return jnp.minimum(pos, _pool_window_lane())


def _pool_centered(prev, cur, cnt):
    tm = cur.shape[0]
    xx = jnp.concatenate([prev, cur], axis=0)
    hi, lo = _split(xx)
    t = _iota((tm, tm + POOL_HALO), 0)
    cc = _iota((tm, tm + POOL_HALO), 1) - POOL_HALO
    wsum = jnp.zeros_like(cur)
    for g, w in enumerate(POOL_WINDOWS):
        band = ((cc <= t) & (cc > t - w)).astype(BF16)
        mg = _lane_group_mask(W_C, g)
        wsum += jnp.dot(band, jnp.where(mg, hi, jnp.zeros_like(hi)), preferred_element_type=F32)
        wsum += jnp.dot(band, jnp.where(mg, lo, jnp.zeros_like(lo)), preferred_element_type=F32)
    return wsum / cnt - cur


def _pool_fwd(p, wbd, sc, name):
    s = p.shape[0]
    tm = _tile(s, 256, POOL_HALO)
    r = tm // POOL_HALO

    def body(pp_ref, p_ref, w_ref, sc_ref, y_ref):
        i = pl.program_id(0)
        prev = jnp.where(i > 0, pp_ref[...], 0.0)
        d = _pool_centered(prev, p_ref[...], _pool_count(tm, i))
        y_ref[...] = _dot(d, w_ref[...]) * sc_ref[...]

    return _pcall(
        body, name=name, grid=(s // tm,),
        in_specs=[pl.BlockSpec((POOL_HALO, W_C), lambda i: (jnp.maximum(i * r - 1, 0), 0)),
                  pl.BlockSpec((tm, W_C), lambda i: (i, 0)), pl.BlockSpec((W_C, W_C), lambda i: (0, 0)),
                  pl.BlockSpec((1, W_C), lambda i: (0, 0))],
        out_specs=pl.BlockSpec((tm, W_C), lambda i: (i, 0)),
        out_shape=jax.ShapeDtypeStruct((s, W_C), F32),
        compiler_params=_params(("parallel",)),
    )(p, p, wbd, sc)


def _pool_bwd(p, dy, wbd, wbdt, sc, name):
    s = p.shape[0]
    tm = _tile(s, 256, POOL_HALO)
    r = tm // POOL_HALO
    nt = s // tm

    def body(pp_ref, p_ref, dy_ref, dyn_ref, w_ref, wt_ref, sc_ref, dp_ref, dw_ref, dsc_ref):
        i = pl.program_id(0)

        @pl.when(i == 0)
        def _():
            dw_ref[...] = jnp.zeros_like(dw_ref)
            dsc_ref[...] = jnp.zeros_like(dsc_ref)

        scv = sc_ref[...]
        cnt = _pool_count(tm, i)
        prev = jnp.where(i > 0, pp_ref[...], 0.0)
        d = _pool_centered(prev, p_ref[...], cnt)
        e = _dot(d, w_ref[...])
        dyv = dy_ref[...]
        de = dyv * scv
        dsc_ref[...] += jnp.sum(dyv * e, axis=0, keepdims=True)
        dw_ref[...] += _dot_tn(d, de)
        dd = _dot(de, wt_ref[...])
        ddn = jnp.where(i < nt - 1, _dot(dyn_ref[...] * scv, wt_ref[...]), 0.0)
        yy = jnp.concatenate([dd / cnt, ddn / _pool_window_lane()], axis=0)
        hi, lo = _split(yy)
        t = _iota((tm, tm + POOL_HALO), 0)
        cc = _iota((tm, tm + POOL_HALO), 1)
        acc = jnp.zeros_like(dd)
        for g, w in enumerate(POOL_WINDOWS):
            band = ((cc >= t) & (cc < t + w)).astype(BF16)
            mg = _lane_group_mask(W_C, g)
            acc += jnp.dot(band, jnp.where(mg, hi, jnp.zeros_like(hi)), preferred_element_type=F32)
            acc += jnp.dot(band, jnp.where(mg, lo, jnp.zeros_like(lo)), preferred_element_type=F32)
        dp_ref[...] = acc - dd

    tile = pl.BlockSpec((tm, W_C), lambda i: (i, 0))
    mat = pl.BlockSpec((W_C, W_C), lambda i: (0, 0))
    vec = pl.BlockSpec((1, W_C), lambda i: (0, 0))
    return _pcall(
        body, name=name, grid=(nt,),
        in_specs=[pl.BlockSpec((POOL_HALO, W_C), lambda i: (jnp.maximum(i * r - 1, 0), 0)), tile, tile,
                  pl.BlockSpec((POOL_HALO, W_C), lambda i: (jnp.minimum((i + 1) * r, s // POOL_HALO - 1), 0)),
                  mat, mat, vec],
        out_specs=[tile, mat, vec],
        out_shape=[jax.ShapeDtypeStruct((s, W_C), F32), jax.ShapeDtypeStruct((W_C, W_C), F32),
                   jax.ShapeDtypeStruct((1, W_C), F32)],
        compiler_params=_params(("arbitrary",)),
    )(p, p, dy, dy, wbd, wbdt, sc)


def _mix_cols(ya_ref, yb_ref, yc_ref, cb):
    if cb < 2:
        return ya_ref[:, cb * LANES:(cb + 1) * LANES]
    if cb < 6:
        return yb_ref[:, (cb - 2) * LANES:(cb - 1) * LANES]
    return yc_ref[:, (cb - 6) * LANES:(cb - 5) * LANES]


def _mix_fwd(ya, yb, yc, g, wo, x, name):
    s = x.shape[0]
    tm = _tile(s, 256)

    def body(ya_ref, yb_ref, yc_ref, g_ref, w_ref, x_ref, o_ref, yn_ref):
        sel = _group_mat(LANES)
        for cb in range(D_MODEL // LANES):
            y = _mix_cols(ya_ref, yb_ref, yc_ref, cb)
            r = lax.rsqrt(_dot_sel(y * y, sel) * (1.0 / HEAD_DIM) + EPS)
            yn_ref[:, cb * LANES:(cb + 1) * LANES] = (y * r * g_ref[:, cb * LANES:(cb + 1) * LANES]).astype(_MXU)
        o_ref[...] = x_ref[...] + jnp.dot(yn_ref[...], w_ref[...], preferred_element_type=F32)

    row = lambda n: pl.BlockSpec((tm, n), lambda i: (i, 0))
    return _pcall(
        body, name=name, grid=(s // tm,),
        in_specs=[row(W_A), row(W_B), row(W_C), pl.BlockSpec((1, D_MODEL), lambda i: (0, 0)),
                  pl.BlockSpec((D_MODEL, D_MODEL), lambda i: (0, 0)), row(D_MODEL)],
        out_specs=[row(D_MODEL), row(D_MODEL)],
        out_shape=[jax.ShapeDtypeStruct((s, D_MODEL), F32), jax.ShapeDtypeStruct((s, D_MODEL), _MXU)],
        compiler_params=_params(("parallel",)),
    )(ya, yb, yc, g, wo, x)


def _mix_bwd(dx, ya, yb, yc, g, wot, name):
    s = dx.shape[0]
    tm = _tile(s, 256)

    def body(dx_ref, ya_ref, yb_ref, yc_ref, g_ref, wt_ref, dya_ref, dyb_ref, dyc_ref, dg_ref):
        @pl.when(pl.program_id(0) == 0)
        def _():
            dg_ref[...] = jnp.zeros_like(dg_ref)

        dyn = _dot(dx_ref[...], wt_ref[...])
        sel = _group_mat(LANES)
        for cb in range(D_MODEL // LANES):
            cols = slice(cb * LANES, (cb + 1) * LANES)
            y = _mix_cols(ya_ref, yb_ref, yc_ref, cb)
            r = lax.rsqrt(_dot_sel(y * y, sel) * (1.0 / HEAD_DIM) + EPS)
            dyc_ = dyn[:, cols]
            dyg = dyc_ * g_ref[:, cols]
            m2 = _dot_sel(dyg * y, sel) * (1.0 / HEAD_DIM)
            dy = r * dyg - y * (r * r * r) * m2
            dg_ref[:, cols] += jnp.sum(dyc_ * y * r, axis=0, keepdims=True)
            if cb < 2:
                dya_ref[:, cb * LANES:(cb + 1) * LANES] = dy
            elif cb < 6:
                dyb_ref[:, (cb - 2) * LANES:(cb - 1) * LANES] = dy
            else:
                dyc_ref[:, (cb - 6) * LANES:(cb - 5) * LANES] = dy

    row = lambda n: pl.BlockSpec((tm, n), lambda i: (i, 0))
    vec = pl.BlockSpec((1, D_MODEL), lambda i: (0, 0))
    return _pcall(
        body, name=name, grid=(s // tm,),
        in_specs=[row(D_MODEL), row(W_A), row(W_B), row(W_C), vec, pl.BlockSpec((D_MODEL, D_MODEL), lambda i: (0, 0))],
        out_specs=[row(W_A), row(W_B), row(W_C), vec],
        out_shape=[jax.ShapeDtypeStruct((s, W_A), F32), jax.ShapeDtypeStruct((s, W_B), F32),
                   jax.ShapeDtypeStruct((s, W_C), F32), jax.ShapeDtypeStruct((1, D_MODEL), F32)],
        compiler_params=_params(("arbitrary",)),
    )(dx, ya, yb, yc, g, wot)


FFN_CHUNK = 1408
FFN_ROWS = 256
N_CHUNKS = D_FF // FFN_CHUNK
CW_ROWS = 8


def _ffn_up_fwd(x, g, w, name):
    s = x.shape[0]
    n = w.shape[1]
    tm, tn = _tile(s, 512), _tile(n, 2816, LANES)

    def body(x_ref, g_ref, w_ref, z_ref, h_ref):
        @pl.when(pl.program_id(1) == 0)
        def _():
            xv = x_ref[...]
            h_ref[...] = (xv * _rms(xv) * g_ref[...]).astype(_MXU)
        z_ref[...] = jnp.dot(h_ref[...], w_ref[...], preferred_element_type=F32)

    return _pcall(
        body, name=name, grid=(s // tm, n // tn),
        in_specs=[pl.BlockSpec((tm, D_MODEL), lambda i, j: (i, 0)), pl.BlockSpec((1, D_MODEL), lambda i, j: (0, 0)),
                  pl.BlockSpec((D_MODEL, tn), lambda i, j: (0, j))],
        out_specs=[pl.BlockSpec((tm, tn), lambda i, j: (i, j)), pl.BlockSpec((tm, D_MODEL), lambda i, j: (i, 0))],
        out_shape=[jax.ShapeDtypeStruct((s, n), F32), jax.ShapeDtypeStruct((s, D_MODEL), _MXU)],
        compiler_params=_params(("parallel", "arbitrary")),
    )(x, g, w)


def _conv(cur, prev8, cw_ref):
    s1 = _shift_down(cur, prev8, 1)
    s2 = _shift_down(cur, prev8, 2)
    zc = cw_ref[3:4, :] + s2 * cw_ref[0:1, :]
    zc = zc + s1 * cw_ref[1:2, :]
    zc = zc + cur * cw_ref[2:3, :]
    return zc, s1, s2


def _halo_specs(tm, s):
    r = tm // SUBLANES
    prev = lambda off: pl.BlockSpec((SUBLANES, FFN_CHUNK), lambda i, j: (jnp.maximum(i * r - 1, 0), j + off))
    nxt = lambda off: pl.BlockSpec((SUBLANES, FFN_CHUNK), lambda i, j: (jnp.minimum((i + 1) * r, s // SUBLANES - 1), j + off))
    return prev, nxt


def _ffn_down_fwd(z, cw, wd, x, name):
    s = x.shape[0]
    tm = _tile(s, FFN_ROWS)
    prev, _ = _halo_specs(tm, s)

    def body(zg_ref, zu_ref, pg_ref, pu_ref, cg_ref, cu_ref, w_ref, x_ref, o_ref, act_ref, acc_ref):
        i, j = pl.program_id(0), pl.program_id(1)
        first = i > 0
        zg, _, _ = _conv(zg_ref[...], jnp.where(first, pg_ref[...], 0.0), cg_ref)
        zu, _, _ = _conv(zu_ref[...], jnp.where(first, pu_ref[...], 0.0), cu_ref)
        act = (zg * jax.nn.sigmoid(zg) * zu).astype(_MXU)
        act_ref[...] = act

        @pl.when(j == 0)
        def _():
            acc_ref[...] = x_ref[...]
        acc_ref[...] += jnp.dot(act, w_ref[...], preferred_element_type=F32)

        @pl.when(j == N_CHUNKS - 1)
        def _():
            o_ref[...] = acc_ref[...]

    zt = lambda off: pl.BlockSpec((tm, FFN_CHUNK), lambda i, j: (i, j + off))
    cwt = lambda off: pl.BlockSpec((CW_ROWS, FFN_CHUNK), lambda i, j: (0, j + off))
    return _pcall(
        body, name=name, grid=(s // tm, N_CHUNKS),
        in_specs=[zt(0), zt(N_CHUNKS), prev(0), prev(N_CHUNKS), cwt(0), cwt(N_CHUNKS),
                  pl.BlockSpec((FFN_CHUNK, D_MODEL), lambda i, j: (j, 0)), pl.BlockSpec((tm, D_MODEL), lambda i, j: (i, 0))],
        out_specs=[pl.BlockSpec((tm, D_MODEL), lambda i, j: (i, 0)), pl.BlockSpec((tm, FFN_CHUNK), lambda i, j: (i, j))],
        out_shape=[jax.ShapeDtypeStruct((s, D_MODEL), F32), jax.ShapeDtypeStruct((s, D_FF), _MXU)],
        scratch_shapes=[pltpu.VMEM((tm, D_MODEL), F32)],
        compiler_params=_params(("parallel", "arbitrary")),
    )(z, z, z, z, cw, cw, wd, x)


def _ffn_down_bwd(dx, z, cw, wdt, name):
    s = dx.shape[0]
    tm = _tile(s, FFN_ROWS)

    def body(dx_ref, zg_ref, zu_ref, pg_ref, pu_ref, cg_ref, cu_ref, wt_ref, dg_ref, du_ref, dcg_ref, dcu_ref):
        i = pl.program_id(1)
        first = i > 0

        @pl.when(i == 0)
        def _():
            dcg_ref[...] = jnp.zeros_like(dcg_ref)
            dcu_ref[...] = jnp.zeros_like(dcu_ref)

        dact = _dot(dx_ref[...], wt_ref[...])
        zg, g1, g2 = _conv(zg_ref[...], jnp.where(first, pg_ref[...], 0.0), cg_ref)
        zu, u1, u2 = _conv(zu_ref[...], jnp.where(first, pu_ref[...], 0.0), cu_ref)
        sg = jax.nn.sigmoid(zg)
        silu = zg * sg
        dzu = dact * silu
        dzg = dact * zu * (sg * (1.0 + zg * (1.0 - sg)))
        dg_ref[...] = dzg
        du_ref[...] = dzu
        for ref, dzc, cur, s1, s2 in ((dcg_ref, dzg, zg_ref[...], g1, g2), (dcu_ref, dzu, zu_ref[...], u1, u2)):
            ref[0:1, :] += jnp.sum(dzc * s2, axis=0, keepdims=True)
            ref[1:2, :] += jnp.sum(dzc * s1, axis=0, keepdims=True)
            ref[2:3, :] += jnp.sum(dzc * cur, axis=0, keepdims=True)
            ref[3:4, :] += jnp.sum(dzc, axis=0, keepdims=True)

    zt = lambda off: pl.BlockSpec((tm, FFN_CHUNK), lambda j, i: (i, j + off))
    r = tm // SUBLANES
    pv = lambda off: pl.BlockSpec((SUBLANES, FFN_CHUNK), lambda j, i: (jnp.maximum(i * r - 1, 0), j + off))
    cwt = lambda off: pl.BlockSpec((CW_ROWS, FFN_CHUNK), lambda j, i: (0, j + off))
    out_t = pl.BlockSpec((tm, FFN_CHUNK), lambda j, i: (i, j))
    dc_t = pl.BlockSpec((CW_ROWS, FFN_CHUNK), lambda j, i: (0, j))
    dzg, dzu, dcg, dcu = _pcall(
        body, name=name, grid=(N_CHUNKS, s // tm),
        in_specs=[pl.BlockSpec((tm, D_MODEL), lambda j, i: (i, 0)), zt(0), zt(N_CHUNKS), pv(0), pv(N_CHUNKS),
                  cwt(0), cwt(N_CHUNKS), pl.BlockSpec((D_MODEL, FFN_CHUNK), lambda j, i: (0, j))],
        out_specs=[out_t, out_t, dc_t, dc_t],
        out_shape=[jax.ShapeDtypeStruct((s, D_FF), F32), jax.ShapeDtypeStruct((s, D_FF), F32),
                   jax.ShapeDtypeStruct((CW_ROWS, D_FF), F32), jax.ShapeDtypeStruct((CW_ROWS, D_FF), F32)],
        compiler_params=_params(("parallel", "arbitrary")),
    )(dx, z, z, z, z, cw, cw, wdt)
    return dzg, dzu, jnp.concatenate([dcg, dcu], axis=1)


def _ffn_up_bwd(dzg, dzu, cw, wut, x, g, dres, name):
    s = x.shape[0]
    tm = _tile(s, FFN_ROWS)
    _, nxt = _halo_specs(tm, s)
    nt = s // tm

    def body(dg_ref, du_ref, ng_ref, nu_ref, cg_ref, cu_ref, wg_ref, wu_ref, x_ref, g_ref, dres_ref,
             dzg_ref, dzu_ref, dx_ref, dgn_ref, acc_ref):
        i, j = pl.program_id(0), pl.program_id(1)
        last = i < nt - 1

        def conv_bwd(cur, nxt8, cw_ref):
            up1 = _shift_up(cur, nxt8, 1)
            up2 = _shift_up(cur, nxt8, 2)
            return cur * cw_ref[2:3, :] + up1 * cw_ref[1:2, :] + up2 * cw_ref[0:1, :]

        dzg_ = conv_bwd(dg_ref[...], jnp.where(last, ng_ref[...], 0.0), cg_ref).astype(_MXU)
        dzu_ = conv_bwd(du_ref[...], jnp.where(last, nu_ref[...], 0.0), cu_ref).astype(_MXU)
        dzg_ref[...] = dzg_
        dzu_ref[...] = dzu_

        @pl.when(j == 0)
        def _():
            acc_ref[...] = jnp.zeros_like(acc_ref)
        acc_ref[...] += (jnp.dot(dzg_, wg_ref[...], preferred_element_type=F32)
                         + jnp.dot(dzu_, wu_ref[...], preferred_element_type=F32))

        @pl.when((i == 0) & (j == 0))
        def _():
            dgn_ref[...] = jnp.zeros_like(dgn_ref)

        @pl.when(j == N_CHUNKS - 1)
        def _():
            dx, dgn = _rms_bwd(x_ref[...], g_ref[...], acc_ref[...])
            dx_ref[...] = dres_ref[...] + dx
            dgn_ref[...] += dgn

    zt = pl.BlockSpec((tm, FFN_CHUNK), lambda i, j: (i, j))
    cwt = lambda off: pl.BlockSpec((CW_ROWS, FFN_CHUNK), lambda i, j: (0, j + off))
    wt = lambda off: pl.BlockSpec((FFN_CHUNK, D_MODEL), lambda i, j: (j + off, 0))
    row = pl.BlockSpec((tm, D_MODEL), lambda i, j: (i, 0))
    vec = pl.BlockSpec((1, D_MODEL), lambda i, j: (0, 0))
    return _pcall(
        body, name=name, grid=(nt, N_CHUNKS),
        in_specs=[zt, zt, nxt(0), nxt(0), cwt(0), cwt(N_CHUNKS), wt(0), wt(N_CHUNKS), row, vec, row],
        out_specs=[zt, zt, row, vec],
        out_shape=[jax.ShapeDtypeStruct((s, D_FF), _MXU), jax.ShapeDtypeStruct((s, D_FF), _MXU),
                   jax.ShapeDtypeStruct((s, D_MODEL), F32), jax.ShapeDtypeStruct((1, D_MODEL), F32)],
        scratch_shapes=[pltpu.VMEM((tm, D_MODEL), F32)],
        compiler_params=_params(("arbitrary", "arbitrary")),
    )(dzg, dzu, dzg, dzu, cw, cw, wut, wut, x, g, dres)


def _final_loss(x, g, tgt, name):
    s = x.shape[0]
    tm = _tile(s, 256)

    def body(x_ref, g_ref, t_ref, loss_ref, dx_ref, dg_ref):
        @pl.when(pl.program_id(0) == 0)
        def _():
            loss_ref[...] = jnp.zeros_like(loss_ref)
            dg_ref[...] = jnp.zeros_like(dg_ref)

        xv, gv = x_ref[...], g_ref[...]
        err = xv * _rms(xv) * gv - t_ref[...]
        per_tok = jnp.mean(err * err, axis=-1, keepdims=True)
        loss_ref[...] += 0.5 * jnp.sum(per_tok, axis=0, keepdims=True)
        dx, dg = _rms_bwd(xv, gv, err * (1.0 / D_MODEL))
        dx_ref[...] = dx
        dg_ref[...] += dg

    row = pl.BlockSpec((tm, D_MODEL), lambda i: (i, 0))
    vec = pl.BlockSpec((1, D_MODEL), lambda i: (0, 0))
    return _pcall(
        body, name=name, grid=(s // tm,),
        in_specs=[row, vec, row], out_specs=[pl.BlockSpec((1, 1), lambda i: (0, 0)), row, vec],
        out_shape=[jax.ShapeDtypeStruct((1, 1), F32), jax.ShapeDtypeStruct((s, D_MODEL), F32),
                   jax.ShapeDtypeStruct((1, D_MODEL), F32)],
        compiler_params=_params(("arbitrary",)),
    )(x, g, tgt)


def _adamw(parts, w, m, v, name):
    r, c = w.shape
    tr = _tile(r, 256)
    c1 = 1.0 - ADAM_B1 ** ADAM_STEP
    c2 = 1.0 - ADAM_B2 ** ADAM_STEP

    def body(p_ref, w_ref, m_ref, v_ref, g_ref, d_ref, mo_ref, vo_ref):
        g = p_ref[0].astype(F32)
        for i in range(1, N_DEV):
            g = g + p_ref[i].astype(F32)
        mn = ADAM_B1 * m_ref[...] + (1.0 - ADAM_B1) * g
        vn = ADAM_B2 * v_ref[...] + (1.0 - ADAM_B2) * (g * g)
        g_ref[...] = g
        mo_ref[...] = mn
        vo_ref[...] = vn
        d_ref[...] = -ADAM_LR * ((mn / c1) / (jnp.sqrt(vn / c2) + ADAM_EPS) + ADAM_WD * w_ref[...])

    t2 = pl.BlockSpec((tr, c), lambda i: (i, 0))
    return _pcall(
        body, name=name, grid=(r // tr,),
        in_specs=[pl.BlockSpec((N_DEV, tr, c), lambda i: (0, i, 0)), t2, t2, t2],
        out_specs=[t2] * 4, out_shape=[jax.ShapeDtypeStruct((r, c), F32)] * 4,
        compiler_params=_params(("parallel",)),
    )(parts, w, m, v)


SMALL = ("norm1_g", "sgu_norm_g", "sgu_w", "sgu_b", "pool_w", "pool_scale", "mix_norm_g", "norm2_g", "conv_b", "final_g")
SHARDED = ("w_in", "w_o", "w_up", "conv_w", "w_down")
ORDER = ("norm1_g", "w_in", "sgu_norm_g", "sgu_w", "sgu_b", "pool_w", "pool_scale", "mix_norm_g", "w_o", "norm2_g",
         "w_up", "conv_w", "conv_b", "w_down", "final_g")


def _pack(tree):
    return jnp.concatenate([tree[n].reshape(-1) for n in SMALL]).reshape(-1, LANES)


def _unpack(flat, like):
    out, off = {}, 0
    flat = flat.reshape(-1)
    for n in SMALL:
        size = math.prod(like[n].shape)
        out[n] = flat[off:off + size].reshape(like[n].shape)
        off += size
    return out


def _block_diag(pw):
    z = jnp.zeros((W_C, W_C), pw.dtype)
    for g in range(4):
        z = z.at[g * 64:(g + 1) * 64, g * 64:(g + 1) * 64].set(pw[g])
    return z


def kernel(x, norm1_g, w_in, sgu_norm_g, sgu_w, sgu_b, pool_w, pool_scale, mix_norm_g, w_o, norm2_g, w_up, conv_w, conv_b, w_down, final_g, loss_target, m_norm1_g, m_w_in, m_sgu_norm_g, m_sgu_w, m_sgu_b, m_pool_w, m_pool_scale, m_mix_norm_g, m_w_o, m_norm2_g, m_w_up, m_conv_w, m_conv_b, m_w_down, m_final_g, v_norm1_g, v_w_in, v_sgu_norm_g, v_sgu_w, v_sgu_b, v_pool_w, v_pool_scale, v_mix_norm_g, v_w_o, v_norm2_g, v_w_up, v_conv_w, v_conv_b, v_w_down, v_final_g):
    weights = dict(norm1_g=norm1_g, w_in=w_in, sgu_norm_g=sgu_norm_g, sgu_w=sgu_w, sgu_b=sgu_b, pool_w=pool_w,
                   pool_scale=pool_scale, mix_norm_g=mix_norm_g, w_o=w_o, norm2_g=norm2_g, w_up=w_up, conv_w=conv_w,
                   conv_b=conv_b, w_down=w_down, final_g=final_g)
    mom = dict(norm1_g=m_norm1_g, w_in=m_w_in, sgu_norm_g=m_sgu_norm_g, sgu_w=m_sgu_w, sgu_b=m_sgu_b, pool_w=m_pool_w,
               pool_scale=m_pool_scale, mix_norm_g=m_mix_norm_g, w_o=m_w_o, norm2_g=m_norm2_g, w_up=m_w_up,
               conv_w=m_conv_w, conv_b=m_conv_b, w_down=m_w_down, final_g=m_final_g)
    var = dict(norm1_g=v_norm1_g, w_in=v_w_in, sgu_norm_g=v_sgu_norm_g, sgu_w=v_sgu_w, sgu_b=v_sgu_b, pool_w=v_pool_w,
               pool_scale=v_pool_scale, mix_norm_g=v_mix_norm_g, w_o=v_w_o, norm2_g=v_norm2_g, w_up=v_w_up,
               conv_w=v_conv_w, conv_b=v_conv_b, w_down=v_w_down, final_g=v_final_g)
    depth = w_in.shape[0]
    s = x.shape[1]
    xs = x.reshape(s, D_MODEL)
    tgt = loss_target.reshape(s, D_MODEL)

    assert depth >= 2
    (g_in0,) = _exchange([w_in[0].astype(_MXU)], "gather_w_in0", False)
    w_in0 = jnp.transpose(g_in0, (1, 0, 2)).reshape(D_MODEL, IN_COLS)
    gather_rest = _Exchange([w_in[1:].astype(_MXU), w_o.astype(_MXU), w_up.astype(_MXU), conv_w, w_down.astype(_MXU)], False)

    tril = jnp.tril(jnp.ones((CHUNK, CHUNK), bool))
    layers = []
    for l in range(depth):
        wbd = _block_diag(pool_w[l])
        layers.append(dict(
            g1=norm1_g[l][None], gn=sgu_norm_g[l][None], wm=jnp.where(tril[None], sgu_w[l], 0.0).astype(_MXU),
            bias=jnp.repeat(sgu_b[l].T, HEAD_DIM, axis=1),
            wbd=wbd.astype(_MXU), wbd_t=wbd.T.astype(_MXU), sc=pool_scale[l][None],
            gmix=mix_norm_g[l][None], g2=norm2_g[l][None]))
    layers[0].update(w_in=w_in0, w_in_t=w_in0.T)

    def place_gathered(g_in, g_o, g_up, g_cw, g_dn):
        full_in = jnp.transpose(g_in, (1, 2, 0, 3)).reshape(depth - 1, D_MODEL, IN_COLS)
        full_o = jnp.transpose(g_o, (1, 0, 2, 3)).reshape(depth, D_MODEL, D_MODEL)
        full_up = jnp.transpose(g_up, (1, 2, 0, 3)).reshape(depth, D_MODEL, 2 * D_FF)
        full_cw = jnp.transpose(g_cw, (1, 2, 0, 3)).reshape(depth, 3, 2 * D_FF)
        full_dn = jnp.transpose(g_dn, (1, 0, 2, 3)).reshape(depth, D_FF, D_MODEL)
        for l in range(depth):
            if l > 0:
                layers[l].update(w_in=full_in[l - 1], w_in_t=full_in[l - 1].T)
            layers[l].update(
                w_o=full_o[l], w_o_t=full_o[l].T, w_up=full_up[l], w_up_t=full_up[l].T,
                cw=jnp.concatenate([full_cw[l], conv_b[l][None], jnp.zeros((CW_ROWS - 4, 2 * D_FF), F32)], axis=0),
                w_dn=full_dn[l], w_dn_t=full_dn[l].T)

    saved = []
    cur = xs
    for l, p in enumerate(layers):
        a_in, qkv, p_in, h1 = _inproj_fwd(cur, p["g1"], p["w_in"], f"inproj_fwd{l}")
        y_a = _sgu_fwd(a_in, p["gn"], p["wm"], p["bias"], f"sgu_fwd{l}")
        y_b, cm, gathered_w = _attn_fwd(qkv, f"attn_fwd{l}", gather_rest if l == 0 else None)
        if l == 0:
            place_gathered(*gathered_w)
        y_c = _pool_fwd(p_in, p["wbd"], p["sc"], f"pool_fwd{l}")
        x_mid, yn = _mix_fwd(y_a, y_b, y_c, p["gmix"], p["w_o"], cur, f"mix_fwd{l}")
        z, h2 = _ffn_up_fwd(x_mid, p["g2"], p["w_up"], f"ffn_up_fwd{l}")
        x_out, act = _ffn_down_fwd(z, p["cw"], p["w_dn"], x_mid, f"ffn_down_fwd{l}")
        saved.append(dict(x_in=cur, a_in=a_in, qkv=qkv, p_in=p_in, h1=h1, y_a=y_a, y_b=y_b, cm=cm, y_c=y_c, x_mid=x_mid,
                          yn=yn, z=z, h2=h2, act=act))
        cur = x_out
    loss_part, dx, dg_final = _final_loss(cur, final_g[None], tgt, "final_loss")

    small = {n: [None] * depth for n in SMALL if n != "final_g"}
    big = {n: [None] * depth for n in SHARDED}
    early = [(n, l) for n in SHARDED for l in range(depth) if (n, l) != ("w_in", 0)]
    wire = lambda n, t: t if n == "conv_w" else t.astype(GRAD_WIRE)
    for l in reversed(range(depth)):
        p, sv = layers[l], saved[l]
        dzg, dzu, dcw = _ffn_down_bwd(dx, sv["z"], p["cw"], p["w_dn_t"], f"ffn_down_bwd{l}")
        big["w_down"][l] = _mm_tn(sv["act"], dx, f"dw_down{l}").reshape(N_DEV, D_FF // N_DEV, D_MODEL)
        dzg_b, dzu_b, dx_mid, dg2 = _ffn_up_bwd(dzg, dzu, p["cw"], p["w_up_t"], sv["x_mid"], p["g2"], dx, f"ffn_up_bwd{l}")
        dw_up = jnp.stack([_mm_tn(sv["h2"], dzg_b, f"dw_up_g{l}"), _mm_tn(sv["h2"], dzu_b, f"dw_up_u{l}")])
        big["w_up"][l] = jnp.transpose(dw_up.reshape(2, D_MODEL, N_DEV // 2, 2 * D_FF // N_DEV), (0, 2, 1, 3)).reshape(
            N_DEV, D_MODEL, 2 * D_FF // N_DEV)
        big["conv_w"][l] = jnp.transpose(dcw[0:3].reshape(3, N_DEV, 2 * D_FF // N_DEV), (1, 0, 2))
        small["conv_b"][l] = dcw[3]
        small["norm2_g"][l] = dg2[0]
        dya, dyb, dyc, dgmix = _mix_bwd(dx_mid, sv["y_a"], sv["y_b"], sv["y_c"], p["gmix"], p["w_o_t"], f"mix_bwd{l}")
        small["mix_norm_g"][l] = dgmix[0]
        big["w_o"][l] = _mm_tn(sv["yn"], dx_mid, f"dw_o{l}").reshape(N_DEV, D_MODEL // N_DEV, D_MODEL)
        dp, dwbd, dsc = _pool_bwd(sv["p_in"], dyc, p["wbd"], p["wbd_t"], p["sc"], f"pool_bwd{l}")
        small["pool_w"][l] = jnp.stack([dwbd[g * 64:(g + 1) * 64, g * 64:(g + 1) * 64] for g in range(4)])
        small["pool_scale"][l] = dsc[0]
        scatter_early = _Exchange([wire(n, big[n][ll]) for n, ll in early], True) if l == 0 else None
        dq, dk, dv, recv_early = _attn_bwd(sv["qkv"], sv["cm"], dyb, f"attn_bwd{l}", scatter_early)
        if l == 0:
            recv_early_all = recv_early
        da, dwm, dbias, dgn = _sgu_bwd(sv["a_in"], dya, p["gn"], p["wm"], p["bias"], f"sgu_bwd{l}")
        small["sgu_w"][l] = dwm
        small["sgu_b"][l] = jnp.sum(dbias.reshape(CHUNK, 4, HEAD_DIM), axis=-1).T
        small["sgu_norm_g"][l] = dgn[0]
        dx, dg1 = _inproj_bwd(da, dq, dk, dv, dp, p["w_in_t"], sv["x_in"], p["g1"], dx_mid, f"inproj_bwd{l}")
        small["norm1_g"][l] = dg1[0]
        pieces = (da, dq, dk, dv, dp)
        dw_in = jnp.concatenate([_mm_tn(sv["h1"], t, f"dw_in{i}_{l}") for i, t in enumerate(pieces)], axis=1)
        big["w_in"][l] = jnp.transpose(dw_in.reshape(D_MODEL, N_DEV, IN_COLS // N_DEV), (1, 0, 2))

    part = {n: jnp.stack(small[n]) for n in small}
    part["final_g"] = dg_final[0]
    packed = _pack(part)
    recv_in0, gathered = _exchange([wire("w_in", big["w_in"][0]), jnp.broadcast_to(packed, (N_DEV,) + packed.shape)],
                                   "scatter_last", True)
    recv = dict(zip(early, recv_early_all))
    recv[("w_in", 0)] = recv_in0

    out_g, out_d, out_m, out_v = {}, {}, {}, {}
    for n in SHARDED:
        res = [_adamw(recv[(n, l)], weights[n][l], mom[n][l], var[n][l], f"adamw_{n}{l}") for l in range(depth)]
        out_g[n], out_d[n], out_m[n], out_v[n] = (jnp.stack([r[i] for r in res]) for i in range(4))

    sg, sd, sm, sv_ = _adamw(gathered, _pack(weights), _pack(mom), _pack(var), "adamw_small")
    for tree, flat in ((out_g, sg), (out_d, sd), (out_m, sm), (out_v, sv_)):
        tree.update(_unpack(flat, weights))

    loss = lax.psum(loss_part[0, 0], ("x", "y", "c"))
    grad_x = dx.reshape(1, s, D_MODEL)
    return (loss, grad_x, *[out_g[n] for n in ORDER], *[out_d[n] for n in ORDER], *[out_m[n] for n in ORDER],
            *[out_v[n] for n in ORDER])
```

```python
import functools
import math

import jax
import jax.numpy as jnp
import numpy as np
from jax import lax
from jax.experimental import pallas as pl
from jax.experimental.pallas import tpu as pltpu

F32 = jnp.float32
BF16 = jnp.bfloat16
_MXU = jnp.bfloat16
GRAD_WIRE = jnp.bfloat16

D_MODEL = 1024
W_A = 256
W_B = 512
W_C = 256
HEAD_DIM = 64
IN_COLS = 2 * W_A + 3 * W_B + W_C
D_FF = 2816
CHUNK = 128
POOL_WINDOWS = (2, 4, 8, 16)
EPS = 1e-6
N_DEV = 8
LANES = 128
SUBLANES = 8
VMEM_LIMIT = 48 * 1024 * 1024

ADAM_LR = 0.001
ADAM_B1 = 0.9
ADAM_B2 = 0.999
ADAM_EPS = 1e-08
ADAM_WD = 0.01
ADAM_STEP = 10

INV_SQRT2 = 1.0 / math.sqrt(2.0)
INV_SQRT_2PI = 1.0 / math.sqrt(2.0 * math.pi)


def _pcall(body, **kw):
    return pl.pallas_call(body, **kw)


def _params(dims=None):
    return pltpu.CompilerParams(dimension_semantics=dims, vmem_limit_bytes=VMEM_LIMIT)


def _tile(n, pref, mult=SUBLANES):
    t = min(n, pref) // mult * mult
    while t >= mult:
        if n % t == 0:
            return t
        t -= mult
    return n


def _iota(shape, dim):
    return lax.broadcasted_iota(jnp.int32, shape, dim)


def _dot(a, b):
    return jnp.dot(a.astype(_MXU), b.astype(_MXU), preferred_element_type=F32)


def _dot_nt(a, b):
    return lax.dot_general(a.astype(_MXU), b.astype(_MXU), (((1,), (1,)), ((), ())), preferred_element_type=F32)


def _dot_tn(a, b):
    return lax.dot_general(a.astype(_MXU), b.astype(_MXU), (((0,), (0,)), ((), ())), preferred_element_type=F32)


def _split(x):
    hi = x.astype(BF16)
    lo = (x - hi.astype(F32)).astype(BF16)
    return hi, lo


def _dot_sel(x, sel):
    hi, lo = _split(x)
    return jnp.dot(hi, sel, preferred_element_type=F32) + jnp.dot(lo, sel, preferred_element_type=F32)


def _sel_dot(sel, x):
    hi, lo = _split(x)
    return jnp.dot(sel, hi, preferred_element_type=F32) + jnp.dot(sel, lo, preferred_element_type=F32)


def _group_mat(n):
    r = jnp.right_shift(_iota((n, n), 0), 6)
    c = jnp.right_shift(_iota((n, n), 1), 6)
    return (r == c).astype(BF16)


def _lane_group_mask(n, g):
    lane = _iota((1, n), 1)
    return (lane >= g * HEAD_DIM) & (lane < (g + 1) * HEAD_DIM)


def _gelu(a):
    return 0.5 * a * (1.0 + lax.erf(a * INV_SQRT2))


def _gelu_grad(a):
    return 0.5 * (1.0 + lax.erf(a * INV_SQRT2)) + a * jnp.exp(-0.5 * a * a) * INV_SQRT_2PI


def _rms(x):
    return lax.rsqrt(jnp.mean(x * x, axis=-1, keepdims=True) + EPS)


def _rms_bwd(x, g, dy):
    r = _rms(x)
    dyg = dy * g
    m2 = jnp.mean(dyg * x, axis=-1, keepdims=True)
    dx = r * dyg - x * (r * r * r) * m2
    dg = jnp.sum(dy * x * r, axis=0, keepdims=True)
    return dx, dg


def _shift_down(cur, prev8, k):
    rolled = pltpu.roll(cur, k, 0)
    row8 = _iota(prev8.shape, 0)
    top = jnp.where(row8 < k, pltpu.roll(prev8, k, 0), rolled[0:SUBLANES])
    return jnp.concatenate([top, rolled[SUBLANES:]], axis=0)


def _shift_up(cur, next8, k):
    n = cur.shape[0]
    rolled = pltpu.roll(cur, n - k, 0)
    row8 = _iota(next8.shape, 0)
    bot = jnp.where(row8 >= SUBLANES - k, pltpu.roll(next8, SUBLANES - k, 0), rolled[n - SUBLANES:])
    return jnp.concatenate([rolled[:n - SUBLANES], bot], axis=0)


def _mesh_pos():
    return lax.axis_index("x"), lax.axis_index("y"), lax.axis_index("c")


def _peer(x, y, c, k):
    px = 1 - x if k & 4 else x
    py = 1 - y if k & 2 else y
    pc = 1 - c if k & 1 else c
    return px, py, pc


class _Exchange:
    def __init__(self, arrs, scatter):
        self.arrs, self.scatter, self.n = list(arrs), scatter, len(arrs)
        self.out_shapes = [jax.ShapeDtypeStruct(a.shape if scatter else (N_DEV,) + a.shape, a.dtype) for a in arrs]
        self.specs = [pl.BlockSpec(memory_space=pl.ANY)] * self.n
        self.semaphores = [pltpu.SemaphoreType.DMA((self.n * (N_DEV - 1),)),
                           pltpu.SemaphoreType.DMA((self.n * (N_DEV - 1),)), pltpu.SemaphoreType.DMA((self.n,))]

    def _copies(self, ins, outs, sems):
        send, recv, loc = sems
        x, y, c = _mesh_pos()
        me = 4 * x + 2 * y + c
        src = (lambda a, idx: ins[a].at[idx]) if self.scatter else (lambda a, idx: ins[a])
        starts = [pltpu.make_async_copy(src(a, me), outs[a].at[me], loc.at[a]) for a in range(self.n)]
        waits = list(starts)
        for k in range(1, N_DEV):
            px, py, pc = _peer(x, y, c, k)
            pidx = 4 * px + 2 * py + pc
            for a in range(self.n):
                s = a * (N_DEV - 1) + k - 1
                common = dict(src_ref=src(a, pidx), send_sem=send.at[s], recv_sem=recv.at[s],
                              device_id=(px, py, pc), device_id_type=pl.DeviceIdType.MESH)
                starts.append(pltpu.make_async_remote_copy(dst_ref=outs[a].at[me], **common))
                waits.append(pltpu.make_async_remote_copy(dst_ref=outs[a].at[pidx], **common))
        return starts, waits

    def start(self, ins, outs, sems):
        for cp in self._copies(ins, outs, sems)[0]:
            cp.start()

    def wait(self, ins, outs, sems):
        for cp in self._copies(ins, outs, sems)[1]:
            cp.wait()


def _with_exchange(compute, ride, n_in, n_out, n_scratch, last_step):
    if ride is None:
        return compute
    nx = ride.n

    def body(*refs):
        ins, ride_in = refs[:n_in], refs[n_in:n_in + nx]
        outs = refs[n_in + nx:n_in + nx + n_out]
        ride_out = refs[n_in + nx + n_out:n_in + 2 * nx + n_out]
        scratch = refs[n_in + 2 * nx + n_out:n_in + 2 * nx + n_out + n_scratch]
        sems = refs[n_in + 2 * nx + n_out + n_scratch:]
        step = (pl.program_id(0), pl.program_id(1))

        @pl.when((step[0] == 0) & (step[1] == 0))
        def _():
            ride.start(ride_in, ride_out, sems)

        compute(*ins, *outs, *scratch)

        @pl.when((step[0] == last_step[0]) & (step[1] == last_step[1]))
        def _():
            ride.wait(ride_in, ride_out, sems)

    return body


def _exchange(arrs, name, scatter):
    ex = _Exchange(arrs, scatter)
    n = ex.n

    def body(*refs):
        ins, outs, sems = refs[:n], refs[n:2 * n], refs[2 * n:]
        ex.start(ins, outs, sems)
        ex.wait(ins, outs, sems)

    return _pcall(body, name=name, out_shape=ex.out_shapes, in_specs=ex.specs, out_specs=ex.specs,
                  scratch_shapes=ex.semaphores, compiler_params=pltpu.CompilerParams(has_side_effects=True))(*arrs)


def _mm_tn(a, b, name):
    s, m = a.shape
    n = b.shape[1]
    tm, tn, tk = _tile(m, 1408, LANES), _tile(n, 1408, LANES), _tile(s, 1024)

    def body(a_ref, b_ref, o_ref):
        @pl.when(pl.program_id(2) == 0)
        def _():
            o_ref[...] = jnp.zeros_like(o_ref)
        o_ref[...] += _dot_tn(a_ref[...], b_ref[...])

    return _pcall(
        body, name=name, grid=(m // tm, n // tn, s // tk),
        in_specs=[pl.BlockSpec((tk, tm), lambda i, j, k: (k, i)), pl.BlockSpec((tk, tn), lambda i, j, k: (k, j))],
        out_specs=pl.BlockSpec((tm, tn), lambda i, j, k: (i, j)),
        out_shape=jax.ShapeDtypeStruct((m, n), F32),
        compiler_params=_params(("parallel", "parallel", "arbitrary")),
    )(a, b)


def _inproj_fwd(x, g, w, name):
    s = x.shape[0]
    tm = _tile(s, 256)

    def body(x_ref, g_ref, w_ref, a_ref, qkv_ref, p_ref, h_ref):
        xv = x_ref[...]
        h = (xv * _rms(xv) * g_ref[...]).astype(_MXU)
        h_ref[...] = h
        a_ref[...] = jnp.dot(h, w_ref[:, 0:2 * W_A], preferred_element_type=F32)
        q = jnp.dot(h, w_ref[:, 2 * W_A:2 * W_A + W_B], preferred_element_type=F32)
        qkv_ref[:, 0:W_B] = (q * 0.125).astype(_MXU)
        kv = jnp.dot(h, w_ref[:, 2 * W_A + W_B:2 * W_A + 3 * W_B], preferred_element_type=F32)
        qkv_ref[:, W_B:3 * W_B] = kv.astype(_MXU)
        p_ref[...] = jnp.dot(h, w_ref[:, 2 * W_A + 3 * W_B:IN_COLS], preferred_element_type=F32)

    row = lambda n: pl.BlockSpec((tm, n), lambda i: (i, 0))
    return _pcall(
        body, name=name, grid=(s // tm,),
        in_specs=[row(D_MODEL), pl.BlockSpec((1, D_MODEL), lambda i: (0, 0)),
                  pl.BlockSpec((D_MODEL, IN_COLS), lambda i: (0, 0))],
        out_specs=[row(2 * W_A), row(3 * W_B), row(W_C), row(D_MODEL)],
        out_shape=[jax.ShapeDtypeStruct((s, 2 * W_A), F32), jax.ShapeDtypeStruct((s, 3 * W_B), _MXU),
                   jax.ShapeDtypeStruct((s, W_C), F32), jax.ShapeDtypeStruct((s, D_MODEL), _MXU)],
        compiler_params=_params(("parallel",)),
    )(x, g, w)


def _inproj_bwd(da, dq, dk, dv, dp, wt, x, g, dres, name):
    s = x.shape[0]
    tm = _tile(s, 256)

    def body(da_ref, dq_ref, dk_ref, dv_ref, dp_ref, wt_ref, x_ref, g_ref, dres_ref, dx_ref, dg_ref):
        dh = _dot(da_ref[...], wt_ref[0:2 * W_A, :])
        dh += _dot(dq_ref[...], wt_ref[2 * W_A:2 * W_A + W_B, :])
        dh += _dot(dk_ref[...], wt_ref[2 * W_A + W_B:2 * W_A + 2 * W_B, :])
        dh += _dot(dv_ref[...], wt_ref[2 * W_A + 2 * W_B:2 * W_A + 3 * W_B, :])
        dh += _dot(dp_ref[...], wt_ref[2 * W_A + 3 * W_B:IN_COLS, :])
        dx, dg = _rms_bwd(x_ref[...], g_ref[...], dh)
        dx_ref[...] = dres_ref[...] + dx

        @pl.when(pl.program_id(0) == 0)
        def _():
            dg_ref[...] = jnp.zeros_like(dg_ref)
        dg_ref[...] += dg

    row = lambda n: pl.BlockSpec((tm, n), lambda i: (i, 0))
    vec = pl.BlockSpec((1, D_MODEL), lambda i: (0, 0))
    return _pcall(
        body, name=name, grid=(s // tm,),
        in_specs=[row(2 * W_A), row(W_B), row(W_B), row(W_B), row(W_C),
                  pl.BlockSpec((IN_COLS, D_MODEL), lambda i: (0, 0)), row(D_MODEL), vec, row(D_MODEL)],
        out_specs=[row(D_MODEL), vec],
        out_shape=[jax.ShapeDtypeStruct((s, D_MODEL), F32), jax.ShapeDtypeStruct((1, D_MODEL), F32)],
        compiler_params=_params(("arbitrary",)),
    )(da, dq, dk, dv, dp, wt, x, g, dres)


def _sgu_core(a, gn, wm_ref, bias):
    ga = _gelu(a)
    u, v0 = ga[:, 0:W_A], ga[:, W_A:2 * W_A]
    r = lax.rsqrt(_dot_sel(v0 * v0, _group_mat(W_A)) * (1.0 / HEAD_DIM) + EPS)
    vn = v0 * r * gn
    sv = bias
    for h in range(W_A // HEAD_DIM):
        sv = sv + _dot(wm_ref[h], jnp.where(_lane_group_mask(W_A, h), vn, 0.0))
    return u, v0, r, vn, sv


def _sgu_fwd(a, gn, wm, bias, name):
    s = a.shape[0]

    def body(a_ref, gn_ref, wm_ref, b_ref, y_ref):
        u, _, _, _, sv = _sgu_core(a_ref[...], gn_ref[...], wm_ref, b_ref[...])
        y_ref[...] = u * sv

    return _pcall(
        body, name=name, grid=(s // CHUNK,),
        in_specs=[pl.BlockSpec((CHUNK, 2 * W_A), lambda i: (i, 0)), pl.BlockSpec((1, W_A), lambda i: (0, 0)),
                  pl.BlockSpec((4, CHUNK, CHUNK), lambda i: (0, 0, 0)), pl.BlockSpec((CHUNK, W_A), lambda i: (0, 0))],
        out_specs=pl.BlockSpec((CHUNK, W_A), lambda i: (i, 0)),
        out_shape=jax.ShapeDtypeStruct((s, W_A), F32),
        compiler_params=_params(("parallel",)),
    )(a, gn, wm, bias)


def _sgu_bwd(a, dy, gn, wm, bias, name):
    s = a.shape[0]

    def body(a_ref, dy_ref, gn_ref, wm_ref, b_ref, da_ref, dwm_ref, db_ref, dgn_ref):
        @pl.when(pl.program_id(0) == 0)
        def _():
            dwm_ref[...] = jnp.zeros_like(dwm_ref)
            db_ref[...] = jnp.zeros_like(db_ref)
            dgn_ref[...] = jnp.zeros_like(dgn_ref)

        av, gnv, dyv = a_ref[...], gn_ref[...], dy_ref[...]
        u, v0, r, vn, sv = _sgu_core(av, gnv, wm_ref, b_ref[...])
        du = dyv * sv
        ds = dyv * u
        db_ref[...] += ds
        tril = _iota((CHUNK, CHUNK), 1) <= _iota((CHUNK, CHUNK), 0)
        dvn = jnp.zeros_like(vn)
        for h in range(W_A // HEAD_DIM):
            dsm = jnp.where(_lane_group_mask(W_A, h), ds, 0.0)
            dwm_ref[h] += jnp.where(tril, _dot_nt(dsm, vn), 0.0)
            dvn = dvn + _dot_tn(wm_ref[h], dsm)
        dgn_ref[...] += jnp.sum(dvn * v0 * r, axis=0, keepdims=True)
        dvg = dvn * gnv
        m2 = _dot_sel(dvg * v0, _group_mat(W_A)) * (1.0 / HEAD_DIM)
        dv0 = r * dvg - v0 * (r * r * r) * m2
        gp = _gelu_grad(av)
        da_ref[:, 0:W_A] = du * gp[:, 0:W_A]
        da_ref[:, W_A:2 * W_A] = dv0 * gp[:, W_A:2 * W_A]

    return _pcall(
        body, name=name, grid=(s // CHUNK,),
        in_specs=[pl.BlockSpec((CHUNK, 2 * W_A), lambda i: (i, 0)), pl.BlockSpec((CHUNK, W_A), lambda i: (i, 0)),
                  pl.BlockSpec((1, W_A), lambda i: (0, 0)), pl.BlockSpec((4, CHUNK, CHUNK), lambda i: (0, 0, 0)),
                  pl.BlockSpec((CHUNK, W_A), lambda i: (0, 0))],
        out_specs=[pl.BlockSpec((CHUNK, 2 * W_A), lambda i: (i, 0)), pl.BlockSpec((4, CHUNK, CHUNK), lambda i: (0, 0, 0)),
                   pl.BlockSpec((CHUNK, W_A), lambda i: (0, 0)), pl.BlockSpec((1, W_A), lambda i: (0, 0))],
        out_shape=[jax.ShapeDtypeStruct((s, 2 * W_A), F32), jax.ShapeDtypeStruct((4, CHUNK, CHUNK), F32),
                   jax.ShapeDtypeStruct((CHUNK, W_A), F32), jax.ShapeDtypeStruct((1, W_A), F32)],
        compiler_params=_params(("arbitrary",)),
    )(a, dy, gn, wm, bias)


KBLK = 128
ROW_CHUNK = 64
FWD_UNROLL = 2
BWD_UNROLL = 4
MASKED_SCORE = -1e30


def _attn_fwd(qkv, name, ride=None):
    s = qkv.shape[0]
    tq = _tile(s, 256, KBLK)
    npairs = W_B // LANES
    assert s // KBLK <= LANES
    rides = [] if ride is None else [ride]

    def body(q_ref, k_ref, v_ref, o_ref, cm_ref, z_ref, zw_ref, sums_ref, carry_ref, hl_ref, a_ref):
        i = pl.program_id(1)
        q = q_ref[...]
        lane = _iota((1, LANES), 1)
        lane_lo = lane < HEAD_DIM
        hmask = (lane_lo, jnp.logical_not(lane_lo))
        tri2 = ((_iota((KBLK, 2 * KBLK), 0) >= _iota((KBLK, 2 * KBLK), 1))
                | (_iota((KBLK, 2 * KBLK), 1) >= KBLK)).astype(BF16)
        dmat = _iota((tq, KBLK), 1) - (_iota((tq, KBLK), 0) + i * tq)
        chunks = [slice(r, r + ROW_CHUNK) for r in range(0, tq, ROW_CHUNK)]
        heads = [slice(hh * LANES, (hh + 1) * LANES) for hh in range(2)]
        nk = (i + 1) * (tq // KBLK)

        cm_ref[...] = jnp.zeros_like(cm_ref)

        def before(b):
            return jnp.where((b >= 0) & (b < nk), -b * KBLK, jnp.iinfo(jnp.int32).min)

        def per_head(block):
            return jnp.concatenate([jnp.where(m, block, jnp.zeros_like(block)) for m in hmask], axis=0)

        def scores(b, p):
            ks = pl.multiple_of(jnp.clip(b, 0, nk - 1) * KBLK, KBLK)
            z_ref[p] = _dot_nt(q, per_head(k_ref[pl.ds(ks, KBLK), :]))

        def logs(b, p):
            t = before(b)
            for hh in range(2):
                for rows in chunks:
                    z = jnp.where(dmat[rows] < t, z_ref[p, rows, heads[hh]], MASKED_SCORE)
                    zw_ref[p, rows, heads[hh]] = z
                    l = -(jnp.maximum(z, 0.0) + jnp.log(1.0 + jnp.exp(-jnp.abs(z))))
                    hi, lo = _split(l)
                    hl_ref[p, hh, rows, 0:KBLK] = hi
                    hl_ref[p, hh, rows, KBLK:2 * KBLK] = lo

        def sums(b, p):
            for hh in range(2):
                sums_ref[p, hh] = (jnp.dot(hl_ref[p, hh, :, 0:KBLK], tri2, preferred_element_type=F32)
                                   + jnp.dot(hl_ref[p, hh, :, KBLK:2 * KBLK], tri2, preferred_element_type=F32))

        def weights(b, p):
            pick = lane == b
            for hh in range(2):
                for rows in chunks:
                    c = carry_ref[hh, rows, :]
                    arg = zw_ref[p, rows, heads[hh]] + c + sums_ref[p, hh, rows, 0:KBLK]
                    a_ref[p, rows, heads[hh]] = jnp.exp(arg).astype(_MXU)
                    cm_ref[rows, heads[hh]] = jnp.where(pick, c, cm_ref[rows, heads[hh]])
                    carry_ref[hh, rows, :] = c + sums_ref[p, hh, rows, KBLK:2 * KBLK]

        def out(b, p, acc):
            ks = pl.multiple_of(jnp.minimum(b, nk - 1) * KBLK, KBLK)
            vb = v_ref[pl.ds(ks, KBLK), :]
            for hh in range(2):
                acc = acc + jnp.dot(a_ref[p, :, heads[hh]], jnp.where(hmask[hh], vb, jnp.zeros_like(vb)),
                                    preferred_element_type=F32)
            return acc

        trips = (nk + 4 + FWD_UNROLL - 1) // FWD_UNROLL

        def step(it, acc):
            for u in range(FWD_UNROLL):
                b = trips * FWD_UNROLL - 1 - (FWD_UNROLL * it + u)
                p = 1 - u % 2
                acc = out(b, p, acc)
                weights(b - 1, 1 - p)
                sums(b - 2, p)
                logs(b - 3, 1 - p)
                scores(b - 4, p)
            return acc

        z_ref[...] = jnp.zeros_like(z_ref)
        zw_ref[...] = jnp.full_like(zw_ref, MASKED_SCORE)
        for ref in (sums_ref, carry_ref, hl_ref, a_ref):
            ref[...] = jnp.zeros_like(ref)
        o_ref[...] = lax.fori_loop(0, trips, step, q.astype(F32) * 0.0)

    scratch = [pltpu.VMEM((2, tq, 2 * KBLK), F32), pltpu.VMEM((2, tq, 2 * KBLK), F32),
               pltpu.VMEM((2, 2, tq, 2 * KBLK), F32), pltpu.VMEM((2, tq, KBLK), F32),
               pltpu.VMEM((2, 2, tq, 2 * KBLK), BF16), pltpu.VMEM((2, tq, 2 * KBLK), _MXU)]
    res = _pcall(
        _with_exchange(body, ride, 3, 2, len(scratch), (npairs - 1, s // tq - 1)), name=name, grid=(npairs, s // tq),
        in_specs=[pl.BlockSpec((tq, LANES), lambda p, i: (i, p)),
                  pl.BlockSpec((s, LANES), lambda p, i: (0, npairs + p)),
                  pl.BlockSpec((s, LANES), lambda p, i: (0, 2 * npairs + p))] + [sp for r in rides for sp in r.specs],
        out_specs=[pl.BlockSpec((tq, LANES), lambda p, i: (i, p)), pl.BlockSpec((tq, 2 * LANES), lambda p, i: (i, p))]
                  + [sp for r in rides for sp in r.specs],
        out_shape=[jax.ShapeDtypeStruct((s, W_B), F32), jax.ShapeDtypeStruct((s, 2 * W_B), F32)]
                  + [sh for r in rides for sh in r.out_shapes],
        scratch_shapes=scratch + [sem for r in rides for sem in r.semaphores],
        compiler_params=_params(("arbitrary", "arbitrary")),
    )(qkv, qkv, qkv, *[a for r in rides for a in r.arrs])
    return res[0], res[1], list(res[2:])


def _attn_bwd(qkv, cm, do, name, ride=None):
    s = qkv.shape[0]
    tq = _tile(s, 256, KBLK)
    npairs = W_B // LANES
    rides = [] if ride is None else [ride]

    nq = s // tq
    per_tile = tq // KBLK
    assert nq % 2 == 0 and per_tile % 2 == 0
    ntot = (nq + 1) * per_tile
    assert (ntot + 6) % BWD_UNROLL == 0

    def body(qa_ref, k_ref, v_ref, cma_ref, doa_ref, qb_ref, cmb_ref, dob_ref, dqa_ref, dqb_ref, dk_ref, dv_ref,
             z_ref, zw_ref, da_ref, g_ref, sig_ref, cum_ref, gp_ref, gcarry_ref, hl_ref, ghl_ref, a_ref, dz_ref,
             dkt_ref, dvt_ref, q_st, do_st, qt_st, dot_st, cm_st):
        i = pl.program_id(1)

        @pl.when(i == 0)
        def _():
            dkt_ref[...] = jnp.zeros_like(dkt_ref)
            dvt_ref[...] = jnp.zeros_like(dvt_ref)

        tiles = (i, nq - 1 - i)
        nk_a = (i + 1) * per_tile
        for t, (q_in, do_in, cm_in) in enumerate(((qa_ref, doa_ref, cma_ref), (qb_ref, dob_ref, cmb_ref))):
            q_st[t] = q_in[...]
            do_st[t] = do_in[...].astype(_MXU)
            qt_st[t] = q_in[...].astype(F32).T.astype(_MXU)
            dot_st[t] = do_in[...].T.astype(_MXU)
            cm_st[t] = cm_in[...]

        lane = _iota((1, LANES), 1)
        lane_lo = lane < HEAD_DIM
        hmask = (lane_lo, jnp.logical_not(lane_lo))
        tri = (_iota((KBLK, KBLK), 0) >= _iota((KBLK, KBLK), 1)).astype(BF16)
        prefix2 = ((_iota((KBLK, 2 * KBLK), 0) <= _iota((KBLK, 2 * KBLK), 1))
                   | (_iota((KBLK, 2 * KBLK), 1) >= KBLK)).astype(BF16)
        dmat = _iota((tq, KBLK), 1) - _iota((tq, KBLK), 0)
        chunks = [slice(r, r + ROW_CHUNK) for r in range(0, tq, ROW_CHUNK)]
        heads = [slice(hh * LANES, (hh + 1) * LANES) for hh in range(2)]

        def locate(v):
            second = v >= nk_a
            return ((v >= 0) & (v < ntot), second.astype(jnp.int32), jnp.where(second, tiles[1], tiles[0]),
                    jnp.where(second, v - nk_a, v))

        def before(v):
            valid, _, tile, b = locate(v)
            return jnp.where(valid, tile * tq - b * KBLK, jnp.iinfo(jnp.int32).min)

        def key_block(v):
            return jnp.clip(locate(v)[3], 0, nblk - 1)

        def block_rows(v):
            return pl.ds(pl.multiple_of(key_block(v) * KBLK, KBLK), KBLK)

        def per_head(block):
            return jnp.concatenate([jnp.where(m, block, jnp.zeros_like(block)) for m in hmask], axis=0)

        feature_lo = _iota((LANES, KBLK), 0) < HEAD_DIM

        def own_features(side_by_side):
            return jnp.where(feature_lo, side_by_side[:, 0:KBLK], side_by_side[:, KBLK:2 * KBLK])

        def m1(b, p):
            z_ref[p] = _dot_nt(q_st[locate(b)[1]], per_head(k_ref[block_rows(b), :]))

        def v1(b, p):
            t = before(b)
            for hh in range(2):
                for rows in chunks:
                    z = jnp.where(dmat[rows] < t, z_ref[p, rows, heads[hh]], MASKED_SCORE)
                    zw_ref[p, rows, heads[hh]] = z
                    l = -(jnp.maximum(z, 0.0) + jnp.log(1.0 + jnp.exp(-jnp.abs(z))))
                    hi, lo = _split(l)
                    hl_ref[p, hh, rows, 0:KBLK] = hi
                    hl_ref[p, hh, rows, KBLK:2 * KBLK] = lo

        def m2(b, p):
            for hh in range(2):
                cum_ref[p, hh] = (jnp.dot(hl_ref[p, hh, :, 0:KBLK], tri, preferred_element_type=F32)
                                  + jnp.dot(hl_ref[p, hh, :, KBLK:2 * KBLK], tri, preferred_element_type=F32))
            da_ref[p] = _dot_nt(do_st[locate(b)[1]], per_head(v_ref[block_rows(b), :]))

        def v2(b, p):
            valid, which, _, blk = locate(b)
            pick = lane == jnp.where(valid, blk, -1)
            for hh in range(2):
                for rows in chunks:
                    c = jnp.sum(jnp.where(pick, cm_st[which, rows, heads[hh]], 0.0), axis=1, keepdims=True)
                    z = zw_ref[p, rows, heads[hh]]
                    a = jnp.exp(z + c + cum_ref[p, hh, rows, :])
                    g = a * da_ref[p, rows, heads[hh]]
                    a_ref[p, rows, heads[hh]] = a.astype(_MXU)
                    g_ref[p, rows, heads[hh]] = g
                    hi, lo = _split(g)
                    ghl_ref[p, hh, rows, 0:KBLK] = hi
                    ghl_ref[p, hh, rows, KBLK:2 * KBLK] = lo
                    sig_ref[p, rows, heads[hh]] = jax.nn.sigmoid(z)

        def m3(b, p):
            for hh in range(2):
                gp_ref[p, hh] = (jnp.dot(ghl_ref[p, hh, :, 0:KBLK], prefix2, preferred_element_type=F32)
                                 + jnp.dot(ghl_ref[p, hh, :, KBLK:2 * KBLK], prefix2, preferred_element_type=F32))
            dvt_ref[key_block(b)] += own_features(jnp.dot(dot_st[locate(b)[1]], a_ref[p], preferred_element_type=F32))

        def v3(b, p):
            restart = b == nk_a
            for hh in range(2):
                for rows in chunks:
                    gc = jnp.where(restart, 0.0, gcarry_ref[hh, rows, :])
                    upto = gc + gp_ref[p, hh, rows, 0:KBLK]
                    dz = g_ref[p, rows, heads[hh]] - sig_ref[p, rows, heads[hh]] * upto
                    dz_ref[p, rows, heads[hh]] = dz.astype(_MXU)
                    gcarry_ref[hh, rows, :] = gc + gp_ref[p, hh, rows, KBLK:2 * KBLK]

        def m4(b, p, dqs):
            which = locate(b)[1]
            kb = k_ref[block_rows(b), :]
            part = None
            for hh in range(2):
                d = jnp.dot(dz_ref[p, :, heads[hh]], jnp.where(hmask[hh], kb, jnp.zeros_like(kb)),
                            preferred_element_type=F32)
                part = d if part is None else part + d
            dkt_ref[key_block(b)] += own_features(jnp.dot(qt_st[which], dz_ref[p], preferred_element_type=F32))
            return dqs[0] + jnp.where(which == 0, part, 0.0), dqs[1] + jnp.where(which == 1, part, 0.0)

        def step(it, dqs):
            for u in range(BWD_UNROLL):
                j = BWD_UNROLL * it + u
                p = u % 2
                dqs = m4(j - 6, p, dqs)
                m3(j - 4, p)
                m2(j - 2, p)
                m1(j, p)
                v3(j - 5, 1 - p)
                v2(j - 3, 1 - p)
                v1(j - 1, 1 - p)
            return dqs

        zw_ref[...] = jnp.full_like(zw_ref, MASKED_SCORE)
        for ref in (z_ref, da_ref, g_ref, sig_ref, cum_ref, gp_ref, gcarry_ref, hl_ref, ghl_ref, a_ref, dz_ref):
            ref[...] = jnp.zeros_like(ref)
        zero = doa_ref[...] * 0.0
        dq_a, dq_b = lax.fori_loop(0, (ntot + 6) // BWD_UNROLL, step, (zero, zero))
        dqa_ref[...] = dq_a * 0.125
        dqb_ref[...] = dq_b * 0.125

        @pl.when(i == nq // 2 - 1)
        def _():
            def untranspose(blk, carry):
                rows = pl.ds(pl.multiple_of(blk * KBLK, KBLK), KBLK)
                dk_ref[rows, :] = dkt_ref[blk].T
                dv_ref[rows, :] = dvt_ref[blk].T
                return carry
            lax.fori_loop(0, nblk, untranspose, 0)

    first = lambda width: pl.BlockSpec((tq, width), lambda p, i: (i, p))
    second = lambda width: pl.BlockSpec((tq, width), lambda p, i: (nq - 1 - i, p))
    second_out = pl.BlockSpec((tq, LANES), lambda p, i: (nq // 2 - 1 - i, p))
    nblk = s // KBLK
    full = pl.BlockSpec((s, LANES), lambda p, i: (0, p))
    scratch = ([pltpu.VMEM((2, tq, 2 * KBLK), F32)] * 5 + [pltpu.VMEM((2, 2, tq, KBLK), F32),
               pltpu.VMEM((2, 2, tq, 2 * KBLK), F32), pltpu.VMEM((2, tq, KBLK), F32)]
               + [pltpu.VMEM((2, 2, tq, 2 * KBLK), BF16)] * 2 + [pltpu.VMEM((2, tq, 2 * KBLK), _MXU)] * 2
               + [pltpu.VMEM((nblk, LANES, KBLK), F32)] * 2
               + [pltpu.VMEM((2, tq, LANES), _MXU)] * 2 + [pltpu.VMEM((2, LANES, tq), _MXU)] * 2
               + [pltpu.VMEM((2, tq, 2 * LANES), F32)])
    res = _pcall(
        _with_exchange(body, ride, 8, 4, len(scratch), (npairs - 1, nq // 2 - 1)), name=name, grid=(npairs, nq // 2),
        in_specs=[first(LANES), pl.BlockSpec((s, LANES), lambda p, i: (0, npairs + p)),
                  pl.BlockSpec((s, LANES), lambda p, i: (0, 2 * npairs + p)), first(2 * LANES), first(LANES),
                  second(LANES), second(2 * LANES), second(LANES)] + [sp for r in rides for sp in r.specs],
        out_specs=[first(LANES), second_out, full, full] + [sp for r in rides for sp in r.specs],
        out_shape=[jax.ShapeDtypeStruct((s // 2, W_B), F32)] * 2 + [jax.ShapeDtypeStruct((s, W_B), F32)] * 2
                  + [sh for r in rides for sh in r.out_shapes],
        scratch_shapes=scratch + [sem for r in rides for sem in r.semaphores],
        compiler_params=_params(("arbitrary", "arbitrary")),
    )(qkv, qkv, qkv, cm, do, qkv, cm, do, *[a for r in rides for a in r.arrs])
    return jnp.concatenate([res[0], res[1]], axis=0), res[2], res[3], list(res[4:])


POOL_HALO = 128


def _pool_window_lane():
    lane = _iota((1, W_C), 1)
    w = jnp.where(lane < 64, POOL_WINDOWS[0], jnp.where(lane < 128, POOL_WINDOWS[1],
                  jnp.where(lane < 192, POOL_WINDOWS[2], POOL_WINDOWS[3])))
    return w.astype(F32)


def _pool_count(tm, i):
    pos = (_iota((tm, W_C), 0) + (i * tm + 1)).astype(F32)
    return jnp.minimum(pos, _pool_window_lane())


def _pool_centered(prev, cur, cnt):
    tm = cur.shape[0]
    xx = jnp.concatenate([prev, cur], axis=0)
    hi, lo = _split(xx)
    t = _iota((tm, tm + POOL_HALO), 0)
    cc = _iota((tm, tm + POOL_HALO), 1) - POOL_HALO
    wsum = jnp.zeros_like(cur)
    for g, w in enumerate(POOL_WINDOWS):
        band = ((cc <= t) & (cc > t - w)).astype(BF16)
        mg = _lane_group_mask(W_C, g)
        wsum += jnp.dot(band, jnp.where(mg, hi, jnp.zeros_like(hi)), preferred_element_type=F32)
        wsum += jnp.dot(band, jnp.where(mg, lo, jnp.zeros_like(lo)), preferred_element_type=F32)
    return wsum / cnt - cur


def _pool_fwd(p, wbd, sc, name):
    s = p.shape[0]
    tm = _tile(s, 256, POOL_HALO)
    r = tm // POOL_HALO

    def body(pp_ref, p_ref, w_ref, sc_ref, y_ref):
        i = pl.program_id(0)
        prev = jnp.where(i > 0, pp_ref[...], 0.0)
        d = _pool_centered(prev, p_ref[...], _pool_count(tm, i))
        y_ref[...] = _dot(d, w_ref[...]) * sc_ref[...]

    return _pcall(
        body, name=name, grid=(s // tm,),
        in_specs=[pl.BlockSpec((POOL_HALO, W_C), lambda i: (jnp.maximum(i * r - 1, 0), 0)),
                  pl.BlockSpec((tm, W_C), lambda i: (i, 0)), pl.BlockSpec((W_C, W_C), lambda i: (0, 0)),
                  pl.BlockSpec((1, W_C), lambda i: (0, 0))],
        out_specs=pl.BlockSpec((tm, W_C), lambda i: (i, 0)),
        out_shape=jax.ShapeDtypeStruct((s, W_C), F32),
        compiler_params=_params(("parallel",)),
    )(p, p, wbd, sc)


def _pool_bwd(p, dy, wbd, wbdt, sc, name):
    s = p.shape[0]
    tm = _tile(s, 256, POOL_HALO)
    r = tm // POOL_HALO
    nt = s // tm

    def body(pp_ref, p_ref, dy_ref, dyn_ref, w_ref, wt_ref, sc_ref, dp_ref, dw_ref, dsc_ref):
        i = pl.program_id(0)

        @pl.when(i == 0)
        def _():
            dw_ref[...] = jnp.zeros_like(dw_ref)
            dsc_ref[...] = jnp.zeros_like(dsc_ref)

        scv = sc_ref[...]
        cnt = _pool_count(tm, i)
        prev = jnp.where(i > 0, pp_ref[...], 0.0)
        d = _pool_centered(prev, p_ref[...], cnt)
        e = _dot(d, w_ref[...])
        dyv = dy_ref[...]
        de = dyv * scv
        dsc_ref[...] += jnp.sum(dyv * e, axis=0, keepdims=True)
        dw_ref[...] += _dot_tn(d, de)
        dd = _dot(de, wt_ref[...])
        ddn = jnp.where(i < nt - 1, _dot(dyn_ref[...] * scv, wt_ref[...]), 0.0)
        yy = jnp.concatenate([dd / cnt, ddn / _pool_window_lane()], axis=0)
        hi, lo = _split(yy)
        t = _iota((tm, tm + POOL_HALO), 0)
        cc = _iota((tm, tm + POOL_HALO), 1)
        acc = jnp.zeros_like(dd)
        for g, w in enumerate(POOL_WINDOWS):
            band = ((cc >= t) & (cc < t + w)).astype(BF16)
            mg = _lane_group_mask(W_C, g)
            acc += jnp.dot(band, jnp.where(mg, hi, jnp.zeros_like(hi)), preferred_element_type=F32)
            acc += jnp.dot(band, jnp.where(mg, lo, jnp.zeros_like(lo)), preferred_element_type=F32)
        dp_ref[...] = acc - dd

    tile = pl.BlockSpec((tm, W_C), lambda i: (i, 0))
    mat = pl.BlockSpec((W_C, W_C), lambda i: (0, 0))
    vec = pl.BlockSpec((1, W_C), lambda i: (0, 0))
    return _pcall(
        body, name=name, grid=(nt,),
        in_specs=[pl.BlockSpec((POOL_HALO, W_C), lambda i: (jnp.maximum(i * r - 1, 0), 0)), tile, tile,
                  pl.BlockSpec((POOL_HALO, W_C), lambda i: (jnp.minimum((i + 1) * r, s // POOL_HALO - 1), 0)),
                  mat, mat, vec],
        out_specs=[tile, mat, vec],
        out_shape=[jax.ShapeDtypeStruct((s, W_C), F32), jax.ShapeDtypeStruct((W_C, W_C), F32),
                   jax.ShapeDtypeStruct((1, W_C), F32)],
        compiler_params=_params(("arbitrary",)),
    )(p, p, dy, dy, wbd, wbdt, sc)


def _mix_cols(ya_ref, yb_ref, yc_ref, cb):
    if cb < 2:
        return ya_ref[:, cb * LANES:(cb + 1) * LANES]
    if cb < 6:
        return yb_ref[:, (cb - 2) * LANES:(cb - 1) * LANES]
    return yc_ref[:, (cb - 6) * LANES:(cb - 5) * LANES]


def _mix_fwd(ya, yb, yc, g, wo, x, name):
    s = x.shape[0]
    tm = _tile(s, 256)

    def body(ya_ref, yb_ref, yc_ref, g_ref, w_ref, x_ref, o_ref, yn_ref):
        sel = _group_mat(LANES)
        for cb in range(D_MODEL // LANES):
            y = _mix_cols(ya_ref, yb_ref, yc_ref, cb)
            r = lax.rsqrt(_dot_sel(y * y, sel) * (1.0 / HEAD_DIM) + EPS)
            yn_ref[:, cb * LANES:(cb + 1) * LANES] = (y * r * g_ref[:, cb * LANES:(cb + 1) * LANES]).astype(_MXU)
        o_ref[...] = x_ref[...] + jnp.dot(yn_ref[...], w_ref[...], preferred_element_type=F32)

    row = lambda n: pl.BlockSpec((tm, n), lambda i: (i, 0))
    return _pcall(
        body, name=name, grid=(s // tm,),
        in_specs=[row(W_A), row(W_B), row(W_C), pl.BlockSpec((1, D_MODEL), lambda i: (0, 0)),
                  pl.BlockSpec((D_MODEL, D_MODEL), lambda i: (0, 0)), row(D_MODEL)],
        out_specs=[row(D_MODEL), row(D_MODEL)],
        out_shape=[jax.ShapeDtypeStruct((s, D_MODEL), F32), jax.ShapeDtypeStruct((s, D_MODEL), _MXU)],
        compiler_params=_params(("parallel",)),
    )(ya, yb, yc, g, wo, x)


def _mix_bwd(dx, ya, yb, yc, g, wot, name):
    s = dx.shape[0]
    tm = _tile(s, 256)

    def body(dx_ref, ya_ref, yb_ref, yc_ref, g_ref, wt_ref, dya_ref, dyb_ref, dyc_ref, dg_ref):
        @pl.when(pl.program_id(0) == 0)
        def _():
            dg_ref[...] = jnp.zeros_like(dg_ref)

        dyn = _dot(dx_ref[...], wt_ref[...])
        sel = _group_mat(LANES)
        for cb in range(D_MODEL // LANES):
            cols = slice(cb * LANES, (cb + 1) * LANES)
            y = _mix_cols(ya_ref, yb_ref, yc_ref, cb)
            r = lax.rsqrt(_dot_sel(y * y, sel) * (1.0 / HEAD_DIM) + EPS)
            dyc_ = dyn[:, cols]
            dyg = dyc_ * g_ref[:, cols]
            m2 = _dot_sel(dyg * y, sel) * (1.0 / HEAD_DIM)
            dy = r * dyg - y * (r * r * r) * m2
            dg_ref[:, cols] += jnp.sum(dyc_ * y * r, axis=0, keepdims=True)
            if cb < 2:
                dya_ref[:, cb * LANES:(cb + 1) * LANES] = dy
            elif cb < 6:
                dyb_ref[:, (cb - 2) * LANES:(cb - 1) * LANES] = dy
            else:
                dyc_ref[:, (cb - 6) * LANES:(cb - 5) * LANES] = dy

    row = lambda n: pl.BlockSpec((tm, n), lambda i: (i, 0))
    vec = pl.BlockSpec((1, D_MODEL), lambda i: (0, 0))
    return _pcall(
        body, name=name, grid=(s // tm,),
        in_specs=[row(D_MODEL), row(W_A), row(W_B), row(W_C), vec, pl.BlockSpec((D_MODEL, D_MODEL), lambda i: (0, 0))],
        out_specs=[row(W_A), row(W_B), row(W_C), vec],
        out_shape=[jax.ShapeDtypeStruct((s, W_A), F32), jax.ShapeDtypeStruct((s, W_B), F32),
                   jax.ShapeDtypeStruct((s, W_C), F32), jax.ShapeDtypeStruct((1, D_MODEL), F32)],
        compiler_params=_params(("arbitrary",)),
    )(dx, ya, yb, yc, g, wot)


FFN_CHUNK = 1408
FFN_ROWS = 256
N_CHUNKS = D_FF // FFN_CHUNK
CW_ROWS = 8


def _ffn_up_fwd(x, g, w, name):
    s = x.shape[0]
    n = w.shape[1]
    tm, tn = _tile(s, 512), _tile(n, 2816, LANES)

    def body(x_ref, g_ref, w_ref, z_ref, h_ref):
        @pl.when(pl.program_id(1) == 0)
        def _():
            xv = x_ref[...]
            h_ref[...] = (xv * _rms(xv) * g_ref[...]).astype(_MXU)
        z_ref[...] = jnp.dot(h_ref[...], w_ref[...], preferred_element_type=F32)

    return _pcall(
        body, name=name, grid=(s // tm, n // tn),
        in_specs=[pl.BlockSpec((tm, D_MODEL), lambda i, j: (i, 0)), pl.BlockSpec((1, D_MODEL), lambda i, j: (0, 0)),
                  pl.BlockSpec((D_MODEL, tn), lambda i, j: (0, j))],
        out_specs=[pl.BlockSpec((tm, tn), lambda i, j: (i, j)), pl.BlockSpec((tm, D_MODEL), lambda i, j: (i, 0))],
        out_shape=[jax.ShapeDtypeStruct((s, n), F32), jax.ShapeDtypeStruct((s, D_MODEL), _MXU)],
        compiler_params=_params(("parallel", "arbitrary")),
    )(x, g, w)


def _conv(cur, prev8, cw_ref):
    s1 = _shift_down(cur, prev8, 1)
    s2 = _shift_down(cur, prev8, 2)
    zc = cw_ref[3:4, :] + s2 * cw_ref[0:1, :]
    zc = zc + s1 * cw_ref[1:2, :]
    zc = zc + cur * cw_ref[2:3, :]
    return zc, s1, s2


def _halo_specs(tm, s):
    r = tm // SUBLANES
    prev = lambda off: pl.BlockSpec((SUBLANES, FFN_CHUNK), lambda i, j: (jnp.maximum(i * r - 1, 0), j + off))
    nxt = lambda off: pl.BlockSpec((SUBLANES, FFN_CHUNK), lambda i, j: (jnp.minimum((i + 1) * r, s // SUBLANES - 1), j + off))
    return prev, nxt


def _ffn_down_fwd(z, cw, wd, x, name):
    s = x.shape[0]
    tm = _tile(s, FFN_ROWS)
    prev, _ = _halo_specs(tm, s)

    def body(zg_ref, zu_ref, pg_ref, pu_ref, cg_ref, cu_ref, w_ref, x_ref, o_ref, act_ref, acc_ref):
        i, j = pl.program_id(0), pl.program_id(1)
        first = i > 0
        zg, _, _ = _conv(zg_ref[...], jnp.where(first, pg_ref[...], 0.0), cg_ref)
        zu, _, _ = _conv(zu_ref[...], jnp.where(first, pu_ref[...], 0.0), cu_ref)
        act = (zg * jax.nn.sigmoid(zg) * zu).astype(_MXU)
        act_ref[...] = act

        @pl.when(j == 0)
        def _():
            acc_ref[...] = x_ref[...]
        acc_ref[...] += jnp.dot(act, w_ref[...], preferred_element_type=F32)

        @pl.when(j == N_CHUNKS - 1)
        def _():
            o_ref[...] = acc_ref[...]

    zt = lambda off: pl.BlockSpec((tm, FFN_CHUNK), lambda i, j: (i, j + off))
    cwt = lambda off: pl.BlockSpec((CW_ROWS, FFN_CHUNK), lambda i, j: (0, j + off))
    return _pcall(
        body, name=name, grid=(s // tm, N_CHUNKS),
        in_specs=[zt(0), zt(N_CHUNKS), prev(0), prev(N_CHUNKS), cwt(0), cwt(N_CHUNKS),
                  pl.BlockSpec((FFN_CHUNK, D_MODEL), lambda i, j: (j, 0)), pl.BlockSpec((tm, D_MODEL), lambda i, j: (i, 0))],
        out_specs=[pl.BlockSpec((tm, D_MODEL), lambda i, j: (i, 0)), pl.BlockSpec((tm, FFN_CHUNK), lambda i, j: (i, j))],
        out_shape=[jax.ShapeDtypeStruct((s, D_MODEL), F32), jax.ShapeDtypeStruct((s, D_FF), _MXU)],
        scratch_shapes=[pltpu.VMEM((tm, D_MODEL), F32)],
        compiler_params=_params(("parallel", "arbitrary")),
    )(z, z, z, z, cw, cw, wd, x)


def _ffn_down_bwd(dx, z, cw, wdt, name):
    s = dx.shape[0]
    tm = _tile(s, FFN_ROWS)

    def body(dx_ref, zg_ref, zu_ref, pg_ref, pu_ref, cg_ref, cu_ref, wt_ref, dg_ref, du_ref, dcg_ref, dcu_ref):
        i = pl.program_id(1)
        first = i > 0

        @pl.when(i == 0)
        def _():
            dcg_ref[...] = jnp.zeros_like(dcg_ref)
            dcu_ref[...] = jnp.zeros_like(dcu_ref)

        dact = _dot(dx_ref[...], wt_ref[...])
        zg, g1, g2 = _conv(zg_ref[...], jnp.where(first, pg_ref[...], 0.0), cg_ref)
        zu, u1, u2 = _conv(zu_ref[...], jnp.where(first, pu_ref[...], 0.0), cu_ref)
        sg = jax.nn.sigmoid(zg)
        silu = zg * sg
        dzu = dact * silu
        dzg = dact * zu * (sg * (1.0 + zg * (1.0 - sg)))
        dg_ref[...] = dzg
        du_ref[...] = dzu
        for ref, dzc, cur, s1, s2 in ((dcg_ref, dzg, zg_ref[...], g1, g2), (dcu_ref, dzu, zu_ref[...], u1, u2)):
            ref[0:1, :] += jnp.sum(dzc * s2, axis=0, keepdims=True)
            ref[1:2, :] += jnp.sum(dzc * s1, axis=0, keepdims=True)
            ref[2:3, :] += jnp.sum(dzc * cur, axis=0, keepdims=True)
            ref[3:4, :] += jnp.sum(dzc, axis=0, keepdims=True)

    zt = lambda off: pl.BlockSpec((tm, FFN_CHUNK), lambda j, i: (i, j + off))
    r = tm // SUBLANES
    pv = lambda off: pl.BlockSpec((SUBLANES, FFN_CHUNK), lambda j, i: (jnp.maximum(i * r - 1, 0), j + off))
    cwt = lambda off: pl.BlockSpec((CW_ROWS, FFN_CHUNK), lambda j, i: (0, j + off))
    out_t = pl.BlockSpec((tm, FFN_CHUNK), lambda j, i: (i, j))
    dc_t = pl.BlockSpec((CW_ROWS, FFN_CHUNK), lambda j, i: (0, j))
    dzg, dzu, dcg, dcu = _pcall(
        body, name=name, grid=(N_CHUNKS, s // tm),
        in_specs=[pl.BlockSpec((tm, D_MODEL), lambda j, i: (i, 0)), zt(0), zt(N_CHUNKS), pv(0), pv(N_CHUNKS),
                  cwt(0), cwt(N_CHUNKS), pl.BlockSpec((D_MODEL, FFN_CHUNK), lambda j, i: (0, j))],
        out_specs=[out_t, out_t, dc_t, dc_t],
        out_shape=[jax.ShapeDtypeStruct((s, D_FF), F32), jax.ShapeDtypeStruct((s, D_FF), F32),
                   jax.ShapeDtypeStruct((CW_ROWS, D_FF), F32), jax.ShapeDtypeStruct((CW_ROWS, D_FF), F32)],
        compiler_params=_params(("parallel", "arbitrary")),
    )(dx, z, z, z, z, cw, cw, wdt)
    return dzg, dzu, jnp.concatenate([dcg, dcu], axis=1)


def _ffn_up_bwd(dzg, dzu, cw, wut, x, g, dres, name):
    s = x.shape[0]
    tm = _tile(s, FFN_ROWS)
    _, nxt = _halo_specs(tm, s)
    nt = s // tm

    def body(dg_ref, du_ref, ng_ref, nu_ref, cg_ref, cu_ref, wg_ref, wu_ref, x_ref, g_ref, dres_ref,
             dzg_ref, dzu_ref, dx_ref, dgn_ref, acc_ref):
        i, j = pl.program_id(0), pl.program_id(1)
        last = i < nt - 1

        def conv_bwd(cur, nxt8, cw_ref):
            up1 = _shift_up(cur, nxt8, 1)
            up2 = _shift_up(cur, nxt8, 2)
            return cur * cw_ref[2:3, :] + up1 * cw_ref[1:2, :] + up2 * cw_ref[0:1, :]

        dzg_ = conv_bwd(dg_ref[...], jnp.where(last, ng_ref[...], 0.0), cg_ref).astype(_MXU)
        dzu_ = conv_bwd(du_ref[...], jnp.where(last, nu_ref[...], 0.0), cu_ref).astype(_MXU)
        dzg_ref[...] = dzg_
        dzu_ref[...] = dzu_

        @pl.when(j == 0)
        def _():
            acc_ref[...] = jnp.zeros_like(acc_ref)
        acc_ref[...] += (jnp.dot(dzg_, wg_ref[...], preferred_element_type=F32)
                         + jnp.dot(dzu_, wu_ref[...], preferred_element_type=F32))

        @pl.when((i == 0) & (j == 0))
        def _():
            dgn_ref[...] = jnp.zeros_like(dgn_ref)

        @pl.when(j == N_CHUNKS - 1)
        def _():
            dx, dgn = _rms_bwd(x_ref[...], g_ref[...], acc_ref[...])
            dx_ref[...] = dres_ref[...] + dx
            dgn_ref[...] += dgn

    zt = pl.BlockSpec((tm, FFN_CHUNK), lambda i, j: (i, j))
    cwt = lambda off: pl.BlockSpec((CW_ROWS, FFN_CHUNK), lambda i, j: (0, j + off))
    wt = lambda off: pl.BlockSpec((FFN_CHUNK, D_MODEL), lambda i, j: (j + off, 0))
    row = pl.BlockSpec((tm, D_MODEL), lambda i, j: (i, 0))
    vec = pl.BlockSpec((1, D_MODEL), lambda i, j: (0, 0))
    return _pcall(
        body, name=name, grid=(nt, N_CHUNKS),
        in_specs=[zt, zt, nxt(0), nxt(0), cwt(0), cwt(N_CHUNKS), wt(0), wt(N_CHUNKS), row, vec, row],
        out_specs=[zt, zt, row, vec],
        out_shape=[jax.ShapeDtypeStruct((s, D_FF), _MXU), jax.ShapeDtypeStruct((s, D_FF), _MXU),
                   jax.ShapeDtypeStruct((s, D_MODEL), F32), jax.ShapeDtypeStruct((1, D_MODEL), F32)],
        scratch_shapes=[pltpu.VMEM((tm, D_MODEL), F32)],
        compiler_params=_params(("arbitrary", "arbitrary")),
    )(dzg, dzu, dzg, dzu, cw, cw, wut, wut, x, g, dres)


def _final_loss(x, g, tgt, name):
    s = x.shape[0]
    tm = _tile(s, 256)

    def body(x_ref, g_ref, t_ref, loss_ref, dx_ref, dg_ref):
        @pl.when(pl.program_id(0) == 0)
        def _():
            loss_ref[...] = jnp.zeros_like(loss_ref)
            dg_ref[...] = jnp.zeros_like(dg_ref)

        xv, gv = x_ref[...], g_ref[...]
        err = xv * _rms(xv) * gv - t_ref[...]
        per_tok = jnp.mean(err * err, axis=-1, keepdims=True)
        loss_ref[...] += 0.5 * jnp.sum(per_tok, axis=0, keepdims=True)
        dx, dg = _rms_bwd(xv, gv, err * (1.0 / D_MODEL))
        dx_ref[...] = dx
        dg_ref[...] += dg

    row = pl.BlockSpec((tm, D_MODEL), lambda i: (i, 0))
    vec = pl.BlockSpec((1, D_MODEL), lambda i: (0, 0))
    return _pcall(
        body, name=name, grid=(s // tm,),
        in_specs=[row, vec, row], out_specs=[pl.BlockSpec((1, 1), lambda i: (0, 0)), row, vec],
        out_shape=[jax.ShapeDtypeStruct((1, 1), F32), jax.ShapeDtypeStruct((s, D_MODEL), F32),
                   jax.ShapeDtypeStruct((1, D_MODEL), F32)],
        compiler_params=_params(("arbitrary",)),
    )(x, g, tgt)


def _adamw(parts, w, m, v, name):
    r, c = w.shape
    tr = _tile(r, 256)
    c1 = 1.0 - ADAM_B1 ** ADAM_STEP
    c2 = 1.0 - ADAM_B2 ** ADAM_STEP

    def body(p_ref, w_ref, m_ref, v_ref, g_ref, d_ref, mo_ref, vo_ref):
        g = p_ref[0].astype(F32)
        for i in range(1, N_DEV):
            g = g + p_ref[i].astype(F32)
        mn = ADAM_B1 * m_ref[...] + (1.0 - ADAM_B1) * g
        vn = ADAM_B2 * v_ref[...] + (1.0 - ADAM_B2) * (g * g)
        g_ref[...] = g
        mo_ref[...] = mn
        vo_ref[...] = vn
        d_ref[...] = -ADAM_LR * ((mn / c1) / (jnp.sqrt(vn / c2) + ADAM_EPS) + ADAM_WD * w_ref[...])

    t2 = pl.BlockSpec((tr, c), lambda i: (i, 0))
    return _pcall(
        body, name=name, grid=(r // tr,),
        in_specs=[pl.BlockSpec((N_DEV, tr, c), lambda i: (0, i, 0)), t2, t2, t2],
        out_specs=[t2] * 4, out_shape=[jax.ShapeDtypeStruct((r, c), F32)] * 4,
        compiler_params=_params(("parallel",)),
    )(parts, w, m, v)


SMALL = ("norm1_g", "sgu_norm_g", "sgu_w", "sgu_b", "pool_w", "pool_scale", "mix_norm_g", "norm2_g", "conv_b", "final_g")
SHARDED = ("w_in", "w_o", "w_up", "conv_w", "w_down")
ORDER = ("norm1_g", "w_in", "sgu_norm_g", "sgu_w", "sgu_b", "pool_w", "pool_scale", "mix_norm_g", "w_o", "norm2_g",
         "w_up", "conv_w", "conv_b", "w_down", "final_g")


def _pack(tree):
    return jnp.concatenate([tree[n].reshape(-1) for n in SMALL]).reshape(-1, LANES)


def _unpack(flat, like):
    out, off = {}, 0
    flat = flat.reshape(-1)
    for n in SMALL:
        size = math.prod(like[n].shape)
        out[n] = flat[off:off + size].reshape(like[n].shape)
        off += size
    return out


def _block_diag(pw):
    z = jnp.zeros((W_C, W_C), pw.dtype)
    for g in range(4):
        z = z.at[g * 64:(g + 1) * 64, g * 64:(g + 1) * 64].set(pw[g])
    return z


def kernel(x, norm1_g, w_in, sgu_norm_g, sgu_w, sgu_b, pool_w, pool_scale, mix_norm_g, w_o, norm2_g, w_up, conv_w, conv_b, w_down, final_g, loss_target, m_norm1_g, m_w_in, m_sgu_norm_g, m_sgu_w, m_sgu_b, m_pool_w, m_pool_scale, m_mix_norm_g, m_w_o, m_norm2_g, m_w_up, m_conv_w, m_conv_b, m_w_down, m_final_g, v_norm1_g, v_w_in, v_sgu_norm_g, v_sgu_w, v_sgu_b, v_pool_w, v_pool_scale, v_mix_norm_g, v_w_o, v_norm2_g, v_w_up, v_conv_w, v_conv_b, v_w_down, v_final_g):
    weights = dict(norm1_g=norm1_g, w_in=w_in, sgu_norm_g=sgu_norm_g, sgu_w=sgu_w, sgu_b=sgu_b, pool_w=pool_w,
                   pool_scale=pool_scale, mix_norm_g=mix_norm_g, w_o=w_o, norm2_g=norm2_g, w_up=w_up, conv_w=conv_w,
                   conv_b=conv_b, w_down=w_down, final_g=final_g)
    mom = dict(norm1_g=m_norm1_g, w_in=m_w_in, sgu_norm_g=m_sgu_norm_g, sgu_w=m_sgu_w, sgu_b=m_sgu_b, pool_w=m_pool_w,
               pool_scale=m_pool_scale, mix_norm_g=m_mix_norm_g, w_o=m_w_o, norm2_g=m_norm2_g, w_up=m_w_up,
               conv_w=m_conv_w, conv_b=m_conv_b, w_down=m_w_down, final_g=m_final_g)
    var = dict(norm1_g=v_norm1_g, w_in=v_w_in, sgu_norm_g=v_sgu_norm_g, sgu_w=v_sgu_w, sgu_b=v_sgu_b, pool_w=v_pool_w,
               pool_scale=v_pool_scale, mix_norm_g=v_mix_norm_g, w_o=v_w_o, norm2_g=v_norm2_g, w_up=v_w_up,
               conv_w=v_conv_w, conv_b=v_conv_b, w_down=v_w_down, final_g=v_final_g)
    depth = w_in.shape[0]
    s = x.shape[1]
    xs = x.reshape(s, D_MODEL)
    tgt = loss_target.reshape(s, D_MODEL)

    assert depth >= 2
    (g_in0,) = _exchange([w_in[0].astype(_MXU)], "gather_w_in0", False)
    w_in0 = jnp.transpose(g_in0, (1, 0, 2)).reshape(D_MODEL, IN_COLS)
    gather_rest = _Exchange([w_in[1:].astype(_MXU), w_o.astype(_MXU), w_up.astype(_MXU), conv_w, w_down.astype(_MXU)], False)

    tril = jnp.tril(jnp.ones((CHUNK, CHUNK), bool))
    layers = []
    for l in range(depth):
        wbd = _block_diag(pool_w[l])
        layers.append(dict(
            g1=norm1_g[l][None], gn=sgu_norm_g[l][None], wm=jnp.where(tril[None], sgu_w[l], 0.0).astype(_MXU),
            bias=jnp.repeat(sgu_b[l].T, HEAD_DIM, axis=1),
            wbd=wbd.astype(_MXU), wbd_t=wbd.T.astype(_MXU), sc=pool_scale[l][None],
            gmix=mix_norm_g[l][None], g2=norm2_g[l][None]))
    layers[0].update(w_in=w_in0, w_in_t=w_in0.T)

    def place_gathered(g_in, g_o, g_up, g_cw, g_dn):
        full_in = jnp.transpose(g_in, (1, 2, 0, 3)).reshape(depth - 1, D_MODEL, IN_COLS)
        full_o = jnp.transpose(g_o, (1, 0, 2, 3)).reshape(depth, D_MODEL, D_MODEL)
        full_up = jnp.transpose(g_up, (1, 2, 0, 3)).reshape(depth, D_MODEL, 2 * D_FF)
        full_cw = jnp.transpose(g_cw, (1, 2, 0, 3)).reshape(depth, 3, 2 * D_FF)
        full_dn = jnp.transpose(g_dn, (1, 0, 2, 3)).reshape(depth, D_FF, D_MODEL)
        for l in range(depth):
            if l > 0:
                layers[l].update(w_in=full_in[l - 1], w_in_t=full_in[l - 1].T)
            layers[l].update(
                w_o=full_o[l], w_o_t=full_o[l].T, w_up=full_up[l], w_up_t=full_up[l].T,
                cw=jnp.concatenate([full_cw[l], conv_b[l][None], jnp.zeros((CW_ROWS - 4, 2 * D_FF), F32)], axis=0),
                w_dn=full_dn[l], w_dn_t=full_dn[l].T)

    saved = []
    cur = xs
    for l, p in enumerate(layers):
        a_in, qkv, p_in, h1 = _inproj_fwd(cur, p["g1"], p["w_in"], f"inproj_fwd{l}")
        y_a = _sgu_fwd(a_in, p["gn"], p["wm"], p["bias"], f"sgu_fwd{l}")
        y_b, cm, gathered_w = _attn_fwd(qkv, f"attn_fwd{l}", gather_rest if l == 0 else None)
        if l == 0:
            place_gathered(*gathered_w)
        y_c = _pool_fwd(p_in, p["wbd"], p["sc"], f"pool_fwd{l}")
        x_mid, yn = _mix_fwd(y_a, y_b, y_c, p["gmix"], p["w_o"], cur, f"mix_fwd{l}")
        z, h2 = _ffn_up_fwd(x_mid, p["g2"], p["w_up"], f"ffn_up_fwd{l}")
        x_out, act = _ffn_down_fwd(z, p["cw"], p["w_dn"], x_mid, f"ffn_down_fwd{l}")
        saved.append(dict(x_in=cur, a_in=a_in, qkv=qkv, p_in=p_in, h1=h1, y_a=y_a, y_b=y_b, cm=cm, y_c=y_c, x_mid=x_mid,
                          yn=yn, z=z, h2=h2, act=act))
        cur = x_out
    loss_part, dx, dg_final = _final_loss(cur, final_g[None], tgt, "final_loss")

    small = {n: [None] * depth for n in SMALL if n != "final_g"}
    big = {n: [None] * depth for n in SHARDED}
    early = [(n, l) for n in SHARDED for l in range(depth) if (n, l) != ("w_in", 0)]
    wire = lambda n, t: t if n == "conv_w" else t.astype(GRAD_WIRE)
    for l in reversed(range(depth)):
        p, sv = layers[l], saved[l]
        dzg, dzu, dcw = _ffn_down_bwd(dx, sv["z"], p["cw"], p["w_dn_t"], f"ffn_down_bwd{l}")
        big["w_down"][l] = _mm_tn(sv["act"], dx, f"dw_down{l}").reshape(N_DEV, D_FF // N_DEV, D_MODEL)
        dzg_b, dzu_b, dx_mid, dg2 = _ffn_up_bwd(dzg, dzu, p["cw"], p["w_up_t"], sv["x_mid"], p["g2"], dx, f"ffn_up_bwd{l}")
        dw_up = jnp.stack([_mm_tn(sv["h2"], dzg_b, f"dw_up_g{l}"), _mm_tn(sv["h2"], dzu_b, f"dw_up_u{l}")])
        big["w_up"][l] = jnp.transpose(dw_up.reshape(2, D_MODEL, N_DEV // 2, 2 * D_FF // N_DEV), (0, 2, 1, 3)).reshape(
            N_DEV, D_MODEL, 2 * D_FF // N_DEV)
        big["conv_w"][l] = jnp.transpose(dcw[0:3].reshape(3, N_DEV, 2 * D_FF // N_DEV), (1, 0, 2))
        small["conv_b"][l] = dcw[3]
        small["norm2_g"][l] = dg2[0]
        dya, dyb, dyc, dgmix = _mix_bwd(dx_mid, sv["y_a"], sv["y_b"], sv["y_c"], p["gmix"], p["w_o_t"], f"mix_bwd{l}")
        small["mix_norm_g"][l] = dgmix[0]
        big["w_o"][l] = _mm_tn(sv["yn"], dx_mid, f"dw_o{l}").reshape(N_DEV, D_MODEL // N_DEV, D_MODEL)
        dp, dwbd, dsc = _pool_bwd(sv["p_in"], dyc, p["wbd"], p["wbd_t"], p["sc"], f"pool_bwd{l}")
        small["pool_w"][l] = jnp.stack([dwbd[g * 64:(g + 1) * 64, g * 64:(g + 1) * 64] for g in range(4)])
        small["pool_scale"][l] = dsc[0]
        scatter_early = _Exchange([wire(n, big[n][ll]) for n, ll in early], True) if l == 0 else None
        dq, dk, dv, recv_early = _attn_bwd(sv["qkv"], sv["cm"], dyb, f"attn_bwd{l}", scatter_early)
        if l == 0:
            recv_early_all = recv_early
        da, dwm, dbias, dgn = _sgu_bwd(sv["a_in"], dya, p["gn"], p["wm"], p["bias"], f"sgu_bwd{l}")
        small["sgu_w"][l] = dwm
        small["sgu_b"][l] = jnp.sum(dbias.reshape(CHUNK, 4, HEAD_DIM), axis=-1).T
        small["sgu_norm_g"][l] = dgn[0]
        dx, dg1 = _inproj_bwd(da, dq, dk, dv, dp, p["w_in_t"], sv["x_in"], p["g1"], dx_mid, f"inproj_bwd{l}")
        small["norm1_g"][l] = dg1[0]
        pieces = (da, dq, dk, dv, dp)
        dw_in = jnp.concatenate([_mm_tn(sv["h1"], t, f"dw_in{i}_{l}") for i, t in enumerate(pieces)], axis=1)
        big["w_in"][l] = jnp.transpose(dw_in.reshape(D_MODEL, N_DEV, IN_COLS // N_DEV), (1, 0, 2))

    part = {n: jnp.stack(small[n]) for n in small}
    part["final_g"] = dg_final[0]
    packed = _pack(part)
    recv_in0, gathered = _exchange([wire("w_in", big["w_in"][0]), jnp.broadcast_to(packed, (N_DEV,) + packed.shape)],
                                   "scatter_last", True)
    recv = dict(zip(early, recv_early_all))
    recv[("w_in", 0)] = recv_in0

    out_g, out_d, out_m, out_v = {}, {}, {}, {}
    for n in SHARDED:
        res = [_adamw(recv[(n, l)], weights[n][l], mom[n][l], var[n][l], f"adamw_{n}{l}") for l in range(depth)]
        out_g[n], out_d[n], out_m[n], out_v[n] = (jnp.stack([r[i] for r in res]) for i in range(4))

    sg, sd, sm, sv_ = _adamw(gathered, _pack(weights), _pack(mom), _pack(var), "adamw_small")
    for tree, flat in ((out_g, sg), (out_d, sd), (out_m, sm), (out_v, sv_)):
        tree.update(_unpack(flat, weights))

    loss = lax.psum(loss_part[0, 0], ("x", "y", "c"))
    grad_x = dx.reshape(1, s, D_MODEL)
    return (loss, grad_x, *[out_g[n] for n in ORDER], *[out_d[n] for n in ORDER], *[out_m[n] for n in ORDER],
            *[out_v[n] for n in ORDER])
```

```python
import functools
import math

import jax
import jax.numpy as jnp
import numpy as np
from jax import lax
from jax.experimental import pallas as pl
from jax.experimental.pallas import tpu as pltpu

F32 = jnp.float32
BF16 = jnp.bfloat16
_MXU = jnp.bfloat16
GRAD_WIRE = jnp.bfloat16

D_MODEL = 1024
W_A = 256
W_B = 512
W_C = 256
HEAD_DIM = 64
IN_COLS = 2 * W_A + 3 * W_B + W_C
D_FF = 2816
CHUNK = 128
POOL_WINDOWS = (2, 4, 8, 16)
EPS = 1e-6
N_DEV = 8
LANES = 128
SUBLANES = 8
VMEM_LIMIT = 48 * 1024 * 1024

ADAM_LR = 0.001
ADAM_B1 = 0.9
ADAM_B2 = 0.999
ADAM_EPS = 1e-08
ADAM_WD = 0.01
ADAM_STEP = 10

INV_SQRT2 = 1.0 / math.sqrt(2.0)
INV_SQRT_2PI = 1.0 / math.sqrt(2.0 * math.pi)


def _pcall(body, **kw):
    return pl.pallas_call(body, **kw)


def _params(dims=None):
    return pltpu.CompilerParams(dimension_semantics=dims, vmem_limit_bytes=VMEM_LIMIT)


def _tile(n, pref, mult=SUBLANES):
    t = min(n, pref) // mult * mult
    while t >= mult:
        if n % t == 0:
            return t
        t -= mult
    return n


def _iota(shape, dim):
    return lax.broadcasted_iota(jnp.int32, shape, dim)


def _dot(a, b):
    return jnp.dot(a.astype(_MXU), b.astype(_MXU), preferred_element_type=F32)


def _dot_nt(a, b):
    return lax.dot_general(a.astype(_MXU), b.astype(_MXU), (((1,), (1,)), ((), ())), preferred_element_type=F32)


def _dot_tn(a, b):
    return lax.dot_general(a.astype(_MXU), b.astype(_MXU), (((0,), (0,)), ((), ())), preferred_element_type=F32)


def _split(x):
    hi = x.astype(BF16)
    lo = (x - hi.astype(F32)).astype(BF16)
    return hi, lo


def _dot_sel(x, sel):
    hi, lo = _split(x)
    return jnp.dot(hi, sel, preferred_element_type=F32) + jnp.dot(lo, sel, preferred_element_type=F32)


def _sel_dot(sel, x):
    hi, lo = _split(x)
    return jnp.dot(sel, hi, preferred_element_type=F32) + jnp.dot(sel, lo, preferred_element_type=F32)


def _group_mat(n):
    r = jnp.right_shift(_iota((n, n), 0), 6)
    c = jnp.right_shift(_iota((n, n), 1), 6)
    return (r == c).astype(BF16)


def _lane_group_mask(n, g):
    lane = _iota((1, n), 1)
    return (lane >= g * HEAD_DIM) & (lane < (g + 1) * HEAD_DIM)


def _gelu(a):
    return 0.5 * a * (1.0 + lax.erf(a * INV_SQRT2))


def _gelu_grad(a):
    return 0.5 * (1.0 + lax.erf(a * INV_SQRT2)) + a * jnp.exp(-0.5 * a * a) * INV_SQRT_2PI


def _rms(x):
    return lax.rsqrt(jnp.mean(x * x, axis=-1, keepdims=True) + EPS)


def _rms_bwd(x, g, dy):
    r = _rms(x)
    dyg = dy * g
    m2 = jnp.mean(dyg * x, axis=-1, keepdims=True)
    dx = r * dyg - x * (r * r * r) * m2
    dg = jnp.sum(dy * x * r, axis=0, keepdims=True)
    return dx, dg


def _shift_down(cur, prev8, k):
    rolled = pltpu.roll(cur, k, 0)
    row8 = _iota(prev8.shape, 0)
    top = jnp.where(row8 < k, pltpu.roll(prev8, k, 0), rolled[0:SUBLANES])
    return jnp.concatenate([top, rolled[SUBLANES:]], axis=0)


def _shift_up(cur, next8, k):
    n = cur.shape[0]
    rolled = pltpu.roll(cur, n - k, 0)
    row8 = _iota(next8.shape, 0)
    bot = jnp.where(row8 >= SUBLANES - k, pltpu.roll(next8, SUBLANES - k, 0), rolled[n - SUBLANES:])
    return jnp.concatenate([rolled[:n - SUBLANES], bot], axis=0)


def _mesh_pos():
    return lax.axis_index("x"), lax.axis_index("y"), lax.axis_index("c")


def _peer(x, y, c, k):
    px = 1 - x if k & 4 else x
    py = 1 - y if k & 2 else y
    pc = 1 - c if k & 1 else c
    return px, py, pc


class _Exchange:
    def __init__(self, arrs, scatter):
        self.arrs, self.scatter, self.n = list(arrs), scatter, len(arrs)
        self.out_shapes = [jax.ShapeDtypeStruct(a.shape if scatter else (N_DEV,) + a.shape, a.dtype) for a in arrs]
        self.specs = [pl.BlockSpec(memory_space=pl.ANY)] * self.n
        self.semaphores = [pltpu.SemaphoreType.DMA((self.n * (N_DEV - 1),)),
                           pltpu.SemaphoreType.DMA((self.n * (N_DEV - 1),)), pltpu.SemaphoreType.DMA((self.n,))]

    def _copies(self, ins, outs, sems):
        send, recv, loc = sems
        x, y, c = _mesh_pos()
        me = 4 * x + 2 * y + c
        src = (lambda a, idx: ins[a].at[idx]) if self.scatter else (lambda a, idx: ins[a])
        starts = [pltpu.make_async_copy(src(a, me), outs[a].at[me], loc.at[a]) for a in range(self.n)]
        waits = list(starts)
        for k in range(1, N_DEV):
            px, py, pc = _peer(x, y, c, k)
            pidx = 4 * px + 2 * py + pc
            for a in range(self.n):
                s = a * (N_DEV - 1) + k - 1
                common = dict(src_ref=src(a, pidx), send_sem=send.at[s], recv_sem=recv.at[s],
                              device_id=(px, py, pc), device_id_type=pl.DeviceIdType.MESH)
                starts.append(pltpu.make_async_remote_copy(dst_ref=outs[a].at[me], **common))
                waits.append(pltpu.make_async_remote_copy(dst_ref=outs[a].at[pidx], **common))
        return starts, waits

    def start(self, ins, outs, sems):
        for cp in self._copies(ins, outs, sems)[0]:
            cp.start()

    def wait(self, ins, outs, sems):
        for cp in self._copies(ins, outs, sems)[1]:
            cp.wait()


def _with_exchange(compute, ride, n_in, n_out, n_scratch, last_step):
    if ride is None:
        return compute
    nx = ride.n

    def body(*refs):
        ins, ride_in = refs[:n_in], refs[n_in:n_in + nx]
        outs = refs[n_in + nx:n_in + nx + n_out]
        ride_out = refs[n_in + nx + n_out:n_in + 2 * nx + n_out]
        scratch = refs[n_in + 2 * nx + n_out:n_in + 2 * nx + n_out + n_scratch]
        sems = refs[n_in + 2 * nx + n_out + n_scratch:]
        step = (pl.program_id(0), pl.program_id(1))

        @pl.when((step[0] == 0) & (step[1] == 0))
        def _():
            ride.start(ride_in, ride_out, sems)

        compute(*ins, *outs, *scratch)

        @pl.when((step[0] == last_step[0]) & (step[1] == last_step[1]))
        def _():
            ride.wait(ride_in, ride_out, sems)

    return body


def _exchange(arrs, name, scatter):
    ex = _Exchange(arrs, scatter)
    n = ex.n

    def body(*refs):
        ins, outs, sems = refs[:n], refs[n:2 * n], refs[2 * n:]
        ex.start(ins, outs, sems)
        ex.wait(ins, outs, sems)

    return _pcall(body, name=name, out_shape=ex.out_shapes, in_specs=ex.specs, out_specs=ex.specs,
                  scratch_shapes=ex.semaphores, compiler_params=pltpu.CompilerParams(has_side_effects=True))(*arrs)


def _mm_tn(a, b, name):
    s, m = a.shape
    n = b.shape[1]
    tm, tn, tk = _tile(m, 1408, LANES), _tile(n, 1408, LANES), _tile(s, 1024)

    def body(a_ref, b_ref, o_ref):
        @pl.when(pl.program_id(2) == 0)
        def _():
            o_ref[...] = jnp.zeros_like(o_ref)
        o_ref[...] += _dot_tn(a_ref[...], b_ref[...])

    return _pcall(
        body, name=name, grid=(m // tm, n // tn, s // tk),
        in_specs=[pl.BlockSpec((tk, tm), lambda i, j, k: (k, i)), pl.BlockSpec((tk, tn), lambda i, j, k: (k, j))],
        out_specs=pl.BlockSpec((tm, tn), lambda i, j, k: (i, j)),
        out_shape=jax.ShapeDtypeStruct((m, n), F32),
        compiler_params=_params(("parallel", "parallel", "arbitrary")),
    )(a, b)


def _inproj_fwd(x, g, w, name):
    s = x.shape[0]
    tm = _tile(s, 256)

    def body(x_ref, g_ref, w_ref, a_ref, qkv_ref, p_ref, h_ref):
        xv = x_ref[...]
        h = (xv * _rms(xv) * g_ref[...]).astype(_MXU)
        h_ref[...] = h
        a_ref[...] = jnp.dot(h, w_ref[:, 0:2 * W_A], preferred_element_type=F32)
        q = jnp.dot(h, w_ref[:, 2 * W_A:2 * W_A + W_B], preferred_element_type=F32)
        qkv_ref[:, 0:W_B] = (q * 0.125).astype(_MXU)
        kv = jnp.dot(h, w_ref[:, 2 * W_A + W_B:2 * W_A + 3 * W_B], preferred_element_type=F32)
        qkv_ref[:, W_B:3 * W_B] = kv.astype(_MXU)
        p_ref[...] = jnp.dot(h, w_ref[:, 2 * W_A + 3 * W_B:IN_COLS], preferred_element_type=F32)

    row = lambda n: pl.BlockSpec((tm, n), lambda i: (i, 0))
    return _pcall(
        body, name=name, grid=(s // tm,),
        in_specs=[row(D_MODEL), pl.BlockSpec((1, D_MODEL), lambda i: (0, 0)),
                  pl.BlockSpec((D_MODEL, IN_COLS), lambda i: (0, 0))],
        out_specs=[row(2 * W_A), row(3 * W_B), row(W_C), row(D_MODEL)],
        out_shape=[jax.ShapeDtypeStruct((s, 2 * W_A), F32), jax.ShapeDtypeStruct((s, 3 * W_B), _MXU),
                   jax.ShapeDtypeStruct((s, W_C), F32), jax.ShapeDtypeStruct((s, D_MODEL), _MXU)],
        compiler_params=_params(("parallel",)),
    )(x, g, w)


def _inproj_bwd(da, dq, dk, dv, dp, wt, x, g, dres, name):
    s = x.shape[0]
    tm = _tile(s, 256)

    def body(da_ref, dq_ref, dk_ref, dv_ref, dp_ref, wt_ref, x_ref, g_ref, dres_ref, dx_ref, dg_ref):
        dh = _dot(da_ref[...], wt_ref[0:2 * W_A, :])
        dh += _dot(dq_ref[...], wt_ref[2 * W_A:2 * W_A + W_B, :])
        dh += _dot(dk_ref[...], wt_ref[2 * W_A + W_B:2 * W_A + 2 * W_B, :])
        dh += _dot(dv_ref[...], wt_ref[2 * W_A + 2 * W_B:2 * W_A + 3 * W_B, :])
        dh += _dot(dp_ref[...], wt_ref[2 * W_A + 3 * W_B:IN_COLS, :])
        dx, dg = _rms_bwd(x_ref[...], g_ref[...], dh)
        dx_ref[...] = dres_ref[...] + dx

        @pl.when(pl.program_id(0) == 0)
        def _():
            dg_ref[...] = jnp.zeros_like(dg_ref)
        dg_ref[...] += dg

    row = lambda n: pl.BlockSpec((tm, n), lambda i: (i, 0))
    vec = pl.BlockSpec((1, D_MODEL), lambda i: (0, 0))
    return _pcall(
        body, name=name, grid=(s // tm,),
        in_specs=[row(2 * W_A), row(W_B), row(W_B), row(W_B), row(W_C),
                  pl.BlockSpec((IN_COLS, D_MODEL), lambda i: (0, 0)), row(D_MODEL), vec, row(D_MODEL)],
        out_specs=[row(D_MODEL), vec],
        out_shape=[jax.ShapeDtypeStruct((s, D_MODEL), F32), jax.ShapeDtypeStruct((1, D_MODEL), F32)],
        compiler_params=_params(("arbitrary",)),
    )(da, dq, dk, dv, dp, wt, x, g, dres)


def _sgu_core(a, gn, wm_ref, bias):
    ga = _gelu(a)
    u, v0 = ga[:, 0:W_A], ga[:, W_A:2 * W_A]
    r = lax.rsqrt(_dot_sel(v0 * v0, _group_mat(W_A)) * (1.0 / HEAD_DIM) + EPS)
    vn = v0 * r * gn
    sv = bias
    for h in range(W_A // HEAD_DIM):
        sv = sv + _dot(wm_ref[h], jnp.where(_lane_group_mask(W_A, h), vn, 0.0))
    return u, v0, r, vn, sv


def _sgu_fwd(a, gn, wm, bias, name):
    s = a.shape[0]

    def body(a_ref, gn_ref, wm_ref, b_ref, y_ref):
        u, _, _, _, sv = _sgu_core(a_ref[...], gn_ref[...], wm_ref, b_ref[...])
        y_ref[...] = u * sv

    return _pcall(
        body, name=name, grid=(s // CHUNK,),
        in_specs=[pl.BlockSpec((CHUNK, 2 * W_A), lambda i: (i, 0)), pl.BlockSpec((1, W_A), lambda i: (0, 0)),
                  pl.BlockSpec((4, CHUNK, CHUNK), lambda i: (0, 0, 0)), pl.BlockSpec((CHUNK, W_A), lambda i: (0, 0))],
        out_specs=pl.BlockSpec((CHUNK, W_A), lambda i: (i, 0)),
        out_shape=jax.ShapeDtypeStruct((s, W_A), F32),
        compiler_params=_params(("parallel",)),
    )(a, gn, wm, bias)


def _sgu_bwd(a, dy, gn, wm, bias, name):
    s = a.shape[0]

    def body(a_ref, dy_ref, gn_ref, wm_ref, b_ref, da_ref, dwm_ref, db_ref, dgn_ref):
        @pl.when(pl.program_id(0) == 0)
        def _():
            dwm_ref[...] = jnp.zeros_like(dwm_ref)
            db_ref[...] = jnp.zeros_like(db_ref)
            dgn_ref[...] = jnp.zeros_like(dgn_ref)

        av, gnv, dyv = a_ref[...], gn_ref[...], dy_ref[...]
        u, v0, r, vn, sv = _sgu_core(av, gnv, wm_ref, b_ref[...])
        du = dyv * sv
        ds = dyv * u
        db_ref[...] += ds
        tril = _iota((CHUNK, CHUNK), 1) <= _iota((CHUNK, CHUNK), 0)
        dvn = jnp.zeros_like(vn)
        for h in range(W_A // HEAD_DIM):
            dsm = jnp.where(_lane_group_mask(W_A, h), ds, 0.0)
            dwm_ref[h] += jnp.where(tril, _dot_nt(dsm, vn), 0.0)
            dvn = dvn + _dot_tn(wm_ref[h], dsm)
        dgn_ref[...] += jnp.sum(dvn * v0 * r, axis=0, keepdims=True)
        dvg = dvn * gnv
        m2 = _dot_sel(dvg * v0, _group_mat(W_A)) * (1.0 / HEAD_DIM)
        dv0 = r * dvg - v0 * (r * r * r) * m2
        gp = _gelu_grad(av)
        da_ref[:, 0:W_A] = du * gp[:, 0:W_A]
        da_ref[:, W_A:2 * W_A] = dv0 * gp[:, W_A:2 * W_A]

    return _pcall(
        body, name=name, grid=(s // CHUNK,),
        in_specs=[pl.BlockSpec((CHUNK, 2 * W_A), lambda i: (i, 0)), pl.BlockSpec((CHUNK, W_A), lambda i: (i, 0)),
                  pl.BlockSpec((1, W_A), lambda i: (0, 0)), pl.BlockSpec((4, CHUNK, CHUNK), lambda i: (0, 0, 0)),
                  pl.BlockSpec((CHUNK, W_A), lambda i: (0, 0))],
        out_specs=[pl.BlockSpec((CHUNK, 2 * W_A), lambda i: (i, 0)), pl.BlockSpec((4, CHUNK, CHUNK), lambda i: (0, 0, 0)),
                   pl.BlockSpec((CHUNK, W_A), lambda i: (0, 0)), pl.BlockSpec((1, W_A), lambda i: (0, 0))],
        out_shape=[jax.ShapeDtypeStruct((s, 2 * W_A), F32), jax.ShapeDtypeStruct((4, CHUNK, CHUNK), F32),
                   jax.ShapeDtypeStruct((CHUNK, W_A), F32), jax.ShapeDtypeStruct((1, W_A), F32)],
        compiler_params=_params(("arbitrary",)),
    )(a, dy, gn, wm, bias)


KBLK = 128
ROW_CHUNK = 64
FWD_UNROLL = 2
BWD_UNROLL = 4
MASKED_SCORE = -1e30


def _attn_fwd(qkv, name, ride=None):
    s = qkv.shape[0]
    tq = _tile(s, 256, KBLK)
    npairs = W_B // LANES
    assert s // KBLK <= LANES
    rides = [] if ride is None else [ride]

    def body(q_ref, k_ref, v_ref, o_ref, cm_ref, z_ref, zw_ref, sums_ref, carry_ref, hl_ref, a_ref):
        i = pl.program_id(1)
        q = q_ref[...]
        lane = _iota((1, LANES), 1)
        lane_lo = lane < HEAD_DIM
        hmask = (lane_lo, jnp.logical_not(lane_lo))
        tri2 = ((_iota((KBLK, 2 * KBLK), 0) >= _iota((KBLK, 2 * KBLK), 1))
                | (_iota((KBLK, 2 * KBLK), 1) >= KBLK)).astype(BF16)
        dmat = _iota((tq, KBLK), 1) - (_iota((tq, KBLK), 0) + i * tq)
        chunks = [slice(r, r + ROW_CHUNK) for r in range(0, tq, ROW_CHUNK)]
        heads = [slice(hh * LANES, (hh + 1) * LANES) for hh in range(2)]
        nk = (i + 1) * (tq // KBLK)

        cm_ref[...] = jnp.zeros_like(cm_ref)

        def before(b):
            return jnp.where((b >= 0) & (b < nk), -b * KBLK, jnp.iinfo(jnp.int32).min)

        def per_head(block):
            return jnp.concatenate([jnp.where(m, block, jnp.zeros_like(block)) for m in hmask], axis=0)

        def scores(b, p):
            ks = pl.multiple_of(jnp.clip(b, 0, nk - 1) * KBLK, KBLK)
            z_ref[p] = _dot_nt(q, per_head(k_ref[pl.ds(ks, KBLK), :]))

        def logs(b, p):
            t = before(b)
            for hh in range(2):
                for rows in chunks:
                    z = jnp.where(dmat[rows] < t, z_ref[p, rows, heads[hh]], MASKED_SCORE)
                    zw_ref[p, rows, heads[hh]] = z
                    l = -(jnp.maximum(z, 0.0) + jnp.log(1.0 + jnp.exp(-jnp.abs(z))))
                    hi, lo = _split(l)
                    hl_ref[p, hh, rows, 0:KBLK] = hi
                    hl_ref[p, hh, rows, KBLK:2 * KBLK] = lo

        def sums(b, p):
            for hh in range(2):
                sums_ref[p, hh] = (jnp.dot(hl_ref[p, hh, :, 0:KBLK], tri2, preferred_element_type=F32)
                                   + jnp.dot(hl_ref[p, hh, :, KBLK:2 * KBLK], tri2, preferred_element_type=F32))

        def weights(b, p):
            pick = lane == b
            for hh in range(2):
                for rows in chunks:
                    c = carry_ref[hh, rows, :]
                    arg = zw_ref[p, rows, heads[hh]] + c + sums_ref[p, hh, rows, 0:KBLK]
                    a_ref[p, rows, heads[hh]] = jnp.exp(arg).astype(_MXU)
                    cm_ref[rows, heads[hh]] = jnp.where(pick, c, cm_ref[rows, heads[hh]])
                    carry_ref[hh, rows, :] = c + sums_ref[p, hh, rows, KBLK:2 * KBLK]

        def out(b, p, acc):
            ks = pl.multiple_of(jnp.minimum(b, nk - 1) * KBLK, KBLK)
            vb = v_ref[pl.ds(ks, KBLK), :]
            for hh in range(2):
                acc = acc + jnp.dot(a_ref[p, :, heads[hh]], jnp.where(hmask[hh], vb, jnp.zeros_like(vb)),
                                    preferred_element_type=F32)
            return acc

        trips = (nk + 4 + FWD_UNROLL - 1) // FWD_UNROLL

        def step(it, acc):
            for u in range(FWD_UNROLL):
                b = trips * FWD_UNROLL - 1 - (FWD_UNROLL * it + u)
                p = 1 - u % 2
                acc = out(b, p, acc)
                weights(b - 1, 1 - p)
                sums(b - 2, p)
                logs(b - 3, 1 - p)
                scores(b - 4, p)
            return acc

        z_ref[...] = jnp.zeros_like(z_ref)
        zw_ref[...] = jnp.full_like(zw_ref, MASKED_SCORE)
        for ref in (sums_ref, carry_ref, hl_ref, a_ref):
            ref[...] = jnp.zeros_like(ref)
        o_ref[...] = lax.fori_loop(0, trips, step, q.astype(F32) * 0.0)

    scratch = [pltpu.VMEM((2, tq, 2 * KBLK), F32), pltpu.VMEM((2, tq, 2 * KBLK), F32),
               pltpu.VMEM((2, 2, tq, 2 * KBLK), F32), pltpu.VMEM((2, tq, KBLK), F32),
               pltpu.VMEM((2, 2, tq, 2 * KBLK), BF16), pltpu.VMEM((2, tq, 2 * KBLK), _MXU)]
    res = _pcall(
        _with_exchange(body, ride, 3, 2, len(scratch), (npairs - 1, s // tq - 1)), name=name, grid=(npairs, s // tq),
        in_specs=[pl.BlockSpec((tq, LANES), lambda p, i: (i, p)),
                  pl.BlockSpec((s, LANES), lambda p, i: (0, npairs + p)),
                  pl.BlockSpec((s, LANES), lambda p, i: (0, 2 * npairs + p))] + [sp for r in rides for sp in r.specs],
        out_specs=[pl.BlockSpec((tq, LANES), lambda p, i: (i, p)), pl.BlockSpec((tq, 2 * LANES), lambda p, i: (i, p))]
                  + [sp for r in rides for sp in r.specs],
        out_shape=[jax.ShapeDtypeStruct((s, W_B), F32), jax.ShapeDtypeStruct((s, 2 * W_B), F32)]
                  + [sh for r in rides for sh in r.out_shapes],
        scratch_shapes=scratch + [sem for r in rides for sem in r.semaphores],
        compiler_params=_params(("arbitrary", "arbitrary")),
    )(qkv, qkv, qkv, *[a for r in rides for a in r.arrs])
    return res[0], res[1], list(res[2:])


def _attn_bwd(qkv, cm, do, name, ride=None):
    s = qkv.shape[0]
    tq = _tile(s, 256, KBLK)
    npairs = W_B // LANES
    rides = [] if ride is None else [ride]

    nq = s // tq
    per_tile = tq // KBLK
    assert nq % 2 == 0 and per_tile % 2 == 0
    ntot = (nq + 1) * per_tile
    assert (ntot + 6) % BWD_UNROLL == 0

    def body(qa_ref, k_ref, v_ref, cma_ref, doa_ref, qb_ref, cmb_ref, dob_ref, dqa_ref, dqb_ref, dk_ref, dv_ref,
             z_ref, zw_ref, da_ref, g_ref, sig_ref, cum_ref, gp_ref, gcarry_ref, lb_ref, gb_ref, a_ref, dz_ref,
             dkt_ref, dvt_ref, q_st, do_st, qt_st, dot_st, cm_st):
        i = pl.program_id(1)

        @pl.when(i == 0)
        def _():
            dkt_ref[...] = jnp.zeros_like(dkt_ref)
            dvt_ref[...] = jnp.zeros_like(dvt_ref)

        tiles = (i, nq - 1 - i)
        nk_a = (i + 1) * per_tile
        for t, (q_in, do_in, cm_in) in enumerate(((qa_ref, doa_ref, cma_ref), (qb_ref, dob_ref, cmb_ref))):
            q_st[t] = q_in[...]
            do_st[t] = do_in[...].astype(_MXU)
            qt_st[t] = q_in[...].astype(F32).T.astype(_MXU)
            dot_st[t] = do_in[...].T.astype(_MXU)
            cm_st[t] = cm_in[...]

        lane = _iota((1, LANES), 1)
        lane_lo = lane < HEAD_DIM
        hmask = (lane_lo, jnp.logical_not(lane_lo))
        tri = (_iota((KBLK, KBLK), 0) >= _iota((KBLK, KBLK), 1)).astype(BF16)
        prefix = (_iota((KBLK, KBLK), 0) <= _iota((KBLK, KBLK), 1)).astype(BF16)
        dmat = _iota((tq, KBLK), 1) - _iota((tq, KBLK), 0)
        chunks = [slice(r, r + ROW_CHUNK) for r in range(0, tq, ROW_CHUNK)]
        heads = [slice(hh * LANES, (hh + 1) * LANES) for hh in range(2)]

        def locate(v):
            second = v >= nk_a
            return ((v >= 0) & (v < ntot), second.astype(jnp.int32), jnp.where(second, tiles[1], tiles[0]),
                    jnp.where(second, v - nk_a, v))

        def before(v):
            valid, _, tile, b = locate(v)
            return jnp.where(valid, tile * tq - b * KBLK, jnp.iinfo(jnp.int32).min)

        def key_block(v):
            return jnp.clip(locate(v)[3], 0, nblk - 1)

        def block_rows(v):
            return pl.ds(pl.multiple_of(key_block(v) * KBLK, KBLK), KBLK)

        def per_head(block):
            return jnp.concatenate([jnp.where(m, block, jnp.zeros_like(block)) for m in hmask], axis=0)

        feature_lo = _iota((LANES, KBLK), 0) < HEAD_DIM

        def own_features(side_by_side):
            return jnp.where(feature_lo, side_by_side[:, 0:KBLK], side_by_side[:, KBLK:2 * KBLK])

        def m1(b, p):
            z_ref[p] = _dot_nt(q_st[locate(b)[1]], per_head(k_ref[block_rows(b), :]))

        def v1(b, p):
            t = before(b)
            for hh in range(2):
                for rows in chunks:
                    z = jnp.where(dmat[rows] < t, z_ref[p, rows, heads[hh]], MASKED_SCORE)
                    zw_ref[p, rows, heads[hh]] = z
                    l = -(jnp.maximum(z, 0.0) + jnp.log(1.0 + jnp.exp(-jnp.abs(z))))
                    lb_ref[p, hh, rows, :] = l.astype(BF16)

        def m2(b, p):
            for hh in range(2):
                cum_ref[p, hh] = jnp.dot(lb_ref[p, hh], tri, preferred_element_type=F32)
            da_ref[p] = _dot_nt(do_st[locate(b)[1]], per_head(v_ref[block_rows(b), :]))

        def v2(b, p):
            valid, which, _, blk = locate(b)
            pick = lane == jnp.where(valid, blk, -1)
            for hh in range(2):
                for rows in chunks:
                    c = jnp.sum(jnp.where(pick, cm_st[which, rows, heads[hh]], 0.0), axis=1, keepdims=True)
                    z = zw_ref[p, rows, heads[hh]]
                    a = jnp.exp(z + c + cum_ref[p, hh, rows, :])
                    g = a * da_ref[p, rows, heads[hh]]
                    a_ref[p, rows, heads[hh]] = a.astype(_MXU)
                    g_ref[p, rows, heads[hh]] = g
                    gb_ref[p, hh, rows, :] = g.astype(BF16)
                    sig_ref[p, rows, heads[hh]] = jax.nn.sigmoid(z)

        def m3(b, p):
            for hh in range(2):
                gp_ref[p, hh] = jnp.dot(gb_ref[p, hh], prefix, preferred_element_type=F32)
            dvt_ref[key_block(b)] += own_features(jnp.dot(dot_st[locate(b)[1]], a_ref[p], preferred_element_type=F32))

        def v3(b, p):
            restart = b == nk_a
            for hh in range(2):
                for rows in chunks:
                    gc = jnp.where(restart, 0.0, gcarry_ref[hh, rows, :])
                    g = g_ref[p, rows, heads[hh]]
                    dz = g - sig_ref[p, rows, heads[hh]] * (gc + gp_ref[p, hh, rows, :])
                    dz_ref[p, rows, heads[hh]] = dz.astype(_MXU)
                    gcarry_ref[hh, rows, :] = gc + jnp.sum(g, axis=1, keepdims=True)

        def m4(b, p, dqs):
            which = locate(b)[1]
            kb = k_ref[block_rows(b), :]
            part = None
            for hh in range(2):
                d = jnp.dot(dz_ref[p, :, heads[hh]], jnp.where(hmask[hh], kb, jnp.zeros_like(kb)),
                            preferred_element_type=F32)
                part = d if part is None else part + d
            dkt_ref[key_block(b)] += own_features(jnp.dot(qt_st[which], dz_ref[p], preferred_element_type=F32))
            return dqs[0] + jnp.where(which == 0, part, 0.0), dqs[1] + jnp.where(which == 1, part, 0.0)

        def step(it, dqs):
            for u in range(BWD_UNROLL):
                j = BWD_UNROLL * it + u
                p = u % 2
                dqs = m4(j - 6, p, dqs)
                m3(j - 4, p)
                m2(j - 2, p)
                m1(j, p)
                v3(j - 5, 1 - p)
                v2(j - 3, 1 - p)
                v1(j - 1, 1 - p)
            return dqs

        zw_ref[...] = jnp.full_like(zw_ref, MASKED_SCORE)
        for ref in (z_ref, da_ref, g_ref, sig_ref, cum_ref, gp_ref, gcarry_ref, lb_ref, gb_ref, a_ref, dz_ref):
            ref[...] = jnp.zeros_like(ref)
        zero = doa_ref[...] * 0.0
        dq_a, dq_b = lax.fori_loop(0, (ntot + 6) // BWD_UNROLL, step, (zero, zero))
        dqa_ref[...] = dq_a * 0.125
        dqb_ref[...] = dq_b * 0.125

        @pl.when(i == nq // 2 - 1)
        def _():
            def untranspose(blk, carry):
                rows = pl.ds(pl.multiple_of(blk * KBLK, KBLK), KBLK)
                dk_ref[rows, :] = dkt_ref[blk].T
                dv_ref[rows, :] = dvt_ref[blk].T
                return carry
            lax.fori_loop(0, nblk, untranspose, 0)

    first = lambda width: pl.BlockSpec((tq, width), lambda p, i: (i, p))
    second = lambda width: pl.BlockSpec((tq, width), lambda p, i: (nq - 1 - i, p))
    second_out = pl.BlockSpec((tq, LANES), lambda p, i: (nq // 2 - 1 - i, p))
    nblk = s // KBLK
    full = pl.BlockSpec((s, LANES), lambda p, i: (0, p))
    scratch = ([pltpu.VMEM((2, tq, 2 * KBLK), F32)] * 5 + [pltpu.VMEM((2, 2, tq, KBLK), F32),
               pltpu.VMEM((2, 2, tq, KBLK), F32), pltpu.VMEM((2, tq, KBLK), F32)]
               + [pltpu.VMEM((2, 2, tq, KBLK), BF16)] * 2 + [pltpu.VMEM((2, tq, 2 * KBLK), _MXU)] * 2
               + [pltpu.VMEM((nblk, LANES, KBLK), F32)] * 2
               + [pltpu.VMEM((2, tq, LANES), _MXU)] * 2 + [pltpu.VMEM((2, LANES, tq), _MXU)] * 2
               + [pltpu.VMEM((2, tq, 2 * LANES), F32)])
    res = _pcall(
        _with_exchange(body, ride, 8, 4, len(scratch), (npairs - 1, nq // 2 - 1)), name=name, grid=(npairs, nq // 2),
        in_specs=[first(LANES), pl.BlockSpec((s, LANES), lambda p, i: (0, npairs + p)),
                  pl.BlockSpec((s, LANES), lambda p, i: (0, 2 * npairs + p)), first(2 * LANES), first(LANES),
                  second(LANES), second(2 * LANES), second(LANES)] + [sp for r in rides for sp in r.specs],
        out_specs=[first(LANES), second_out, full, full] + [sp for r in rides for sp in r.specs],
        out_shape=[jax.ShapeDtypeStruct((s // 2, W_B), F32)] * 2 + [jax.ShapeDtypeStruct((s, W_B), F32)] * 2
                  + [sh for r in rides for sh in r.out_shapes],
        scratch_shapes=scratch + [sem for r in rides for sem in r.semaphores],
        compiler_params=_params(("arbitrary", "arbitrary")),
    )(qkv, qkv, qkv, cm, do, qkv, cm, do, *[a for r in rides for a in r.arrs])
    return jnp.concatenate([res[0], res[1]], axis=0), res[2], res[3], list(res[4:])


POOL_HALO = 128


def _pool_window_lane():
    lane = _iota((1, W_C), 1)
    w = jnp.where(lane < 64, POOL_WINDOWS[0], jnp.where(lane < 128, POOL_WINDOWS[1],
                  jnp.where(lane < 192, POOL_WINDOWS[2], POOL_WINDOWS[3])))
    return w.astype(F32)


def _pool_count(tm, i):
    pos = (_iota((tm, W_C), 0) + (i * tm + 1)).astype(F32)
    return jnp.minimum(pos, _pool_window_lane())


def _pool_centered(prev, cur, cnt):
    tm = cur.shape[0]
    xx = jnp.concatenate([prev, cur], axis=0)
    hi, lo = _split(xx)
    t = _iota((tm, tm + POOL_HALO), 0)
    cc = _iota((tm, tm + POOL_HALO), 1) - POOL_HALO
    wsum = jnp.zeros_like(cur)
    for g, w in enumerate(POOL_WINDOWS):
        band = ((cc <= t) & (cc > t - w)).astype(BF16)
        mg = _lane_group_mask(W_C, g)
        wsum += jnp.dot(band, jnp.where(mg, hi, jnp.zeros_like(hi)), preferred_element_type=F32)
        wsum += jnp.dot(band, jnp.where(mg, lo, jnp.zeros_like(lo)), preferred_element_type=F32)
    return wsum / cnt - cur


def _pool_fwd(p, wbd, sc, name):
    s = p.shape[0]
    tm = _tile(s, 256, POOL_HALO)
    r = tm // POOL_HALO

    def body(pp_ref, p_ref, w_ref, sc_ref, y_ref):
        i = pl.program_id(0)
        prev = jnp.where(i > 0, pp_ref[...], 0.0)
        d = _pool_centered(prev, p_ref[...], _pool_count(tm, i))
        y_ref[...] = _dot(d, w_ref[...]) * sc_ref[...]

    return _pcall(
        body, name=name, grid=(s // tm,),
        in_specs=[pl.BlockSpec((POOL_HALO, W_C), lambda i: (jnp.maximum(i * r - 1, 0), 0)),
                  pl.BlockSpec((tm, W_C), lambda i: (i, 0)), pl.BlockSpec((W_C, W_C), lambda i: (0, 0)),
                  pl.BlockSpec((1, W_C), lambda i: (0, 0))],
        out_specs=pl.BlockSpec((tm, W_C), lambda i: (i, 0)),
        out_shape=jax.ShapeDtypeStruct((s, W_C), F32),
        compiler_params=_params(("parallel",)),
    )(p, p, wbd, sc)


def _pool_bwd(p, dy, wbd, wbdt, sc, name):
    s = p.shape[0]
    tm = _tile(s, 256, POOL_HALO)
    r = tm // POOL_HALO
    nt = s // tm

    def body(pp_ref, p_ref, dy_ref, dyn_ref, w_ref, wt_ref, sc_ref, dp_ref, dw_ref, dsc_ref):
        i = pl.program_id(0)

        @pl.when(i == 0)
        def _():
            dw_ref[...] = jnp.zeros_like(dw_ref)
            dsc_ref[...] = jnp.zeros_like(dsc_ref)

        scv = sc_ref[...]
        cnt = _pool_count(tm, i)
        prev = jnp.where(i > 0, pp_ref[...], 0.0)
        d = _pool_centered(prev, p_ref[...], cnt)
        e = _dot(d, w_ref[...])
        dyv = dy_ref[...]
        de = dyv * scv
        dsc_ref[...] += jnp.sum(dyv * e, axis=0, keepdims=True)
        dw_ref[...] += _dot_tn(d, de)
        dd = _dot(de, wt_ref[...])
        ddn = jnp.where(i < nt - 1, _dot(dyn_ref[...] * scv, wt_ref[...]), 0.0)
        yy = jnp.concatenate([dd / cnt, ddn / _pool_window_lane()], axis=0)
        hi, lo = _split(yy)
        t = _iota((tm, tm + POOL_HALO), 0)
        cc = _iota((tm, tm + POOL_HALO), 1)
        acc = jnp.zeros_like(dd)
        for g, w in enumerate(POOL_WINDOWS):
            band = ((cc >= t) & (cc < t + w)).astype(BF16)
            mg = _lane_group_mask(W_C, g)
            acc += jnp.dot(band, jnp.where(mg, hi, jnp.zeros_like(hi)), preferred_element_type=F32)
            acc += jnp.dot(band, jnp.where(mg, lo, jnp.zeros_like(lo)), preferred_element_type=F32)
        dp_ref[...] = acc - dd

    tile = pl.BlockSpec((tm, W_C), lambda i: (i, 0))
    mat = pl.BlockSpec((W_C, W_C), lambda i: (0, 0))
    vec = pl.BlockSpec((1, W_C), lambda i: (0, 0))
    return _pcall(
        body, name=name, grid=(nt,),
        in_specs=[pl.BlockSpec((POOL_HALO, W_C), lambda i: (jnp.maximum(i * r - 1, 0), 0)), tile, tile,
                  pl.BlockSpec((POOL_HALO, W_C), lambda i: (jnp.minimum((i + 1) * r, s // POOL_HALO - 1), 0)),
                  mat, mat, vec],
        out_specs=[tile, mat, vec],
        out_shape=[jax.ShapeDtypeStruct((s, W_C), F32), jax.ShapeDtypeStruct((W_C, W_C), F32),
                   jax.ShapeDtypeStruct((1, W_C), F32)],
        compiler_params=_params(("arbitrary",)),
    )(p, p, dy, dy, wbd, wbdt, sc)


def _mix_cols(ya_ref, yb_ref, yc_ref, cb):
    if cb < 2:
        return ya_ref[:, cb * LANES:(cb + 1) * LANES]
    if cb < 6:
        return yb_ref[:, (cb - 2) * LANES:(cb - 1) * LANES]
    return yc_ref[:, (cb - 6) * LANES:(cb - 5) * LANES]


def _mix_fwd(ya, yb, yc, g, wo, x, name):
    s = x.shape[0]
    tm = _tile(s, 256)

    def body(ya_ref, yb_ref, yc_ref, g_ref, w_ref, x_ref, o_ref, yn_ref):
        sel = _group_mat(LANES)
        for cb in range(D_MODEL // LANES):
            y = _mix_cols(ya_ref, yb_ref, yc_ref, cb)
            r = lax.rsqrt(_dot_sel(y * y, sel) * (1.0 / HEAD_DIM) + EPS)
            yn_ref[:, cb * LANES:(cb + 1) * LANES] = (y * r * g_ref[:, cb * LANES:(cb + 1) * LANES]).astype(_MXU)
        o_ref[...] = x_ref[...] + jnp.dot(yn_ref[...], w_ref[...], preferred_element_type=F32)

    row = lambda n: pl.BlockSpec((tm, n), lambda i: (i, 0))
    return _pcall(
        body, name=name, grid=(s // tm,),
        in_specs=[row(W_A), row(W_B), row(W_C), pl.BlockSpec((1, D_MODEL), lambda i: (0, 0)),
                  pl.BlockSpec((D_MODEL, D_MODEL), lambda i: (0, 0)), row(D_MODEL)],
        out_specs=[row(D_MODEL), row(D_MODEL)],
        out_shape=[jax.ShapeDtypeStruct((s, D_MODEL), F32), jax.ShapeDtypeStruct((s, D_MODEL), _MXU)],
        compiler_params=_params(("parallel",)),
    )(ya, yb, yc, g, wo, x)


def _mix_bwd(dx, ya, yb, yc, g, wot, name):
    s = dx.shape[0]
    tm = _tile(s, 256)

    def body(dx_ref, ya_ref, yb_ref, yc_ref, g_ref, wt_ref, dya_ref, dyb_ref, dyc_ref, dg_ref):
        @pl.when(pl.program_id(0) == 0)
        def _():
            dg_ref[...] = jnp.zeros_like(dg_ref)

        dyn = _dot(dx_ref[...], wt_ref[...])
        sel = _group_mat(LANES)
        for cb in range(D_MODEL // LANES):
            cols = slice(cb * LANES, (cb + 1) * LANES)
            y = _mix_cols(ya_ref, yb_ref, yc_ref, cb)
            r = lax.rsqrt(_dot_sel(y * y, sel) * (1.0 / HEAD_DIM) + EPS)
            dyc_ = dyn[:, cols]
            dyg = dyc_ * g_ref[:, cols]
            m2 = _dot_sel(dyg * y, sel) * (1.0 / HEAD_DIM)
            dy = r * dyg - y * (r * r * r) * m2
            dg_ref[:, cols] += jnp.sum(dyc_ * y * r, axis=0, keepdims=True)
            if cb < 2:
                dya_ref[:, cb * LANES:(cb + 1) * LANES] = dy
            elif cb < 6:
                dyb_ref[:, (cb - 2) * LANES:(cb - 1) * LANES] = dy
            else:
                dyc_ref[:, (cb - 6) * LANES:(cb - 5) * LANES] = dy

    row = lambda n: pl.BlockSpec((tm, n), lambda i: (i, 0))
    vec = pl.BlockSpec((1, D_MODEL), lambda i: (0, 0))
    return _pcall(
        body, name=name, grid=(s // tm,),
        in_specs=[row(D_MODEL), row(W_A), row(W_B), row(W_C), vec, pl.BlockSpec((D_MODEL, D_MODEL), lambda i: (0, 0))],
        out_specs=[row(W_A), row(W_B), row(W_C), vec],
        out_shape=[jax.ShapeDtypeStruct((s, W_A), F32), jax.ShapeDtypeStruct((s, W_B), F32),
                   jax.ShapeDtypeStruct((s, W_C), F32), jax.ShapeDtypeStruct((1, D_MODEL), F32)],
        compiler_params=_params(("arbitrary",)),
    )(dx, ya, yb, yc, g, wot)


FFN_CHUNK = 1408
FFN_ROWS = 256
N_CHUNKS = D_FF // FFN_CHUNK
CW_ROWS = 8


def _ffn_up_fwd(x, g, w, name):
    s = x.shape[0]
    n = w.shape[1]
    tm, tn = _tile(s, 512), _tile(n, 2816, LANES)

    def body(x_ref, g_ref, w_ref, z_ref, h_ref):
        @pl.when(pl.program_id(1) == 0)
        def _():
            xv = x_ref[...]
            h_ref[...] = (xv * _rms(xv) * g_ref[...]).astype(_MXU)
        z_ref[...] = jnp.dot(h_ref[...], w_ref[...], preferred_element_type=F32)

    return _pcall(
        body, name=name, grid=(s // tm, n // tn),
        in_specs=[pl.BlockSpec((tm, D_MODEL), lambda i, j: (i, 0)), pl.BlockSpec((1, D_MODEL), lambda i, j: (0, 0)),
                  pl.BlockSpec((D_MODEL, tn), lambda i, j: (0, j))],
        out_specs=[pl.BlockSpec((tm, tn), lambda i, j: (i, j)), pl.BlockSpec((tm, D_MODEL), lambda i, j: (i, 0))],
        out_shape=[jax.ShapeDtypeStruct((s, n), F32), jax.ShapeDtypeStruct((s, D_MODEL), _MXU)],
        compiler_params=_params(("parallel", "arbitrary")),
    )(x, g, w)


def _conv(cur, prev8, cw_ref):
    s1 = _shift_down(cur, prev8, 1)
    s2 = _shift_down(cur, prev8, 2)
    zc = cw_ref[3:4, :] + s2 * cw_ref[0:1, :]
    zc = zc + s1 * cw_ref[1:2, :]
    zc = zc + cur * cw_ref[2:3, :]
    return zc, s1, s2


def _halo_specs(tm, s):
    r = tm // SUBLANES
    prev = lambda off: pl.BlockSpec((SUBLANES, FFN_CHUNK), lambda i, j: (jnp.maximum(i * r - 1, 0), j + off))
    nxt = lambda off: pl.BlockSpec((SUBLANES, FFN_CHUNK), lambda i, j: (jnp.minimum((i + 1) * r, s // SUBLANES - 1), j + off))
    return prev, nxt


def _ffn_down_fwd(z, cw, wd, x, name):
    s = x.shape[0]
    tm = _tile(s, FFN_ROWS)
    prev, _ = _halo_specs(tm, s)

    def body(zg_ref, zu_ref, pg_ref, pu_ref, cg_ref, cu_ref, w_ref, x_ref, o_ref, act_ref, acc_ref):
        i, j = pl.program_id(0), pl.program_id(1)
        first = i > 0
        zg, _, _ = _conv(zg_ref[...], jnp.where(first, pg_ref[...], 0.0), cg_ref)
        zu, _, _ = _conv(zu_ref[...], jnp.where(first, pu_ref[...], 0.0), cu_ref)
        act = (zg * jax.nn.sigmoid(zg) * zu).astype(_MXU)
        act_ref[...] = act

        @pl.when(j == 0)
        def _():
            acc_ref[...] = x_ref[...]
        acc_ref[...] += jnp.dot(act, w_ref[...], preferred_element_type=F32)

        @pl.when(j == N_CHUNKS - 1)
        def _():
            o_ref[...] = acc_ref[...]

    zt = lambda off: pl.BlockSpec((tm, FFN_CHUNK), lambda i, j: (i, j + off))
    cwt = lambda off: pl.BlockSpec((CW_ROWS, FFN_CHUNK), lambda i, j: (0, j + off))
    return _pcall(
        body, name=name, grid=(s // tm, N_CHUNKS),
        in_specs=[zt(0), zt(N_CHUNKS), prev(0), prev(N_CHUNKS), cwt(0), cwt(N_CHUNKS),
                  pl.BlockSpec((FFN_CHUNK, D_MODEL), lambda i, j: (j, 0)), pl.BlockSpec((tm, D_MODEL), lambda i, j: (i, 0))],
        out_specs=[pl.BlockSpec((tm, D_MODEL), lambda i, j: (i, 0)), pl.BlockSpec((tm, FFN_CHUNK), lambda i, j: (i, j))],
        out_shape=[jax.ShapeDtypeStruct((s, D_MODEL), F32), jax.ShapeDtypeStruct((s, D_FF), _MXU)],
        scratch_shapes=[pltpu.VMEM((tm, D_MODEL), F32)],
        compiler_params=_params(("parallel", "arbitrary")),
    )(z, z, z, z, cw, cw, wd, x)


def _ffn_down_bwd(dx, z, cw, wdt, name):
    s = dx.shape[0]
    tm = _tile(s, FFN_ROWS)

    def body(dx_ref, zg_ref, zu_ref, pg_ref, pu_ref, cg_ref, cu_ref, wt_ref, dg_ref, du_ref, dcg_ref, dcu_ref):
        i = pl.program_id(1)
        first = i > 0

        @pl.when(i == 0)
        def _():
            dcg_ref[...] = jnp.zeros_like(dcg_ref)
            dcu_ref[...] = jnp.zeros_like(dcu_ref)

        dact = _dot(dx_ref[...], wt_ref[...])
        zg, g1, g2 = _conv(zg_ref[...], jnp.where(first, pg_ref[...], 0.0), cg_ref)
        zu, u1, u2 = _conv(zu_ref[...], jnp.where(first, pu_ref[...], 0.0), cu_ref)
        sg = jax.nn.sigmoid(zg)
        silu = zg * sg
        dzu = dact * silu
        dzg = dact * zu * (sg * (1.0 + zg * (1.0 - sg)))
        dg_ref[...] = dzg
        du_ref[...] = dzu
        for ref, dzc, cur, s1, s2 in ((dcg_ref, dzg, zg_ref[...], g1, g2), (dcu_ref, dzu, zu_ref[...], u1, u2)):
            ref[0:1, :] += jnp.sum(dzc * s2, axis=0, keepdims=True)
            ref[1:2, :] += jnp.sum(dzc * s1, axis=0, keepdims=True)
            ref[2:3, :] += jnp.sum(dzc * cur, axis=0, keepdims=True)
            ref[3:4, :] += jnp.sum(dzc, axis=0, keepdims=True)

    zt = lambda off: pl.BlockSpec((tm, FFN_CHUNK), lambda j, i: (i, j + off))
    r = tm // SUBLANES
    pv = lambda off: pl.BlockSpec((SUBLANES, FFN_CHUNK), lambda j, i: (jnp.maximum(i * r - 1, 0), j + off))
    cwt = lambda off: pl.BlockSpec((CW_ROWS, FFN_CHUNK), lambda j, i: (0, j + off))
    out_t = pl.BlockSpec((tm, FFN_CHUNK), lambda j, i: (i, j))
    dc_t = pl.BlockSpec((CW_ROWS, FFN_CHUNK), lambda j, i: (0, j))
    dzg, dzu, dcg, dcu = _pcall(
        body, name=name, grid=(N_CHUNKS, s // tm),
        in_specs=[pl.BlockSpec((tm, D_MODEL), lambda j, i: (i, 0)), zt(0), zt(N_CHUNKS), pv(0), pv(N_CHUNKS),
                  cwt(0), cwt(N_CHUNKS), pl.BlockSpec((D_MODEL, FFN_CHUNK), lambda j, i: (0, j))],
        out_specs=[out_t, out_t, dc_t, dc_t],
        out_shape=[jax.ShapeDtypeStruct((s, D_FF), F32), jax.ShapeDtypeStruct((s, D_FF), F32),
                   jax.ShapeDtypeStruct((CW_ROWS, D_FF), F32), jax.ShapeDtypeStruct((CW_ROWS, D_FF), F32)],
        compiler_params=_params(("parallel", "arbitrary")),
    )(dx, z, z, z, z, cw, cw, wdt)
    return dzg, dzu, jnp.concatenate([dcg, dcu], axis=1)


def _ffn_up_bwd(dzg, dzu, cw, wut, x, g, dres, name):
    s = x.shape[0]
    tm = _tile(s, FFN_ROWS)
    _, nxt = _halo_specs(tm, s)
    nt = s // tm

    def body(dg_ref, du_ref, ng_ref, nu_ref, cg_ref, cu_ref, wg_ref, wu_ref, x_ref, g_ref, dres_ref,
             dzg_ref, dzu_ref, dx_ref, dgn_ref, acc_ref):
        i, j = pl.program_id(0), pl.program_id(1)
        last = i < nt - 1

        def conv_bwd(cur, nxt8, cw_ref):
            up1 = _shift_up(cur, nxt8, 1)
            up2 = _shift_up(cur, nxt8, 2)
            return cur * cw_ref[2:3, :] + up1 * cw_ref[1:2, :] + up2 * cw_ref[0:1, :]

        dzg_ = conv_bwd(dg_ref[...], jnp.where(last, ng_ref[...], 0.0), cg_ref).astype(_MXU)
        dzu_ = conv_bwd(du_ref[...], jnp.where(last, nu_ref[...], 0.0), cu_ref).astype(_MXU)
        dzg_ref[...] = dzg_
        dzu_ref[...] = dzu_

        @pl.when(j == 0)
        def _():
            acc_ref[...] = jnp.zeros_like(acc_ref)
        acc_ref[...] += (jnp.dot(dzg_, wg_ref[...], preferred_element_type=F32)
                         + jnp.dot(dzu_, wu_ref[...], preferred_element_type=F32))

        @pl.when((i == 0) & (j == 0))
        def _():
            dgn_ref[...] = jnp.zeros_like(dgn_ref)

        @pl.when(j == N_CHUNKS - 1)
        def _():
            dx, dgn = _rms_bwd(x_ref[...], g_ref[...], acc_ref[...])
            dx_ref[...] = dres_ref[...] + dx
            dgn_ref[...] += dgn

    zt = pl.BlockSpec((tm, FFN_CHUNK), lambda i, j: (i, j))
    cwt = lambda off: pl.BlockSpec((CW_ROWS, FFN_CHUNK), lambda i, j: (0, j + off))
    wt = lambda off: pl.BlockSpec((FFN_CHUNK, D_MODEL), lambda i, j: (j + off, 0))
    row = pl.BlockSpec((tm, D_MODEL), lambda i, j: (i, 0))
    vec = pl.BlockSpec((1, D_MODEL), lambda i, j: (0, 0))
    return _pcall(
        body, name=name, grid=(nt, N_CHUNKS),
        in_specs=[zt, zt, nxt(0), nxt(0), cwt(0), cwt(N_CHUNKS), wt(0), wt(N_CHUNKS), row, vec, row],
        out_specs=[zt, zt, row, vec],
        out_shape=[jax.ShapeDtypeStruct((s, D_FF), _MXU), jax.ShapeDtypeStruct((s, D_FF), _MXU),
                   jax.ShapeDtypeStruct((s, D_MODEL), F32), jax.ShapeDtypeStruct((1, D_MODEL), F32)],
        scratch_shapes=[pltpu.VMEM((tm, D_MODEL), F32)],
        compiler_params=_params(("arbitrary", "arbitrary")),
    )(dzg, dzu, dzg, dzu, cw, cw, wut, wut, x, g, dres)


def _final_loss(x, g, tgt, name):
    s = x.shape[0]
    tm = _tile(s, 256)

    def body(x_ref, g_ref, t_ref, loss_ref, dx_ref, dg_ref):
        @pl.when(pl.program_id(0) == 0)
        def _():
            loss_ref[...] = jnp.zeros_like(loss_ref)
            dg_ref[...] = jnp.zeros_like(dg_ref)

        xv, gv = x_ref[...], g_ref[...]
        err = xv * _rms(xv) * gv - t_ref[...]
        per_tok = jnp.mean(err * err, axis=-1, keepdims=True)
        loss_ref[...] += 0.5 * jnp.sum(per_tok, axis=0, keepdims=True)
        dx, dg = _rms_bwd(xv, gv, err * (1.0 / D_MODEL))
        dx_ref[...] = dx
        dg_ref[...] += dg

    row = pl.BlockSpec((tm, D_MODEL), lambda i: (i, 0))
    vec = pl.BlockSpec((1, D_MODEL), lambda i: (0, 0))
    return _pcall(
        body, name=name, grid=(s // tm,),
        in_specs=[row, vec, row], out_specs=[pl.BlockSpec((1, 1), lambda i: (0, 0)), row, vec],
        out_shape=[jax.ShapeDtypeStruct((1, 1), F32), jax.ShapeDtypeStruct((s, D_MODEL), F32),
                   jax.ShapeDtypeStruct((1, D_MODEL), F32)],
        compiler_params=_params(("arbitrary",)),
    )(x, g, tgt)


def _adamw(parts, w, m, v, name):
    r, c = w.shape
    tr = _tile(r, 256)
    c1 = 1.0 - ADAM_B1 ** ADAM_STEP
    c2 = 1.0 - ADAM_B2 ** ADAM_STEP

    def body(p_ref, w_ref, m_ref, v_ref, g_ref, d_ref, mo_ref, vo_ref):
        g = p_ref[0].astype(F32)
        for i in range(1, N_DEV):
            g = g + p_ref[i].astype(F32)
        mn = ADAM_B1 * m_ref[...] + (1.0 - ADAM_B1) * g
        vn = ADAM_B2 * v_ref[...] + (1.0 - ADAM_B2) * (g * g)
        g_ref[...] = g
        mo_ref[...] = mn
        vo_ref[...] = vn
        d_ref[...] = -ADAM_LR * ((mn / c1) / (jnp.sqrt(vn / c2) + ADAM_EPS) + ADAM_WD * w_ref[...])

    t2 = pl.BlockSpec((tr, c), lambda i: (i, 0))
    return _pcall(
        body, name=name, grid=(r // tr,),
        in_specs=[pl.BlockSpec((N_DEV, tr, c), lambda i: (0, i, 0)), t2, t2, t2],
        out_specs=[t2] * 4, out_shape=[jax.ShapeDtypeStruct((r, c), F32)] * 4,
        compiler_params=_params(("parallel",)),
    )(parts, w, m, v)


SMALL = ("norm1_g", "sgu_norm_g", "sgu_w", "sgu_b", "pool_w", "pool_scale", "mix_norm_g", "norm2_g", "conv_b", "final_g")
SHARDED = ("w_in", "w_o", "w_up", "conv_w", "w_down")
ORDER = ("norm1_g", "w_in", "sgu_norm_g", "sgu_w", "sgu_b", "pool_w", "pool_scale", "mix_norm_g", "w_o", "norm2_g",
         "w_up", "conv_w", "conv_b", "w_down", "final_g")


def _pack(tree):
    return jnp.concatenate([tree[n].reshape(-1) for n in SMALL]).reshape(-1, LANES)


def _unpack(flat, like):
    out, off = {}, 0
    flat = flat.reshape(-1)
    for n in SMALL:
        size = math.prod(like[n].shape)
        out[n] = flat[off:off + size].reshape(like[n].shape)
        off += size
    return out


def _block_diag(pw):
    z = jnp.zeros((W_C, W_C), pw.dtype)
    for g in range(4):
        z = z.at[g * 64:(g + 1) * 64, g * 64:(g + 1) * 64].set(pw[g])
    return z


def kernel(x, norm1_g, w_in, sgu_norm_g, sgu_w, sgu_b, pool_w, pool_scale, mix_norm_g, w_o, norm2_g, w_up, conv_w, conv_b, w_down, final_g, loss_target, m_norm1_g, m_w_in, m_sgu_norm_g, m_sgu_w, m_sgu_b, m_pool_w, m_pool_scale, m_mix_norm_g, m_w_o, m_norm2_g, m_w_up, m_conv_w, m_conv_b, m_w_down, m_final_g, v_norm1_g, v_w_in, v_sgu_norm_g, v_sgu_w, v_sgu_b, v_pool_w, v_pool_scale, v_mix_norm_g, v_w_o, v_norm2_g, v_w_up, v_conv_w, v_conv_b, v_w_down, v_final_g):
    weights = dict(norm1_g=norm1_g, w_in=w_in, sgu_norm_g=sgu_norm_g, sgu_w=sgu_w, sgu_b=sgu_b, pool_w=pool_w,
                   pool_scale=pool_scale, mix_norm_g=mix_norm_g, w_o=w_o, norm2_g=norm2_g, w_up=w_up, conv_w=conv_w,
                   conv_b=conv_b, w_down=w_down, final_g=final_g)
    mom = dict(norm1_g=m_norm1_g, w_in=m_w_in, sgu_norm_g=m_sgu_norm_g, sgu_w=m_sgu_w, sgu_b=m_sgu_b, pool_w=m_pool_w,
               pool_scale=m_pool_scale, mix_norm_g=m_mix_norm_g, w_o=m_w_o, norm2_g=m_norm2_g, w_up=m_w_up,
               conv_w=m_conv_w, conv_b=m_conv_b, w_down=m_w_down, final_g=m_final_g)
    var = dict(norm1_g=v_norm1_g, w_in=v_w_in, sgu_norm_g=v_sgu_norm_g, sgu_w=v_sgu_w, sgu_b=v_sgu_b, pool_w=v_pool_w,
               pool_scale=v_pool_scale, mix_norm_g=v_mix_norm_g, w_o=v_w_o, norm2_g=v_norm2_g, w_up=v_w_up,
               conv_w=v_conv_w, conv_b=v_conv_b, w_down=v_w_down, final_g=v_final_g)
    depth = w_in.shape[0]
    s = x.shape[1]
    xs = x.reshape(s, D_MODEL)
    tgt = loss_target.reshape(s, D_MODEL)

    assert depth >= 2
    (g_in0,) = _exchange([w_in[0].astype(_MXU)], "gather_w_in0", False)
    w_in0 = jnp.transpose(g_in0, (1, 0, 2)).reshape(D_MODEL, IN_COLS)
    gather_rest = _Exchange([w_in[1:].astype(_MXU), w_o.astype(_MXU), w_up.astype(_MXU), conv_w, w_down.astype(_MXU)], False)

    tril = jnp.tril(jnp.ones((CHUNK, CHUNK), bool))
    layers = []
    for l in range(depth):
        wbd = _block_diag(pool_w[l])
        layers.append(dict(
            g1=norm1_g[l][None], gn=sgu_norm_g[l][None], wm=jnp.where(tril[None], sgu_w[l], 0.0).astype(_MXU),
            bias=jnp.repeat(sgu_b[l].T, HEAD_DIM, axis=1),
            wbd=wbd.astype(_MXU), wbd_t=wbd.T.astype(_MXU), sc=pool_scale[l][None],
            gmix=mix_norm_g[l][None], g2=norm2_g[l][None]))
    layers[0].update(w_in=w_in0, w_in_t=w_in0.T)

    def place_gathered(g_in, g_o, g_up, g_cw, g_dn):
        full_in = jnp.transpose(g_in, (1, 2, 0, 3)).reshape(depth - 1, D_MODEL, IN_COLS)
        full_o = jnp.transpose(g_o, (1, 0, 2, 3)).reshape(depth, D_MODEL, D_MODEL)
        full_up = jnp.transpose(g_up, (1, 2, 0, 3)).reshape(depth, D_MODEL, 2 * D_FF)
        full_cw = jnp.transpose(g_cw, (1, 2, 0, 3)).reshape(depth, 3, 2 * D_FF)
        full_dn = jnp.transpose(g_dn, (1, 0, 2, 3)).reshape(depth, D_FF, D_MODEL)
        for l in range(depth):
            if l > 0:
                layers[l].update(w_in=full_in[l - 1], w_in_t=full_in[l - 1].T)
            layers[l].update(
                w_o=full_o[l], w_o_t=full_o[l].T, w_up=full_up[l], w_up_t=full_up[l].T,
                cw=jnp.concatenate([full_cw[l], conv_b[l][None], jnp.zeros((CW_ROWS - 4, 2 * D_FF), F32)], axis=0),
                w_dn=full_dn[l], w_dn_t=full_dn[l].T)

    saved = []
    cur = xs
    for l, p in enumerate(layers):
        a_in, qkv, p_in, h1 = _inproj_fwd(cur, p["g1"], p["w_in"], f"inproj_fwd{l}")
        y_a = _sgu_fwd(a_in, p["gn"], p["wm"], p["bias"], f"sgu_fwd{l}")
        y_b, cm, gathered_w = _attn_fwd(qkv, f"attn_fwd{l}", gather_rest if l == 0 else None)
        if l == 0:
            place_gathered(*gathered_w)
        y_c = _pool_fwd(p_in, p["wbd"], p["sc"], f"pool_fwd{l}")
        x_mid, yn = _mix_fwd(y_a, y_b, y_c, p["gmix"], p["w_o"], cur, f"mix_fwd{l}")
        z, h2 = _ffn_up_fwd(x_mid, p["g2"], p["w_up"], f"ffn_up_fwd{l}")
        x_out, act = _ffn_down_fwd(z, p["cw"], p["w_dn"], x_mid, f"ffn_down_fwd{l}")
        saved.append(dict(x_in=cur, a_in=a_in, qkv=qkv, p_in=p_in, h1=h1, y_a=y_a, y_b=y_b, cm=cm, y_c=y_c, x_mid=x_mid,
                          yn=yn, z=z, h2=h2, act=act))
        cur = x_out
    loss_part, dx, dg_final = _final_loss(cur, final_g[None], tgt, "final_loss")

    small = {n: [None] * depth for n in SMALL if n != "final_g"}
    big = {n: [None] * depth for n in SHARDED}
    early = [(n, l) for n in SHARDED for l in range(depth) if (n, l) != ("w_in", 0)]
    wire = lambda n, t: t if n == "conv_w" else t.astype(GRAD_WIRE)
    for l in reversed(range(depth)):
        p, sv = layers[l], saved[l]
        dzg, dzu, dcw = _ffn_down_bwd(dx, sv["z"], p["cw"], p["w_dn_t"], f"ffn_down_bwd{l}")
        big["w_down"][l] = _mm_tn(sv["act"], dx, f"dw_down{l}").reshape(N_DEV, D_FF // N_DEV, D_MODEL)
        dzg_b, dzu_b, dx_mid, dg2 = _ffn_up_bwd(dzg, dzu, p["cw"], p["w_up_t"], sv["x_mid"], p["g2"], dx, f"ffn_up_bwd{l}")
        dw_up = jnp.stack([_mm_tn(sv["h2"], dzg_b, f"dw_up_g{l}"), _mm_tn(sv["h2"], dzu_b, f"dw_up_u{l}")])
        big["w_up"][l] = jnp.transpose(dw_up.reshape(2, D_MODEL, N_DEV // 2, 2 * D_FF // N_DEV), (0, 2, 1, 3)).reshape(
            N_DEV, D_MODEL, 2 * D_FF // N_DEV)
        big["conv_w"][l] = jnp.transpose(dcw[0:3].reshape(3, N_DEV, 2 * D_FF // N_DEV), (1, 0, 2))
        small["conv_b"][l] = dcw[3]
        small["norm2_g"][l] = dg2[0]
        dya, dyb, dyc, dgmix = _mix_bwd(dx_mid, sv["y_a"], sv["y_b"], sv["y_c"], p["gmix"], p["w_o_t"], f"mix_bwd{l}")
        small["mix_norm_g"][l] = dgmix[0]
        big["w_o"][l] = _mm_tn(sv["yn"], dx_mid, f"dw_o{l}").reshape(N_DEV, D_MODEL // N_DEV, D_MODEL)
        dp, dwbd, dsc = _pool_bwd(sv["p_in"], dyc, p["wbd"], p["wbd_t"], p["sc"], f"pool_bwd{l}")
        small["pool_w"][l] = jnp.stack([dwbd[g * 64:(g + 1) * 64, g * 64:(g + 1) * 64] for g in range(4)])
        small["pool_scale"][l] = dsc[0]
        scatter_early = _Exchange([wire(n, big[n][ll]) for n, ll in early], True) if l == 0 else None
        dq, dk, dv, recv_early = _attn_bwd(sv["qkv"], sv["cm"], dyb, f"attn_bwd{l}", scatter_early)
        if l == 0:
            recv_early_all = recv_early
        da, dwm, dbias, dgn = _sgu_bwd(sv["a_in"], dya, p["gn"], p["wm"], p["bias"], f"sgu_bwd{l}")
        small["sgu_w"][l] = dwm
        small["sgu_b"][l] = jnp.sum(dbias.reshape(CHUNK, 4, HEAD_DIM), axis=-1).T
        small["sgu_norm_g"][l] = dgn[0]
        dx, dg1 = _inproj_bwd(da, dq, dk, dv, dp, p["w_in_t"], sv["x_in"], p["g1"], dx_mid, f"inproj_bwd{l}")
        small["norm1_g"][l] = dg1[0]
        pieces = (da, dq, dk, dv, dp)
        dw_in = jnp.concatenate([_mm_tn(sv["h1"], t, f"dw_in{i}_{l}") for i, t in enumerate(pieces)], axis=1)
        big["w_in"][l] = jnp.transpose(dw_in.reshape(D_MODEL, N_DEV, IN_COLS // N_DEV), (1, 0, 2))

    part = {n: jnp.stack(small[n]) for n in small}
    part["final_g"] = dg_final[0]
    packed = _pack(part)
    recv_in0, gathered = _exchange([wire("w_in", big["w_in"][0]), jnp.broadcast_to(packed, (N_DEV,) + packed.shape)],
                                   "scatter_last", True)
    recv = dict(zip(early, recv_early_all))
    recv[("w_in", 0)] = recv_in0

    out_g, out_d, out_m, out_v = {}, {}, {}, {}
    for n in SHARDED:
        res = [_adamw(recv[(n, l)], weights[n][l], mom[n][l], var[n][l], f"adamw_{n}{l}") for l in range(depth)]
        out_g[n], out_d[n], out_m[n], out_v[n] = (jnp.stack([r[i] for r in res]) for i in range(4))

    sg, sd, sm, sv_ = _adamw(gathered, _pack(weights), _pack(mom), _pack(var), "adamw_small")
    for tree, flat in ((out_g, sg), (out_d, sd), (out_m, sm), (out_v, sv_)):
        tree.update(_unpack(flat, weights))

    loss = lax.psum(loss_part[0, 0], ("x", "y", "c"))
    grad_x = dx.reshape(1, s, D_MODEL)
    return (loss, grad_x, *[out_g[n] for n in ORDER], *[out_d[n] for n in ORDER], *[out_m[n] for n in ORDER],
            *[out_v[n] for n in ORDER])
```

```python
import functools
import math

import jax
import jax.numpy as jnp
import numpy as np
from jax import lax
from jax.experimental import pallas as pl
from jax.experimental.pallas import tpu as pltpu

F32 = jnp.float32
BF16 = jnp.bfloat16
_MXU = jnp.bfloat16
GRAD_WIRE = jnp.bfloat16

D_MODEL = 1024
W_A = 256
W_B = 512
W_C = 256
HEAD_DIM = 64
IN_COLS = 2 * W_A + 3 * W_B + W_C
D_FF = 2816
CHUNK = 128
POOL_WINDOWS = (2, 4, 8, 16)
EPS = 1e-6
N_DEV = 8
LANES = 128
SUBLANES = 8
VMEM_LIMIT = 48 * 1024 * 1024

ADAM_LR = 0.001
ADAM_B1 = 0.9
ADAM_B2 = 0.999
ADAM_EPS = 1e-08
ADAM_WD = 0.01
ADAM_STEP = 10

INV_SQRT2 = 1.0 / math.sqrt(2.0)
INV_SQRT_2PI = 1.0 / math.sqrt(2.0 * math.pi)


def _pcall(body, **kw):
    return pl.pallas_call(body, **kw)


def _params(dims=None):
    return pltpu.CompilerParams(dimension_semantics=dims, vmem_limit_bytes=VMEM_LIMIT)


def _tile(n, pref, mult=SUBLANES):
    t = min(n, pref) // mult * mult
    while t >= mult:
        if n % t == 0:
            return t
        t -= mult
    return n


def _iota(shape, dim):
    return lax.broadcasted_iota(jnp.int32, shape, dim)


def _dot(a, b):
    return jnp.dot(a.astype(_MXU), b.astype(_MXU), preferred_element_type=F32)


def _dot_nt(a, b):
    return lax.dot_general(a.astype(_MXU), b.astype(_MXU), (((1,), (1,)), ((), ())), preferred_element_type=F32)


def _dot_tn(a, b):
    return lax.dot_general(a.astype(_MXU), b.astype(_MXU), (((0,), (0,)), ((), ())), preferred_element_type=F32)


def _split(x):
    hi = x.astype(BF16)
    lo = (x - hi.astype(F32)).astype(BF16)
    return hi, lo


def _dot_sel(x, sel):
    hi, lo = _split(x)
    return jnp.dot(hi, sel, preferred_element_type=F32) + jnp.dot(lo, sel, preferred_element_type=F32)


def _sel_dot(sel, x):
    hi, lo = _split(x)
    return jnp.dot(sel, hi, preferred_element_type=F32) + jnp.dot(sel, lo, preferred_element_type=F32)


def _group_mat(n):
    r = jnp.right_shift(_iota((n, n), 0), 6)
    c = jnp.right_shift(_iota((n, n), 1), 6)
    return (r == c).astype(BF16)


def _lane_group_mask(n, g):
    lane = _iota((1, n), 1)
    return (lane >= g * HEAD_DIM) & (lane < (g + 1) * HEAD_DIM)


def _gelu(a):
    return 0.5 * a * (1.0 + lax.erf(a * INV_SQRT2))


def _gelu_grad(a):
    return 0.5 * (1.0 + lax.erf(a * INV_SQRT2)) + a * jnp.exp(-0.5 * a * a) * INV_SQRT_2PI


def _rms(x):
    return lax.rsqrt(jnp.mean(x * x, axis=-1, keepdims=True) + EPS)


def _rms_bwd(x, g, dy):
    r = _rms(x)
    dyg = dy * g
    m2 = jnp.mean(dyg * x, axis=-1, keepdims=True)
    dx = r * dyg - x * (r * r * r) * m2
    dg = jnp.sum(dy * x * r, axis=0, keepdims=True)
    return dx, dg


def _shift_down(cur, prev8, k):
    rolled = pltpu.roll(cur, k, 0)
    row8 = _iota(prev8.shape, 0)
    top = jnp.where(row8 < k, pltpu.roll(prev8, k, 0), rolled[0:SUBLANES])
    return jnp.concatenate([top, rolled[SUBLANES:]], axis=0)


def _shift_up(cur, next8, k):
    n = cur.shape[0]
    rolled = pltpu.roll(cur, n - k, 0)
    row8 = _iota(next8.shape, 0)
    bot = jnp.where(row8 >= SUBLANES - k, pltpu.roll(next8, SUBLANES - k, 0), rolled[n - SUBLANES:])
    return jnp.concatenate([rolled[:n - SUBLANES], bot], axis=0)


def _mesh_pos():
    return lax.axis_index("x"), lax.axis_index("y"), lax.axis_index("c")


def _peer(x, y, c, k):
    px = 1 - x if k & 4 else x
    py = 1 - y if k & 2 else y
    pc = 1 - c if k & 1 else c
    return px, py, pc


class _Exchange:
    def __init__(self, arrs, scatter):
        self.arrs, self.scatter, self.n = list(arrs), scatter, len(arrs)
        self.out_shapes = [jax.ShapeDtypeStruct(a.shape if scatter else (N_DEV,) + a.shape, a.dtype) for a in arrs]
        self.specs = [pl.BlockSpec(memory_space=pl.ANY)] * self.n
        self.semaphores = [pltpu.SemaphoreType.DMA((self.n * (N_DEV - 1),)),
                           pltpu.SemaphoreType.DMA((self.n * (N_DEV - 1),)), pltpu.SemaphoreType.DMA((self.n,))]

    def _copies(self, ins, outs, sems):
        send, recv, loc = sems
        x, y, c = _mesh_pos()
        me = 4 * x + 2 * y + c
        src = (lambda a, idx: ins[a].at[idx]) if self.scatter else (lambda a, idx: ins[a])
        starts = [pltpu.make_async_copy(src(a, me), outs[a].at[me], loc.at[a]) for a in range(self.n)]
        waits = list(starts)
        for k in range(1, N_DEV):
            px, py, pc = _peer(x, y, c, k)
            pidx = 4 * px + 2 * py + pc
            for a in range(self.n):
                s = a * (N_DEV - 1) + k - 1
                common = dict(src_ref=src(a, pidx), send_sem=send.at[s], recv_sem=recv.at[s],
                              device_id=(px, py, pc), device_id_type=pl.DeviceIdType.MESH)
                starts.append(pltpu.make_async_remote_copy(dst_ref=outs[a].at[me], **common))
                waits.append(pltpu.make_async_remote_copy(dst_ref=outs[a].at[pidx], **common))
        return starts, waits

    def start(self, ins, outs, sems):
        for cp in self._copies(ins, outs, sems)[0]:
            cp.start()

    def wait(self, ins, outs, sems):
        for cp in self._copies(ins, outs, sems)[1]:
            cp.wait()


def _with_exchange(compute, ride, n_in, n_out, n_scratch, last_step):
    if ride is None:
        return compute
    nx = ride.n

    def body(*refs):
        ins, ride_in = refs[:n_in], refs[n_in:n_in + nx]
        outs = refs[n_in + nx:n_in + nx + n_out]
        ride_out = refs[n_in + nx + n_out:n_in + 2 * nx + n_out]
        scratch = refs[n_in + 2 * nx + n_out:n_in + 2 * nx + n_out + n_scratch]
        sems = refs[n_in + 2 * nx + n_out + n_scratch:]
        step = (pl.program_id(0), pl.program_id(1))

        @pl.when((step[0] == 0) & (step[1] == 0))
        def _():
            ride.start(ride_in, ride_out, sems)

        compute(*ins, *outs, *scratch)

        @pl.when((step[0] == last_step[0]) & (step[1] == last_step[1]))
        def _():
            ride.wait(ride_in, ride_out, sems)

    return body


def _exchange(arrs, name, scatter):
    ex = _Exchange(arrs, scatter)
    n = ex.n

    def body(*refs):
        ins, outs, sems = refs[:n], refs[n:2 * n], refs[2 * n:]
        ex.start(ins, outs, sems)
        ex.wait(ins, outs, sems)

    return _pcall(body, name=name, out_shape=ex.out_shapes, in_specs=ex.specs, out_specs=ex.specs,
                  scratch_shapes=ex.semaphores, compiler_params=pltpu.CompilerParams(has_side_effects=True))(*arrs)


def _mm_tn(a, b, name):
    s, m = a.shape
    n = b.shape[1]
    tm, tn, tk = _tile(m, 1408, LANES), _tile(n, 1408, LANES), _tile(s, 1024)

    def body(a_ref, b_ref, o_ref):
        @pl.when(pl.program_id(2) == 0)
        def _():
            o_ref[...] = jnp.zeros_like(o_ref)
        o_ref[...] += _dot_tn(a_ref[...], b_ref[...])

    return _pcall(
        body, name=name, grid=(m // tm, n // tn, s // tk),
        in_specs=[pl.BlockSpec((tk, tm), lambda i, j, k: (k, i)), pl.BlockSpec((tk, tn), lambda i, j, k: (k, j))],
        out_specs=pl.BlockSpec((tm, tn), lambda i, j, k: (i, j)),
        out_shape=jax.ShapeDtypeStruct((m, n), F32),
        compiler_params=_params(("parallel", "parallel", "arbitrary")),
    )(a, b)


def _inproj_fwd(x, g, w, name):
    s = x.shape[0]
    tm = _tile(s, 256)

    def body(x_ref, g_ref, w_ref, a_ref, qkv_ref, p_ref, h_ref):
        xv = x_ref[...]
        h = (xv * _rms(xv) * g_ref[...]).astype(_MXU)
        h_ref[...] = h
        a_ref[...] = jnp.dot(h, w_ref[:, 0:2 * W_A], preferred_element_type=F32)
        q = jnp.dot(h, w_ref[:, 2 * W_A:2 * W_A + W_B], preferred_element_type=F32)
        qkv_ref[:, 0:W_B] = (q * 0.125).astype(_MXU)
        kv = jnp.dot(h, w_ref[:, 2 * W_A + W_B:2 * W_A + 3 * W_B], preferred_element_type=F32)
        qkv_ref[:, W_B:3 * W_B] = kv.astype(_MXU)
        p_ref[...] = jnp.dot(h, w_ref[:, 2 * W_A + 3 * W_B:IN_COLS], preferred_element_type=F32)

    row = lambda n: pl.BlockSpec((tm, n), lambda i: (i, 0))
    return _pcall(
        body, name=name, grid=(s // tm,),
        in_specs=[row(D_MODEL), pl.BlockSpec((1, D_MODEL), lambda i: (0, 0)),
                  pl.BlockSpec((D_MODEL, IN_COLS), lambda i: (0, 0))],
        out_specs=[row(2 * W_A), row(3 * W_B), row(W_C), row(D_MODEL)],
        out_shape=[jax.ShapeDtypeStruct((s, 2 * W_A), F32), jax.ShapeDtypeStruct((s, 3 * W_B), _MXU),
                   jax.ShapeDtypeStruct((s, W_C), F32), jax.ShapeDtypeStruct((s, D_MODEL), _MXU)],
        compiler_params=_params(("parallel",)),
    )(x, g, w)


def _inproj_bwd(da, dq, dk, dv, dp, wt, x, g, dres, name):
    s = x.shape[0]
    tm = _tile(s, 256)

    def body(da_ref, dq_ref, dk_ref, dv_ref, dp_ref, wt_ref, x_ref, g_ref, dres_ref, dx_ref, dg_ref):
        dh = _dot(da_ref[...], wt_ref[0:2 * W_A, :])
        dh += _dot(dq_ref[...], wt_ref[2 * W_A:2 * W_A + W_B, :])
        dh += _dot(dk_ref[...], wt_ref[2 * W_A + W_B:2 * W_A + 2 * W_B, :])
        dh += _dot(dv_ref[...], wt_ref[2 * W_A + 2 * W_B:2 * W_A + 3 * W_B, :])
        dh += _dot(dp_ref[...], wt_ref[2 * W_A + 3 * W_B:IN_COLS, :])
        dx, dg = _rms_bwd(x_ref[...], g_ref[...], dh)
        dx_ref[...] = dres_ref[...] + dx

        @pl.when(pl.program_id(0) == 0)
        def _():
            dg_ref[...] = jnp.zeros_like(dg_ref)
        dg_ref[...] += dg

    row = lambda n: pl.BlockSpec((tm, n), lambda i: (i, 0))
    vec = pl.BlockSpec((1, D_MODEL), lambda i: (0, 0))
    return _pcall(
        body, name=name, grid=(s // tm,),
        in_specs=[row(2 * W_A), row(W_B), row(W_B), row(W_B), row(W_C),
                  pl.BlockSpec((IN_COLS, D_MODEL), lambda i: (0, 0)), row(D_MODEL), vec, row(D_MODEL)],
        out_specs=[row(D_MODEL), vec],
        out_shape=[jax.ShapeDtypeStruct((s, D_MODEL), F32), jax.ShapeDtypeStruct((1, D_MODEL), F32)],
        compiler_params=_params(("arbitrary",)),
    )(da, dq, dk, dv, dp, wt, x, g, dres)


def _sgu_core(a, gn, wm_ref, bias):
    ga = _gelu(a)
    u, v0 = ga[:, 0:W_A], ga[:, W_A:2 * W_A]
    r = lax.rsqrt(_dot_sel(v0 * v0, _group_mat(W_A)) * (1.0 / HEAD_DIM) + EPS)
    vn = v0 * r * gn
    sv = bias
    for h in range(W_A // HEAD_DIM):
        sv = sv + _dot(wm_ref[h], jnp.where(_lane_group_mask(W_A, h), vn, 0.0))
    return u, v0, r, vn, sv


def _sgu_fwd(a, gn, wm, bias, name):
    s = a.shape[0]

    def body(a_ref, gn_ref, wm_ref, b_ref, y_ref):
        u, _, _, _, sv = _sgu_core(a_ref[...], gn_ref[...], wm_ref, b_ref[...])
        y_ref[...] = u * sv

    return _pcall(
        body, name=name, grid=(s // CHUNK,),
        in_specs=[pl.BlockSpec((CHUNK, 2 * W_A), lambda i: (i, 0)), pl.BlockSpec((1, W_A), lambda i: (0, 0)),
                  pl.BlockSpec((4, CHUNK, CHUNK), lambda i: (0, 0, 0)), pl.BlockSpec((CHUNK, W_A), lambda i: (0, 0))],
        out_specs=pl.BlockSpec((CHUNK, W_A), lambda i: (i, 0)),
        out_shape=jax.ShapeDtypeStruct((s, W_A), F32),
        compiler_params=_params(("parallel",)),
    )(a, gn, wm, bias)


def _sgu_bwd(a, dy, gn, wm, bias, name):
    s = a.shape[0]

    def body(a_ref, dy_ref, gn_ref, wm_ref, b_ref, da_ref, dwm_ref, db_ref, dgn_ref):
        @pl.when(pl.program_id(0) == 0)
        def _():
            dwm_ref[...] = jnp.zeros_like(dwm_ref)
            db_ref[...] = jnp.zeros_like(db_ref)
            dgn_ref[...] = jnp.zeros_like(dgn_ref)

        av, gnv, dyv = a_ref[...], gn_ref[...], dy_ref[...]
        u, v0, r, vn, sv = _sgu_core(av, gnv, wm_ref, b_ref[...])
        du = dyv * sv
        ds = dyv * u
        db_ref[...] += ds
        tril = _iota((CHUNK, CHUNK), 1) <= _iota((CHUNK, CHUNK), 0)
        dvn = jnp.zeros_like(vn)
        for h in range(W_A // HEAD_DIM):
            dsm = jnp.where(_lane_group_mask(W_A, h), ds, 0.0)
            dwm_ref[h] += jnp.where(tril, _dot_nt(dsm, vn), 0.0)
            dvn = dvn + _dot_tn(wm_ref[h], dsm)
        dgn_ref[...] += jnp.sum(dvn * v0 * r, axis=0, keepdims=True)
        dvg = dvn * gnv
        m2 = _dot_sel(dvg * v0, _group_mat(W_A)) * (1.0 / HEAD_DIM)
        dv0 = r * dvg - v0 * (r * r * r) * m2
        gp = _gelu_grad(av)
        da_ref[:, 0:W_A] = du * gp[:, 0:W_A]
        da_ref[:, W_A:2 * W_A] = dv0 * gp[:, W_A:2 * W_A]

    return _pcall(
        body, name=name, grid=(s // CHUNK,),
        in_specs=[pl.BlockSpec((CHUNK, 2 * W_A), lambda i: (i, 0)), pl.BlockSpec((CHUNK, W_A), lambda i: (i, 0)),
                  pl.BlockSpec((1, W_A), lambda i: (0, 0)), pl.BlockSpec((4, CHUNK, CHUNK), lambda i: (0, 0, 0)),
                  pl.BlockSpec((CHUNK, W_A), lambda i: (0, 0))],
        out_specs=[pl.BlockSpec((CHUNK, 2 * W_A), lambda i: (i, 0)), pl.BlockSpec((4, CHUNK, CHUNK), lambda i: (0, 0, 0)),
                   pl.BlockSpec((CHUNK, W_A), lambda i: (0, 0)), pl.BlockSpec((1, W_A), lambda i: (0, 0))],
        out_shape=[jax.ShapeDtypeStruct((s, 2 * W_A), F32), jax.ShapeDtypeStruct((4, CHUNK, CHUNK), F32),
                   jax.ShapeDtypeStruct((CHUNK, W_A), F32), jax.ShapeDtypeStruct((1, W_A), F32)],
        compiler_params=_params(("arbitrary",)),
    )(a, dy, gn, wm, bias)


KBLK = 128
ROW_CHUNK = 64
FWD_UNROLL = 2
BWD_UNROLL = 4
MASKED_SCORE = -1e30


def _attn_fwd(qkv, name, ride=None):
    s = qkv.shape[0]
    tq = _tile(s, 256, KBLK)
    npairs = W_B // LANES
    assert s // KBLK <= LANES
    rides = [] if ride is None else [ride]

    def body(q_ref, k_ref, v_ref, o_ref, cm_ref, z_ref, zw_ref, sums_ref, carry_ref, rs_ref, lb_ref, a_ref):
        i = pl.program_id(1)
        q = q_ref[...]
        lane = _iota((1, LANES), 1)
        lane_lo = lane < HEAD_DIM
        hmask = (lane_lo, jnp.logical_not(lane_lo))
        tri = (_iota((KBLK, KBLK), 0) >= _iota((KBLK, KBLK), 1)).astype(BF16)
        dmat = _iota((tq, KBLK), 1) - (_iota((tq, KBLK), 0) + i * tq)
        chunks = [slice(r, r + ROW_CHUNK) for r in range(0, tq, ROW_CHUNK)]
        heads = [slice(hh * LANES, (hh + 1) * LANES) for hh in range(2)]
        nk = (i + 1) * (tq // KBLK)

        cm_ref[...] = jnp.zeros_like(cm_ref)

        def before(b):
            return jnp.where((b >= 0) & (b < nk), -b * KBLK, jnp.iinfo(jnp.int32).min)

        def per_head(block):
            return jnp.concatenate([jnp.where(m, block, jnp.zeros_like(block)) for m in hmask], axis=0)

        def scores(b, p):
            ks = pl.multiple_of(jnp.clip(b, 0, nk - 1) * KBLK, KBLK)
            z_ref[p] = _dot_nt(q, per_head(k_ref[pl.ds(ks, KBLK), :]))

        def logs(b, p):
            t = before(b)
            for hh in range(2):
                for rows in chunks:
                    z = jnp.where(dmat[rows] < t, z_ref[p, rows, heads[hh]], MASKED_SCORE)
                    zw_ref[p, rows, heads[hh]] = z
                    l = -(jnp.maximum(z, 0.0) + jnp.log(1.0 + jnp.exp(-jnp.abs(z))))
                    lb_ref[p, hh, rows, :] = l.astype(BF16)
                    rs_ref[p, hh, rows, :] = jnp.zeros((ROW_CHUNK, KBLK), F32) + jnp.sum(l, axis=1, keepdims=True)

        def sums(b, p):
            for hh in range(2):
                sums_ref[p, hh] = jnp.dot(lb_ref[p, hh], tri, preferred_element_type=F32)

        def weights(b, p):
            pick = lane == b
            for hh in range(2):
                for rows in chunks:
                    c = carry_ref[hh, rows, :]
                    arg = zw_ref[p, rows, heads[hh]] + c + sums_ref[p, hh, rows, :]
                    a_ref[p, rows, heads[hh]] = jnp.exp(arg).astype(_MXU)
                    cm_ref[rows, heads[hh]] = jnp.where(pick, c, cm_ref[rows, heads[hh]])
                    carry_ref[hh, rows, :] = c + rs_ref[p, hh, rows, :]

        def out(b, p, acc):
            ks = pl.multiple_of(jnp.minimum(b, nk - 1) * KBLK, KBLK)
            vb = v_ref[pl.ds(ks, KBLK), :]
            for hh in range(2):
                acc = acc + jnp.dot(a_ref[p, :, heads[hh]], jnp.where(hmask[hh], vb, jnp.zeros_like(vb)),
                                    preferred_element_type=F32)
            return acc

        trips = (nk + 4 + FWD_UNROLL - 1) // FWD_UNROLL

        def step(it, acc):
            for u in range(FWD_UNROLL):
                b = trips * FWD_UNROLL - 1 - (FWD_UNROLL * it + u)
                p = 1 - u % 2
                acc = out(b, p, acc)
                weights(b - 1, 1 - p)
                sums(b - 2, p)
                logs(b - 3, 1 - p)
                scores(b - 4, p)
            return acc

        z_ref[...] = jnp.zeros_like(z_ref)
        zw_ref[...] = jnp.full_like(zw_ref, MASKED_SCORE)
        for ref in (sums_ref, carry_ref, rs_ref, lb_ref, a_ref):
            ref[...] = jnp.zeros_like(ref)
        o_ref[...] = lax.fori_loop(0, trips, step, q.astype(F32) * 0.0)

    scratch = [pltpu.VMEM((2, tq, 2 * KBLK), F32), pltpu.VMEM((2, tq, 2 * KBLK), F32),
               pltpu.VMEM((2, 2, tq, KBLK), F32), pltpu.VMEM((2, tq, KBLK), F32), pltpu.VMEM((2, 2, tq, KBLK), F32),
               pltpu.VMEM((2, 2, tq, KBLK), BF16), pltpu.VMEM((2, tq, 2 * KBLK), _MXU)]
    res = _pcall(
        _with_exchange(body, ride, 3, 2, len(scratch), (npairs - 1, s // tq - 1)), name=name, grid=(npairs, s // tq),
        in_specs=[pl.BlockSpec((tq, LANES), lambda p, i: (i, p)),
                  pl.BlockSpec((s, LANES), lambda p, i: (0, npairs + p)),
                  pl.BlockSpec((s, LANES), lambda p, i: (0, 2 * npairs + p))] + [sp for r in rides for sp in r.specs],
        out_specs=[pl.BlockSpec((tq, LANES), lambda p, i: (i, p)), pl.BlockSpec((tq, 2 * LANES), lambda p, i: (i, p))]
                  + [sp for r in rides for sp in r.specs],
        out_shape=[jax.ShapeDtypeStruct((s, W_B), F32), jax.ShapeDtypeStruct((s, 2 * W_B), F32)]
                  + [sh for r in rides for sh in r.out_shapes],
        scratch_shapes=scratch + [sem for r in rides for sem in r.semaphores],
        compiler_params=_params(("arbitrary", "arbitrary")),
    )(qkv, qkv, qkv, *[a for r in rides for a in r.arrs])
    return res[0], res[1], list(res[2:])


def _attn_bwd(qkv, cm, do, name, ride=None):
    s = qkv.shape[0]
    tq = _tile(s, 256, KBLK)
    npairs = W_B // LANES
    rides = [] if ride is None else [ride]

    nq = s // tq
    per_tile = tq // KBLK
    assert nq % 2 == 0 and per_tile % 2 == 0
    ntot = (nq + 1) * per_tile
    assert (ntot + 6) % BWD_UNROLL == 0

    def body(qa_ref, k_ref, v_ref, cma_ref, doa_ref, qb_ref, cmb_ref, dob_ref, dqa_ref, dqb_ref, dk_ref, dv_ref,
             z_ref, zw_ref, da_ref, g_ref, sig_ref, cum_ref, gp_ref, gcarry_ref, lb_ref, gb_ref, a_ref, dz_ref,
             dkt_ref, dvt_ref, q_st, do_st, qt_st, dot_st, cm_st):
        i = pl.program_id(1)

        @pl.when(i == 0)
        def _():
            dkt_ref[...] = jnp.zeros_like(dkt_ref)
            dvt_ref[...] = jnp.zeros_like(dvt_ref)

        tiles = (i, nq - 1 - i)
        nk_a = (i + 1) * per_tile
        for t, (q_in, do_in, cm_in) in enumerate(((qa_ref, doa_ref, cma_ref), (qb_ref, dob_ref, cmb_ref))):
            q_st[t] = q_in[...]
            do_st[t] = do_in[...].astype(_MXU)
            qt_st[t] = q_in[...].astype(F32).T.astype(_MXU)
            dot_st[t] = do_in[...].T.astype(_MXU)
            cm_st[t] = cm_in[...]

        lane = _iota((1, LANES), 1)
        lane_lo = lane < HEAD_DIM
        hmask = (lane_lo, jnp.logical_not(lane_lo))
        tri = (_iota((KBLK, KBLK), 0) >= _iota((KBLK, KBLK), 1)).astype(BF16)
        prefix = (_iota((KBLK, KBLK), 0) <= _iota((KBLK, KBLK), 1)).astype(BF16)
        dmat = _iota((tq, KBLK), 1) - _iota((tq, KBLK), 0)
        chunks = [slice(r, r + ROW_CHUNK) for r in range(0, tq, ROW_CHUNK)]
        heads = [slice(hh * LANES, (hh + 1) * LANES) for hh in range(2)]

        def locate(v):
            second = v >= nk_a
            return ((v >= 0) & (v < ntot), second.astype(jnp.int32), jnp.where(second, tiles[1], tiles[0]),
                    jnp.where(second, v - nk_a, v))

        def before(v):
            valid, _, tile, b = locate(v)
            return jnp.where(valid, tile * tq - b * KBLK, jnp.iinfo(jnp.int32).min)

        def key_block(v):
            return jnp.clip(locate(v)[3], 0, nblk - 1)

        def block_rows(v):
            return pl.ds(pl.multiple_of(key_block(v) * KBLK, KBLK), KBLK)

        def per_head(block):
            return jnp.concatenate([jnp.where(m, block, jnp.zeros_like(block)) for m in hmask], axis=0)

        feature_lo = _iota((LANES, KBLK), 0) < HEAD_DIM

        def own_features(side_by_side):
            return jnp.where(feature_lo, side_by_side[:, 0:KBLK], side_by_side[:, KBLK:2 * KBLK])

        def m1(b, p):
            z_ref[p] = _dot_nt(q_st[locate(b)[1]], per_head(k_ref[block_rows(b), :]))

        def v1(b, p):
            t = before(b)
            for hh in range(2):
                for rows in chunks:
                    z = jnp.where(dmat[rows] < t, z_ref[p, rows, heads[hh]], MASKED_SCORE)
                    zw_ref[p, rows, heads[hh]] = z
                    l = -(jnp.maximum(z, 0.0) + jnp.log(1.0 + jnp.exp(-jnp.abs(z))))
                    lb_ref[p, hh, rows, :] = l.astype(BF16)

        def m2(b, p):
            for hh in range(2):
                cum_ref[p, hh] = jnp.dot(lb_ref[p, hh], tri, preferred_element_type=F32)
            da_ref[p] = _dot_nt(do_st[locate(b)[1]], per_head(v_ref[block_rows(b), :]))

        def v2(b, p):
            valid, which, _, blk = locate(b)
            pick = lane == jnp.where(valid, blk, -1)
            for hh in range(2):
                for rows in chunks:
                    c = jnp.sum(jnp.where(pick, cm_st[which, rows, heads[hh]], 0.0), axis=1, keepdims=True)
                    z = zw_ref[p, rows, heads[hh]]
                    a = jnp.exp(z + c + cum_ref[p, hh, rows, :])
                    g = a * da_ref[p, rows, heads[hh]]
                    a_ref[p, rows, heads[hh]] = a.astype(_MXU)
                    g_ref[p, rows, heads[hh]] = g
                    gb_ref[p, hh, rows, :] = g.astype(BF16)
                    sig_ref[p, rows, heads[hh]] = jax.nn.sigmoid(z)

        def m3(b, p):
            for hh in range(2):
                gp_ref[p, hh] = jnp.dot(gb_ref[p, hh], prefix, preferred_element_type=F32)
            dvt_ref[key_block(b)] += own_features(jnp.dot(dot_st[locate(b)[1]], a_ref[p], preferred_element_type=F32))

        def v3(b, p):
            restart = b == nk_a
            for hh in range(2):
                for rows in chunks:
                    gc = jnp.where(restart, 0.0, gcarry_ref[hh, rows, :])
                    g = g_ref[p, rows, heads[hh]]
                    dz = g - sig_ref[p, rows, heads[hh]] * (gc + gp_ref[p, hh, rows, :])
                    dz_ref[p, rows, heads[hh]] = dz.astype(_MXU)
                    gcarry_ref[hh, rows, :] = gc + jnp.sum(g, axis=1, keepdims=True)

        def m4(b, p, dqs):
            which = locate(b)[1]
            kb = k_ref[block_rows(b), :]
            part = None
            for hh in range(2):
                d = jnp.dot(dz_ref[p, :, heads[hh]], jnp.where(hmask[hh], kb, jnp.zeros_like(kb)),
                            preferred_element_type=F32)
                part = d if part is None else part + d
            dkt_ref[key_block(b)] += own_features(jnp.dot(qt_st[which], dz_ref[p], preferred_element_type=F32))
            return dqs[0] + jnp.where(which == 0, part, 0.0), dqs[1] + jnp.where(which == 1, part, 0.0)

        def step(it, dqs):
            for u in range(BWD_UNROLL):
                j = BWD_UNROLL * it + u
                p = u % 2
                dqs = m4(j - 6, p, dqs)
                m3(j - 4, p)
                m2(j - 2, p)
                m1(j, p)
                v3(j - 5, 1 - p)
                v2(j - 3, 1 - p)
                v1(j - 1, 1 - p)
            return dqs

        zw_ref[...] = jnp.full_like(zw_ref, MASKED_SCORE)
        for ref in (z_ref, da_ref, g_ref, sig_ref, cum_ref, gp_ref, gcarry_ref, lb_ref, gb_ref, a_ref, dz_ref):
            ref[...] = jnp.zeros_like(ref)
        zero = doa_ref[...] * 0.0
        dq_a, dq_b = lax.fori_loop(0, (ntot + 6) // BWD_UNROLL, step, (zero, zero))
        dqa_ref[...] = dq_a * 0.125
        dqb_ref[...] = dq_b * 0.125

        @pl.when(i == nq // 2 - 1)
        def _():
            def untranspose(blk, carry):
                rows = pl.ds(pl.multiple_of(blk * KBLK, KBLK), KBLK)
                dk_ref[rows, :] = dkt_ref[blk].T
                dv_ref[rows, :] = dvt_ref[blk].T
                return carry
            lax.fori_loop(0, nblk, untranspose, 0)

    first = lambda width: pl.BlockSpec((tq, width), lambda p, i: (i, p))
    second = lambda width: pl.BlockSpec((tq, width), lambda p, i: (nq - 1 - i, p))
    second_out = pl.BlockSpec((tq, LANES), lambda p, i: (nq // 2 - 1 - i, p))
    nblk = s // KBLK
    full = pl.BlockSpec((s, LANES), lambda p, i: (0, p))
    scratch = ([pltpu.VMEM((2, tq, 2 * KBLK), F32)] * 5 + [pltpu.VMEM((2, 2, tq, KBLK), F32),
               pltpu.VMEM((2, 2, tq, KBLK), F32), pltpu.VMEM((2, tq, KBLK), F32)]
               + [pltpu.VMEM((2, 2, tq, KBLK), BF16)] * 2 + [pltpu.VMEM((2, tq, 2 * KBLK), _MXU)] * 2
               + [pltpu.VMEM((nblk, LANES, KBLK), F32)] * 2
               + [pltpu.VMEM((2, tq, LANES), _MXU)] * 2 + [pltpu.VMEM((2, LANES, tq), _MXU)] * 2
               + [pltpu.VMEM((2, tq, 2 * LANES), F32)])
    res = _pcall(
        _with_exchange(body, ride, 8, 4, len(scratch), (npairs - 1, nq // 2 - 1)), name=name, grid=(npairs, nq // 2),
        in_specs=[first(LANES), pl.BlockSpec((s, LANES), lambda p, i: (0, npairs + p)),
                  pl.BlockSpec((s, LANES), lambda p, i: (0, 2 * npairs + p)), first(2 * LANES), first(LANES),
                  second(LANES), second(2 * LANES), second(LANES)] + [sp for r in rides for sp in r.specs],
        out_specs=[first(LANES), second_out, full, full] + [sp for r in rides for sp in r.specs],
        out_shape=[jax.ShapeDtypeStruct((s // 2, W_B), F32)] * 2 + [jax.ShapeDtypeStruct((s, W_B), F32)] * 2
                  + [sh for r in rides for sh in r.out_shapes],
        scratch_shapes=scratch + [sem for r in rides for sem in r.semaphores],
        compiler_params=_params(("arbitrary", "arbitrary")),
    )(qkv, qkv, qkv, cm, do, qkv, cm, do, *[a for r in rides for a in r.arrs])
    return jnp.concatenate([res[0], res[1]], axis=0), res[2], res[3], list(res[4:])


POOL_HALO = 128


def _pool_window_lane():
    lane = _iota((1, W_C), 1)
    w = jnp.where(lane < 64, POOL_WINDOWS[0], jnp.where(lane < 128, POOL_WINDOWS[1],
                  jnp.where(lane < 192, POOL_WINDOWS[2], POOL_WINDOWS[3])))
    return w.astype(F32)


def _pool_count(tm, i):
    pos = (_iota((tm, W_C), 0) + (i * tm + 1)).astype(F32)
    return jnp.minimum(pos, _pool_window_lane())


def _pool_centered(prev, cur, cnt):
    tm = cur.shape[0]
    xx = jnp.concatenate([prev, cur], axis=0)
    hi, lo = _split(xx)
    t = _iota((tm, tm + POOL_HALO), 0)
    cc = _iota((tm, tm + POOL_HALO), 1) - POOL_HALO
    wsum = jnp.zeros_like(cur)
    for g, w in enumerate(POOL_WINDOWS):
        band = ((cc <= t) & (cc > t - w)).astype(BF16)
        mg = _lane_group_mask(W_C, g)
        wsum += jnp.dot(band, jnp.where(mg, hi, jnp.zeros_like(hi)), preferred_element_type=F32)
        wsum += jnp.dot(band, jnp.where(mg, lo, jnp.zeros_like(lo)), preferred_element_type=F32)
    return wsum / cnt - cur


def _pool_fwd(p, wbd, sc, name):
    s = p.shape[0]
    tm = _tile(s, 256, POOL_HALO)
    r = tm // POOL_HALO

    def body(pp_ref, p_ref, w_ref, sc_ref, y_ref):
        i = pl.program_id(0)
        prev = jnp.where(i > 0, pp_ref[...], 0.0)
        d = _pool_centered(prev, p_ref[...], _pool_count(tm, i))
        y_ref[...] = _dot(d, w_ref[...]) * sc_ref[...]

    return _pcall(
        body, name=name, grid=(s // tm,),
        in_specs=[pl.BlockSpec((POOL_HALO, W_C), lambda i: (jnp.maximum(i * r - 1, 0), 0)),
                  pl.BlockSpec((tm, W_C), lambda i: (i, 0)), pl.BlockSpec((W_C, W_C), lambda i: (0, 0)),
                  pl.BlockSpec((1, W_C), lambda i: (0, 0))],
        out_specs=pl.BlockSpec((tm, W_C), lambda i: (i, 0)),
        out_shape=jax.ShapeDtypeStruct((s, W_C), F32),
        compiler_params=_params(("parallel",)),
    )(p, p, wbd, sc)


def _pool_bwd(p, dy, wbd, wbdt, sc, name):
    s = p.shape[0]
    tm = _tile(s, 256, POOL_HALO)
    r = tm // POOL_HALO
    nt = s // tm

    def body(pp_ref, p_ref, dy_ref, dyn_ref, w_ref, wt_ref, sc_ref, dp_ref, dw_ref, dsc_ref):
        i = pl.program_id(0)

        @pl.when(i == 0)
        def _():
            dw_ref[...] = jnp.zeros_like(dw_ref)
            dsc_ref[...] = jnp.zeros_like(dsc_ref)

        scv = sc_ref[...]
        cnt = _pool_count(tm, i)
        prev = jnp.where(i > 0, pp_ref[...], 0.0)
        d = _pool_centered(prev, p_ref[...], cnt)
        e = _dot(d, w_ref[...])
        dyv = dy_ref[...]
        de = dyv * scv
        dsc_ref[...] += jnp.sum(dyv * e, axis=0, keepdims=True)
        dw_ref[...] += _dot_tn(d, de)
        dd = _dot(de, wt_ref[...])
        ddn = jnp.where(i < nt - 1, _dot(dyn_ref[...] * scv, wt_ref[...]), 0.0)
        yy = jnp.concatenate([dd / cnt, ddn / _pool_window_lane()], axis=0)
        hi, lo = _split(yy)
        t = _iota((tm, tm + POOL_HALO), 0)
        cc = _iota((tm, tm + POOL_HALO), 1)
        acc = jnp.zeros_like(dd)
        for g, w in enumerate(POOL_WINDOWS):
            band = ((cc >= t) & (cc < t + w)).astype(BF16)
            mg = _lane_group_mask(W_C, g)
            acc += jnp.dot(band, jnp.where(mg, hi, jnp.zeros_like(hi)), preferred_element_type=F32)
            acc += jnp.dot(band, jnp.where(mg, lo, jnp.zeros_like(lo)), preferred_element_type=F32)
        dp_ref[...] = acc - dd

    tile = pl.BlockSpec((tm, W_C), lambda i: (i, 0))
    mat = pl.BlockSpec((W_C, W_C), lambda i: (0, 0))
    vec = pl.BlockSpec((1, W_C), lambda i: (0, 0))
    return _pcall(
        body, name=name, grid=(nt,),
        in_specs=[pl.BlockSpec((POOL_HALO, W_C), lambda i: (jnp.maximum(i * r - 1, 0), 0)), tile, tile,
                  pl.BlockSpec((POOL_HALO, W_C), lambda i: (jnp.minimum((i + 1) * r, s // POOL_HALO - 1), 0)),
                  mat, mat, vec],
        out_specs=[tile, mat, vec],
        out_shape=[jax.ShapeDtypeStruct((s, W_C), F32), jax.ShapeDtypeStruct((W_C, W_C), F32),
                   jax.ShapeDtypeStruct((1, W_C), F32)],
        compiler_params=_params(("arbitrary",)),
    )(p, p, dy, dy, wbd, wbdt, sc)


def _mix_cols(ya_ref, yb_ref, yc_ref, cb):
    if cb < 2:
        return ya_ref[:, cb * LANES:(cb + 1) * LANES]
    if cb < 6:
        return yb_ref[:, (cb - 2) * LANES:(cb - 1) * LANES]
    return yc_ref[:, (cb - 6) * LANES:(cb - 5) * LANES]


def _mix_fwd(ya, yb, yc, g, wo, x, name):
    s = x.shape[0]
    tm = _tile(s, 256)

    def body(ya_ref, yb_ref, yc_ref, g_ref, w_ref, x_ref, o_ref, yn_ref):
        sel = _group_mat(LANES)
        for cb in range(D_MODEL // LANES):
            y = _mix_cols(ya_ref, yb_ref, yc_ref, cb)
            r = lax.rsqrt(_dot_sel(y * y, sel) * (1.0 / HEAD_DIM) + EPS)
            yn_ref[:, cb * LANES:(cb + 1) * LANES] = (y * r * g_ref[:, cb * LANES:(cb + 1) * LANES]).astype(_MXU)
        o_ref[...] = x_ref[...] + jnp.dot(yn_ref[...], w_ref[...], preferred_element_type=F32)

    row = lambda n: pl.BlockSpec((tm, n), lambda i: (i, 0))
    return _pcall(
        body, name=name, grid=(s // tm,),
        in_specs=[row(W_A), row(W_B), row(W_C), pl.BlockSpec((1, D_MODEL), lambda i: (0, 0)),
                  pl.BlockSpec((D_MODEL, D_MODEL), lambda i: (0, 0)), row(D_MODEL)],
        out_specs=[row(D_MODEL), row(D_MODEL)],
        out_shape=[jax.ShapeDtypeStruct((s, D_MODEL), F32), jax.ShapeDtypeStruct((s, D_MODEL), _MXU)],
        compiler_params=_params(("parallel",)),
    )(ya, yb, yc, g, wo, x)


def _mix_bwd(dx, ya, yb, yc, g, wot, name):
    s = dx.shape[0]
    tm = _tile(s, 256)

    def body(dx_ref, ya_ref, yb_ref, yc_ref, g_ref, wt_ref, dya_ref, dyb_ref, dyc_ref, dg_ref):
        @pl.when(pl.program_id(0) == 0)
        def _():
            dg_ref[...] = jnp.zeros_like(dg_ref)

        dyn = _dot(dx_ref[...], wt_ref[...])
        sel = _group_mat(LANES)
        for cb in range(D_MODEL // LANES):
            cols = slice(cb * LANES, (cb + 1) * LANES)
            y = _mix_cols(ya_ref, yb_ref, yc_ref, cb)
            r = lax.rsqrt(_dot_sel(y * y, sel) * (1.0 / HEAD_DIM) + EPS)
            dyc_ = dyn[:, cols]
            dyg = dyc_ * g_ref[:, cols]
            m2 = _dot_sel(dyg * y, sel) * (1.0 / HEAD_DIM)
            dy = r * dyg - y * (r * r * r) * m2
            dg_ref[:, cols] += jnp.sum(dyc_ * y * r, axis=0, keepdims=True)
            if cb < 2:
                dya_ref[:, cb * LANES:(cb + 1) * LANES] = dy
            elif cb < 6:
                dyb_ref[:, (cb - 2) * LANES:(cb - 1) * LANES] = dy
            else:
                dyc_ref[:, (cb - 6) * LANES:(cb - 5) * LANES] = dy

    row = lambda n: pl.BlockSpec((tm, n), lambda i: (i, 0))
    vec = pl.BlockSpec((1, D_MODEL), lambda i: (0, 0))
    return _pcall(
        body, name=name, grid=(s // tm,),
        in_specs=[row(D_MODEL), row(W_A), row(W_B), row(W_C), vec, pl.BlockSpec((D_MODEL, D_MODEL), lambda i: (0, 0))],
        out_specs=[row(W_A), row(W_B), row(W_C), vec],
        out_shape=[jax.ShapeDtypeStruct((s, W_A), F32), jax.ShapeDtypeStruct((s, W_B), F32),
                   jax.ShapeDtypeStruct((s, W_C), F32), jax.ShapeDtypeStruct((1, D_MODEL), F32)],
        compiler_params=_params(("arbitrary",)),
    )(dx, ya, yb, yc, g, wot)


FFN_CHUNK = 1408
FFN_ROWS = 256
N_CHUNKS = D_FF // FFN_CHUNK
CW_ROWS = 8


def _ffn_up_fwd(x, g, w, name):
    s = x.shape[0]
    n = w.shape[1]
    tm, tn = _tile(s, 512), _tile(n, 2816, LANES)

    def body(x_ref, g_ref, w_ref, z_ref, h_ref):
        @pl.when(pl.program_id(1) == 0)
        def _():
            xv = x_ref[...]
            h_ref[...] = (xv * _rms(xv) * g_ref[...]).astype(_MXU)
        z_ref[...] = jnp.dot(h_ref[...], w_ref[...], preferred_element_type=F32)

    return _pcall(
        body, name=name, grid=(s // tm, n // tn),
        in_specs=[pl.BlockSpec((tm, D_MODEL), lambda i, j: (i, 0)), pl.BlockSpec((1, D_MODEL), lambda i, j: (0, 0)),
                  pl.BlockSpec((D_MODEL, tn), lambda i, j: (0, j))],
        out_specs=[pl.BlockSpec((tm, tn), lambda i, j: (i, j)), pl.BlockSpec((tm, D_MODEL), lambda i, j: (i, 0))],
        out_shape=[jax.ShapeDtypeStruct((s, n), F32), jax.ShapeDtypeStruct((s, D_MODEL), _MXU)],
        compiler_params=_params(("parallel", "arbitrary")),
    )(x, g, w)


def _conv(cur, prev8, cw_ref):
    s1 = _shift_down(cur, prev8, 1)
    s2 = _shift_down(cur, prev8, 2)
    zc = cw_ref[3:4, :] + s2 * cw_ref[0:1, :]
    zc = zc + s1 * cw_ref[1:2, :]
    zc = zc + cur * cw_ref[2:3, :]
    return zc, s1, s2


def _halo_specs(tm, s):
    r = tm // SUBLANES
    prev = lambda off: pl.BlockSpec((SUBLANES, FFN_CHUNK), lambda i, j: (jnp.maximum(i * r - 1, 0), j + off))
    nxt = lambda off: pl.BlockSpec((SUBLANES, FFN_CHUNK), lambda i, j: (jnp.minimum((i + 1) * r, s // SUBLANES - 1), j + off))
    return prev, nxt


def _ffn_down_fwd(z, cw, wd, x, name):
    s = x.shape[0]
    tm = _tile(s, FFN_ROWS)
    prev, _ = _halo_specs(tm, s)

    def body(zg_ref, zu_ref, pg_ref, pu_ref, cg_ref, cu_ref, w_ref, x_ref, o_ref, act_ref, acc_ref):
        i, j = pl.program_id(0), pl.program_id(1)
        first = i > 0
        zg, _, _ = _conv(zg_ref[...], jnp.where(first, pg_ref[...], 0.0), cg_ref)
        zu, _, _ = _conv(zu_ref[...], jnp.where(first, pu_ref[...], 0.0), cu_ref)
        act = (zg * jax.nn.sigmoid(zg) * zu).astype(_MXU)
        act_ref[...] = act

        @pl.when(j == 0)
        def _():
            acc_ref[...] = x_ref[...]
        acc_ref[...] += jnp.dot(act, w_ref[...], preferred_element_type=F32)

        @pl.when(j == N_CHUNKS - 1)
        def _():
            o_ref[...] = acc_ref[...]

    zt = lambda off: pl.BlockSpec((tm, FFN_CHUNK), lambda i, j: (i, j + off))
    cwt = lambda off: pl.BlockSpec((CW_ROWS, FFN_CHUNK), lambda i, j: (0, j + off))
    return _pcall(
        body, name=name, grid=(s // tm, N_CHUNKS),
        in_specs=[zt(0), zt(N_CHUNKS), prev(0), prev(N_CHUNKS), cwt(0), cwt(N_CHUNKS),
                  pl.BlockSpec((FFN_CHUNK, D_MODEL), lambda i, j: (j, 0)), pl.BlockSpec((tm, D_MODEL), lambda i, j: (i, 0))],
        out_specs=[pl.BlockSpec((tm, D_MODEL), lambda i, j: (i, 0)), pl.BlockSpec((tm, FFN_CHUNK), lambda i, j: (i, j))],
        out_shape=[jax.ShapeDtypeStruct((s, D_MODEL), F32), jax.ShapeDtypeStruct((s, D_FF), _MXU)],
        scratch_shapes=[pltpu.VMEM((tm, D_MODEL), F32)],
        compiler_params=_params(("parallel", "arbitrary")),
    )(z, z, z, z, cw, cw, wd, x)


def _ffn_down_bwd(dx, z, cw, wdt, name):
    s = dx.shape[0]
    tm = _tile(s, FFN_ROWS)

    def body(dx_ref, zg_ref, zu_ref, pg_ref, pu_ref, cg_ref, cu_ref, wt_ref, dg_ref, du_ref, dcg_ref, dcu_ref):
        i = pl.program_id(1)
        first = i > 0

        @pl.when(i == 0)
        def _():
            dcg_ref[...] = jnp.zeros_like(dcg_ref)
            dcu_ref[...] = jnp.zeros_like(dcu_ref)

        dact = _dot(dx_ref[...], wt_ref[...])
        zg, g1, g2 = _conv(zg_ref[...], jnp.where(first, pg_ref[...], 0.0), cg_ref)
        zu, u1, u2 = _conv(zu_ref[...], jnp.where(first, pu_ref[...], 0.0), cu_ref)
        sg = jax.nn.sigmoid(zg)
        silu = zg * sg
        dzu = dact * silu
        dzg = dact * zu * (sg * (1.0 + zg * (1.0 - sg)))
        dg_ref[...] = dzg
        du_ref[...] = dzu
        for ref, dzc, cur, s1, s2 in ((dcg_ref, dzg, zg_ref[...], g1, g2), (dcu_ref, dzu, zu_ref[...], u1, u2)):
            ref[0:1, :] += jnp.sum(dzc * s2, axis=0, keepdims=True)
            ref[1:2, :] += jnp.sum(dzc * s1, axis=0, keepdims=True)
            ref[2:3, :] += jnp.sum(dzc * cur, axis=0, keepdims=True)
            ref[3:4, :] += jnp.sum(dzc, axis=0, keepdims=True)

    zt = lambda off: pl.BlockSpec((tm, FFN_CHUNK), lambda j, i: (i, j + off))
    r = tm // SUBLANES
    pv = lambda off: pl.BlockSpec((SUBLANES, FFN_CHUNK), lambda j, i: (jnp.maximum(i * r - 1, 0), j + off))
    cwt = lambda off: pl.BlockSpec((CW_ROWS, FFN_CHUNK), lambda j, i: (0, j + off))
    out_t = pl.BlockSpec((tm, FFN_CHUNK), lambda j, i: (i, j))
    dc_t = pl.BlockSpec((CW_ROWS, FFN_CHUNK), lambda j, i: (0, j))
    dzg, dzu, dcg, dcu = _pcall(
        body, name=name, grid=(N_CHUNKS, s // tm),
        in_specs=[pl.BlockSpec((tm, D_MODEL), lambda j, i: (i, 0)), zt(0), zt(N_CHUNKS), pv(0), pv(N_CHUNKS),
                  cwt(0), cwt(N_CHUNKS), pl.BlockSpec((D_MODEL, FFN_CHUNK), lambda j, i: (0, j))],
        out_specs=[out_t, out_t, dc_t, dc_t],
        out_shape=[jax.ShapeDtypeStruct((s, D_FF), F32), jax.ShapeDtypeStruct((s, D_FF), F32),
                   jax.ShapeDtypeStruct((CW_ROWS, D_FF), F32), jax.ShapeDtypeStruct((CW_ROWS, D_FF), F32)],
        compiler_params=_params(("parallel", "arbitrary")),
    )(dx, z, z, z, z, cw, cw, wdt)
    return dzg, dzu, jnp.concatenate([dcg, dcu], axis=1)


def _ffn_up_bwd(dzg, dzu, cw, wut, x, g, dres, name):
    s = x.shape[0]
    tm = _tile(s, FFN_ROWS)
    _, nxt = _halo_specs(tm, s)
    nt = s // tm

    def body(dg_ref, du_ref, ng_ref, nu_ref, cg_ref, cu_ref, wg_ref, wu_ref, x_ref, g_ref, dres_ref,
             dzg_ref, dzu_ref, dx_ref, dgn_ref, acc_ref):
        i, j = pl.program_id(0), pl.program_id(1)
        last = i < nt - 1

        def conv_bwd(cur, nxt8, cw_ref):
            up1 = _shift_up(cur, nxt8, 1)
            up2 = _shift_up(cur, nxt8, 2)
            return cur * cw_ref[2:3, :] + up1 * cw_ref[1:2, :] + up2 * cw_ref[0:1, :]

        dzg_ = conv_bwd(dg_ref[...], jnp.where(last, ng_ref[...], 0.0), cg_ref).astype(_MXU)
        dzu_ = conv_bwd(du_ref[...], jnp.where(last, nu_ref[...], 0.0), cu_ref).astype(_MXU)
        dzg_ref[...] = dzg_
        dzu_ref[...] = dzu_

        @pl.when(j == 0)
        def _():
            acc_ref[...] = jnp.zeros_like(acc_ref)
        acc_ref[...] += (jnp.dot(dzg_, wg_ref[...], preferred_element_type=F32)
                         + jnp.dot(dzu_, wu_ref[...], preferred_element_type=F32))

        @pl.when((i == 0) & (j == 0))
        def _():
            dgn_ref[...] = jnp.zeros_like(dgn_ref)

        @pl.when(j == N_CHUNKS - 1)
        def _():
            dx, dgn = _rms_bwd(x_ref[...], g_ref[...], acc_ref[...])
            dx_ref[...] = dres_ref[...] + dx
            dgn_ref[...] += dgn

    zt = pl.BlockSpec((tm, FFN_CHUNK), lambda i, j: (i, j))
    cwt = lambda off: pl.BlockSpec((CW_ROWS, FFN_CHUNK), lambda i, j: (0, j + off))
    wt = lambda off: pl.BlockSpec((FFN_CHUNK, D_MODEL), lambda i, j: (j + off, 0))
    row = pl.BlockSpec((tm, D_MODEL), lambda i, j: (i, 0))
    vec = pl.BlockSpec((1, D_MODEL), lambda i, j: (0, 0))
    return _pcall(
        body, name=name, grid=(nt, N_CHUNKS),
        in_specs=[zt, zt, nxt(0), nxt(0), cwt(0), cwt(N_CHUNKS), wt(0), wt(N_CHUNKS), row, vec, row],
        out_specs=[zt, zt, row, vec],
        out_shape=[jax.ShapeDtypeStruct((s, D_FF), _MXU), jax.ShapeDtypeStruct((s, D_FF), _MXU),
                   jax.ShapeDtypeStruct((s, D_MODEL), F32), jax.ShapeDtypeStruct((1, D_MODEL), F32)],
        scratch_shapes=[pltpu.VMEM((tm, D_MODEL), F32)],
        compiler_params=_params(("arbitrary", "arbitrary")),
    )(dzg, dzu, dzg, dzu, cw, cw, wut, wut, x, g, dres)


def _final_loss(x, g, tgt, name):
    s = x.shape[0]
    tm = _tile(s, 256)

    def body(x_ref, g_ref, t_ref, loss_ref, dx_ref, dg_ref):
        @pl.when(pl.program_id(0) == 0)
        def _():
            loss_ref[...] = jnp.zeros_like(loss_ref)
            dg_ref[...] = jnp.zeros_like(dg_ref)

        xv, gv = x_ref[...], g_ref[...]
        err = xv * _rms(xv) * gv - t_ref[...]
        per_tok = jnp.mean(err * err, axis=-1, keepdims=True)
        loss_ref[...] += 0.5 * jnp.sum(per_tok, axis=0, keepdims=True)
        dx, dg = _rms_bwd(xv, gv, err * (1.0 / D_MODEL))
        dx_ref[...] = dx
        dg_ref[...] += dg

    row = pl.BlockSpec((tm, D_MODEL), lambda i: (i, 0))
    vec = pl.BlockSpec((1, D_MODEL), lambda i: (0, 0))
    return _pcall(
        body, name=name, grid=(s // tm,),
        in_specs=[row, vec, row], out_specs=[pl.BlockSpec((1, 1), lambda i: (0, 0)), row, vec],
        out_shape=[jax.ShapeDtypeStruct((1, 1), F32), jax.ShapeDtypeStruct((s, D_MODEL), F32),
                   jax.ShapeDtypeStruct((1, D_MODEL), F32)],
        compiler_params=_params(("arbitrary",)),
    )(x, g, tgt)


def _adamw(parts, w, m, v, name):
    r, c = w.shape
    tr = _tile(r, 256)
    c1 = 1.0 - ADAM_B1 ** ADAM_STEP
    c2 = 1.0 - ADAM_B2 ** ADAM_STEP

    def body(p_ref, w_ref, m_ref, v_ref, g_ref, d_ref, mo_ref, vo_ref):
        g = p_ref[0].astype(F32)
        for i in range(1, N_DEV):
            g = g + p_ref[i].astype(F32)
        mn = ADAM_B1 * m_ref[...] + (1.0 - ADAM_B1) * g
        vn = ADAM_B2 * v_ref[...] + (1.0 - ADAM_B2) * (g * g)
        g_ref[...] = g
        mo_ref[...] = mn
        vo_ref[...] = vn
        d_ref[...] = -ADAM_LR * ((mn / c1) / (jnp.sqrt(vn / c2) + ADAM_EPS) + ADAM_WD * w_ref[...])

    t2 = pl.BlockSpec((tr, c), lambda i: (i, 0))
    return _pcall(
        body, name=name, grid=(r // tr,),
        in_specs=[pl.BlockSpec((N_DEV, tr, c), lambda i: (0, i, 0)), t2, t2, t2],
        out_specs=[t2] * 4, out_shape=[jax.ShapeDtypeStruct((r, c), F32)] * 4,
        compiler_params=_params(("parallel",)),
    )(parts, w, m, v)


SMALL = ("norm1_g", "sgu_norm_g", "sgu_w", "sgu_b", "pool_w", "pool_scale", "mix_norm_g", "norm2_g", "conv_b", "final_g")
SHARDED = ("w_in", "w_o", "w_up", "conv_w", "w_down")
ORDER = ("norm1_g", "w_in", "sgu_norm_g", "sgu_w", "sgu_b", "pool_w", "pool_scale", "mix_norm_g", "w_o", "norm2_g",
         "w_up", "conv_w", "conv_b", "w_down", "final_g")


def _pack(tree):
    return jnp.concatenate([tree[n].reshape(-1) for n in SMALL]).reshape(-1, LANES)


def _unpack(flat, like):
    out, off = {}, 0
    flat = flat.reshape(-1)
    for n in SMALL:
        size = math.prod(like[n].shape)
        out[n] = flat[off:off + size].reshape(like[n].shape)
        off += size
    return out


def _block_diag(pw):
    z = jnp.zeros((W_C, W_C), pw.dtype)
    for g in range(4):
        z = z.at[g * 64:(g + 1) * 64, g * 64:(g + 1) * 64].set(pw[g])
    return z


def kernel(x, norm1_g, w_in, sgu_norm_g, sgu_w, sgu_b, pool_w, pool_scale, mix_norm_g, w_o, norm2_g, w_up, conv_w, conv_b, w_down, final_g, loss_target, m_norm1_g, m_w_in, m_sgu_norm_g, m_sgu_w, m_sgu_b, m_pool_w, m_pool_scale, m_mix_norm_g, m_w_o, m_norm2_g, m_w_up, m_conv_w, m_conv_b, m_w_down, m_final_g, v_norm1_g, v_w_in, v_sgu_norm_g, v_sgu_w, v_sgu_b, v_pool_w, v_pool_scale, v_mix_norm_g, v_w_o, v_norm2_g, v_w_up, v_conv_w, v_conv_b, v_w_down, v_final_g):
    weights = dict(norm1_g=norm1_g, w_in=w_in, sgu_norm_g=sgu_norm_g, sgu_w=sgu_w, sgu_b=sgu_b, pool_w=pool_w,
                   pool_scale=pool_scale, mix_norm_g=mix_norm_g, w_o=w_o, norm2_g=norm2_g, w_up=w_up, conv_w=conv_w,
                   conv_b=conv_b, w_down=w_down, final_g=final_g)
    mom = dict(norm1_g=m_norm1_g, w_in=m_w_in, sgu_norm_g=m_sgu_norm_g, sgu_w=m_sgu_w, sgu_b=m_sgu_b, pool_w=m_pool_w,
               pool_scale=m_pool_scale, mix_norm_g=m_mix_norm_g, w_o=m_w_o, norm2_g=m_norm2_g, w_up=m_w_up,
               conv_w=m_conv_w, conv_b=m_conv_b, w_down=m_w_down, final_g=m_final_g)
    var = dict(norm1_g=v_norm1_g, w_in=v_w_in, sgu_norm_g=v_sgu_norm_g, sgu_w=v_sgu_w, sgu_b=v_sgu_b, pool_w=v_pool_w,
               pool_scale=v_pool_scale, mix_norm_g=v_mix_norm_g, w_o=v_w_o, norm2_g=v_norm2_g, w_up=v_w_up,
               conv_w=v_conv_w, conv_b=v_conv_b, w_down=v_w_down, final_g=v_final_g)
    depth = w_in.shape[0]
    s = x.shape[1]
    xs = x.reshape(s, D_MODEL)
    tgt = loss_target.reshape(s, D_MODEL)

    assert depth >= 2
    (g_in0,) = _exchange([w_in[0].astype(_MXU)], "gather_w_in0", False)
    w_in0 = jnp.transpose(g_in0, (1, 0, 2)).reshape(D_MODEL, IN_COLS)
    gather_rest = _Exchange([w_in[1:].astype(_MXU), w_o.astype(_MXU), w_up.astype(_MXU), conv_w, w_down.astype(_MXU)], False)

    tril = jnp.tril(jnp.ones((CHUNK, CHUNK), bool))
    layers = []
    for l in range(depth):
        wbd = _block_diag(pool_w[l])
        layers.append(dict(
            g1=norm1_g[l][None], gn=sgu_norm_g[l][None], wm=jnp.where(tril[None], sgu_w[l], 0.0).astype(_MXU),
            bias=jnp.repeat(sgu_b[l].T, HEAD_DIM, axis=1),
            wbd=wbd.astype(_MXU), wbd_t=wbd.T.astype(_MXU), sc=pool_scale[l][None],
            gmix=mix_norm_g[l][None], g2=norm2_g[l][None]))
    layers[0].update(w_in=w_in0, w_in_t=w_in0.T)

    def place_gathered(g_in, g_o, g_up, g_cw, g_dn):
        full_in = jnp.transpose(g_in, (1, 2, 0, 3)).reshape(depth - 1, D_MODEL, IN_COLS)
        full_o = jnp.transpose(g_o, (1, 0, 2, 3)).reshape(depth, D_MODEL, D_MODEL)
        full_up = jnp.transpose(g_up, (1, 2, 0, 3)).reshape(depth, D_MODEL, 2 * D_FF)
        full_cw = jnp.transpose(g_cw, (1, 2, 0, 3)).reshape(depth, 3, 2 * D_FF)
        full_dn = jnp.transpose(g_dn, (1, 0, 2, 3)).reshape(depth, D_FF, D_MODEL)
        for l in range(depth):
            if l > 0:
                layers[l].update(w_in=full_in[l - 1], w_in_t=full_in[l - 1].T)
            layers[l].update(
                w_o=full_o[l], w_o_t=full_o[l].T, w_up=full_up[l], w_up_t=full_up[l].T,
                cw=jnp.concatenate([full_cw[l], conv_b[l][None], jnp.zeros((CW_ROWS - 4, 2 * D_FF), F32)], axis=0),
                w_dn=full_dn[l], w_dn_t=full_dn[l].T)

    saved = []
    cur = xs
    for l, p in enumerate(layers):
        a_in, qkv, p_in, h1 = _inproj_fwd(cur, p["g1"], p["w_in"], f"inproj_fwd{l}")
        y_a = _sgu_fwd(a_in, p["gn"], p["wm"], p["bias"], f"sgu_fwd{l}")
        y_b, cm, gathered_w = _attn_fwd(qkv, f"attn_fwd{l}", gather_rest if l == 0 else None)
        if l == 0:
            place_gathered(*gathered_w)
        y_c = _pool_fwd(p_in, p["wbd"], p["sc"], f"pool_fwd{l}")
        x_mid, yn = _mix_fwd(y_a, y_b, y_c, p["gmix"], p["w_o"], cur, f"mix_fwd{l}")
        z, h2 = _ffn_up_fwd(x_mid, p["g2"], p["w_up"], f"ffn_up_fwd{l}")
        x_out, act = _ffn_down_fwd(z, p["cw"], p["w_dn"], x_mid, f"ffn_down_fwd{l}")
        saved.append(dict(x_in=cur, a_in=a_in, qkv=qkv, p_in=p_in, h1=h1, y_a=y_a, y_b=y_b, cm=cm, y_c=y_c, x_mid=x_mid,
                          yn=yn, z=z, h2=h2, act=act))
        cur = x_out
    loss_part, dx, dg_final = _final_loss(cur, final_g[None], tgt, "final_loss")

    small = {n: [None] * depth for n in SMALL if n != "final_g"}
    big = {n: [None] * depth for n in SHARDED}
    early = [(n, l) for n in SHARDED for l in range(depth) if (n, l) != ("w_in", 0)]
    wire = lambda n, t: t if n == "conv_w" else t.astype(GRAD_WIRE)
    for l in reversed(range(depth)):
        p, sv = layers[l], saved[l]
        dzg, dzu, dcw = _ffn_down_bwd(dx, sv["z"], p["cw"], p["w_dn_t"], f"ffn_down_bwd{l}")
        big["w_down"][l] = _mm_tn(sv["act"], dx, f"dw_down{l}").reshape(N_DEV, D_FF // N_DEV, D_MODEL)
        dzg_b, dzu_b, dx_mid, dg2 = _ffn_up_bwd(dzg, dzu, p["cw"], p["w_up_t"], sv["x_mid"], p["g2"], dx, f"ffn_up_bwd{l}")
        dw_up = jnp.stack([_mm_tn(sv["h2"], dzg_b, f"dw_up_g{l}"), _mm_tn(sv["h2"], dzu_b, f"dw_up_u{l}")])
        big["w_up"][l] = jnp.transpose(dw_up.reshape(2, D_MODEL, N_DEV // 2, 2 * D_FF // N_DEV), (0, 2, 1, 3)).reshape(
            N_DEV, D_MODEL, 2 * D_FF // N_DEV)
        big["conv_w"][l] = jnp.transpose(dcw[0:3].reshape(3, N_DEV, 2 * D_FF // N_DEV), (1, 0, 2))
        small["conv_b"][l] = dcw[3]
        small["norm2_g"][l] = dg2[0]
        dya, dyb, dyc, dgmix = _mix_bwd(dx_mid, sv["y_a"], sv["y_b"], sv["y_c"], p["gmix"], p["w_o_t"], f"mix_bwd{l}")
        small["mix_norm_g"][l] = dgmix[0]
        big["w_o"][l] = _mm_tn(sv["yn"], dx_mid, f"dw_o{l}").reshape(N_DEV, D_MODEL // N_DEV, D_MODEL)
        dp, dwbd, dsc = _pool_bwd(sv["p_in"], dyc, p["wbd"], p["wbd_t"], p["sc"], f"pool_bwd{l}")
        small["pool_w"][l] = jnp.stack([dwbd[g * 64:(g + 1) * 64, g * 64:(g + 1) * 64] for g in range(4)])
        small["pool_scale"][l] = dsc[0]
        scatter_early = _Exchange([wire(n, big[n][ll]) for n, ll in early], True) if l == 0 else None
        dq, dk, dv, recv_early = _attn_bwd(sv["qkv"], sv["cm"], dyb, f"attn_bwd{l}", scatter_early)
        if l == 0:
            recv_early_all = recv_early
        da, dwm, dbias, dgn = _sgu_bwd(sv["a_in"], dya, p["gn"], p["wm"], p["bias"], f"sgu_bwd{l}")
        small["sgu_w"][l] = dwm
        small["sgu_b"][l] = jnp.sum(dbias.reshape(CHUNK, 4, HEAD_DIM), axis=-1).T
        small["sgu_norm_g"][l] = dgn[0]
        dx, dg1 = _inproj_bwd(da, dq, dk, dv, dp, p["w_in_t"], sv["x_in"], p["g1"], dx_mid, f"inproj_bwd{l}")
        small["norm1_g"][l] = dg1[0]
        pieces = (da, dq, dk, dv, dp)
        dw_in = jnp.concatenate([_mm_tn(sv["h1"], t, f"dw_in{i}_{l}") for i, t in enumerate(pieces)], axis=1)
        big["w_in"][l] = jnp.transpose(dw_in.reshape(D_MODEL, N_DEV, IN_COLS // N_DEV), (1, 0, 2))

    part = {n: jnp.stack(small[n]) for n in small}
    part["final_g"] = dg_final[0]
    packed = _pack(part)
    recv_in0, gathered = _exchange([wire("w_in", big["w_in"][0]), jnp.broadcast_to(packed, (N_DEV,) + packed.shape)],
                                   "scatter_last", True)
    recv = dict(zip(early, recv_early_all))
    recv[("w_in", 0)] = recv_in0

    out_g, out_d, out_m, out_v = {}, {}, {}, {}
    for n in SHARDED:
        res = [_adamw(recv[(n, l)], weights[n][l], mom[n][l], var[n][l], f"adamw_{n}{l}") for l in range(depth)]
        out_g[n], out_d[n], out_m[n], out_v[n] = (jnp.stack([r[i] for r in res]) for i in range(4))

    sg, sd, sm, sv_ = _adamw(gathered, _pack(weights), _pack(mom), _pack(var), "adamw_small")
    for tree, flat in ((out_g, sg), (out_d, sd), (out_m, sm), (out_v, sv_)):
        tree.update(_unpack(flat, weights))

    loss = lax.psum(loss_part[0, 0], ("x", "y", "c"))
    grad_x = dx.reshape(1, s, D_MODEL)
    return (loss, grad_x, *[out_g[n] for n in ORDER], *[out_d[n] for n in ORDER], *[out_m[n] for n in ORDER],
            *[out_v[n] for n in ORDER])
```

```python
import functools
import math

import jax
import jax.numpy as jnp
import numpy as np
from jax import lax
from jax.experimental import pallas as pl
from jax.experimental.pallas import tpu as pltpu

F32 = jnp.float32
BF16 = jnp.bfloat16
_MXU = jnp.bfloat16
GRAD_WIRE = jnp.bfloat16

D_MODEL = 1024
W_A = 256
W_B = 512
W_C = 256
HEAD_DIM = 64
IN_COLS = 2 * W_A + 3 * W_B + W_C
D_FF = 2816
CHUNK = 128
POOL_WINDOWS = (2, 4, 8, 16)
EPS = 1e-6
N_DEV = 8
LANES = 128
SUBLANES = 8
VMEM_LIMIT = 48 * 1024 * 1024

ADAM_LR = 0.001
ADAM_B1 = 0.9
ADAM_B2 = 0.999
ADAM_EPS = 1e-08
ADAM_WD = 0.01
ADAM_STEP = 10

INV_SQRT2 = 1.0 / math.sqrt(2.0)
INV_SQRT_2PI = 1.0 / math.sqrt(2.0 * math.pi)


def _pcall(body, **kw):
    return pl.pallas_call(body, **kw)


def _params(dims=None):
    return pltpu.CompilerParams(dimension_semantics=dims, vmem_limit_bytes=VMEM_LIMIT)


def _tile(n, pref, mult=SUBLANES):
    t = min(n, pref) // mult * mult
    while t >= mult:
        if n % t == 0:
            return t
        t -= mult
    return n


def _iota(shape, dim):
    return lax.broadcasted_iota(jnp.int32, shape, dim)


def _dot(a, b):
    return jnp.dot(a.astype(_MXU), b.astype(_MXU), preferred_element_type=F32)


def _dot_nt(a, b):
    return lax.dot_general(a.astype(_MXU), b.astype(_MXU), (((1,), (1,)), ((), ())), preferred_element_type=F32)


def _dot_tn(a, b):
    return lax.dot_general(a.astype(_MXU), b.astype(_MXU), (((0,), (0,)), ((), ())), preferred_element_type=F32)


def _split(x):
    hi = x.astype(BF16)
    lo = (x - hi.astype(F32)).astype(BF16)
    return hi, lo


def _dot_sel(x, sel):
    hi, lo = _split(x)
    return jnp.dot(hi, sel, preferred_element_type=F32) + jnp.dot(lo, sel, preferred_element_type=F32)


def _sel_dot(sel, x):
    hi, lo = _split(x)
    return jnp.dot(sel, hi, preferred_element_type=F32) + jnp.dot(sel, lo, preferred_element_type=F32)


def _group_mat(n):
    r = jnp.right_shift(_iota((n, n), 0), 6)
    c = jnp.right_shift(_iota((n, n), 1), 6)
    return (r == c).astype(BF16)


def _lane_group_mask(n, g):
    lane = _iota((1, n), 1)
    return (lane >= g * HEAD_DIM) & (lane < (g + 1) * HEAD_DIM)


def _gelu(a):
    return 0.5 * a * (1.0 + lax.erf(a * INV_SQRT2))


def _gelu_grad(a):
    return 0.5 * (1.0 + lax.erf(a * INV_SQRT2)) + a * jnp.exp(-0.5 * a * a) * INV_SQRT_2PI


def _rms(x):
    return lax.rsqrt(jnp.mean(x * x, axis=-1, keepdims=True) + EPS)


def _rms_bwd(x, g, dy):
    r = _rms(x)
    dyg = dy * g
    m2 = jnp.mean(dyg * x, axis=-1, keepdims=True)
    dx = r * dyg - x * (r * r * r) * m2
    dg = jnp.sum(dy * x * r, axis=0, keepdims=True)
    return dx, dg


def _shift_down(cur, prev8, k):
    rolled = pltpu.roll(cur, k, 0)
    row8 = _iota(prev8.shape, 0)
    top = jnp.where(row8 < k, pltpu.roll(prev8, k, 0), rolled[0:SUBLANES])
    return jnp.concatenate([top, rolled[SUBLANES:]], axis=0)


def _shift_up(cur, next8, k):
    n = cur.shape[0]
    rolled = pltpu.roll(cur, n - k, 0)
    row8 = _iota(next8.shape, 0)
    bot = jnp.where(row8 >= SUBLANES - k, pltpu.roll(next8, SUBLANES - k, 0), rolled[n - SUBLANES:])
    return jnp.concatenate([rolled[:n - SUBLANES], bot], axis=0)


def _mesh_pos():
    return lax.axis_index("x"), lax.axis_index("y"), lax.axis_index("c")


def _peer(x, y, c, k):
    px = 1 - x if k & 4 else x
    py = 1 - y if k & 2 else y
    pc = 1 - c if k & 1 else c
    return px, py, pc


class _Exchange:
    def __init__(self, arrs, scatter):
        self.arrs, self.scatter, self.n = list(arrs), scatter, len(arrs)
        self.out_shapes = [jax.ShapeDtypeStruct(a.shape if scatter else (N_DEV,) + a.shape, a.dtype) for a in arrs]
        self.specs = [pl.BlockSpec(memory_space=pl.ANY)] * self.n
        self.semaphores = [pltpu.SemaphoreType.DMA((self.n * (N_DEV - 1),)),
                           pltpu.SemaphoreType.DMA((self.n * (N_DEV - 1),)), pltpu.SemaphoreType.DMA((self.n,))]

    def _copies(self, ins, outs, sems):
        send, recv, loc = sems
        x, y, c = _mesh_pos()
        me = 4 * x + 2 * y + c
        src = (lambda a, idx: ins[a].at[idx]) if self.scatter else (lambda a, idx: ins[a])
        starts = [pltpu.make_async_copy(src(a, me), outs[a].at[me], loc.at[a]) for a in range(self.n)]
        waits = list(starts)
        for k in range(1, N_DEV):
            px, py, pc = _peer(x, y, c, k)
            pidx = 4 * px + 2 * py + pc
            for a in range(self.n):
                s = a * (N_DEV - 1) + k - 1
                common = dict(src_ref=src(a, pidx), send_sem=send.at[s], recv_sem=recv.at[s],
                              device_id=(px, py, pc), device_id_type=pl.DeviceIdType.MESH)
                starts.append(pltpu.make_async_remote_copy(dst_ref=outs[a].at[me], **common))
                waits.append(pltpu.make_async_remote_copy(dst_ref=outs[a].at[pidx], **common))
        return starts, waits

    def start(self, ins, outs, sems):
        for cp in self._copies(ins, outs, sems)[0]:
            cp.start()

    def wait(self, ins, outs, sems):
        for cp in self._copies(ins, outs, sems)[1]:
            cp.wait()


def _with_exchange(compute, ride, n_in, n_out, n_scratch, last_step):
    if ride is None:
        return compute
    nx = ride.n

    def body(*refs):
        ins, ride_in = refs[:n_in], refs[n_in:n_in + nx]
        outs = refs[n_in + nx:n_in + nx + n_out]
        ride_out = refs[n_in + nx + n_out:n_in + 2 * nx + n_out]
        scratch = refs[n_in + 2 * nx + n_out:n_in + 2 * nx + n_out + n_scratch]
        sems = refs[n_in + 2 * nx + n_out + n_scratch:]
        step = (pl.program_id(0), pl.program_id(1))

        @pl.when((step[0] == 0) & (step[1] == 0))
        def _():
            ride.start(ride_in, ride_out, sems)

        compute(*ins, *outs, *scratch)

        @pl.when((step[0] == last_step[0]) & (step[1] == last_step[1]))
        def _():
            ride.wait(ride_in, ride_out, sems)

    return body


def _exchange(arrs, name, scatter):
    ex = _Exchange(arrs, scatter)
    n = ex.n

    def body(*refs):
        ins, outs, sems = refs[:n], refs[n:2 * n], refs[2 * n:]
        ex.start(ins, outs, sems)
        ex.wait(ins, outs, sems)

    return _pcall(body, name=name, out_shape=ex.out_shapes, in_specs=ex.specs, out_specs=ex.specs,
                  scratch_shapes=ex.semaphores, compiler_params=pltpu.CompilerParams(has_side_effects=True))(*arrs)


def _mm_tn(a, b, name):
    s, m = a.shape
    n = b.shape[1]
    tm, tn, tk = _tile(m, 1408, LANES), _tile(n, 1408, LANES), _tile(s, 1024)

    def body(a_ref, b_ref, o_ref):
        @pl.when(pl.program_id(2) == 0)
        def _():
            o_ref[...] = jnp.zeros_like(o_ref)
        o_ref[...] += _dot_tn(a_ref[...], b_ref[...])

    return _pcall(
        body, name=name, grid=(m // tm, n // tn, s // tk),
        in_specs=[pl.BlockSpec((tk, tm), lambda i, j, k: (k, i)), pl.BlockSpec((tk, tn), lambda i, j, k: (k, j))],
        out_specs=pl.BlockSpec((tm, tn), lambda i, j, k: (i, j)),
        out_shape=jax.ShapeDtypeStruct((m, n), F32),
        compiler_params=_params(("parallel", "parallel", "arbitrary")),
    )(a, b)


def _inproj_fwd(x, g, w, name):
    s = x.shape[0]
    tm = _tile(s, 256)

    def body(x_ref, g_ref, w_ref, a_ref, qkv_ref, p_ref, h_ref):
        xv = x_ref[...]
        h = (xv * _rms(xv) * g_ref[...]).astype(_MXU)
        h_ref[...] = h
        a_ref[...] = jnp.dot(h, w_ref[:, 0:2 * W_A], preferred_element_type=F32)
        q = jnp.dot(h, w_ref[:, 2 * W_A:2 * W_A + W_B], preferred_element_type=F32)
        qkv_ref[:, 0:W_B] = (q * 0.125).astype(_MXU)
        kv = jnp.dot(h, w_ref[:, 2 * W_A + W_B:2 * W_A + 3 * W_B], preferred_element_type=F32)
        qkv_ref[:, W_B:3 * W_B] = kv.astype(_MXU)
        p_ref[...] = jnp.dot(h, w_ref[:, 2 * W_A + 3 * W_B:IN_COLS], preferred_element_type=F32)

    row = lambda n: pl.BlockSpec((tm, n), lambda i: (i, 0))
    return _pcall(
        body, name=name, grid=(s // tm,),
        in_specs=[row(D_MODEL), pl.BlockSpec((1, D_MODEL), lambda i: (0, 0)),
                  pl.BlockSpec((D_MODEL, IN_COLS), lambda i: (0, 0))],
        out_specs=[row(2 * W_A), row(3 * W_B), row(W_C), row(D_MODEL)],
        out_shape=[jax.ShapeDtypeStruct((s, 2 * W_A), F32), jax.ShapeDtypeStruct((s, 3 * W_B), _MXU),
                   jax.ShapeDtypeStruct((s, W_C), F32), jax.ShapeDtypeStruct((s, D_MODEL), _MXU)],
        compiler_params=_params(("parallel",)),
    )(x, g, w)


def _inproj_bwd(da, dq, dk, dv, dp, wt, x, g, dres, name):
    s = x.shape[0]
    tm = _tile(s, 256)

    def body(da_ref, dq_ref, dk_ref, dv_ref, dp_ref, wt_ref, x_ref, g_ref, dres_ref, dx_ref, dg_ref):
        dh = _dot(da_ref[...], wt_ref[0:2 * W_A, :])
        dh += _dot(dq_ref[...], wt_ref[2 * W_A:2 * W_A + W_B, :])
        dh += _dot(dk_ref[...], wt_ref[2 * W_A + W_B:2 * W_A + 2 * W_B, :])
        dh += _dot(dv_ref[...], wt_ref[2 * W_A + 2 * W_B:2 * W_A + 3 * W_B, :])
        dh += _dot(dp_ref[...], wt_ref[2 * W_A + 3 * W_B:IN_COLS, :])
        dx, dg = _rms_bwd(x_ref[...], g_ref[...], dh)
        dx_ref[...] = dres_ref[...] + dx

        @pl.when(pl.program_id(0) == 0)
        def _():
            dg_ref[...] = jnp.zeros_like(dg_ref)
        dg_ref[...] += dg

    row = lambda n: pl.BlockSpec((tm, n), lambda i: (i, 0))
    vec = pl.BlockSpec((1, D_MODEL), lambda i: (0, 0))
    return _pcall(
        body, name=name, grid=(s // tm,),
        in_specs=[row(2 * W_A), row(W_B), row(W_B), row(W_B), row(W_C),
                  pl.BlockSpec((IN_COLS, D_MODEL), lambda i: (0, 0)), row(D_MODEL), vec, row(D_MODEL)],
        out_specs=[row(D_MODEL), vec],
        out_shape=[jax.ShapeDtypeStruct((s, D_MODEL), F32), jax.ShapeDtypeStruct((1, D_MODEL), F32)],
        compiler_params=_params(("arbitrary",)),
    )(da, dq, dk, dv, dp, wt, x, g, dres)


def _sgu_core(a, gn, wm_ref, bias):
    ga = _gelu(a)
    u, v0 = ga[:, 0:W_A], ga[:, W_A:2 * W_A]
    r = lax.rsqrt(_dot_sel(v0 * v0, _group_mat(W_A)) * (1.0 / HEAD_DIM) + EPS)
    vn = v0 * r * gn
    sv = bias
    for h in range(W_A // HEAD_DIM):
        sv = sv + _dot(wm_ref[h], jnp.where(_lane_group_mask(W_A, h), vn, 0.0))
    return u, v0, r, vn, sv


def _sgu_fwd(a, gn, wm, bias, name):
    s = a.shape[0]

    def body(a_ref, gn_ref, wm_ref, b_ref, y_ref):
        u, _, _, _, sv = _sgu_core(a_ref[...], gn_ref[...], wm_ref, b_ref[...])
        y_ref[...] = u * sv

    return _pcall(
        body, name=name, grid=(s // CHUNK,),
        in_specs=[pl.BlockSpec((CHUNK, 2 * W_A), lambda i: (i, 0)), pl.BlockSpec((1, W_A), lambda i: (0, 0)),
                  pl.BlockSpec((4, CHUNK, CHUNK), lambda i: (0, 0, 0)), pl.BlockSpec((CHUNK, W_A), lambda i: (0, 0))],
        out_specs=pl.BlockSpec((CHUNK, W_A), lambda i: (i, 0)),
        out_shape=jax.ShapeDtypeStruct((s, W_A), F32),
        compiler_params=_params(("parallel",)),
    )(a, gn, wm, bias)


def _sgu_bwd(a, dy, gn, wm, bias, name):
    s = a.shape[0]

    def body(a_ref, dy_ref, gn_ref, wm_ref, b_ref, da_ref, dwm_ref, db_ref, dgn_ref):
        @pl.when(pl.program_id(0) == 0)
        def _():
            dwm_ref[...] = jnp.zeros_like(dwm_ref)
            db_ref[...] = jnp.zeros_like(db_ref)
            dgn_ref[...] = jnp.zeros_like(dgn_ref)

        av, gnv, dyv = a_ref[...], gn_ref[...], dy_ref[...]
        u, v0, r, vn, sv = _sgu_core(av, gnv, wm_ref, b_ref[...])
        du = dyv * sv
        ds = dyv * u
        db_ref[...] += ds
        tril = _iota((CHUNK, CHUNK), 1) <= _iota((CHUNK, CHUNK), 0)
        dvn = jnp.zeros_like(vn)
        for h in range(W_A // HEAD_DIM):
            dsm = jnp.where(_lane_group_mask(W_A, h), ds, 0.0)
            dwm_ref[h] += jnp.where(tril, _dot_nt(dsm, vn), 0.0)
            dvn = dvn + _dot_tn(wm_ref[h], dsm)
        dgn_ref[...] += jnp.sum(dvn * v0 * r, axis=0, keepdims=True)
        dvg = dvn * gnv
        m2 = _dot_sel(dvg * v0, _group_mat(W_A)) * (1.0 / HEAD_DIM)
        dv0 = r * dvg - v0 * (r * r * r) * m2
        gp = _gelu_grad(av)
        da_ref[:, 0:W_A] = du * gp[:, 0:W_A]
        da_ref[:, W_A:2 * W_A] = dv0 * gp[:, W_A:2 * W_A]

    return _pcall(
        body, name=name, grid=(s // CHUNK,),
        in_specs=[pl.BlockSpec((CHUNK, 2 * W_A), lambda i: (i, 0)), pl.BlockSpec((CHUNK, W_A), lambda i: (i, 0)),
                  pl.BlockSpec((1, W_A), lambda i: (0, 0)), pl.BlockSpec((4, CHUNK, CHUNK), lambda i: (0, 0, 0)),
                  pl.BlockSpec((CHUNK, W_A), lambda i: (0, 0))],
        out_specs=[pl.BlockSpec((CHUNK, 2 * W_A), lambda i: (i, 0)), pl.BlockSpec((4, CHUNK, CHUNK), lambda i: (0, 0, 0)),
                   pl.BlockSpec((CHUNK, W_A), lambda i: (0, 0)), pl.BlockSpec((1, W_A), lambda i: (0, 0))],
        out_shape=[jax.ShapeDtypeStruct((s, 2 * W_A), F32), jax.ShapeDtypeStruct((4, CHUNK, CHUNK), F32),
                   jax.ShapeDtypeStruct((CHUNK, W_A), F32), jax.ShapeDtypeStruct((1, W_A), F32)],
        compiler_params=_params(("arbitrary",)),
    )(a, dy, gn, wm, bias)


KBLK = 128
ROW_CHUNK = 64
FWD_UNROLL = 4
BWD_UNROLL = 4
MASKED_SCORE = -1e30


def _attn_fwd(qkv, name, ride=None):
    s = qkv.shape[0]
    tq = _tile(s, 256, KBLK)
    npairs = W_B // LANES
    assert s // KBLK <= LANES
    rides = [] if ride is None else [ride]

    def body(q_ref, k_ref, v_ref, o_ref, cm_ref, z_ref, zw_ref, sums_ref, carry_ref, rs_ref, lb_ref, a_ref):
        i = pl.program_id(1)
        q = q_ref[...]
        lane = _iota((1, LANES), 1)
        lane_lo = lane < HEAD_DIM
        hmask = (lane_lo, jnp.logical_not(lane_lo))
        tri = (_iota((KBLK, KBLK), 0) >= _iota((KBLK, KBLK), 1)).astype(BF16)
        dmat = _iota((tq, KBLK), 1) - (_iota((tq, KBLK), 0) + i * tq)
        chunks = [slice(r, r + ROW_CHUNK) for r in range(0, tq, ROW_CHUNK)]
        heads = [slice(hh * LANES, (hh + 1) * LANES) for hh in range(2)]
        nk = (i + 1) * (tq // KBLK)

        cm_ref[...] = jnp.zeros_like(cm_ref)

        def before(b):
            return jnp.where((b >= 0) & (b < nk), -b * KBLK, jnp.iinfo(jnp.int32).min)

        def per_head(block):
            return jnp.concatenate([jnp.where(m, block, jnp.zeros_like(block)) for m in hmask], axis=0)

        def scores(b, p):
            ks = pl.multiple_of(jnp.clip(b, 0, nk - 1) * KBLK, KBLK)
            z_ref[p] = _dot_nt(q, per_head(k_ref[pl.ds(ks, KBLK), :]))

        def logs(b, p):
            t = before(b)
            for hh in range(2):
                for rows in chunks:
                    z = jnp.where(dmat[rows] < t, z_ref[p, rows, heads[hh]], MASKED_SCORE)
                    zw_ref[p, rows, heads[hh]] = z
                    l = -(jnp.maximum(z, 0.0) + jnp.log(1.0 + jnp.exp(-jnp.abs(z))))
                    lb_ref[p, hh, rows, :] = l.astype(BF16)
                    rs_ref[p, hh, rows, :] = jnp.zeros((ROW_CHUNK, KBLK), F32) + jnp.sum(l, axis=1, keepdims=True)

        def sums(b, p):
            for hh in range(2):
                sums_ref[p, hh] = jnp.dot(lb_ref[p, hh], tri, preferred_element_type=F32)

        def weights(b, p):
            pick = lane == b
            for hh in range(2):
                for rows in chunks:
                    c = carry_ref[hh, rows, :]
                    arg = zw_ref[p, rows, heads[hh]] + c + sums_ref[p, hh, rows, :]
                    a_ref[p, rows, heads[hh]] = jnp.exp(arg).astype(_MXU)
                    cm_ref[rows, heads[hh]] = jnp.where(pick, c, cm_ref[rows, heads[hh]])
                    carry_ref[hh, rows, :] = c + rs_ref[p, hh, rows, :]

        def out(b, p, acc):
            ks = pl.multiple_of(jnp.minimum(b, nk - 1) * KBLK, KBLK)
            vb = v_ref[pl.ds(ks, KBLK), :]
            for hh in range(2):
                acc = acc + jnp.dot(a_ref[p, :, heads[hh]], jnp.where(hmask[hh], vb, jnp.zeros_like(vb)),
                                    preferred_element_type=F32)
            return acc

        trips = (nk + 4 + FWD_UNROLL - 1) // FWD_UNROLL

        def step(it, acc):
            for u in range(FWD_UNROLL):
                b = trips * FWD_UNROLL - 1 - (FWD_UNROLL * it + u)
                p = 1 - u % 2
                acc = out(b, p, acc)
                weights(b - 1, 1 - p)
                sums(b - 2, p)
                logs(b - 3, 1 - p)
                scores(b - 4, p)
            return acc

        z_ref[...] = jnp.zeros_like(z_ref)
        zw_ref[...] = jnp.full_like(zw_ref, MASKED_SCORE)
        for ref in (sums_ref, carry_ref, rs_ref, lb_ref, a_ref):
            ref[...] = jnp.zeros_like(ref)
        o_ref[...] = lax.fori_loop(0, trips, step, q.astype(F32) * 0.0)

    scratch = [pltpu.VMEM((2, tq, 2 * KBLK), F32), pltpu.VMEM((2, tq, 2 * KBLK), F32),
               pltpu.VMEM((2, 2, tq, KBLK), F32), pltpu.VMEM((2, tq, KBLK), F32), pltpu.VMEM((2, 2, tq, KBLK), F32),
               pltpu.VMEM((2, 2, tq, KBLK), BF16), pltpu.VMEM((2, tq, 2 * KBLK), _MXU)]
    res = _pcall(
        _with_exchange(body, ride, 3, 2, len(scratch), (npairs - 1, s // tq - 1)), name=name, grid=(npairs, s // tq),
        in_specs=[pl.BlockSpec((tq, LANES), lambda p, i: (i, p)),
                  pl.BlockSpec((s, LANES), lambda p, i: (0, npairs + p)),
                  pl.BlockSpec((s, LANES), lambda p, i: (0, 2 * npairs + p))] + [sp for r in rides for sp in r.specs],
        out_specs=[pl.BlockSpec((tq, LANES), lambda p, i: (i, p)), pl.BlockSpec((tq, 2 * LANES), lambda p, i: (i, p))]
                  + [sp for r in rides for sp in r.specs],
        out_shape=[jax.ShapeDtypeStruct((s, W_B), F32), jax.ShapeDtypeStruct((s, 2 * W_B), F32)]
                  + [sh for r in rides for sh in r.out_shapes],
        scratch_shapes=scratch + [sem for r in rides for sem in r.semaphores],
        compiler_params=_params(("arbitrary", "arbitrary")),
    )(qkv, qkv, qkv, *[a for r in rides for a in r.arrs])
    return res[0], res[1], list(res[2:])


def _attn_bwd(qkv, cm, do, name, ride=None):
    s = qkv.shape[0]
    tq = _tile(s, 256, KBLK)
    npairs = W_B // LANES
    rides = [] if ride is None else [ride]

    nq = s // tq
    per_tile = tq // KBLK
    assert nq % 2 == 0 and per_tile % 2 == 0
    ntot = (nq + 1) * per_tile
    assert (ntot + 6) % BWD_UNROLL == 0

    def body(qa_ref, k_ref, v_ref, cma_ref, doa_ref, qb_ref, cmb_ref, dob_ref, dqa_ref, dqb_ref, dk_ref, dv_ref,
             z_ref, zw_ref, da_ref, g_ref, sig_ref, cum_ref, gp_ref, gcarry_ref, lb_ref, gb_ref, a_ref, dz_ref,
             dkt_ref, dvt_ref, q_st, do_st, qt_st, dot_st, cm_st):
        i = pl.program_id(1)

        @pl.when(i == 0)
        def _():
            dkt_ref[...] = jnp.zeros_like(dkt_ref)
            dvt_ref[...] = jnp.zeros_like(dvt_ref)

        tiles = (i, nq - 1 - i)
        nk_a = (i + 1) * per_tile
        for t, (q_in, do_in, cm_in) in enumerate(((qa_ref, doa_ref, cma_ref), (qb_ref, dob_ref, cmb_ref))):
            q_st[t] = q_in[...]
            do_st[t] = do_in[...].astype(_MXU)
            qt_st[t] = q_in[...].astype(F32).T.astype(_MXU)
            dot_st[t] = do_in[...].T.astype(_MXU)
            cm_st[t] = cm_in[...]

        lane = _iota((1, LANES), 1)
        lane_lo = lane < HEAD_DIM
        hmask = (lane_lo, jnp.logical_not(lane_lo))
        tri = (_iota((KBLK, KBLK), 0) >= _iota((KBLK, KBLK), 1)).astype(BF16)
        prefix = (_iota((KBLK, KBLK), 0) <= _iota((KBLK, KBLK), 1)).astype(BF16)
        dmat = _iota((tq, KBLK), 1) - _iota((tq, KBLK), 0)
        chunks = [slice(r, r + ROW_CHUNK) for r in range(0, tq, ROW_CHUNK)]
        heads = [slice(hh * LANES, (hh + 1) * LANES) for hh in range(2)]

        def locate(v):
            second = v >= nk_a
            return ((v >= 0) & (v < ntot), second.astype(jnp.int32), jnp.where(second, tiles[1], tiles[0]),
                    jnp.where(second, v - nk_a, v))

        def before(v):
            valid, _, tile, b = locate(v)
            return jnp.where(valid, tile * tq - b * KBLK, jnp.iinfo(jnp.int32).min)

        def key_block(v):
            return jnp.clip(locate(v)[3], 0, nblk - 1)

        def block_rows(v):
            return pl.ds(pl.multiple_of(key_block(v) * KBLK, KBLK), KBLK)

        def per_head(block):
            return jnp.concatenate([jnp.where(m, block, jnp.zeros_like(block)) for m in hmask], axis=0)

        feature_lo = _iota((LANES, KBLK), 0) < HEAD_DIM

        def own_features(side_by_side):
            return jnp.where(feature_lo, side_by_side[:, 0:KBLK], side_by_side[:, KBLK:2 * KBLK])

        def m1(b, p):
            z_ref[p] = _dot_nt(q_st[locate(b)[1]], per_head(k_ref[block_rows(b), :]))

        def v1(b, p):
            t = before(b)
            for hh in range(2):
                for rows in chunks:
                    z = jnp.where(dmat[rows] < t, z_ref[p, rows, heads[hh]], MASKED_SCORE)
                    zw_ref[p, rows, heads[hh]] = z
                    l = -(jnp.maximum(z, 0.0) + jnp.log(1.0 + jnp.exp(-jnp.abs(z))))
                    lb_ref[p, hh, rows, :] = l.astype(BF16)

        def m2(b, p):
            for hh in range(2):
                cum_ref[p, hh] = jnp.dot(lb_ref[p, hh], tri, preferred_element_type=F32)
            da_ref[p] = _dot_nt(do_st[locate(b)[1]], per_head(v_ref[block_rows(b), :]))

        def v2(b, p):
            valid, which, _, blk = locate(b)
            pick = lane == jnp.where(valid, blk, -1)
            for hh in range(2):
                for rows in chunks:
                    c = jnp.sum(jnp.where(pick, cm_st[which, rows, heads[hh]], 0.0), axis=1, keepdims=True)
                    z = zw_ref[p, rows, heads[hh]]
                    a = jnp.exp(z + c + cum_ref[p, hh, rows, :])
                    g = a * da_ref[p, rows, heads[hh]]
                    a_ref[p, rows, heads[hh]] = a.astype(_MXU)
                    g_ref[p, rows, heads[hh]] = g
                    gb_ref[p, hh, rows, :] = g.astype(BF16)
                    sig_ref[p, rows, heads[hh]] = jax.nn.sigmoid(z)

        def m3(b, p):
            for hh in range(2):
                gp_ref[p, hh] = jnp.dot(gb_ref[p, hh], prefix, preferred_element_type=F32)
            dvt_ref[key_block(b)] += own_features(jnp.dot(dot_st[locate(b)[1]], a_ref[p], preferred_element_type=F32))

        def v3(b, p):
            restart = b == nk_a
            for hh in range(2):
                for rows in chunks:
                    gc = jnp.where(restart, 0.0, gcarry_ref[hh, rows, :])
                    g = g_ref[p, rows, heads[hh]]
                    dz = g - sig_ref[p, rows, heads[hh]] * (gc + gp_ref[p, hh, rows, :])
                    dz_ref[p, rows, heads[hh]] = dz.astype(_MXU)
                    gcarry_ref[hh, rows, :] = gc + jnp.sum(g, axis=1, keepdims=True)

        def m4(b, p, dqs):
            which = locate(b)[1]
            kb = k_ref[block_rows(b), :]
            part = None
            for hh in range(2):
                d = jnp.dot(dz_ref[p, :, heads[hh]], jnp.where(hmask[hh], kb, jnp.zeros_like(kb)),
                            preferred_element_type=F32)
                part = d if part is None else part + d
            dkt_ref[key_block(b)] += own_features(jnp.dot(qt_st[which], dz_ref[p], preferred_element_type=F32))
            return dqs[0] + jnp.where(which == 0, part, 0.0), dqs[1] + jnp.where(which == 1, part, 0.0)

        def step(it, dqs):
            for u in range(BWD_UNROLL):
                j = BWD_UNROLL * it + u
                p = u % 2
                dqs = m4(j - 6, p, dqs)
                m3(j - 4, p)
                m2(j - 2, p)
                m1(j, p)
                v3(j - 5, 1 - p)
                v2(j - 3, 1 - p)
                v1(j - 1, 1 - p)
            return dqs

        zw_ref[...] = jnp.full_like(zw_ref, MASKED_SCORE)
        for ref in (z_ref, da_ref, g_ref, sig_ref, cum_ref, gp_ref, gcarry_ref, lb_ref, gb_ref, a_ref, dz_ref):
            ref[...] = jnp.zeros_like(ref)
        zero = doa_ref[...] * 0.0
        dq_a, dq_b = lax.fori_loop(0, (ntot + 6) // BWD_UNROLL, step, (zero, zero))
        dqa_ref[...] = dq_a * 0.125
        dqb_ref[...] = dq_b * 0.125

        @pl.when(i == nq // 2 - 1)
        def _():
            def untranspose(blk, carry):
                rows = pl.ds(pl.multiple_of(blk * KBLK, KBLK), KBLK)
                dk_ref[rows, :] = dkt_ref[blk].T
                dv_ref[rows, :] = dvt_ref[blk].T
                return carry
            lax.fori_loop(0, nblk, untranspose, 0)

    first = lambda width: pl.BlockSpec((tq, width), lambda p, i: (i, p))
    second = lambda width: pl.BlockSpec((tq, width), lambda p, i: (nq - 1 - i, p))
    second_out = pl.BlockSpec((tq, LANES), lambda p, i: (nq // 2 - 1 - i, p))
    nblk = s // KBLK
    full = pl.BlockSpec((s, LANES), lambda p, i: (0, p))
    scratch = ([pltpu.VMEM((2, tq, 2 * KBLK), F32)] * 5 + [pltpu.VMEM((2, 2, tq, KBLK), F32),
               pltpu.VMEM((2, 2, tq, KBLK), F32), pltpu.VMEM((2, tq, KBLK), F32)]
               + [pltpu.VMEM((2, 2, tq, KBLK), BF16)] * 2 + [pltpu.VMEM((2, tq, 2 * KBLK), _MXU)] * 2
               + [pltpu.VMEM((nblk, LANES, KBLK), F32)] * 2
               + [pltpu.VMEM((2, tq, LANES), _MXU)] * 2 + [pltpu.VMEM((2, LANES, tq), _MXU)] * 2
               + [pltpu.VMEM((2, tq, 2 * LANES), F32)])
    res = _pcall(
        _with_exchange(body, ride, 8, 4, len(scratch), (npairs - 1, nq // 2 - 1)), name=name, grid=(npairs, nq // 2),
        in_specs=[first(LANES), pl.BlockSpec((s, LANES), lambda p, i: (0, npairs + p)),
                  pl.BlockSpec((s, LANES), lambda p, i: (0, 2 * npairs + p)), first(2 * LANES), first(LANES),
                  second(LANES), second(2 * LANES), second(LANES)] + [sp for r in rides for sp in r.specs],
        out_specs=[first(LANES), second_out, full, full] + [sp for r in rides for sp in r.specs],
        out_shape=[jax.ShapeDtypeStruct((s // 2, W_B), F32)] * 2 + [jax.ShapeDtypeStruct((s, W_B), F32)] * 2
                  + [sh for r in rides for sh in r.out_shapes],
        scratch_shapes=scratch + [sem for r in rides for sem in r.semaphores],
        compiler_params=_params(("arbitrary", "arbitrary")),
    )(qkv, qkv, qkv, cm, do, qkv, cm, do, *[a for r in rides for a in r.arrs])
    return jnp.concatenate([res[0], res[1]], axis=0), res[2], res[3], list(res[4:])


POOL_HALO = 128


def _pool_window_lane():
    lane = _iota((1, W_C), 1)
    w = jnp.where(lane < 64, POOL_WINDOWS[0], jnp.where(lane < 128, POOL_WINDOWS[1],
                  jnp.where(lane < 192, POOL_WINDOWS[2], POOL_WINDOWS[3])))
    return w.astype(F32)


def _pool_count(tm, i):
    pos = (_iota((tm, W_C), 0) + (i * tm + 1)).astype(F32)
    return jnp.minimum(pos, _pool_window_lane())


def _pool_centered(prev, cur, cnt):
    tm = cur.shape[0]
    xx = jnp.concatenate([prev, cur], axis=0)
    hi, lo = _split(xx)
    t = _iota((tm, tm + POOL_HALO), 0)
    cc = _iota((tm, tm + POOL_HALO), 1) - POOL_HALO
    wsum = jnp.zeros_like(cur)
    for g, w in enumerate(POOL_WINDOWS):
        band = ((cc <= t) & (cc > t - w)).astype(BF16)
        mg = _lane_group_mask(W_C, g)
        wsum += jnp.dot(band, jnp.where(mg, hi, jnp.zeros_like(hi)), preferred_element_type=F32)
        wsum += jnp.dot(band, jnp.where(mg, lo, jnp.zeros_like(lo)), preferred_element_type=F32)
    return wsum / cnt - cur


def _pool_fwd(p, wbd, sc, name):
    s = p.shape[0]
    tm = _tile(s, 256, POOL_HALO)
    r = tm // POOL_HALO

    def body(pp_ref, p_ref, w_ref, sc_ref, y_ref):
        i = pl.program_id(0)
        prev = jnp.where(i > 0, pp_ref[...], 0.0)
        d = _pool_centered(prev, p_ref[...], _pool_count(tm, i))
        y_ref[...] = _dot(d, w_ref[...]) * sc_ref[...]

    return _pcall(
        body, name=name, grid=(s // tm,),
        in_specs=[pl.BlockSpec((POOL_HALO, W_C), lambda i: (jnp.maximum(i * r - 1, 0), 0)),
                  pl.BlockSpec((tm, W_C), lambda i: (i, 0)), pl.BlockSpec((W_C, W_C), lambda i: (0, 0)),
                  pl.BlockSpec((1, W_C), lambda i: (0, 0))],
        out_specs=pl.BlockSpec((tm, W_C), lambda i: (i, 0)),
        out_shape=jax.ShapeDtypeStruct((s, W_C), F32),
        compiler_params=_params(("parallel",)),
    )(p, p, wbd, sc)


def _pool_bwd(p, dy, wbd, wbdt, sc, name):
    s = p.shape[0]
    tm = _tile(s, 256, POOL_HALO)
    r = tm // POOL_HALO
    nt = s // tm

    def body(pp_ref, p_ref, dy_ref, dyn_ref, w_ref, wt_ref, sc_ref, dp_ref, dw_ref, dsc_ref):
        i = pl.program_id(0)

        @pl.when(i == 0)
        def _():
            dw_ref[...] = jnp.zeros_like(dw_ref)
            dsc_ref[...] = jnp.zeros_like(dsc_ref)

        scv = sc_ref[...]
        cnt = _pool_count(tm, i)
        prev = jnp.where(i > 0, pp_ref[...], 0.0)
        d = _pool_centered(prev, p_ref[...], cnt)
        e = _dot(d, w_ref[...])
        dyv = dy_ref[...]
        de = dyv * scv
        dsc_ref[...] += jnp.sum(dyv * e, axis=0, keepdims=True)
        dw_ref[...] += _dot_tn(d, de)
        dd = _dot(de, wt_ref[...])
        ddn = jnp.where(i < nt - 1, _dot(dyn_ref[...] * scv, wt_ref[...]), 0.0)
        yy = jnp.concatenate([dd / cnt, ddn / _pool_window_lane()], axis=0)
        hi, lo = _split(yy)
        t = _iota((tm, tm + POOL_HALO), 0)
        cc = _iota((tm, tm + POOL_HALO), 1)
        acc = jnp.zeros_like(dd)
        for g, w in enumerate(POOL_WINDOWS):
            band = ((cc >= t) & (cc < t + w)).astype(BF16)
            mg = _lane_group_mask(W_C, g)
            acc += jnp.dot(band, jnp.where(mg, hi, jnp.zeros_like(hi)), preferred_element_type=F32)
            acc += jnp.dot(band, jnp.where(mg, lo, jnp.zeros_like(lo)), preferred_element_type=F32)
        dp_ref[...] = acc - dd

    tile = pl.BlockSpec((tm, W_C), lambda i: (i, 0))
    mat = pl.BlockSpec((W_C, W_C), lambda i: (0, 0))
    vec = pl.BlockSpec((1, W_C), lambda i: (0, 0))
    return _pcall(
        body, name=name, grid=(nt,),
        in_specs=[pl.BlockSpec((POOL_HALO, W_C), lambda i: (jnp.maximum(i * r - 1, 0), 0)), tile, tile,
                  pl.BlockSpec((POOL_HALO, W_C), lambda i: (jnp.minimum((i + 1) * r, s // POOL_HALO - 1), 0)),
                  mat, mat, vec],
        out_specs=[tile, mat, vec],
        out_shape=[jax.ShapeDtypeStruct((s, W_C), F32), jax.ShapeDtypeStruct((W_C, W_C), F32),
                   jax.ShapeDtypeStruct((1, W_C), F32)],
        compiler_params=_params(("arbitrary",)),
    )(p, p, dy, dy, wbd, wbdt, sc)


def _mix_cols(ya_ref, yb_ref, yc_ref, cb):
    if cb < 2:
        return ya_ref[:, cb * LANES:(cb + 1) * LANES]
    if cb < 6:
        return yb_ref[:, (cb - 2) * LANES:(cb - 1) * LANES]
    return yc_ref[:, (cb - 6) * LANES:(cb - 5) * LANES]


def _mix_fwd(ya, yb, yc, g, wo, x, name):
    s = x.shape[0]
    tm = _tile(s, 256)

    def body(ya_ref, yb_ref, yc_ref, g_ref, w_ref, x_ref, o_ref, yn_ref):
        sel = _group_mat(LANES)
        for cb in range(D_MODEL // LANES):
            y = _mix_cols(ya_ref, yb_ref, yc_ref, cb)
            r = lax.rsqrt(_dot_sel(y * y, sel) * (1.0 / HEAD_DIM) + EPS)
            yn_ref[:, cb * LANES:(cb + 1) * LANES] = (y * r * g_ref[:, cb * LANES:(cb + 1) * LANES]).astype(_MXU)
        o_ref[...] = x_ref[...] + jnp.dot(yn_ref[...], w_ref[...], preferred_element_type=F32)

    row = lambda n: pl.BlockSpec((tm, n), lambda i: (i, 0))
    return _pcall(
        body, name=name, grid=(s // tm,),
        in_specs=[row(W_A), row(W_B), row(W_C), pl.BlockSpec((1, D_MODEL), lambda i: (0, 0)),
                  pl.BlockSpec((D_MODEL, D_MODEL), lambda i: (0, 0)), row(D_MODEL)],
        out_specs=[row(D_MODEL), row(D_MODEL)],
        out_shape=[jax.ShapeDtypeStruct((s, D_MODEL), F32), jax.ShapeDtypeStruct((s, D_MODEL), _MXU)],
        compiler_params=_params(("parallel",)),
    )(ya, yb, yc, g, wo, x)


def _mix_bwd(dx, ya, yb, yc, g, wot, name):
    s = dx.shape[0]
    tm = _tile(s, 256)

    def body(dx_ref, ya_ref, yb_ref, yc_ref, g_ref, wt_ref, dya_ref, dyb_ref, dyc_ref, dg_ref):
        @pl.when(pl.program_id(0) == 0)
        def _():
            dg_ref[...] = jnp.zeros_like(dg_ref)

        dyn = _dot(dx_ref[...], wt_ref[...])
        sel = _group_mat(LANES)
        for cb in range(D_MODEL // LANES):
            cols = slice(cb * LANES, (cb + 1) * LANES)
            y = _mix_cols(ya_ref, yb_ref, yc_ref, cb)
            r = lax.rsqrt(_dot_sel(y * y, sel) * (1.0 / HEAD_DIM) + EPS)
            dyc_ = dyn[:, cols]
            dyg = dyc_ * g_ref[:, cols]
            m2 = _dot_sel(dyg * y, sel) * (1.0 / HEAD_DIM)
            dy = r * dyg - y * (r * r * r) * m2
            dg_ref[:, cols] += jnp.sum(dyc_ * y * r, axis=0, keepdims=True)
            if cb < 2:
                dya_ref[:, cb * LANES:(cb + 1) * LANES] = dy
            elif cb < 6:
                dyb_ref[:, (cb - 2) * LANES:(cb - 1) * LANES] = dy
            else:
                dyc_ref[:, (cb - 6) * LANES:(cb - 5) * LANES] = dy

    row = lambda n: pl.BlockSpec((tm, n), lambda i: (i, 0))
    vec = pl.BlockSpec((1, D_MODEL), lambda i: (0, 0))
    return _pcall(
        body, name=name, grid=(s // tm,),
        in_specs=[row(D_MODEL), row(W_A), row(W_B), row(W_C), vec, pl.BlockSpec((D_MODEL, D_MODEL), lambda i: (0, 0))],
        out_specs=[row(W_A), row(W_B), row(W_C), vec],
        out_shape=[jax.ShapeDtypeStruct((s, W_A), F32), jax.ShapeDtypeStruct((s, W_B), F32),
                   jax.ShapeDtypeStruct((s, W_C), F32), jax.ShapeDtypeStruct((1, D_MODEL), F32)],
        compiler_params=_params(("arbitrary",)),
    )(dx, ya, yb, yc, g, wot)


FFN_CHUNK = 1408
FFN_ROWS = 256
N_CHUNKS = D_FF // FFN_CHUNK
CW_ROWS = 8


def _ffn_up_fwd(x, g, w, name):
    s = x.shape[0]
    n = w.shape[1]
    tm, tn = _tile(s, 512), _tile(n, 2816, LANES)

    def body(x_ref, g_ref, w_ref, z_ref, h_ref):
        @pl.when(pl.program_id(1) == 0)
        def _():
            xv = x_ref[...]
            h_ref[...] = (xv * _rms(xv) * g_ref[...]).astype(_MXU)
        z_ref[...] = jnp.dot(h_ref[...], w_ref[...], preferred_element_type=F32)

    return _pcall(
        body, name=name, grid=(s // tm, n // tn),
        in_specs=[pl.BlockSpec((tm, D_MODEL), lambda i, j: (i, 0)), pl.BlockSpec((1, D_MODEL), lambda i, j: (0, 0)),
                  pl.BlockSpec((D_MODEL, tn), lambda i, j: (0, j))],
        out_specs=[pl.BlockSpec((tm, tn), lambda i, j: (i, j)), pl.BlockSpec((tm, D_MODEL), lambda i, j: (i, 0))],
        out_shape=[jax.ShapeDtypeStruct((s, n), F32), jax.ShapeDtypeStruct((s, D_MODEL), _MXU)],
        compiler_params=_params(("parallel", "arbitrary")),
    )(x, g, w)


def _conv(cur, prev8, cw_ref):
    s1 = _shift_down(cur, prev8, 1)
    s2 = _shift_down(cur, prev8, 2)
    zc = cw_ref[3:4, :] + s2 * cw_ref[0:1, :]
    zc = zc + s1 * cw_ref[1:2, :]
    zc = zc + cur * cw_ref[2:3, :]
    return zc, s1, s2


def _halo_specs(tm, s):
    r = tm // SUBLANES
    prev = lambda off: pl.BlockSpec((SUBLANES, FFN_CHUNK), lambda i, j: (jnp.maximum(i * r - 1, 0), j + off))
    nxt = lambda off: pl.BlockSpec((SUBLANES, FFN_CHUNK), lambda i, j: (jnp.minimum((i + 1) * r, s // SUBLANES - 1), j + off))
    return prev, nxt


def _ffn_down_fwd(z, cw, wd, x, name):
    s = x.shape[0]
    tm = _tile(s, FFN_ROWS)
    prev, _ = _halo_specs(tm, s)

    def body(zg_ref, zu_ref, pg_ref, pu_ref, cg_ref, cu_ref, w_ref, x_ref, o_ref, act_ref, acc_ref):
        i, j = pl.program_id(0), pl.program_id(1)
        first = i > 0
        zg, _, _ = _conv(zg_ref[...], jnp.where(first, pg_ref[...], 0.0), cg_ref)
        zu, _, _ = _conv(zu_ref[...], jnp.where(first, pu_ref[...], 0.0), cu_ref)
        act = (zg * jax.nn.sigmoid(zg) * zu).astype(_MXU)
        act_ref[...] = act

        @pl.when(j == 0)
        def _():
            acc_ref[...] = x_ref[...]
        acc_ref[...] += jnp.dot(act, w_ref[...], preferred_element_type=F32)

        @pl.when(j == N_CHUNKS - 1)
        def _():
            o_ref[...] = acc_ref[...]

    zt = lambda off: pl.BlockSpec((tm, FFN_CHUNK), lambda i, j: (i, j + off))
    cwt = lambda off: pl.BlockSpec((CW_ROWS, FFN_CHUNK), lambda i, j: (0, j + off))
    return _pcall(
        body, name=name, grid=(s // tm, N_CHUNKS),
        in_specs=[zt(0), zt(N_CHUNKS), prev(0), prev(N_CHUNKS), cwt(0), cwt(N_CHUNKS),
                  pl.BlockSpec((FFN_CHUNK, D_MODEL), lambda i, j: (j, 0)), pl.BlockSpec((tm, D_MODEL), lambda i, j: (i, 0))],
        out_specs=[pl.BlockSpec((tm, D_MODEL), lambda i, j: (i, 0)), pl.BlockSpec((tm, FFN_CHUNK), lambda i, j: (i, j))],
        out_shape=[jax.ShapeDtypeStruct((s, D_MODEL), F32), jax.ShapeDtypeStruct((s, D_FF), _MXU)],
        scratch_shapes=[pltpu.VMEM((tm, D_MODEL), F32)],
        compiler_params=_params(("parallel", "arbitrary")),
    )(z, z, z, z, cw, cw, wd, x)


def _ffn_down_bwd(dx, z, cw, wdt, name):
    s = dx.shape[0]
    tm = _tile(s, FFN_ROWS)

    def body(dx_ref, zg_ref, zu_ref, pg_ref, pu_ref, cg_ref, cu_ref, wt_ref, dg_ref, du_ref, dcg_ref, dcu_ref):
        i = pl.program_id(1)
        first = i > 0

        @pl.when(i == 0)
        def _():
            dcg_ref[...] = jnp.zeros_like(dcg_ref)
            dcu_ref[...] = jnp.zeros_like(dcu_ref)

        dact = _dot(dx_ref[...], wt_ref[...])
        zg, g1, g2 = _conv(zg_ref[...], jnp.where(first, pg_ref[...], 0.0), cg_ref)
        zu, u1, u2 = _conv(zu_ref[...], jnp.where(first, pu_ref[...], 0.0), cu_ref)
        sg = jax.nn.sigmoid(zg)
        silu = zg * sg
        dzu = dact * silu
        dzg = dact * zu * (sg * (1.0 + zg * (1.0 - sg)))
        dg_ref[...] = dzg
        du_ref[...] = dzu
        for ref, dzc, cur, s1, s2 in ((dcg_ref, dzg, zg_ref[...], g1, g2), (dcu_ref, dzu, zu_ref[...], u1, u2)):
            ref[0:1, :] += jnp.sum(dzc * s2, axis=0, keepdims=True)
            ref[1:2, :] += jnp.sum(dzc * s1, axis=0, keepdims=True)
            ref[2:3, :] += jnp.sum(dzc * cur, axis=0, keepdims=True)
            ref[3:4, :] += jnp.sum(dzc, axis=0, keepdims=True)

    zt = lambda off: pl.BlockSpec((tm, FFN_CHUNK), lambda j, i: (i, j + off))
    r = tm // SUBLANES
    pv = lambda off: pl.BlockSpec((SUBLANES, FFN_CHUNK), lambda j, i: (jnp.maximum(i * r - 1, 0), j + off))
    cwt = lambda off: pl.BlockSpec((CW_ROWS, FFN_CHUNK), lambda j, i: (0, j + off))
    out_t = pl.BlockSpec((tm, FFN_CHUNK), lambda j, i: (i, j))
    dc_t = pl.BlockSpec((CW_ROWS, FFN_CHUNK), lambda j, i: (0, j))
    dzg, dzu, dcg, dcu = _pcall(
        body, name=name, grid=(N_CHUNKS, s // tm),
        in_specs=[pl.BlockSpec((tm, D_MODEL), lambda j, i: (i, 0)), zt(0), zt(N_CHUNKS), pv(0), pv(N_CHUNKS),
                  cwt(0), cwt(N_CHUNKS), pl.BlockSpec((D_MODEL, FFN_CHUNK), lambda j, i: (0, j))],
        out_specs=[out_t, out_t, dc_t, dc_t],
        out_shape=[jax.ShapeDtypeStruct((s, D_FF), F32), jax.ShapeDtypeStruct((s, D_FF), F32),
                   jax.ShapeDtypeStruct((CW_ROWS, D_FF), F32), jax.ShapeDtypeStruct((CW_ROWS, D_FF), F32)],
        compiler_params=_params(("parallel", "arbitrary")),
    )(dx, z, z, z, z, cw, cw, wdt)
    return dzg, dzu, jnp.concatenate([dcg, dcu], axis=1)


def _ffn_up_bwd(dzg, dzu, cw, wut, x, g, dres, name):
    s = x.shape[0]
    tm = _tile(s, FFN_ROWS)
    _, nxt = _halo_specs(tm, s)
    nt = s // tm

    def body(dg_ref, du_ref, ng_ref, nu_ref, cg_ref, cu_ref, wg_ref, wu_ref, x_ref, g_ref, dres_ref,
             dzg_ref, dzu_ref, dx_ref, dgn_ref, acc_ref):
        i, j = pl.program_id(0), pl.program_id(1)
        last = i < nt - 1

        def conv_bwd(cur, nxt8, cw_ref):
            up1 = _shift_up(cur, nxt8, 1)
            up2 = _shift_up(cur, nxt8, 2)
            return cur * cw_ref[2:3, :] + up1 * cw_ref[1:2, :] + up2 * cw_ref[0:1, :]

        dzg_ = conv_bwd(dg_ref[...], jnp.where(last, ng_ref[...], 0.0), cg_ref).astype(_MXU)
        dzu_ = conv_bwd(du_ref[...], jnp.where(last, nu_ref[...], 0.0), cu_ref).astype(_MXU)
        dzg_ref[...] = dzg_
        dzu_ref[...] = dzu_

        @pl.when(j == 0)
        def _():
            acc_ref[...] = jnp.zeros_like(acc_ref)
        acc_ref[...] += (jnp.dot(dzg_, wg_ref[...], preferred_element_type=F32)
                         + jnp.dot(dzu_, wu_ref[...], preferred_element_type=F32))

        @pl.when((i == 0) & (j == 0))
        def _():
            dgn_ref[...] = jnp.zeros_like(dgn_ref)

        @pl.when(j == N_CHUNKS - 1)
        def _():
            dx, dgn = _rms_bwd(x_ref[...], g_ref[...], acc_ref[...])
            dx_ref[...] = dres_ref[...] + dx
            dgn_ref[...] += dgn

    zt = pl.BlockSpec((tm, FFN_CHUNK), lambda i, j: (i, j))
    cwt = lambda off: pl.BlockSpec((CW_ROWS, FFN_CHUNK), lambda i, j: (0, j + off))
    wt = lambda off: pl.BlockSpec((FFN_CHUNK, D_MODEL), lambda i, j: (j + off, 0))
    row = pl.BlockSpec((tm, D_MODEL), lambda i, j: (i, 0))
    vec = pl.BlockSpec((1, D_MODEL), lambda i, j: (0, 0))
    return _pcall(
        body, name=name, grid=(nt, N_CHUNKS),
        in_specs=[zt, zt, nxt(0), nxt(0), cwt(0), cwt(N_CHUNKS), wt(0), wt(N_CHUNKS), row, vec, row],
        out_specs=[zt, zt, row, vec],
        out_shape=[jax.ShapeDtypeStruct((s, D_FF), _MXU), jax.ShapeDtypeStruct((s, D_FF), _MXU),
                   jax.ShapeDtypeStruct((s, D_MODEL), F32), jax.ShapeDtypeStruct((1, D_MODEL), F32)],
        scratch_shapes=[pltpu.VMEM((tm, D_MODEL), F32)],
        compiler_params=_params(("arbitrary", "arbitrary")),
    )(dzg, dzu, dzg, dzu, cw, cw, wut, wut, x, g, dres)


def _final_loss(x, g, tgt, name):
    s = x.shape[0]
    tm = _tile(s, 256)

    def body(x_ref, g_ref, t_ref, loss_ref, dx_ref, dg_ref):
        @pl.when(pl.program_id(0) == 0)
        def _():
            loss_ref[...] = jnp.zeros_like(loss_ref)
            dg_ref[...] = jnp.zeros_like(dg_ref)

        xv, gv = x_ref[...], g_ref[...]
        err = xv * _rms(xv) * gv - t_ref[...]
        per_tok = jnp.mean(err * err, axis=-1, keepdims=True)
        loss_ref[...] += 0.5 * jnp.sum(per_tok, axis=0, keepdims=True)
        dx, dg = _rms_bwd(xv, gv, err * (1.0 / D_MODEL))
        dx_ref[...] = dx
        dg_ref[...] += dg

    row = pl.BlockSpec((tm, D_MODEL), lambda i: (i, 0))
    vec = pl.BlockSpec((1, D_MODEL), lambda i: (0, 0))
    return _pcall(
        body, name=name, grid=(s // tm,),
        in_specs=[row, vec, row], out_specs=[pl.BlockSpec((1, 1), lambda i: (0, 0)), row, vec],
        out_shape=[jax.ShapeDtypeStruct((1, 1), F32), jax.ShapeDtypeStruct((s, D_MODEL), F32),
                   jax.ShapeDtypeStruct((1, D_MODEL), F32)],
        compiler_params=_params(("arbitrary",)),
    )(x, g, tgt)


def _adamw(parts, w, m, v, name):
    r, c = w.shape
    tr = _tile(r, 256)
    c1 = 1.0 - ADAM_B1 ** ADAM_STEP
    c2 = 1.0 - ADAM_B2 ** ADAM_STEP

    def body(p_ref, w_ref, m_ref, v_ref, g_ref, d_ref, mo_ref, vo_ref):
        g = p_ref[0].astype(F32)
        for i in range(1, N_DEV):
            g = g + p_ref[i].astype(F32)
        mn = ADAM_B1 * m_ref[...] + (1.0 - ADAM_B1) * g
        vn = ADAM_B2 * v_ref[...] + (1.0 - ADAM_B2) * (g * g)
        g_ref[...] = g
        mo_ref[...] = mn
        vo_ref[...] = vn
        d_ref[...] = -ADAM_LR * ((mn / c1) / (jnp.sqrt(vn / c2) + ADAM_EPS) + ADAM_WD * w_ref[...])

    t2 = pl.BlockSpec((tr, c), lambda i: (i, 0))
    return _pcall(
        body, name=name, grid=(r // tr,),
        in_specs=[pl.BlockSpec((N_DEV, tr, c), lambda i: (0, i, 0)), t2, t2, t2],
        out_specs=[t2] * 4, out_shape=[jax.ShapeDtypeStruct((r, c), F32)] * 4,
        compiler_params=_params(("parallel",)),
    )(parts, w, m, v)


SMALL = ("norm1_g", "sgu_norm_g", "sgu_w", "sgu_b", "pool_w", "pool_scale", "mix_norm_g", "norm2_g", "conv_b", "final_g")
SHARDED = ("w_in", "w_o", "w_up", "conv_w", "w_down")
ORDER = ("norm1_g", "w_in", "sgu_norm_g", "sgu_w", "sgu_b", "pool_w", "pool_scale", "mix_norm_g", "w_o", "norm2_g",
         "w_up", "conv_w", "conv_b", "w_down", "final_g")


def _pack(tree):
    return jnp.concatenate([tree[n].reshape(-1) for n in SMALL]).reshape(-1, LANES)


def _unpack(flat, like):
    out, off = {}, 0
    flat = flat.reshape(-1)
    for n in SMALL:
        size = math.prod(like[n].shape)
        out[n] = flat[off:off + size].reshape(like[n].shape)
        off += size
    return out


def _block_diag(pw):
    z = jnp.zeros((W_C, W_C), pw.dtype)
    for g in range(4):
        z = z.at[g * 64:(g + 1) * 64, g * 64:(g + 1) * 64].set(pw[g])
    return z


def kernel(x, norm1_g, w_in, sgu_norm_g, sgu_w, sgu_b, pool_w, pool_scale, mix_norm_g, w_o, norm2_g, w_up, conv_w, conv_b, w_down, final_g, loss_target, m_norm1_g, m_w_in, m_sgu_norm_g, m_sgu_w, m_sgu_b, m_pool_w, m_pool_scale, m_mix_norm_g, m_w_o, m_norm2_g, m_w_up, m_conv_w, m_conv_b, m_w_down, m_final_g, v_norm1_g, v_w_in, v_sgu_norm_g, v_sgu_w, v_sgu_b, v_pool_w, v_pool_scale, v_mix_norm_g, v_w_o, v_norm2_g, v_w_up, v_conv_w, v_conv_b, v_w_down, v_final_g):
    weights = dict(norm1_g=norm1_g, w_in=w_in, sgu_norm_g=sgu_norm_g, sgu_w=sgu_w, sgu_b=sgu_b, pool_w=pool_w,
                   pool_scale=pool_scale, mix_norm_g=mix_norm_g, w_o=w_o, norm2_g=norm2_g, w_up=w_up, conv_w=conv_w,
                   conv_b=conv_b, w_down=w_down, final_g=final_g)
    mom = dict(norm1_g=m_norm1_g, w_in=m_w_in, sgu_norm_g=m_sgu_norm_g, sgu_w=m_sgu_w, sgu_b=m_sgu_b, pool_w=m_pool_w,
               pool_scale=m_pool_scale, mix_norm_g=m_mix_norm_g, w_o=m_w_o, norm2_g=m_norm2_g, w_up=m_w_up,
               conv_w=m_conv_w, conv_b=m_conv_b, w_down=m_w_down, final_g=m_final_g)
    var = dict(norm1_g=v_norm1_g, w_in=v_w_in, sgu_norm_g=v_sgu_norm_g, sgu_w=v_sgu_w, sgu_b=v_sgu_b, pool_w=v_pool_w,
               pool_scale=v_pool_scale, mix_norm_g=v_mix_norm_g, w_o=v_w_o, norm2_g=v_norm2_g, w_up=v_w_up,
               conv_w=v_conv_w, conv_b=v_conv_b, w_down=v_w_down, final_g=v_final_g)
    depth = w_in.shape[0]
    s = x.shape[1]
    xs = x.reshape(s, D_MODEL)
    tgt = loss_target.reshape(s, D_MODEL)

    assert depth >= 2
    (g_in0,) = _exchange([w_in[0].astype(_MXU)], "gather_w_in0", False)
    w_in0 = jnp.transpose(g_in0, (1, 0, 2)).reshape(D_MODEL, IN_COLS)
    gather_rest = _Exchange([w_in[1:].astype(_MXU), w_o.astype(_MXU), w_up.astype(_MXU), conv_w, w_down.astype(_MXU)], False)

    tril = jnp.tril(jnp.ones((CHUNK, CHUNK), bool))
    layers = []
    for l in range(depth):
        wbd = _block_diag(pool_w[l])
        layers.append(dict(
            g1=norm1_g[l][None], gn=sgu_norm_g[l][None], wm=jnp.where(tril[None], sgu_w[l], 0.0).astype(_MXU),
            bias=jnp.repeat(sgu_b[l].T, HEAD_DIM, axis=1),
            wbd=wbd.astype(_MXU), wbd_t=wbd.T.astype(_MXU), sc=pool_scale[l][None],
            gmix=mix_norm_g[l][None], g2=norm2_g[l][None]))
    layers[0].update(w_in=w_in0, w_in_t=w_in0.T)

    def place_gathered(g_in, g_o, g_up, g_cw, g_dn):
        full_in = jnp.transpose(g_in, (1, 2, 0, 3)).reshape(depth - 1, D_MODEL, IN_COLS)
        full_o = jnp.transpose(g_o, (1, 0, 2, 3)).reshape(depth, D_MODEL, D_MODEL)
        full_up = jnp.transpose(g_up, (1, 2, 0, 3)).reshape(depth, D_MODEL, 2 * D_FF)
        full_cw = jnp.transpose(g_cw, (1, 2, 0, 3)).reshape(depth, 3, 2 * D_FF)
        full_dn = jnp.transpose(g_dn, (1, 0, 2, 3)).reshape(depth, D_FF, D_MODEL)
        for l in range(depth):
            if l > 0:
                layers[l].update(w_in=full_in[l - 1], w_in_t=full_in[l - 1].T)
            layers[l].update(
                w_o=full_o[l], w_o_t=full_o[l].T, w_up=full_up[l], w_up_t=full_up[l].T,
                cw=jnp.concatenate([full_cw[l], conv_b[l][None], jnp.zeros((CW_ROWS - 4, 2 * D_FF), F32)], axis=0),
                w_dn=full_dn[l], w_dn_t=full_dn[l].T)

    saved = []
    cur = xs
    for l, p in enumerate(layers):
        a_in, qkv, p_in, h1 = _inproj_fwd(cur, p["g1"], p["w_in"], f"inproj_fwd{l}")
        y_a = _sgu_fwd(a_in, p["gn"], p["wm"], p["bias"], f"sgu_fwd{l}")
        y_b, cm, gathered_w = _attn_fwd(qkv, f"attn_fwd{l}", gather_rest if l == 0 else None)
        if l == 0:
            place_gathered(*gathered_w)
        y_c = _pool_fwd(p_in, p["wbd"], p["sc"], f"pool_fwd{l}")
        x_mid, yn = _mix_fwd(y_a, y_b, y_c, p["gmix"], p["w_o"], cur, f"mix_fwd{l}")
        z, h2 = _ffn_up_fwd(x_mid, p["g2"], p["w_up"], f"ffn_up_fwd{l}")
        x_out, act = _ffn_down_fwd(z, p["cw"], p["w_dn"], x_mid, f"ffn_down_fwd{l}")
        saved.append(dict(x_in=cur, a_in=a_in, qkv=qkv, p_in=p_in, h1=h1, y_a=y_a, y_b=y_b, cm=cm, y_c=y_c, x_mid=x_mid,
                          yn=yn, z=z, h2=h2, act=act))
        cur = x_out
    loss_part, dx, dg_final = _final_loss(cur, final_g[None], tgt, "final_loss")

    small = {n: [None] * depth for n in SMALL if n != "final_g"}
    big = {n: [None] * depth for n in SHARDED}
    early = [(n, l) for n in SHARDED for l in range(depth) if (n, l) != ("w_in", 0)]
    wire = lambda n, t: t if n == "conv_w" else t.astype(GRAD_WIRE)
    for l in reversed(range(depth)):
        p, sv = layers[l], saved[l]
        dzg, dzu, dcw = _ffn_down_bwd(dx, sv["z"], p["cw"], p["w_dn_t"], f"ffn_down_bwd{l}")
        big["w_down"][l] = _mm_tn(sv["act"], dx, f"dw_down{l}").reshape(N_DEV, D_FF // N_DEV, D_MODEL)
        dzg_b, dzu_b, dx_mid, dg2 = _ffn_up_bwd(dzg, dzu, p["cw"], p["w_up_t"], sv["x_mid"], p["g2"], dx, f"ffn_up_bwd{l}")
        dw_up = jnp.stack([_mm_tn(sv["h2"], dzg_b, f"dw_up_g{l}"), _mm_tn(sv["h2"], dzu_b, f"dw_up_u{l}")])
        big["w_up"][l] = jnp.transpose(dw_up.reshape(2, D_MODEL, N_DEV // 2, 2 * D_FF // N_DEV), (0, 2, 1, 3)).reshape(
            N_DEV, D_MODEL, 2 * D_FF // N_DEV)
        big["conv_w"][l] = jnp.transpose(dcw[0:3].reshape(3, N_DEV, 2 * D_FF // N_DEV), (1, 0, 2))
        small["conv_b"][l] = dcw[3]
        small["norm2_g"][l] = dg2[0]
        dya, dyb, dyc, dgmix = _mix_bwd(dx_mid, sv["y_a"], sv["y_b"], sv["y_c"], p["gmix"], p["w_o_t"], f"mix_bwd{l}")
        small["mix_norm_g"][l] = dgmix[0]
        big["w_o"][l] = _mm_tn(sv["yn"], dx_mid, f"dw_o{l}").reshape(N_DEV, D_MODEL // N_DEV, D_MODEL)
        dp, dwbd, dsc = _pool_bwd(sv["p_in"], dyc, p["wbd"], p["wbd_t"], p["sc"], f"pool_bwd{l}")
        small["pool_w"][l] = jnp.stack([dwbd[g * 64:(g + 1) * 64, g * 64:(g + 1) * 64] for g in range(4)])
        small["pool_scale"][l] = dsc[0]
        scatter_early = _Exchange([wire(n, big[n][ll]) for n, ll in early], True) if l == 0 else None
        dq, dk, dv, recv_early = _attn_bwd(sv["qkv"], sv["cm"], dyb, f"attn_bwd{l}", scatter_early)
        if l == 0:
            recv_early_all = recv_early
        da, dwm, dbias, dgn = _sgu_bwd(sv["a_in"], dya, p["gn"], p["wm"], p["bias"], f"sgu_bwd{l}")
        small["sgu_w"][l] = dwm
        small["sgu_b"][l] = jnp.sum(dbias.reshape(CHUNK, 4, HEAD_DIM), axis=-1).T
        small["sgu_norm_g"][l] = dgn[0]
        dx, dg1 = _inproj_bwd(da, dq, dk, dv, dp, p["w_in_t"], sv["x_in"], p["g1"], dx_mid, f"inproj_bwd{l}")
        small["norm1_g"][l] = dg1[0]
        pieces = (da, dq, dk, dv, dp)
        dw_in = jnp.concatenate([_mm_tn(sv["h1"], t, f"dw_in{i}_{l}") for i, t in enumerate(pieces)], axis=1)
        big["w_in"][l] = jnp.transpose(dw_in.reshape(D_MODEL, N_DEV, IN_COLS // N_DEV), (1, 0, 2))

    part = {n: jnp.stack(small[n]) for n in small}
    part["final_g"] = dg_final[0]
    packed = _pack(part)
    recv_in0, gathered = _exchange([wire("w_in", big["w_in"][0]), jnp.broadcast_to(packed, (N_DEV,) + packed.shape)],
                                   "scatter_last", True)
    recv = dict(zip(early, recv_early_all))
    recv[("w_in", 0)] = recv_in0

    out_g, out_d, out_m, out_v = {}, {}, {}, {}
    for n in SHARDED:
        res = [_adamw(recv[(n, l)], weights[n][l], mom[n][l], var[n][l], f"adamw_{n}{l}") for l in range(depth)]
        out_g[n], out_d[n], out_m[n], out_v[n] = (jnp.stack([r[i] for r in res]) for i in range(4))

    sg, sd, sm, sv_ = _adamw(gathered, _pack(weights), _pack(mom), _pack(var), "adamw_small")
    for tree, flat in ((out_g, sg), (out_d, sd), (out_m, sm), (out_v, sv_)):
        tree.update(_unpack(flat, weights))

    loss = lax.psum(loss_part[0, 0], ("x", "y", "c"))
    grad_x = dx.reshape(1, s, D_MODEL)
    return (loss, grad_x, *[out_g[n] for n in ORDER], *[out_d[n] for n in ORDER], *[out_m[n] for n in ORDER],
            *[out_v[n] for n in ORDER])
```

```python
import functools
import math

import jax
import jax.numpy as jnp
import numpy as np
from jax import lax
from jax.experimental import pallas as pl
from jax.experimental.pallas import tpu as pltpu

F32 = jnp.float32
BF16 = jnp.bfloat16
_MXU = jnp.bfloat16
GRAD_WIRE = jnp.bfloat16

D_MODEL = 1024
W_A = 256
W_B = 512
W_C = 256
HEAD_DIM = 64
IN_COLS = 2 * W_A + 3 * W_B + W_C
D_FF = 2816
CHUNK = 128
POOL_WINDOWS = (2, 4, 8, 16)
EPS = 1e-6
N_DEV = 8
LANES = 128
SUBLANES = 8
VMEM_LIMIT = 48 * 1024 * 1024
ROW_TILE = 512

ADAM_LR = 0.001
ADAM_B1 = 0.9
ADAM_B2 = 0.999
ADAM_EPS = 1e-08
ADAM_WD = 0.01
ADAM_STEP = 10

INV_SQRT2 = 1.0 / math.sqrt(2.0)
INV_SQRT_2PI = 1.0 / math.sqrt(2.0 * math.pi)


def _pcall(body, **kw):
    return pl.pallas_call(body, **kw)


def _params(dims=None):
    return pltpu.CompilerParams(dimension_semantics=dims, vmem_limit_bytes=VMEM_LIMIT)


def _tile(n, pref, mult=SUBLANES):
    t = min(n, pref) // mult * mult
    while t >= mult:
        if n % t == 0:
            return t
        t -= mult
    return n


def _iota(shape, dim):
    return lax.broadcasted_iota(jnp.int32, shape, dim)


def _dot(a, b):
    return jnp.dot(a.astype(_MXU), b.astype(_MXU), preferred_element_type=F32)


def _dot_nt(a, b):
    return lax.dot_general(a.astype(_MXU), b.astype(_MXU), (((1,), (1,)), ((), ())), preferred_element_type=F32)


def _dot_tn(a, b):
    return lax.dot_general(a.astype(_MXU), b.astype(_MXU), (((0,), (0,)), ((), ())), preferred_element_type=F32)


def _split(x):
    hi = x.astype(BF16)
    lo = (x - hi.astype(F32)).astype(BF16)
    return hi, lo


def _dot_sel(x, sel):
    hi, lo = _split(x)
    return jnp.dot(hi, sel, preferred_element_type=F32) + jnp.dot(lo, sel, preferred_element_type=F32)


def _sel_dot(sel, x):
    hi, lo = _split(x)
    return jnp.dot(sel, hi, preferred_element_type=F32) + jnp.dot(sel, lo, preferred_element_type=F32)


def _group_mat(n):
    r = jnp.right_shift(_iota((n, n), 0), 6)
    c = jnp.right_shift(_iota((n, n), 1), 6)
    return (r == c).astype(BF16)


def _lane_group_mask(n, g):
    lane = _iota((1, n), 1)
    return (lane >= g * HEAD_DIM) & (lane < (g + 1) * HEAD_DIM)


def _gelu(a):
    return 0.5 * a * (1.0 + lax.erf(a * INV_SQRT2))


def _gelu_grad(a):
    return 0.5 * (1.0 + lax.erf(a * INV_SQRT2)) + a * jnp.exp(-0.5 * a * a) * INV_SQRT_2PI


def _rms(x):
    return lax.rsqrt(jnp.mean(x * x, axis=-1, keepdims=True) + EPS)


def _rms_bwd(x, g, dy):
    r = _rms(x)
    dyg = dy * g
    m2 = jnp.mean(dyg * x, axis=-1, keepdims=True)
    dx = r * dyg - x * (r * r * r) * m2
    dg = jnp.sum(dy * x * r, axis=0, keepdims=True)
    return dx, dg


def _shift_down(cur, prev8, k):
    rolled = pltpu.roll(cur, k, 0)
    row8 = _iota(prev8.shape, 0)
    top = jnp.where(row8 < k, pltpu.roll(prev8, k, 0), rolled[0:SUBLANES])
    return jnp.concatenate([top, rolled[SUBLANES:]], axis=0)


def _shift_up(cur, next8, k):
    n = cur.shape[0]
    rolled = pltpu.roll(cur, n - k, 0)
    row8 = _iota(next8.shape, 0)
    bot = jnp.where(row8 >= SUBLANES - k, pltpu.roll(next8, SUBLANES - k, 0), rolled[n - SUBLANES:])
    return jnp.concatenate([rolled[:n - SUBLANES], bot], axis=0)


def _mesh_pos():
    return lax.axis_index("x"), lax.axis_index("y"), lax.axis_index("c")


def _peer(x, y, c, k):
    px = 1 - x if k & 4 else x
    py = 1 - y if k & 2 else y
    pc = 1 - c if k & 1 else c
    return px, py, pc


class _Exchange:
    def __init__(self, arrs, scatter):
        self.arrs, self.scatter, self.n = list(arrs), scatter, len(arrs)
        self.out_shapes = [jax.ShapeDtypeStruct(a.shape if scatter else (N_DEV,) + a.shape, a.dtype) for a in arrs]
        self.specs = [pl.BlockSpec(memory_space=pl.ANY)] * self.n
        self.semaphores = [pltpu.SemaphoreType.DMA((self.n * (N_DEV - 1),)),
                           pltpu.SemaphoreType.DMA((self.n * (N_DEV - 1),)), pltpu.SemaphoreType.DMA((self.n,))]

    def _copies(self, ins, outs, sems):
        send, recv, loc = sems
        x, y, c = _mesh_pos()
        me = 4 * x + 2 * y + c
        src = (lambda a, idx: ins[a].at[idx]) if self.scatter else (lambda a, idx: ins[a])
        starts = [pltpu.make_async_copy(src(a, me), outs[a].at[me], loc.at[a]) for a in range(self.n)]
        waits = list(starts)
        for k in range(1, N_DEV):
            px, py, pc = _peer(x, y, c, k)
            pidx = 4 * px + 2 * py + pc
            for a in range(self.n):
                s = a * (N_DEV - 1) + k - 1
                common = dict(src_ref=src(a, pidx), send_sem=send.at[s], recv_sem=recv.at[s],
                              device_id=(px, py, pc), device_id_type=pl.DeviceIdType.MESH)
                starts.append(pltpu.make_async_remote_copy(dst_ref=outs[a].at[me], **common))
                waits.append(pltpu.make_async_remote_copy(dst_ref=outs[a].at[pidx], **common))
        return starts, waits

    def start(self, ins, outs, sems):
        for cp in self._copies(ins, outs, sems)[0]:
            cp.start()

    def wait(self, ins, outs, sems):
        for cp in self._copies(ins, outs, sems)[1]:
            cp.wait()


def _with_exchange(compute, ride, n_in, n_out, n_scratch, last_step):
    if ride is None:
        return compute
    nx = ride.n

    def body(*refs):
        ins, ride_in = refs[:n_in], refs[n_in:n_in + nx]
        outs = refs[n_in + nx:n_in + nx + n_out]
        ride_out = refs[n_in + nx + n_out:n_in + 2 * nx + n_out]
        scratch = refs[n_in + 2 * nx + n_out:n_in + 2 * nx + n_out + n_scratch]
        sems = refs[n_in + 2 * nx + n_out + n_scratch:]
        step = (pl.program_id(0), pl.program_id(1))

        @pl.when((step[0] == 0) & (step[1] == 0))
        def _():
            ride.start(ride_in, ride_out, sems)

        compute(*ins, *outs, *scratch)

        @pl.when((step[0] == last_step[0]) & (step[1] == last_step[1]))
        def _():
            ride.wait(ride_in, ride_out, sems)

    return body


def _exchange(arrs, name, scatter):
    ex = _Exchange(arrs, scatter)
    n = ex.n

    def body(*refs):
        ins, outs, sems = refs[:n], refs[n:2 * n], refs[2 * n:]
        ex.start(ins, outs, sems)
        ex.wait(ins, outs, sems)

    return _pcall(body, name=name, out_shape=ex.out_shapes, in_specs=ex.specs, out_specs=ex.specs,
                  scratch_shapes=ex.semaphores, compiler_params=pltpu.CompilerParams(has_side_effects=True))(*arrs)


def _mm_tn(a, b, name):
    s, m = a.shape
    n = b.shape[1]
    tm, tn, tk = _tile(m, 1408, LANES), _tile(n, 1408, LANES), _tile(s, 1024)

    def body(a_ref, b_ref, o_ref):
        @pl.when(pl.program_id(2) == 0)
        def _():
            o_ref[...] = jnp.zeros_like(o_ref)
        o_ref[...] += _dot_tn(a_ref[...], b_ref[...])

    return _pcall(
        body, name=name, grid=(m // tm, n // tn, s // tk),
        in_specs=[pl.BlockSpec((tk, tm), lambda i, j, k: (k, i)), pl.BlockSpec((tk, tn), lambda i, j, k: (k, j))],
        out_specs=pl.BlockSpec((tm, tn), lambda i, j, k: (i, j)),
        out_shape=jax.ShapeDtypeStruct((m, n), F32),
        compiler_params=_params(("parallel", "parallel", "arbitrary")),
    )(a, b)


def _inproj_fwd(x, g, w, name):
    s = x.shape[0]
    tm = _tile(s, ROW_TILE)

    def body(x_ref, g_ref, w_ref, a_ref, qkv_ref, p_ref, h_ref):
        xv = x_ref[...]
        h = (xv * _rms(xv) * g_ref[...]).astype(_MXU)
        h_ref[...] = h
        a_ref[...] = jnp.dot(h, w_ref[:, 0:2 * W_A], preferred_element_type=F32)
        q = jnp.dot(h, w_ref[:, 2 * W_A:2 * W_A + W_B], preferred_element_type=F32)
        qkv_ref[:, 0:W_B] = (q * 0.125).astype(_MXU)
        kv = jnp.dot(h, w_ref[:, 2 * W_A + W_B:2 * W_A + 3 * W_B], preferred_element_type=F32)
        qkv_ref[:, W_B:3 * W_B] = kv.astype(_MXU)
        p_ref[...] = jnp.dot(h, w_ref[:, 2 * W_A + 3 * W_B:IN_COLS], preferred_element_type=F32)

    row = lambda n: pl.BlockSpec((tm, n), lambda i: (i, 0))
    return _pcall(
        body, name=name, grid=(s // tm,),
        in_specs=[row(D_MODEL), pl.BlockSpec((1, D_MODEL), lambda i: (0, 0)),
                  pl.BlockSpec((D_MODEL, IN_COLS), lambda i: (0, 0))],
        out_specs=[row(2 * W_A), row(3 * W_B), row(W_C), row(D_MODEL)],
        out_shape=[jax.ShapeDtypeStruct((s, 2 * W_A), F32), jax.ShapeDtypeStruct((s, 3 * W_B), _MXU),
                   jax.ShapeDtypeStruct((s, W_C), F32), jax.ShapeDtypeStruct((s, D_MODEL), _MXU)],
        compiler_params=_params(("parallel",)),
    )(x, g, w)


def _inproj_bwd(da, dq, dk, dv, dp, wt, x, g, dres, name):
    s = x.shape[0]
    tm = _tile(s, ROW_TILE)

    def body(da_ref, dq_ref, dk_ref, dv_ref, dp_ref, wt_ref, x_ref, g_ref, dres_ref, dx_ref, dg_ref):
        dh = _dot(da_ref[...], wt_ref[0:2 * W_A, :])
        dh += _dot(dq_ref[...], wt_ref[2 * W_A:2 * W_A + W_B, :])
        dh += _dot(dk_ref[...], wt_ref[2 * W_A + W_B:2 * W_A + 2 * W_B, :])
        dh += _dot(dv_ref[...], wt_ref[2 * W_A + 2 * W_B:2 * W_A + 3 * W_B, :])
        dh += _dot(dp_ref[...], wt_ref[2 * W_A + 3 * W_B:IN_COLS, :])
        dx, dg = _rms_bwd(x_ref[...], g_ref[...], dh)
        dx_ref[...] = dres_ref[...] + dx

        @pl.when(pl.program_id(0) == 0)
        def _():
            dg_ref[...] = jnp.zeros_like(dg_ref)
        dg_ref[...] += dg

    row = lambda n: pl.BlockSpec((tm, n), lambda i: (i, 0))
    vec = pl.BlockSpec((1, D_MODEL), lambda i: (0, 0))
    return _pcall(
        body, name=name, grid=(s // tm,),
        in_specs=[row(2 * W_A), row(W_B), row(W_B), row(W_B), row(W_C),
                  pl.BlockSpec((IN_COLS, D_MODEL), lambda i: (0, 0)), row(D_MODEL), vec, row(D_MODEL)],
        out_specs=[row(D_MODEL), vec],
        out_shape=[jax.ShapeDtypeStruct((s, D_MODEL), F32), jax.ShapeDtypeStruct((1, D_MODEL), F32)],
        compiler_params=_params(("arbitrary",)),
    )(da, dq, dk, dv, dp, wt, x, g, dres)


def _sgu_core(a, gn, wm_ref, bias):
    ga = _gelu(a)
    u, v0 = ga[:, 0:W_A], ga[:, W_A:2 * W_A]
    r = lax.rsqrt(_dot_sel(v0 * v0, _group_mat(W_A)) * (1.0 / HEAD_DIM) + EPS)
    vn = v0 * r * gn
    sv = bias
    for h in range(W_A // HEAD_DIM):
        sv = sv + _dot(wm_ref[h], jnp.where(_lane_group_mask(W_A, h), vn, 0.0))
    return u, v0, r, vn, sv


def _sgu_fwd(a, gn, wm, bias, name):
    s = a.shape[0]

    def body(a_ref, gn_ref, wm_ref, b_ref, y_ref):
        u, _, _, _, sv = _sgu_core(a_ref[...], gn_ref[...], wm_ref, b_ref[...])
        y_ref[...] = u * sv

    return _pcall(
        body, name=name, grid=(s // CHUNK,),
        in_specs=[pl.BlockSpec((CHUNK, 2 * W_A), lambda i: (i, 0)), pl.BlockSpec((1, W_A), lambda i: (0, 0)),
                  pl.BlockSpec((4, CHUNK, CHUNK), lambda i: (0, 0, 0)), pl.BlockSpec((CHUNK, W_A), lambda i: (0, 0))],
        out_specs=pl.BlockSpec((CHUNK, W_A), lambda i: (i, 0)),
        out_shape=jax.ShapeDtypeStruct((s, W_A), F32),
        compiler_params=_params(("parallel",)),
    )(a, gn, wm, bias)


def _sgu_bwd(a, dy, gn, wm, bias, name):
    s = a.shape[0]

    def body(a_ref, dy_ref, gn_ref, wm_ref, b_ref, da_ref, dwm_ref, db_ref, dgn_ref):
        @pl.when(pl.program_id(0) == 0)
        def _():
            dwm_ref[...] = jnp.zeros_like(dwm_ref)
            db_ref[...] = jnp.zeros_like(db_ref)
            dgn_ref[...] = jnp.zeros_like(dgn_ref)

        av, gnv, dyv = a_ref[...], gn_ref[...], dy_ref[...]
        u, v0, r, vn, sv = _sgu_core(av, gnv, wm_ref, b_ref[...])
        du = dyv * sv
        ds = dyv * u
        db_ref[...] += ds
        tril = _iota((CHUNK, CHUNK), 1) <= _iota((CHUNK, CHUNK), 0)
        dvn = jnp.zeros_like(vn)
        for h in range(W_A // HEAD_DIM):
            dsm = jnp.where(_lane_group_mask(W_A, h), ds, 0.0)
            dwm_ref[h] += jnp.where(tril, _dot_nt(dsm, vn), 0.0)
            dvn = dvn + _dot_tn(wm_ref[h], dsm)
        dgn_ref[...] += jnp.sum(dvn * v0 * r, axis=0, keepdims=True)
        dvg = dvn * gnv
        m2 = _dot_sel(dvg * v0, _group_mat(W_A)) * (1.0 / HEAD_DIM)
        dv0 = r * dvg - v0 * (r * r * r) * m2
        gp = _gelu_grad(av)
        da_ref[:, 0:W_A] = du * gp[:, 0:W_A]
        da_ref[:, W_A:2 * W_A] = dv0 * gp[:, W_A:2 * W_A]

    return _pcall(
        body, name=name, grid=(s // CHUNK,),
        in_specs=[pl.BlockSpec((CHUNK, 2 * W_A), lambda i: (i, 0)), pl.BlockSpec((CHUNK, W_A), lambda i: (i, 0)),
                  pl.BlockSpec((1, W_A), lambda i: (0, 0)), pl.BlockSpec((4, CHUNK, CHUNK), lambda i: (0, 0, 0)),
                  pl.BlockSpec((CHUNK, W_A), lambda i: (0, 0))],
        out_specs=[pl.BlockSpec((CHUNK, 2 * W_A), lambda i: (i, 0)), pl.BlockSpec((4, CHUNK, CHUNK), lambda i: (0, 0, 0)),
                   pl.BlockSpec((CHUNK, W_A), lambda i: (0, 0)), pl.BlockSpec((1, W_A), lambda i: (0, 0))],
        out_shape=[jax.ShapeDtypeStruct((s, 2 * W_A), F32), jax.ShapeDtypeStruct((4, CHUNK, CHUNK), F32),
                   jax.ShapeDtypeStruct((CHUNK, W_A), F32), jax.ShapeDtypeStruct((1, W_A), F32)],
        compiler_params=_params(("arbitrary",)),
    )(a, dy, gn, wm, bias)


KBLK = 128
ROW_CHUNK = 64
FWD_UNROLL = 4
BWD_UNROLL = 4
MASKED_SCORE = -1e30


def _attn_fwd(qkv, name, ride=None):
    s = qkv.shape[0]
    tq = _tile(s, 256, KBLK)
    npairs = W_B // LANES
    assert s // KBLK <= LANES
    rides = [] if ride is None else [ride]

    def body(q_ref, k_ref, v_ref, o_ref, cm_ref, z_ref, zw_ref, sums_ref, carry_ref, rs_ref, lb_ref, a_ref):
        i = pl.program_id(1)
        q = q_ref[...]
        lane = _iota((1, LANES), 1)
        lane_lo = lane < HEAD_DIM
        hmask = (lane_lo, jnp.logical_not(lane_lo))
        tri = (_iota((KBLK, KBLK), 0) >= _iota((KBLK, KBLK), 1)).astype(BF16)
        dmat = _iota((tq, KBLK), 1) - (_iota((tq, KBLK), 0) + i * tq)
        chunks = [slice(r, r + ROW_CHUNK) for r in range(0, tq, ROW_CHUNK)]
        heads = [slice(hh * LANES, (hh + 1) * LANES) for hh in range(2)]
        nk = (i + 1) * (tq // KBLK)

        cm_ref[...] = jnp.zeros_like(cm_ref)

        def before(b):
            return jnp.where((b >= 0) & (b < nk), -b * KBLK, jnp.iinfo(jnp.int32).min)

        def per_head(block):
            return jnp.concatenate([jnp.where(m, block, jnp.zeros_like(block)) for m in hmask], axis=0)

        def scores(b, p):
            ks = pl.multiple_of(jnp.clip(b, 0, nk - 1) * KBLK, KBLK)
            z_ref[p] = _dot_nt(q, per_head(k_ref[pl.ds(ks, KBLK), :]))

        def logs(b, p):
            t = before(b)
            for hh in range(2):
                for rows in chunks:
                    z = jnp.where(dmat[rows] < t, z_ref[p, rows, heads[hh]], MASKED_SCORE)
                    zw_ref[p, rows, heads[hh]] = z
                    l = -(jnp.maximum(z, 0.0) + jnp.log(1.0 + jnp.exp(-jnp.abs(z))))
                    lb_ref[p, hh, rows, :] = l.astype(BF16)
                    rs_ref[p, hh, rows, :] = jnp.zeros((ROW_CHUNK, KBLK), F32) + jnp.sum(l, axis=1, keepdims=True)

        def sums(b, p):
            for hh in range(2):
                sums_ref[p, hh] = jnp.dot(lb_ref[p, hh], tri, preferred_element_type=F32)

        def weights(b, p):
            pick = lane == b
            for hh in range(2):
                for rows in chunks:
                    c = carry_ref[hh, rows, :]
                    arg = zw_ref[p, rows, heads[hh]] + c + sums_ref[p, hh, rows, :]
                    a_ref[p, rows, heads[hh]] = jnp.exp(arg).astype(_MXU)
                    cm_ref[rows, heads[hh]] = jnp.where(pick, c, cm_ref[rows, heads[hh]])
                    carry_ref[hh, rows, :] = c + rs_ref[p, hh, rows, :]

        def out(b, p, acc):
            ks = pl.multiple_of(jnp.minimum(b, nk - 1) * KBLK, KBLK)
            vb = v_ref[pl.ds(ks, KBLK), :]
            for hh in range(2):
                acc = acc + jnp.dot(a_ref[p, :, heads[hh]], jnp.where(hmask[hh], vb, jnp.zeros_like(vb)),
                                    preferred_element_type=F32)
            return acc

        trips = (nk + 4 + FWD_UNROLL - 1) // FWD_UNROLL

        def step(it, acc):
            for u in range(FWD_UNROLL):
                b = trips * FWD_UNROLL - 1 - (FWD_UNROLL * it + u)
                p = 1 - u % 2
                acc = out(b, p, acc)
                weights(b - 1, 1 - p)
                sums(b - 2, p)
                logs(b - 3, 1 - p)
                scores(b - 4, p)
            return acc

        z_ref[...] = jnp.zeros_like(z_ref)
        zw_ref[...] = jnp.full_like(zw_ref, MASKED_SCORE)
        for ref in (sums_ref, carry_ref, rs_ref, lb_ref, a_ref):
            ref[...] = jnp.zeros_like(ref)
        o_ref[...] = lax.fori_loop(0, trips, step, q.astype(F32) * 0.0)

    scratch = [pltpu.VMEM((2, tq, 2 * KBLK), F32), pltpu.VMEM((2, tq, 2 * KBLK), F32),
               pltpu.VMEM((2, 2, tq, KBLK), F32), pltpu.VMEM((2, tq, KBLK), F32), pltpu.VMEM((2, 2, tq, KBLK), F32),
               pltpu.VMEM((2, 2, tq, KBLK), BF16), pltpu.VMEM((2, tq, 2 * KBLK), _MXU)]
    res = _pcall(
        _with_exchange(body, ride, 3, 2, len(scratch), (npairs - 1, s // tq - 1)), name=name, grid=(npairs, s // tq),
        in_specs=[pl.BlockSpec((tq, LANES), lambda p, i: (i, p)),
                  pl.BlockSpec((s, LANES), lambda p, i: (0, npairs + p)),
                  pl.BlockSpec((s, LANES), lambda p, i: (0, 2 * npairs + p))] + [sp for r in rides for sp in r.specs],
        out_specs=[pl.BlockSpec((tq, LANES), lambda p, i: (i, p)), pl.BlockSpec((tq, 2 * LANES), lambda p, i: (i, p))]
                  + [sp for r in rides for sp in r.specs],
        out_shape=[jax.ShapeDtypeStruct((s, W_B), F32), jax.ShapeDtypeStruct((s, 2 * W_B), F32)]
                  + [sh for r in rides for sh in r.out_shapes],
        scratch_shapes=scratch + [sem for r in rides for sem in r.semaphores],
        compiler_params=_params(("arbitrary", "arbitrary")),
    )(qkv, qkv, qkv, *[a for r in rides for a in r.arrs])
    return res[0], res[1], list(res[2:])


def _attn_bwd(qkv, cm, do, name, ride=None):
    s = qkv.shape[0]
    tq = _tile(s, 256, KBLK)
    npairs = W_B // LANES
    rides = [] if ride is None else [ride]

    nq = s // tq
    per_tile = tq // KBLK
    assert nq % 2 == 0 and per_tile % 2 == 0
    ntot = (nq + 1) * per_tile
    assert (ntot + 6) % BWD_UNROLL == 0

    def body(qa_ref, k_ref, v_ref, cma_ref, doa_ref, qb_ref, cmb_ref, dob_ref, dqa_ref, dqb_ref, dk_ref, dv_ref,
             z_ref, zw_ref, da_ref, g_ref, sig_ref, cum_ref, gp_ref, gcarry_ref, lb_ref, gb_ref, a_ref, dz_ref,
             dkt_ref, dvt_ref, q_st, do_st, qt_st, dot_st, cm_st):
        i = pl.program_id(1)

        @pl.when(i == 0)
        def _():
            dkt_ref[...] = jnp.zeros_like(dkt_ref)
            dvt_ref[...] = jnp.zeros_like(dvt_ref)

        tiles = (i, nq - 1 - i)
        nk_a = (i + 1) * per_tile
        for t, (q_in, do_in, cm_in) in enumerate(((qa_ref, doa_ref, cma_ref), (qb_ref, dob_ref, cmb_ref))):
            q_st[t] = q_in[...]
            do_st[t] = do_in[...].astype(_MXU)
            qt_st[t] = q_in[...].astype(F32).T.astype(_MXU)
            dot_st[t] = do_in[...].T.astype(_MXU)
            cm_st[t] = cm_in[...]

        lane = _iota((1, LANES), 1)
        lane_lo = lane < HEAD_DIM
        hmask = (lane_lo, jnp.logical_not(lane_lo))
        tri = (_iota((KBLK, KBLK), 0) >= _iota((KBLK, KBLK), 1)).astype(BF16)
        prefix = (_iota((KBLK, KBLK), 0) <= _iota((KBLK, KBLK), 1)).astype(BF16)
        dmat = _iota((tq, KBLK), 1) - _iota((tq, KBLK), 0)
        chunks = [slice(r, r + ROW_CHUNK) for r in range(0, tq, ROW_CHUNK)]
        heads = [slice(hh * LANES, (hh + 1) * LANES) for hh in range(2)]

        def locate(v):
            second = v >= nk_a
            return ((v >= 0) & (v < ntot), second.astype(jnp.int32), jnp.where(second, tiles[1], tiles[0]),
                    jnp.where(second, v - nk_a, v))

        def before(v):
            valid, _, tile, b = locate(v)
            return jnp.where(valid, tile * tq - b * KBLK, jnp.iinfo(jnp.int32).min)

        def key_block(v):
            return jnp.clip(locate(v)[3], 0, nblk - 1)

        def block_rows(v):
            return pl.ds(pl.multiple_of(key_block(v) * KBLK, KBLK), KBLK)

        def per_head(block):
            return jnp.concatenate([jnp.where(m, block, jnp.zeros_like(block)) for m in hmask], axis=0)

        feature_lo = _iota((LANES, KBLK), 0) < HEAD_DIM

        def own_features(side_by_side):
            return jnp.where(feature_lo, side_by_side[:, 0:KBLK], side_by_side[:, KBLK:2 * KBLK])

        def m1(b, p):
            z_ref[p] = _dot_nt(q_st[locate(b)[1]], per_head(k_ref[block_rows(b), :]))

        def v1(b, p):
            t = before(b)
            for hh in range(2):
                for rows in chunks:
                    z = jnp.where(dmat[rows] < t, z_ref[p, rows, heads[hh]], MASKED_SCORE)
                    zw_ref[p, rows, heads[hh]] = z
                    l = -(jnp.maximum(z, 0.0) + jnp.log(1.0 + jnp.exp(-jnp.abs(z))))
                    lb_ref[p, hh, rows, :] = l.astype(BF16)

        def m2(b, p):
            for hh in range(2):
                cum_ref[p, hh] = jnp.dot(lb_ref[p, hh], tri, preferred_element_type=F32)
            da_ref[p] = _dot_nt(do_st[locate(b)[1]], per_head(v_ref[block_rows(b), :]))

        def v2(b, p):
            valid, which, _, blk = locate(b)
            pick = lane == jnp.where(valid, blk, -1)
            for hh in range(2):
                for rows in chunks:
                    c = jnp.sum(jnp.where(pick, cm_st[which, rows, heads[hh]], 0.0), axis=1, keepdims=True)
                    z = zw_ref[p, rows, heads[hh]]
                    a = jnp.exp(z + c + cum_ref[p, hh, rows, :])
                    g = a * da_ref[p, rows, heads[hh]]
                    a_ref[p, rows, heads[hh]] = a.astype(_MXU)
                    g_ref[p, rows, heads[hh]] = g
                    gb_ref[p, hh, rows, :] = g.astype(BF16)
                    sig_ref[p, rows, heads[hh]] = jax.nn.sigmoid(z)

        def m3(b, p):
            for hh in range(2):
                gp_ref[p, hh] = jnp.dot(gb_ref[p, hh], prefix, preferred_element_type=F32)
            dvt_ref[key_block(b)] += own_features(jnp.dot(dot_st[locate(b)[1]], a_ref[p], preferred_element_type=F32))

        def v3(b, p):
            restart = b == nk_a
            for hh in range(2):
                for rows in chunks:
                    gc = jnp.where(restart, 0.0, gcarry_ref[hh, rows, :])
                    g = g_ref[p, rows, heads[hh]]
                    dz = g - sig_ref[p, rows, heads[hh]] * (gc + gp_ref[p, hh, rows, :])
                    dz_ref[p, rows, heads[hh]] = dz.astype(_MXU)
                    gcarry_ref[hh, rows, :] = gc + jnp.sum(g, axis=1, keepdims=True)

        def m4(b, p, dqs):
            which = locate(b)[1]
            kb = k_ref[block_rows(b), :]
            part = None
            for hh in range(2):
                d = jnp.dot(dz_ref[p, :, heads[hh]], jnp.where(hmask[hh], kb, jnp.zeros_like(kb)),
                            preferred_element_type=F32)
                part = d if part is None else part + d
            dkt_ref[key_block(b)] += own_features(jnp.dot(qt_st[which], dz_ref[p], preferred_element_type=F32))
            return dqs[0] + jnp.where(which == 0, part, 0.0), dqs[1] + jnp.where(which == 1, part, 0.0)

        def step(it, dqs):
            for u in range(BWD_UNROLL):
                j = BWD_UNROLL * it + u
                p = u % 2
                dqs = m4(j - 6, p, dqs)
                m3(j - 4, p)
                m2(j - 2, p)
                m1(j, p)
                v3(j - 5, 1 - p)
                v2(j - 3, 1 - p)
                v1(j - 1, 1 - p)
            return dqs

        zw_ref[...] = jnp.full_like(zw_ref, MASKED_SCORE)
        for ref in (z_ref, da_ref, g_ref, sig_ref, cum_ref, gp_ref, gcarry_ref, lb_ref, gb_ref, a_ref, dz_ref):
            ref[...] = jnp.zeros_like(ref)
        zero = doa_ref[...] * 0.0
        dq_a, dq_b = lax.fori_loop(0, (ntot + 6) // BWD_UNROLL, step, (zero, zero))
        dqa_ref[...] = dq_a * 0.125
        dqb_ref[...] = dq_b * 0.125

        @pl.when(i == nq // 2 - 1)
        def _():
            def untranspose(blk, carry):
                rows = pl.ds(pl.multiple_of(blk * KBLK, KBLK), KBLK)
                dk_ref[rows, :] = dkt_ref[blk].T
                dv_ref[rows, :] = dvt_ref[blk].T
                return carry
            lax.fori_loop(0, nblk, untranspose, 0)

    first = lambda width: pl.BlockSpec((tq, width), lambda p, i: (i, p))
    second = lambda width: pl.BlockSpec((tq, width), lambda p, i: (nq - 1 - i, p))
    second_out = pl.BlockSpec((tq, LANES), lambda p, i: (nq // 2 - 1 - i, p))
    nblk = s // KBLK
    full = pl.BlockSpec((s, LANES), lambda p, i: (0, p))
    scratch = ([pltpu.VMEM((2, tq, 2 * KBLK), F32)] * 5 + [pltpu.VMEM((2, 2, tq, KBLK), F32),
               pltpu.VMEM((2, 2, tq, KBLK), F32), pltpu.VMEM((2, tq, KBLK), F32)]
               + [pltpu.VMEM((2, 2, tq, KBLK), BF16)] * 2 + [pltpu.VMEM((2, tq, 2 * KBLK), _MXU)] * 2
               + [pltpu.VMEM((nblk, LANES, KBLK), F32)] * 2
               + [pltpu.VMEM((2, tq, LANES), _MXU)] * 2 + [pltpu.VMEM((2, LANES, tq), _MXU)] * 2
               + [pltpu.VMEM((2, tq, 2 * LANES), F32)])
    res = _pcall(
        _with_exchange(body, ride, 8, 4, len(scratch), (npairs - 1, nq // 2 - 1)), name=name, grid=(npairs, nq // 2),
        in_specs=[first(LANES), pl.BlockSpec((s, LANES), lambda p, i: (0, npairs + p)),
                  pl.BlockSpec((s, LANES), lambda p, i: (0, 2 * npairs + p)), first(2 * LANES), first(LANES),
                  second(LANES), second(2 * LANES), second(LANES)] + [sp for r in rides for sp in r.specs],
        out_specs=[first(LANES), second_out, full, full] + [sp for r in rides for sp in r.specs],
        out_shape=[jax.ShapeDtypeStruct((s // 2, W_B), F32)] * 2 + [jax.ShapeDtypeStruct((s, W_B), F32)] * 2
                  + [sh for r in rides for sh in r.out_shapes],
        scratch_shapes=scratch + [sem for r in rides for sem in r.semaphores],
        compiler_params=_params(("arbitrary", "arbitrary")),
    )(qkv, qkv, qkv, cm, do, qkv, cm, do, *[a for r in rides for a in r.arrs])
    return jnp.concatenate([res[0], res[1]], axis=0), res[2], res[3], list(res[4:])


POOL_HALO = 128


def _pool_window_lane():
    lane = _iota((1, W_C), 1)
    w = jnp.where(lane < 64, POOL_WINDOWS[0], jnp.where(lane < 128, POOL_WINDOWS[1],
                  jnp.where(lane < 192, POOL_WINDOWS[2], POOL_WINDOWS[3])))
    return w.astype(F32)


def _pool_count(tm, i):
    pos = (_iota((tm, W_C), 0) + (i * tm + 1)).astype(F32)
    return jnp.minimum(pos, _pool_window_lane())


def _pool_centered(prev, cur, cnt):
    tm = cur.shape[0]
    xx = jnp.concatenate([prev, cur], axis=0)
    hi, lo = _split(xx)
    t = _iota((tm, tm + POOL_HALO), 0)
    cc = _iota((tm, tm + POOL_HALO), 1) - POOL_HALO
    wsum = jnp.zeros_like(cur)
    for g, w in enumerate(POOL_WINDOWS):
        band = ((cc <= t) & (cc > t - w)).astype(BF16)
        mg = _lane_group_mask(W_C, g)
        wsum += jnp.dot(band, jnp.where(mg, hi, jnp.zeros_like(hi)), preferred_element_type=F32)
        wsum += jnp.dot(band, jnp.where(mg, lo, jnp.zeros_like(lo)), preferred_element_type=F32)
    return wsum / cnt - cur


def _pool_fwd(p, wbd, sc, name):
    s = p.shape[0]
    tm = _tile(s, 256, POOL_HALO)
    r = tm // POOL_HALO

    def body(pp_ref, p_ref, w_ref, sc_ref, y_ref):
        i = pl.program_id(0)
        prev = jnp.where(i > 0, pp_ref[...], 0.0)
        d = _pool_centered(prev, p_ref[...], _pool_count(tm, i))
        y_ref[...] = _dot(d, w_ref[...]) * sc_ref[...]

    return _pcall(
        body, name=name, grid=(s // tm,),
        in_specs=[pl.BlockSpec((POOL_HALO, W_C), lambda i: (jnp.maximum(i * r - 1, 0), 0)),
                  pl.BlockSpec((tm, W_C), lambda i: (i, 0)), pl.BlockSpec((W_C, W_C), lambda i: (0, 0)),
                  pl.BlockSpec((1, W_C), lambda i: (0, 0))],
        out_specs=pl.BlockSpec((tm, W_C), lambda i: (i, 0)),
        out_shape=jax.ShapeDtypeStruct((s, W_C), F32),
        compiler_params=_params(("parallel",)),
    )(p, p, wbd, sc)


def _pool_bwd(p, dy, wbd, wbdt, sc, name):
    s = p.shape[0]
    tm = _tile(s, 256, POOL_HALO)
    r = tm // POOL_HALO
    nt = s // tm

    def body(pp_ref, p_ref, dy_ref, dyn_ref, w_ref, wt_ref, sc_ref, dp_ref, dw_ref, dsc_ref):
        i = pl.program_id(0)

        @pl.when(i == 0)
        def _():
            dw_ref[...] = jnp.zeros_like(dw_ref)
            dsc_ref[...] = jnp.zeros_like(dsc_ref)

        scv = sc_ref[...]
        cnt = _pool_count(tm, i)
        prev = jnp.where(i > 0, pp_ref[...], 0.0)
        d = _pool_centered(prev, p_ref[...], cnt)
        e = _dot(d, w_ref[...])
        dyv = dy_ref[...]
        de = dyv * scv
        dsc_ref[...] += jnp.sum(dyv * e, axis=0, keepdims=True)
        dw_ref[...] += _dot_tn(d, de)
        dd = _dot(de, wt_ref[...])
        ddn = jnp.where(i < nt - 1, _dot(dyn_ref[...] * scv, wt_ref[...]), 0.0)
        yy = jnp.concatenate([dd / cnt, ddn / _pool_window_lane()], axis=0)
        hi, lo = _split(yy)
        t = _iota((tm, tm + POOL_HALO), 0)
        cc = _iota((tm, tm + POOL_HALO), 1)
        acc = jnp.zeros_like(dd)
        for g, w in enumerate(POOL_WINDOWS):
            band = ((cc >= t) & (cc < t + w)).astype(BF16)
            mg = _lane_group_mask(W_C, g)
            acc += jnp.dot(band, jnp.where(mg, hi, jnp.zeros_like(hi)), preferred_element_type=F32)
            acc += jnp.dot(band, jnp.where(mg, lo, jnp.zeros_like(lo)), preferred_element_type=F32)
        dp_ref[...] = acc - dd

    tile = pl.BlockSpec((tm, W_C), lambda i: (i, 0))
    mat = pl.BlockSpec((W_C, W_C), lambda i: (0, 0))
    vec = pl.BlockSpec((1, W_C), lambda i: (0, 0))
    return _pcall(
        body, name=name, grid=(nt,),
        in_specs=[pl.BlockSpec((POOL_HALO, W_C), lambda i: (jnp.maximum(i * r - 1, 0), 0)), tile, tile,
                  pl.BlockSpec((POOL_HALO, W_C), lambda i: (jnp.minimum((i + 1) * r, s // POOL_HALO - 1), 0)),
                  mat, mat, vec],
        out_specs=[tile, mat, vec],
        out_shape=[jax.ShapeDtypeStruct((s, W_C), F32), jax.ShapeDtypeStruct((W_C, W_C), F32),
                   jax.ShapeDtypeStruct((1, W_C), F32)],
        compiler_params=_params(("arbitrary",)),
    )(p, p, dy, dy, wbd, wbdt, sc)


def _mix_cols(ya_ref, yb_ref, yc_ref, cb):
    if cb < 2:
        return ya_ref[:, cb * LANES:(cb + 1) * LANES]
    if cb < 6:
        return yb_ref[:, (cb - 2) * LANES:(cb - 1) * LANES]
    return yc_ref[:, (cb - 6) * LANES:(cb - 5) * LANES]


def _mix_fwd(ya, yb, yc, g, wo, x, name):
    s = x.shape[0]
    tm = _tile(s, ROW_TILE)

    def body(ya_ref, yb_ref, yc_ref, g_ref, w_ref, x_ref, o_ref, yn_ref):
        sel = _group_mat(LANES)
        for cb in range(D_MODEL // LANES):
            y = _mix_cols(ya_ref, yb_ref, yc_ref, cb)
            r = lax.rsqrt(_dot_sel(y * y, sel) * (1.0 / HEAD_DIM) + EPS)
            yn_ref[:, cb * LANES:(cb + 1) * LANES] = (y * r * g_ref[:, cb * LANES:(cb + 1) * LANES]).astype(_MXU)
        o_ref[...] = x_ref[...] + jnp.dot(yn_ref[...], w_ref[...], preferred_element_type=F32)

    row = lambda n: pl.BlockSpec((tm, n), lambda i: (i, 0))
    return _pcall(
        body, name=name, grid=(s // tm,),
        in_specs=[row(W_A), row(W_B), row(W_C), pl.BlockSpec((1, D_MODEL), lambda i: (0, 0)),
                  pl.BlockSpec((D_MODEL, D_MODEL), lambda i: (0, 0)), row(D_MODEL)],
        out_specs=[row(D_MODEL), row(D_MODEL)],
        out_shape=[jax.ShapeDtypeStruct((s, D_MODEL), F32), jax.ShapeDtypeStruct((s, D_MODEL), _MXU)],
        compiler_params=_params(("parallel",)),
    )(ya, yb, yc, g, wo, x)


def _mix_bwd(dx, ya, yb, yc, g, wot, name):
    s = dx.shape[0]
    tm = _tile(s, ROW_TILE)

    def body(dx_ref, ya_ref, yb_ref, yc_ref, g_ref, wt_ref, dya_ref, dyb_ref, dyc_ref, dg_ref):
        @pl.when(pl.program_id(0) == 0)
        def _():
            dg_ref[...] = jnp.zeros_like(dg_ref)

        dyn = _dot(dx_ref[...], wt_ref[...])
        sel = _group_mat(LANES)
        for cb in range(D_MODEL // LANES):
            cols = slice(cb * LANES, (cb + 1) * LANES)
            y = _mix_cols(ya_ref, yb_ref, yc_ref, cb)
            r = lax.rsqrt(_dot_sel(y * y, sel) * (1.0 / HEAD_DIM) + EPS)
            dyc_ = dyn[:, cols]
            dyg = dyc_ * g_ref[:, cols]
            m2 = _dot_sel(dyg * y, sel) * (1.0 / HEAD_DIM)
            dy = r * dyg - y * (r * r * r) * m2
            dg_ref[:, cols] += jnp.sum(dyc_ * y * r, axis=0, keepdims=True)
            if cb < 2:
                dya_ref[:, cb * LANES:(cb + 1) * LANES] = dy
            elif cb < 6:
                dyb_ref[:, (cb - 2) * LANES:(cb - 1) * LANES] = dy
            else:
                dyc_ref[:, (cb - 6) * LANES:(cb - 5) * LANES] = dy

    row = lambda n: pl.BlockSpec((tm, n), lambda i: (i, 0))
    vec = pl.BlockSpec((1, D_MODEL), lambda i: (0, 0))
    return _pcall(
        body, name=name, grid=(s // tm,),
        in_specs=[row(D_MODEL), row(W_A), row(W_B), row(W_C), vec, pl.BlockSpec((D_MODEL, D_MODEL), lambda i: (0, 0))],
        out_specs=[row(W_A), row(W_B), row(W_C), vec],
        out_shape=[jax.ShapeDtypeStruct((s, W_A), F32), jax.ShapeDtypeStruct((s, W_B), F32),
                   jax.ShapeDtypeStruct((s, W_C), F32), jax.ShapeDtypeStruct((1, D_MODEL), F32)],
        compiler_params=_params(("arbitrary",)),
    )(dx, ya, yb, yc, g, wot)


FFN_CHUNK = 1408
FFN_ROWS = 256
N_CHUNKS = D_FF // FFN_CHUNK
CW_ROWS = 8


def _ffn_up_fwd(x, g, w, name):
    s = x.shape[0]
    n = w.shape[1]
    tm, tn = _tile(s, 512), _tile(n, 2816, LANES)

    def body(x_ref, g_ref, w_ref, z_ref, h_ref):
        @pl.when(pl.program_id(1) == 0)
        def _():
            xv = x_ref[...]
            h_ref[...] = (xv * _rms(xv) * g_ref[...]).astype(_MXU)
        z_ref[...] = jnp.dot(h_ref[...], w_ref[...], preferred_element_type=F32)

    return _pcall(
        body, name=name, grid=(s // tm, n // tn),
        in_specs=[pl.BlockSpec((tm, D_MODEL), lambda i, j: (i, 0)), pl.BlockSpec((1, D_MODEL), lambda i, j: (0, 0)),
                  pl.BlockSpec((D_MODEL, tn), lambda i, j: (0, j))],
        out_specs=[pl.BlockSpec((tm, tn), lambda i, j: (i, j)), pl.BlockSpec((tm, D_MODEL), lambda i, j: (i, 0))],
        out_shape=[jax.ShapeDtypeStruct((s, n), F32), jax.ShapeDtypeStruct((s, D_MODEL), _MXU)],
        compiler_params=_params(("parallel", "arbitrary")),
    )(x, g, w)


def _conv(cur, prev8, cw_ref):
    s1 = _shift_down(cur, prev8, 1)
    s2 = _shift_down(cur, prev8, 2)
    zc = cw_ref[3:4, :] + s2 * cw_ref[0:1, :]
    zc = zc + s1 * cw_ref[1:2, :]
    zc = zc + cur * cw_ref[2:3, :]
    return zc, s1, s2


def _halo_specs(tm, s):
    r = tm // SUBLANES
    prev = lambda off: pl.BlockSpec((SUBLANES, FFN_CHUNK), lambda i, j: (jnp.maximum(i * r - 1, 0), j + off))
    nxt = lambda off: pl.BlockSpec((SUBLANES, FFN_CHUNK), lambda i, j: (jnp.minimum((i + 1) * r, s // SUBLANES - 1), j + off))
    return prev, nxt


def _ffn_down_fwd(z, cw, wd, x, name):
    s = x.shape[0]
    tm = _tile(s, FFN_ROWS)
    prev, _ = _halo_specs(tm, s)

    def body(zg_ref, zu_ref, pg_ref, pu_ref, cg_ref, cu_ref, w_ref, x_ref, o_ref, act_ref, acc_ref):
        i, j = pl.program_id(0), pl.program_id(1)
        first = i > 0
        zg, _, _ = _conv(zg_ref[...], jnp.where(first, pg_ref[...], 0.0), cg_ref)
        zu, _, _ = _conv(zu_ref[...], jnp.where(first, pu_ref[...], 0.0), cu_ref)
        act = (zg * jax.nn.sigmoid(zg) * zu).astype(_MXU)
        act_ref[...] = act

        @pl.when(j == 0)
        def _():
            acc_ref[...] = x_ref[...]
        acc_ref[...] += jnp.dot(act, w_ref[...], preferred_element_type=F32)

        @pl.when(j == N_CHUNKS - 1)
        def _():
            o_ref[...] = acc_ref[...]

    zt = lambda off: pl.BlockSpec((tm, FFN_CHUNK), lambda i, j: (i, j + off))
    cwt = lambda off: pl.BlockSpec((CW_ROWS, FFN_CHUNK), lambda i, j: (0, j + off))
    return _pcall(
        body, name=name, grid=(s // tm, N_CHUNKS),
        in_specs=[zt(0), zt(N_CHUNKS), prev(0), prev(N_CHUNKS), cwt(0), cwt(N_CHUNKS),
                  pl.BlockSpec((FFN_CHUNK, D_MODEL), lambda i, j: (j, 0)), pl.BlockSpec((tm, D_MODEL), lambda i, j: (i, 0))],
        out_specs=[pl.BlockSpec((tm, D_MODEL), lambda i, j: (i, 0)), pl.BlockSpec((tm, FFN_CHUNK), lambda i, j: (i, j))],
        out_shape=[jax.ShapeDtypeStruct((s, D_MODEL), F32), jax.ShapeDtypeStruct((s, D_FF), _MXU)],
        scratch_shapes=[pltpu.VMEM((tm, D_MODEL), F32)],
        compiler_params=_params(("parallel", "arbitrary")),
    )(z, z, z, z, cw, cw, wd, x)


def _ffn_down_bwd(dx, z, cw, wdt, name):
    s = dx.shape[0]
    tm = _tile(s, FFN_ROWS)

    def body(dx_ref, zg_ref, zu_ref, pg_ref, pu_ref, cg_ref, cu_ref, wt_ref, dg_ref, du_ref, dcg_ref, dcu_ref):
        i = pl.program_id(1)
        first = i > 0

        @pl.when(i == 0)
        def _():
            dcg_ref[...] = jnp.zeros_like(dcg_ref)
            dcu_ref[...] = jnp.zeros_like(dcu_ref)

        dact = _dot(dx_ref[...], wt_ref[...])
        zg, g1, g2 = _conv(zg_ref[...], jnp.where(first, pg_ref[...], 0.0), cg_ref)
        zu, u1, u2 = _conv(zu_ref[...], jnp.where(first, pu_ref[...], 0.0), cu_ref)
        sg = jax.nn.sigmoid(zg)
        silu = zg * sg
        dzu = dact * silu
        dzg = dact * zu * (sg * (1.0 + zg * (1.0 - sg)))
        dg_ref[...] = dzg
        du_ref[...] = dzu
        for ref, dzc, cur, s1, s2 in ((dcg_ref, dzg, zg_ref[...], g1, g2), (dcu_ref, dzu, zu_ref[...], u1, u2)):
            ref[0:1, :] += jnp.sum(dzc * s2, axis=0, keepdims=True)
            ref[1:2, :] += jnp.sum(dzc * s1, axis=0, keepdims=True)
            ref[2:3, :] += jnp.sum(dzc * cur, axis=0, keepdims=True)
            ref[3:4, :] += jnp.sum(dzc, axis=0, keepdims=True)

    zt = lambda off: pl.BlockSpec((tm, FFN_CHUNK), lambda j, i: (i, j + off))
    r = tm // SUBLANES
    pv = lambda off: pl.BlockSpec((SUBLANES, FFN_CHUNK), lambda j, i: (jnp.maximum(i * r - 1, 0), j + off))
    cwt = lambda off: pl.BlockSpec((CW_ROWS, FFN_CHUNK), lambda j, i: (0, j + off))
    out_t = pl.BlockSpec((tm, FFN_CHUNK), lambda j, i: (i, j))
    dc_t = pl.BlockSpec((CW_ROWS, FFN_CHUNK), lambda j, i: (0, j))
    dzg, dzu, dcg, dcu = _pcall(
        body, name=name, grid=(N_CHUNKS, s // tm),
        in_specs=[pl.BlockSpec((tm, D_MODEL), lambda j, i: (i, 0)), zt(0), zt(N_CHUNKS), pv(0), pv(N_CHUNKS),
                  cwt(0), cwt(N_CHUNKS), pl.BlockSpec((D_MODEL, FFN_CHUNK), lambda j, i: (0, j))],
        out_specs=[out_t, out_t, dc_t, dc_t],
        out_shape=[jax.ShapeDtypeStruct((s, D_FF), F32), jax.ShapeDtypeStruct((s, D_FF), F32),
                   jax.ShapeDtypeStruct((CW_ROWS, D_FF), F32), jax.ShapeDtypeStruct((CW_ROWS, D_FF), F32)],
        compiler_params=_params(("parallel", "arbitrary")),
    )(dx, z, z, z, z, cw, cw, wdt)
    return dzg, dzu, jnp.concatenate([dcg, dcu], axis=1)


def _ffn_up_bwd(dzg, dzu, cw, wut, x, g, dres, name):
    s = x.shape[0]
    tm = _tile(s, FFN_ROWS)
    _, nxt = _halo_specs(tm, s)
    nt = s // tm

    def body(dg_ref, du_ref, ng_ref, nu_ref, cg_ref, cu_ref, wg_ref, wu_ref, x_ref, g_ref, dres_ref,
             dzg_ref, dzu_ref, dx_ref, dgn_ref, acc_ref):
        i, j = pl.program_id(0), pl.program_id(1)
        last = i < nt - 1

        def conv_bwd(cur, nxt8, cw_ref):
            up1 = _shift_up(cur, nxt8, 1)
            up2 = _shift_up(cur, nxt8, 2)
            return cur * cw_ref[2:3, :] + up1 * cw_ref[1:2, :] + up2 * cw_ref[0:1, :]

        dzg_ = conv_bwd(dg_ref[...], jnp.where(last, ng_ref[...], 0.0), cg_ref).astype(_MXU)
        dzu_ = conv_bwd(du_ref[...], jnp.where(last, nu_ref[...], 0.0), cu_ref).astype(_MXU)
        dzg_ref[...] = dzg_
        dzu_ref[...] = dzu_

        @pl.when(j == 0)
        def _():
            acc_ref[...] = jnp.zeros_like(acc_ref)
        acc_ref[...] += (jnp.dot(dzg_, wg_ref[...], preferred_element_type=F32)
                         + jnp.dot(dzu_, wu_ref[...], preferred_element_type=F32))

        @pl.when((i == 0) & (j == 0))
        def _():
            dgn_ref[...] = jnp.zeros_like(dgn_ref)

        @pl.when(j == N_CHUNKS - 1)
        def _():
            dx, dgn = _rms_bwd(x_ref[...], g_ref[...], acc_ref[...])
            dx_ref[...] = dres_ref[...] + dx
            dgn_ref[...] += dgn

    zt = pl.BlockSpec((tm, FFN_CHUNK), lambda i, j: (i, j))
    cwt = lambda off: pl.BlockSpec((CW_ROWS, FFN_CHUNK), lambda i, j: (0, j + off))
    wt = lambda off: pl.BlockSpec((FFN_CHUNK, D_MODEL), lambda i, j: (j + off, 0))
    row = pl.BlockSpec((tm, D_MODEL), lambda i, j: (i, 0))
    vec = pl.BlockSpec((1, D_MODEL), lambda i, j: (0, 0))
    return _pcall(
        body, name=name, grid=(nt, N_CHUNKS),
        in_specs=[zt, zt, nxt(0), nxt(0), cwt(0), cwt(N_CHUNKS), wt(0), wt(N_CHUNKS), row, vec, row],
        out_specs=[zt, zt, row, vec],
        out_shape=[jax.ShapeDtypeStruct((s, D_FF), _MXU), jax.ShapeDtypeStruct((s, D_FF), _MXU),
                   jax.ShapeDtypeStruct((s, D_MODEL), F32), jax.ShapeDtypeStruct((1, D_MODEL), F32)],
        scratch_shapes=[pltpu.VMEM((tm, D_MODEL), F32)],
        compiler_params=_params(("arbitrary", "arbitrary")),
    )(dzg, dzu, dzg, dzu, cw, cw, wut, wut, x, g, dres)


def _final_loss(x, g, tgt, name):
    s = x.shape[0]
    tm = _tile(s, ROW_TILE)

    def body(x_ref, g_ref, t_ref, loss_ref, dx_ref, dg_ref):
        @pl.when(pl.program_id(0) == 0)
        def _():
            loss_ref[...] = jnp.zeros_like(loss_ref)
            dg_ref[...] = jnp.zeros_like(dg_ref)

        xv, gv = x_ref[...], g_ref[...]
        err = xv * _rms(xv) * gv - t_ref[...]
        per_tok = jnp.mean(err * err, axis=-1, keepdims=True)
        loss_ref[...] += 0.5 * jnp.sum(per_tok, axis=0, keepdims=True)
        dx, dg = _rms_bwd(xv, gv, err * (1.0 / D_MODEL))
        dx_ref[...] = dx
        dg_ref[...] += dg

    row = pl.BlockSpec((tm, D_MODEL), lambda i: (i, 0))
    vec = pl.BlockSpec((1, D_MODEL), lambda i: (0, 0))
    return _pcall(
        body, name=name, grid=(s // tm,),
        in_specs=[row, vec, row], out_specs=[pl.BlockSpec((1, 1), lambda i: (0, 0)), row, vec],
        out_shape=[jax.ShapeDtypeStruct((1, 1), F32), jax.ShapeDtypeStruct((s, D_MODEL), F32),
                   jax.ShapeDtypeStruct((1, D_MODEL), F32)],
        compiler_params=_params(("arbitrary",)),
    )(x, g, tgt)


def _adamw(parts, w, m, v, name):
    r, c = w.shape
    tr = _tile(r, 256)
    c1 = 1.0 - ADAM_B1 ** ADAM_STEP
    c2 = 1.0 - ADAM_B2 ** ADAM_STEP

    def body(p_ref, w_ref, m_ref, v_ref, g_ref, d_ref, mo_ref, vo_ref):
        g = p_ref[0].astype(F32)
        for i in range(1, N_DEV):
            g = g + p_ref[i].astype(F32)
        mn = ADAM_B1 * m_ref[...] + (1.0 - ADAM_B1) * g
        vn = ADAM_B2 * v_ref[...] + (1.0 - ADAM_B2) * (g * g)
        g_ref[...] = g
        mo_ref[...] = mn
        vo_ref[...] = vn
        d_ref[...] = -ADAM_LR * ((mn / c1) / (jnp.sqrt(vn / c2) + ADAM_EPS) + ADAM_WD * w_ref[...])

    t2 = pl.BlockSpec((tr, c), lambda i: (i, 0))
    return _pcall(
        body, name=name, grid=(r // tr,),
        in_specs=[pl.BlockSpec((N_DEV, tr, c), lambda i: (0, i, 0)), t2, t2, t2],
        out_specs=[t2] * 4, out_shape=[jax.ShapeDtypeStruct((r, c), F32)] * 4,
        compiler_params=_params(("parallel",)),
    )(parts, w, m, v)


SMALL = ("norm1_g", "sgu_norm_g", "sgu_w", "sgu_b", "pool_w", "pool_scale", "mix_norm_g", "norm2_g", "conv_b", "final_g")
SHARDED = ("w_in", "w_o", "w_up", "conv_w", "w_down")
ORDER = ("norm1_g", "w_in", "sgu_norm_g", "sgu_w", "sgu_b", "pool_w", "pool_scale", "mix_norm_g", "w_o", "norm2_g",
         "w_up", "conv_w", "conv_b", "w_down", "final_g")


def _pack(tree):
    return jnp.concatenate([tree[n].reshape(-1) for n in SMALL]).reshape(-1, LANES)


def _unpack(flat, like):
    out, off = {}, 0
    flat = flat.reshape(-1)
    for n in SMALL:
        size = math.prod(like[n].shape)
        out[n] = flat[off:off + size].reshape(like[n].shape)
        off += size
    return out


def _block_diag(pw):
    z = jnp.zeros((W_C, W_C), pw.dtype)
    for g in range(4):
        z = z.at[g * 64:(g + 1) * 64, g * 64:(g + 1) * 64].set(pw[g])
    return z


def kernel(x, norm1_g, w_in, sgu_norm_g, sgu_w, sgu_b, pool_w, pool_scale, mix_norm_g, w_o, norm2_g, w_up, conv_w, conv_b, w_down, final_g, loss_target, m_norm1_g, m_w_in, m_sgu_norm_g, m_sgu_w, m_sgu_b, m_pool_w, m_pool_scale, m_mix_norm_g, m_w_o, m_norm2_g, m_w_up, m_conv_w, m_conv_b, m_w_down, m_final_g, v_norm1_g, v_w_in, v_sgu_norm_g, v_sgu_w, v_sgu_b, v_pool_w, v_pool_scale, v_mix_norm_g, v_w_o, v_norm2_g, v_w_up, v_conv_w, v_conv_b, v_w_down, v_final_g):
    weights = dict(norm1_g=norm1_g, w_in=w_in, sgu_norm_g=sgu_norm_g, sgu_w=sgu_w, sgu_b=sgu_b, pool_w=pool_w,
                   pool_scale=pool_scale, mix_norm_g=mix_norm_g, w_o=w_o, norm2_g=norm2_g, w_up=w_up, conv_w=conv_w,
                   conv_b=conv_b, w_down=w_down, final_g=final_g)
    mom = dict(norm1_g=m_norm1_g, w_in=m_w_in, sgu_norm_g=m_sgu_norm_g, sgu_w=m_sgu_w, sgu_b=m_sgu_b, pool_w=m_pool_w,
               pool_scale=m_pool_scale, mix_norm_g=m_mix_norm_g, w_o=m_w_o, norm2_g=m_norm2_g, w_up=m_w_up,
               conv_w=m_conv_w, conv_b=m_conv_b, w_down=m_w_down, final_g=m_final_g)
    var = dict(norm1_g=v_norm1_g, w_in=v_w_in, sgu_norm_g=v_sgu_norm_g, sgu_w=v_sgu_w, sgu_b=v_sgu_b, pool_w=v_pool_w,
               pool_scale=v_pool_scale, mix_norm_g=v_mix_norm_g, w_o=v_w_o, norm2_g=v_norm2_g, w_up=v_w_up,
               conv_w=v_conv_w, conv_b=v_conv_b, w_down=v_w_down, final_g=v_final_g)
    depth = w_in.shape[0]
    s = x.shape[1]
    xs = x.reshape(s, D_MODEL)
    tgt = loss_target.reshape(s, D_MODEL)

    assert depth >= 2
    (g_in0,) = _exchange([w_in[0].astype(_MXU)], "gather_w_in0", False)
    w_in0 = jnp.transpose(g_in0, (1, 0, 2)).reshape(D_MODEL, IN_COLS)
    gather_rest = _Exchange([w_in[1:].astype(_MXU), w_o.astype(_MXU), w_up.astype(_MXU), conv_w, w_down.astype(_MXU)], False)

    tril = jnp.tril(jnp.ones((CHUNK, CHUNK), bool))
    layers = []
    for l in range(depth):
        wbd = _block_diag(pool_w[l])
        layers.append(dict(
            g1=norm1_g[l][None], gn=sgu_norm_g[l][None], wm=jnp.where(tril[None], sgu_w[l], 0.0).astype(_MXU),
            bias=jnp.repeat(sgu_b[l].T, HEAD_DIM, axis=1),
            wbd=wbd.astype(_MXU), wbd_t=wbd.T.astype(_MXU), sc=pool_scale[l][None],
            gmix=mix_norm_g[l][None], g2=norm2_g[l][None]))
    layers[0].update(w_in=w_in0, w_in_t=w_in0.T)

    def place_gathered(g_in, g_o, g_up, g_cw, g_dn):
        full_in = jnp.transpose(g_in, (1, 2, 0, 3)).reshape(depth - 1, D_MODEL, IN_COLS)
        full_o = jnp.transpose(g_o, (1, 0, 2, 3)).reshape(depth, D_MODEL, D_MODEL)
        full_up = jnp.transpose(g_up, (1, 2, 0, 3)).reshape(depth, D_MODEL, 2 * D_FF)
        full_cw = jnp.transpose(g_cw, (1, 2, 0, 3)).reshape(depth, 3, 2 * D_FF)
        full_dn = jnp.transpose(g_dn, (1, 0, 2, 3)).reshape(depth, D_FF, D_MODEL)
        for l in range(depth):
            if l > 0:
                layers[l].update(w_in=full_in[l - 1], w_in_t=full_in[l - 1].T)
            layers[l].update(
                w_o=full_o[l], w_o_t=full_o[l].T, w_up=full_up[l], w_up_t=full_up[l].T,
                cw=jnp.concatenate([full_cw[l], conv_b[l][None], jnp.zeros((CW_ROWS - 4, 2 * D_FF), F32)], axis=0),
                w_dn=full_dn[l], w_dn_t=full_dn[l].T)

    saved = []
    cur = xs
    for l, p in enumerate(layers):
        a_in, qkv, p_in, h1 = _inproj_fwd(cur, p["g1"], p["w_in"], f"inproj_fwd{l}")
        y_a = _sgu_fwd(a_in, p["gn"], p["wm"], p["bias"], f"sgu_fwd{l}")
        y_b, cm, gathered_w = _attn_fwd(qkv, f"attn_fwd{l}", gather_rest if l == 0 else None)
        if l == 0:
            place_gathered(*gathered_w)
        y_c = _pool_fwd(p_in, p["wbd"], p["sc"], f"pool_fwd{l}")
        x_mid, yn = _mix_fwd(y_a, y_b, y_c, p["gmix"], p["w_o"], cur, f"mix_fwd{l}")
        z, h2 = _ffn_up_fwd(x_mid, p["g2"], p["w_up"], f"ffn_up_fwd{l}")
        x_out, act = _ffn_down_fwd(z, p["cw"], p["w_dn"], x_mid, f"ffn_down_fwd{l}")
        saved.append(dict(x_in=cur, a_in=a_in, qkv=qkv, p_in=p_in, h1=h1, y_a=y_a, y_b=y_b, cm=cm, y_c=y_c, x_mid=x_mid,
                          yn=yn, z=z, h2=h2, act=act))
        cur = x_out
    loss_part, dx, dg_final = _final_loss(cur, final_g[None], tgt, "final_loss")

    small = {n: [None] * depth for n in SMALL if n != "final_g"}
    big = {n: [None] * depth for n in SHARDED}
    early = [(n, l) for n in SHARDED for l in range(depth) if (n, l) != ("w_in", 0)]
    wire = lambda n, t: t if n == "conv_w" else t.astype(GRAD_WIRE)
    for l in reversed(range(depth)):
        p, sv = layers[l], saved[l]
        dzg, dzu, dcw = _ffn_down_bwd(dx, sv["z"], p["cw"], p["w_dn_t"], f"ffn_down_bwd{l}")
        big["w_down"][l] = _mm_tn(sv["act"], dx, f"dw_down{l}").reshape(N_DEV, D_FF // N_DEV, D_MODEL)
        dzg_b, dzu_b, dx_mid, dg2 = _ffn_up_bwd(dzg, dzu, p["cw"], p["w_up_t"], sv["x_mid"], p["g2"], dx, f"ffn_up_bwd{l}")
        dw_up = jnp.stack([_mm_tn(sv["h2"], dzg_b, f"dw_up_g{l}"), _mm_tn(sv["h2"], dzu_b, f"dw_up_u{l}")])
        big["w_up"][l] = jnp.transpose(dw_up.reshape(2, D_MODEL, N_DEV // 2, 2 * D_FF // N_DEV), (0, 2, 1, 3)).reshape(
            N_DEV, D_MODEL, 2 * D_FF // N_DEV)
        big["conv_w"][l] = jnp.transpose(dcw[0:3].reshape(3, N_DEV, 2 * D_FF // N_DEV), (1, 0, 2))
        small["conv_b"][l] = dcw[3]
        small["norm2_g"][l] = dg2[0]
        dya, dyb, dyc, dgmix = _mix_bwd(dx_mid, sv["y_a"], sv["y_b"], sv["y_c"], p["gmix"], p["w_o_t"], f"mix_bwd{l}")
        small["mix_norm_g"][l] = dgmix[0]
        big["w_o"][l] = _mm_tn(sv["yn"], dx_mid, f"dw_o{l}").reshape(N_DEV, D_MODEL // N_DEV, D_MODEL)
        dp, dwbd, dsc = _pool_bwd(sv["p_in"], dyc, p["wbd"], p["wbd_t"], p["sc"], f"pool_bwd{l}")
        small["pool_w"][l] = jnp.stack([dwbd[g * 64:(g + 1) * 64, g * 64:(g + 1) * 64] for g in range(4)])
        small["pool_scale"][l] = dsc[0]
        scatter_early = _Exchange([wire(n, big[n][ll]) for n, ll in early], True) if l == 0 else None
        dq, dk, dv, recv_early = _attn_bwd(sv["qkv"], sv["cm"], dyb, f"attn_bwd{l}", scatter_early)
        if l == 0:
            recv_early_all = recv_early
        da, dwm, dbias, dgn = _sgu_bwd(sv["a_in"], dya, p["gn"], p["wm"], p["bias"], f"sgu_bwd{l}")
        small["sgu_w"][l] = dwm
        small["sgu_b"][l] = jnp.sum(dbias.reshape(CHUNK, 4, HEAD_DIM), axis=-1).T
        small["sgu_norm_g"][l] = dgn[0]
        dx, dg1 = _inproj_bwd(da, dq, dk, dv, dp, p["w_in_t"], sv["x_in"], p["g1"], dx_mid, f"inproj_bwd{l}")
        small["norm1_g"][l] = dg1[0]
        pieces = (da, dq, dk, dv, dp)
        dw_in = jnp.concatenate([_mm_tn(sv["h1"], t, f"dw_in{i}_{l}") for i, t in enumerate(pieces)], axis=1)
        big["w_in"][l] = jnp.transpose(dw_in.reshape(D_MODEL, N_DEV, IN_COLS // N_DEV), (1, 0, 2))

    part = {n: jnp.stack(small[n]) for n in small}
    part["final_g"] = dg_final[0]
    packed = _pack(part)
    recv_in0, gathered = _exchange([wire("w_in", big["w_in"][0]), jnp.broadcast_to(packed, (N_DEV,) + packed.shape)],
                                   "scatter_last", True)
    recv = dict(zip(early, recv_early_all))
    recv[("w_in", 0)] = recv_in0

    out_g, out_d, out_m, out_v = {}, {}, {}, {}
    for n in SHARDED:
        res = [_adamw(recv[(n, l)], weights[n][l], mom[n][l], var[n][l], f"adamw_{n}{l}") for l in range(depth)]
        out_g[n], out_d[n], out_m[n], out_v[n] = (jnp.stack([r[i] for r in res]) for i in range(4))

    sg, sd, sm, sv_ = _adamw(gathered, _pack(weights), _pack(mom), _pack(var), "adamw_small")
    for tree, flat in ((out_g, sg), (out_d, sd), (out_m, sm), (out_v, sv_)):
        tree.update(_unpack(flat, weights))

    loss = lax.psum(loss_part[0, 0], ("x", "y", "c"))
    grad_x = dx.reshape(1, s, D_MODEL)
    return (loss, grad_x, *[out_g[n] for n in ORDER], *[out_d[n] for n in ORDER], *[out_m[n] for n in ORDER],
            *[out_v[n] for n in ORDER])
```

```python
import functools
import math

import jax
import jax.numpy as jnp
import numpy as np
from jax import lax
from jax.experimental import pallas as pl
from jax.experimental.pallas import tpu as pltpu

F32 = jnp.float32
BF16 = jnp.bfloat16
_MXU = jnp.bfloat16
GRAD_WIRE = jnp.bfloat16

D_MODEL = 1024
W_A = 256
W_B = 512
W_C = 256
HEAD_DIM = 64
IN_COLS = 2 * W_A + 3 * W_B + W_C
D_FF = 2816
CHUNK = 128
POOL_WINDOWS = (2, 4, 8, 16)
EPS = 1e-6
N_DEV = 8
LANES = 128
SUBLANES = 8
VMEM_LIMIT = 48 * 1024 * 1024
ROW_TILE = 512

ADAM_LR = 0.001
ADAM_B1 = 0.9
ADAM_B2 = 0.999
ADAM_EPS = 1e-08
ADAM_WD = 0.01
ADAM_STEP = 10

INV_SQRT2 = 1.0 / math.sqrt(2.0)
INV_SQRT_2PI = 1.0 / math.sqrt(2.0 * math.pi)


def _pcall(body, **kw):
    return pl.pallas_call(body, **kw)


def _params(dims=None):
    return pltpu.CompilerParams(dimension_semantics=dims, vmem_limit_bytes=VMEM_LIMIT)


def _tile(n, pref, mult=SUBLANES):
    t = min(n, pref) // mult * mult
    while t >= mult:
        if n % t == 0:
            return t
        t -= mult
    return n


def _iota(shape, dim):
    return lax.broadcasted_iota(jnp.int32, shape, dim)


def _dot(a, b):
    return jnp.dot(a.astype(_MXU), b.astype(_MXU), preferred_element_type=F32)


def _dot_nt(a, b):
    return lax.dot_general(a.astype(_MXU), b.astype(_MXU), (((1,), (1,)), ((), ())), preferred_element_type=F32)


def _dot_tn(a, b):
    return lax.dot_general(a.astype(_MXU), b.astype(_MXU), (((0,), (0,)), ((), ())), preferred_element_type=F32)


def _split(x):
    hi = x.astype(BF16)
    lo = (x - hi.astype(F32)).astype(BF16)
    return hi, lo


def _dot_sel(x, sel):
    hi, lo = _split(x)
    return jnp.dot(hi, sel, preferred_element_type=F32) + jnp.dot(lo, sel, preferred_element_type=F32)


def _sel_dot(sel, x):
    hi, lo = _split(x)
    return jnp.dot(sel, hi, preferred_element_type=F32) + jnp.dot(sel, lo, preferred_element_type=F32)


def _group_mat(n):
    r = jnp.right_shift(_iota((n, n), 0), 6)
    c = jnp.right_shift(_iota((n, n), 1), 6)
    return (r == c).astype(BF16)


def _lane_group_mask(n, g):
    lane = _iota((1, n), 1)
    return (lane >= g * HEAD_DIM) & (lane < (g + 1) * HEAD_DIM)


def _gelu(a):
    return 0.5 * a * (1.0 + lax.erf(a * INV_SQRT2))


def _gelu_grad(a):
    return 0.5 * (1.0 + lax.erf(a * INV_SQRT2)) + a * jnp.exp(-0.5 * a * a) * INV_SQRT_2PI


def _rms(x):
    return lax.rsqrt(jnp.mean(x * x, axis=-1, keepdims=True) + EPS)


def _rms_bwd(x, g, dy):
    r = _rms(x)
    dyg = dy * g
    m2 = jnp.mean(dyg * x, axis=-1, keepdims=True)
    dx = r * dyg - x * (r * r * r) * m2
    dg = jnp.sum(dy * x * r, axis=0, keepdims=True)
    return dx, dg


def _shift_down(cur, prev8, k):
    rolled = pltpu.roll(cur, k, 0)
    row8 = _iota(prev8.shape, 0)
    top = jnp.where(row8 < k, pltpu.roll(prev8, k, 0), rolled[0:SUBLANES])
    return jnp.concatenate([top, rolled[SUBLANES:]], axis=0)


def _shift_up(cur, next8, k):
    n = cur.shape[0]
    rolled = pltpu.roll(cur, n - k, 0)
    row8 = _iota(next8.shape, 0)
    bot = jnp.where(row8 >= SUBLANES - k, pltpu.roll(next8, SUBLANES - k, 0), rolled[n - SUBLANES:])
    return jnp.concatenate([rolled[:n - SUBLANES], bot], axis=0)


def _mesh_pos():
    return lax.axis_index("x"), lax.axis_index("y"), lax.axis_index("c")


def _peer(x, y, c, k):
    px = 1 - x if k & 4 else x
    py = 1 - y if k & 2 else y
    pc = 1 - c if k & 1 else c
    return px, py, pc


class _Exchange:
    def __init__(self, arrs, scatter):
        self.arrs, self.scatter, self.n = list(arrs), scatter, len(arrs)
        self.out_shapes = [jax.ShapeDtypeStruct(a.shape if scatter else (N_DEV,) + a.shape, a.dtype) for a in arrs]
        self.specs = [pl.BlockSpec(memory_space=pl.ANY)] * self.n
        self.semaphores = [pltpu.SemaphoreType.DMA((self.n * (N_DEV - 1),)),
                           pltpu.SemaphoreType.DMA((self.n * (N_DEV - 1),)), pltpu.SemaphoreType.DMA((self.n,))]

    def _copies(self, ins, outs, sems):
        send, recv, loc = sems
        x, y, c = _mesh_pos()
        me = 4 * x + 2 * y + c
        src = (lambda a, idx: ins[a].at[idx]) if self.scatter else (lambda a, idx: ins[a])
        starts = [pltpu.make_async_copy(src(a, me), outs[a].at[me], loc.at[a]) for a in range(self.n)]
        waits = list(starts)
        for k in range(1, N_DEV):
            px, py, pc = _peer(x, y, c, k)
            pidx = 4 * px + 2 * py + pc
            for a in range(self.n):
                s = a * (N_DEV - 1) + k - 1
                common = dict(src_ref=src(a, pidx), send_sem=send.at[s], recv_sem=recv.at[s],
                              device_id=(px, py, pc), device_id_type=pl.DeviceIdType.MESH)
                starts.append(pltpu.make_async_remote_copy(dst_ref=outs[a].at[me], **common))
                waits.append(pltpu.make_async_remote_copy(dst_ref=outs[a].at[pidx], **common))
        return starts, waits

    def start(self, ins, outs, sems):
        for cp in self._copies(ins, outs, sems)[0]:
            cp.start()

    def wait(self, ins, outs, sems):
        for cp in self._copies(ins, outs, sems)[1]:
            cp.wait()


def _with_exchange(compute, ride, n_in, n_out, n_scratch, last_step):
    if ride is None:
        return compute
    nx = ride.n

    def body(*refs):
        ins, ride_in = refs[:n_in], refs[n_in:n_in + nx]
        outs = refs[n_in + nx:n_in + nx + n_out]
        ride_out = refs[n_in + nx + n_out:n_in + 2 * nx + n_out]
        scratch = refs[n_in + 2 * nx + n_out:n_in + 2 * nx + n_out + n_scratch]
        sems = refs[n_in + 2 * nx + n_out + n_scratch:]
        step = (pl.program_id(0), pl.program_id(1))

        @pl.when((step[0] == 0) & (step[1] == 0))
        def _():
            ride.start(ride_in, ride_out, sems)

        compute(*ins, *outs, *scratch)

        @pl.when((step[0] == last_step[0]) & (step[1] == last_step[1]))
        def _():
            ride.wait(ride_in, ride_out, sems)

    return body


def _exchange(arrs, name, scatter):
    ex = _Exchange(arrs, scatter)
    n = ex.n

    def body(*refs):
        ins, outs, sems = refs[:n], refs[n:2 * n], refs[2 * n:]
        ex.start(ins, outs, sems)
        ex.wait(ins, outs, sems)

    return _pcall(body, name=name, out_shape=ex.out_shapes, in_specs=ex.specs, out_specs=ex.specs,
                  scratch_shapes=ex.semaphores, compiler_params=pltpu.CompilerParams(has_side_effects=True))(*arrs)


def _mm_tn(a, b, name):
    s, m = a.shape
    n = b.shape[1]
    tm, tn, tk = _tile(m, 1408, LANES), _tile(n, 1408, LANES), _tile(s, 1024)

    def body(a_ref, b_ref, o_ref):
        @pl.when(pl.program_id(2) == 0)
        def _():
            o_ref[...] = jnp.zeros_like(o_ref)
        o_ref[...] += _dot_tn(a_ref[...], b_ref[...])

    return _pcall(
        body, name=name, grid=(m // tm, n // tn, s // tk),
        in_specs=[pl.BlockSpec((tk, tm), lambda i, j, k: (k, i)), pl.BlockSpec((tk, tn), lambda i, j, k: (k, j))],
        out_specs=pl.BlockSpec((tm, tn), lambda i, j, k: (i, j)),
        out_shape=jax.ShapeDtypeStruct((m, n), F32),
        compiler_params=_params(("parallel", "parallel", "arbitrary")),
    )(a, b)


def _inproj_fwd(x, g, w, name):
    s = x.shape[0]
    tm = _tile(s, ROW_TILE)

    def body(x_ref, g_ref, w_ref, a_ref, qkv_ref, p_ref, h_ref):
        xv = x_ref[...]
        h = (xv * _rms(xv) * g_ref[...]).astype(_MXU)
        h_ref[...] = h
        a_ref[...] = jnp.dot(h, w_ref[:, 0:2 * W_A], preferred_element_type=F32)
        q = jnp.dot(h, w_ref[:, 2 * W_A:2 * W_A + W_B], preferred_element_type=F32)
        qkv_ref[:, 0:W_B] = (q * 0.125).astype(_MXU)
        kv = jnp.dot(h, w_ref[:, 2 * W_A + W_B:2 * W_A + 3 * W_B], preferred_element_type=F32)
        qkv_ref[:, W_B:3 * W_B] = kv.astype(_MXU)
        p_ref[...] = jnp.dot(h, w_ref[:, 2 * W_A + 3 * W_B:IN_COLS], preferred_element_type=F32)

    row = lambda n: pl.BlockSpec((tm, n), lambda i: (i, 0))
    return _pcall(
        body, name=name, grid=(s // tm,),
        in_specs=[row(D_MODEL), pl.BlockSpec((1, D_MODEL), lambda i: (0, 0)),
                  pl.BlockSpec((D_MODEL, IN_COLS), lambda i: (0, 0))],
        out_specs=[row(2 * W_A), row(3 * W_B), row(W_C), row(D_MODEL)],
        out_shape=[jax.ShapeDtypeStruct((s, 2 * W_A), F32), jax.ShapeDtypeStruct((s, 3 * W_B), _MXU),
                   jax.ShapeDtypeStruct((s, W_C), F32), jax.ShapeDtypeStruct((s, D_MODEL), _MXU)],
        compiler_params=_params(("parallel",)),
    )(x, g, w)


def _inproj_bwd(da, dq, dk, dv, dp, wt, x, g, dres, name):
    s = x.shape[0]
    tm = _tile(s, ROW_TILE)

    def body(da_ref, dq_ref, dk_ref, dv_ref, dp_ref, wt_ref, x_ref, g_ref, dres_ref, dx_ref, dg_ref):
        dh = _dot(da_ref[...], wt_ref[0:2 * W_A, :])
        dh += _dot(dq_ref[...], wt_ref[2 * W_A:2 * W_A + W_B, :])
        dh += _dot(dk_ref[...], wt_ref[2 * W_A + W_B:2 * W_A + 2 * W_B, :])
        dh += _dot(dv_ref[...], wt_ref[2 * W_A + 2 * W_B:2 * W_A + 3 * W_B, :])
        dh += _dot(dp_ref[...], wt_ref[2 * W_A + 3 * W_B:IN_COLS, :])
        dx, dg = _rms_bwd(x_ref[...], g_ref[...], dh)
        dx_ref[...] = dres_ref[...] + dx

        @pl.when(pl.program_id(0) == 0)
        def _():
            dg_ref[...] = jnp.zeros_like(dg_ref)
        dg_ref[...] += dg

    row = lambda n: pl.BlockSpec((tm, n), lambda i: (i, 0))
    vec = pl.BlockSpec((1, D_MODEL), lambda i: (0, 0))
    return _pcall(
        body, name=name, grid=(s // tm,),
        in_specs=[row(2 * W_A), row(W_B), row(W_B), row(W_B), row(W_C),
                  pl.BlockSpec((IN_COLS, D_MODEL), lambda i: (0, 0)), row(D_MODEL), vec, row(D_MODEL)],
        out_specs=[row(D_MODEL), vec],
        out_shape=[jax.ShapeDtypeStruct((s, D_MODEL), F32), jax.ShapeDtypeStruct((1, D_MODEL), F32)],
        compiler_params=_params(("arbitrary",)),
    )(da, dq, dk, dv, dp, wt, x, g, dres)


def _sgu_core(a, gn, wm_ref, bias):
    ga = _gelu(a)
    u, v0 = ga[:, 0:W_A], ga[:, W_A:2 * W_A]
    r = lax.rsqrt(_dot_sel(v0 * v0, _group_mat(W_A)) * (1.0 / HEAD_DIM) + EPS)
    vn = v0 * r * gn
    sv = bias
    for h in range(W_A // HEAD_DIM):
        sv = sv + _dot(wm_ref[h], jnp.where(_lane_group_mask(W_A, h), vn, 0.0))
    return u, v0, r, vn, sv


def _sgu_fwd(a, gn, wm, bias, name):
    s = a.shape[0]

    def body(a_ref, gn_ref, wm_ref, b_ref, y_ref):
        u, _, _, _, sv = _sgu_core(a_ref[...], gn_ref[...], wm_ref, b_ref[...])
        y_ref[...] = u * sv

    return _pcall(
        body, name=name, grid=(s // CHUNK,),
        in_specs=[pl.BlockSpec((CHUNK, 2 * W_A), lambda i: (i, 0)), pl.BlockSpec((1, W_A), lambda i: (0, 0)),
                  pl.BlockSpec((4, CHUNK, CHUNK), lambda i: (0, 0, 0)), pl.BlockSpec((CHUNK, W_A), lambda i: (0, 0))],
        out_specs=pl.BlockSpec((CHUNK, W_A), lambda i: (i, 0)),
        out_shape=jax.ShapeDtypeStruct((s, W_A), F32),
        compiler_params=_params(("parallel",)),
    )(a, gn, wm, bias)


def _sgu_bwd(a, dy, gn, wm, bias, name):
    s = a.shape[0]

    def body(a_ref, dy_ref, gn_ref, wm_ref, b_ref, da_ref, dwm_ref, db_ref, dgn_ref):
        @pl.when(pl.program_id(0) == 0)
        def _():
            dwm_ref[...] = jnp.zeros_like(dwm_ref)
            db_ref[...] = jnp.zeros_like(db_ref)
            dgn_ref[...] = jnp.zeros_like(dgn_ref)

        av, gnv, dyv = a_ref[...], gn_ref[...], dy_ref[...]
        u, v0, r, vn, sv = _sgu_core(av, gnv, wm_ref, b_ref[...])
        du = dyv * sv
        ds = dyv * u
        db_ref[...] += ds
        tril = _iota((CHUNK, CHUNK), 1) <= _iota((CHUNK, CHUNK), 0)
        dvn = jnp.zeros_like(vn)
        for h in range(W_A // HEAD_DIM):
            dsm = jnp.where(_lane_group_mask(W_A, h), ds, 0.0)
            dwm_ref[h] += jnp.where(tril, _dot_nt(dsm, vn), 0.0)
            dvn = dvn + _dot_tn(wm_ref[h], dsm)
        dgn_ref[...] += jnp.sum(dvn * v0 * r, axis=0, keepdims=True)
        dvg = dvn * gnv
        m2 = _dot_sel(dvg * v0, _group_mat(W_A)) * (1.0 / HEAD_DIM)
        dv0 = r * dvg - v0 * (r * r * r) * m2
        gp = _gelu_grad(av)
        da_ref[:, 0:W_A] = du * gp[:, 0:W_A]
        da_ref[:, W_A:2 * W_A] = dv0 * gp[:, W_A:2 * W_A]

    return _pcall(
        body, name=name, grid=(s // CHUNK,),
        in_specs=[pl.BlockSpec((CHUNK, 2 * W_A), lambda i: (i, 0)), pl.BlockSpec((CHUNK, W_A), lambda i: (i, 0)),
                  pl.BlockSpec((1, W_A), lambda i: (0, 0)), pl.BlockSpec((4, CHUNK, CHUNK), lambda i: (0, 0, 0)),
                  pl.BlockSpec((CHUNK, W_A), lambda i: (0, 0))],
        out_specs=[pl.BlockSpec((CHUNK, 2 * W_A), lambda i: (i, 0)), pl.BlockSpec((4, CHUNK, CHUNK), lambda i: (0, 0, 0)),
                   pl.BlockSpec((CHUNK, W_A), lambda i: (0, 0)), pl.BlockSpec((1, W_A), lambda i: (0, 0))],
        out_shape=[jax.ShapeDtypeStruct((s, 2 * W_A), F32), jax.ShapeDtypeStruct((4, CHUNK, CHUNK), F32),
                   jax.ShapeDtypeStruct((CHUNK, W_A), F32), jax.ShapeDtypeStruct((1, W_A), F32)],
        compiler_params=_params(("arbitrary",)),
    )(a, dy, gn, wm, bias)


KBLK = 128
ROW_CHUNK = 64
FWD_UNROLL = 4
BWD_UNROLL = 4
MASKED_SCORE = -1e30


def _attn_fwd(qkv, name, ride=None):
    s = qkv.shape[0]
    tq = _tile(s, 256, KBLK)
    npairs = W_B // LANES
    assert s // KBLK <= LANES
    rides = [] if ride is None else [ride]

    def body(q_ref, k_ref, v_ref, o_ref, cm_ref, z_ref, zw_ref, sums_ref, carry_ref, rs_ref, lb_ref, a_ref):
        i = pl.program_id(1)
        q = q_ref[...]
        lane = _iota((1, LANES), 1)
        lane_lo = lane < HEAD_DIM
        hmask = (lane_lo, jnp.logical_not(lane_lo))
        tri = (_iota((KBLK, KBLK), 0) >= _iota((KBLK, KBLK), 1)).astype(BF16)
        dmat = _iota((tq, KBLK), 1) - (_iota((tq, KBLK), 0) + i * tq)
        chunks = [slice(r, r + ROW_CHUNK) for r in range(0, tq, ROW_CHUNK)]
        heads = [slice(hh * LANES, (hh + 1) * LANES) for hh in range(2)]
        nk = (i + 1) * (tq // KBLK)

        cm_ref[...] = jnp.zeros_like(cm_ref)

        def before(b):
            return jnp.where((b >= 0) & (b < nk), -b * KBLK, jnp.iinfo(jnp.int32).min)

        def per_head(block):
            return jnp.concatenate([jnp.where(m, block, jnp.zeros_like(block)) for m in hmask], axis=0)

        def scores(b, p):
            ks = pl.multiple_of(jnp.clip(b, 0, nk - 1) * KBLK, KBLK)
            z_ref[p] = _dot_nt(q, per_head(k_ref[pl.ds(ks, KBLK), :]))

        def logs(b, p):
            t = before(b)
            for hh in range(2):
                for rows in chunks:
                    z = jnp.where(dmat[rows] < t, z_ref[p, rows, heads[hh]], MASKED_SCORE)
                    zw_ref[p, rows, heads[hh]] = z
                    l = -(jnp.maximum(z, 0.0) + jnp.log(1.0 + jnp.exp(-jnp.abs(z))))
                    lb_ref[p, hh, rows, :] = l.astype(BF16)
                    rs_ref[p, hh, rows, :] = jnp.zeros((ROW_CHUNK, KBLK), F32) + jnp.sum(l, axis=1, keepdims=True)

        def sums(b, p):
            for hh in range(2):
                sums_ref[p, hh] = jnp.dot(lb_ref[p, hh], tri, preferred_element_type=F32)

        def weights(b, p):
            pick = lane == b
            for hh in range(2):
                for rows in chunks:
                    c = carry_ref[hh, rows, :]
                    arg = zw_ref[p, rows, heads[hh]] + c + sums_ref[p, hh, rows, :]
                    a_ref[p, rows, heads[hh]] = jnp.exp(arg).astype(_MXU)
                    cm_ref[rows, heads[hh]] = jnp.where(pick, c, cm_ref[rows, heads[hh]])
                    carry_ref[hh, rows, :] = c + rs_ref[p, hh, rows, :]

        def out(b, p, acc):
            ks = pl.multiple_of(jnp.minimum(b, nk - 1) * KBLK, KBLK)
            vb = v_ref[pl.ds(ks, KBLK), :]
            for hh in range(2):
                acc = acc + jnp.dot(a_ref[p, :, heads[hh]], jnp.where(hmask[hh], vb, jnp.zeros_like(vb)),
                                    preferred_element_type=F32)
            return acc

        trips = (nk + 4 + FWD_UNROLL - 1) // FWD_UNROLL

        def step(it, acc):
            for u in range(FWD_UNROLL):
                b = trips * FWD_UNROLL - 1 - (FWD_UNROLL * it + u)
                p = 1 - u % 2
                acc = out(b, p, acc)
                weights(b - 1, 1 - p)
                sums(b - 2, p)
                logs(b - 3, 1 - p)
                scores(b - 4, p)
            return acc

        z_ref[...] = jnp.zeros_like(z_ref)
        zw_ref[...] = jnp.full_like(zw_ref, MASKED_SCORE)
        for ref in (sums_ref, carry_ref, rs_ref, lb_ref, a_ref):
            ref[...] = jnp.zeros_like(ref)
        o_ref[...] = lax.fori_loop(0, trips, step, q.astype(F32) * 0.0)

    scratch = [pltpu.VMEM((2, tq, 2 * KBLK), F32), pltpu.VMEM((2, tq, 2 * KBLK), F32),
               pltpu.VMEM((2, 2, tq, KBLK), F32), pltpu.VMEM((2, tq, KBLK), F32), pltpu.VMEM((2, 2, tq, KBLK), F32),
               pltpu.VMEM((2, 2, tq, KBLK), BF16), pltpu.VMEM((2, tq, 2 * KBLK), _MXU)]
    res = _pcall(
        _with_exchange(body, ride, 3, 2, len(scratch), (npairs - 1, s // tq - 1)), name=name, grid=(npairs, s // tq),
        in_specs=[pl.BlockSpec((tq, LANES), lambda p, i: (i, p)),
                  pl.BlockSpec((s, LANES), lambda p, i: (0, npairs + p)),
                  pl.BlockSpec((s, LANES), lambda p, i: (0, 2 * npairs + p))] + [sp for r in rides for sp in r.specs],
        out_specs=[pl.BlockSpec((tq, LANES), lambda p, i: (i, p)), pl.BlockSpec((tq, 2 * LANES), lambda p, i: (i, p))]
                  + [sp for r in rides for sp in r.specs],
        out_shape=[jax.ShapeDtypeStruct((s, W_B), F32), jax.ShapeDtypeStruct((s, 2 * W_B), F32)]
                  + [sh for r in rides for sh in r.out_shapes],
        scratch_shapes=scratch + [sem for r in rides for sem in r.semaphores],
        compiler_params=_params(("arbitrary", "arbitrary")),
    )(qkv, qkv, qkv, *[a for r in rides for a in r.arrs])
    return res[0], res[1], list(res[2:])


def _attn_bwd(qkv, cm, do, name, ride=None):
    s = qkv.shape[0]
    tq = _tile(s, 256, KBLK)
    npairs = W_B // LANES
    rides = [] if ride is None else [ride]

    nq = s // tq
    per_tile = tq // KBLK
    assert nq % 2 == 0 and per_tile % 2 == 0
    ntot = (nq + 1) * per_tile
    assert (ntot + 6) % BWD_UNROLL == 0

    def body(qa_ref, k_ref, v_ref, cma_ref, doa_ref, qb_ref, cmb_ref, dob_ref, dqa_ref, dqb_ref, dk_ref, dv_ref,
             z_ref, zw_ref, da_ref, g_ref, sig_ref, cum_ref, gp_ref, gcarry_ref, lb_ref, gb_ref, a_ref, dz_ref,
             dkt_ref, dvt_ref, q_st, do_st, qt_st, dot_st, cm_st):
        i = pl.program_id(1)

        @pl.when(i == 0)
        def _():
            dkt_ref[...] = jnp.zeros_like(dkt_ref)
            dvt_ref[...] = jnp.zeros_like(dvt_ref)

        tiles = (i, nq - 1 - i)
        nk_a = (i + 1) * per_tile
        for t, (q_in, do_in, cm_in) in enumerate(((qa_ref, doa_ref, cma_ref), (qb_ref, dob_ref, cmb_ref))):
            q_st[t] = q_in[...]
            do_st[t] = do_in[...].astype(_MXU)
            qt_st[t] = q_in[...].astype(F32).T.astype(_MXU)
            dot_st[t] = do_in[...].T.astype(_MXU)
            cm_st[t] = cm_in[...]

        lane = _iota((1, LANES), 1)
        lane_lo = lane < HEAD_DIM
        hmask = (lane_lo, jnp.logical_not(lane_lo))
        tri = (_iota((KBLK, KBLK), 0) >= _iota((KBLK, KBLK), 1)).astype(BF16)
        prefix = (_iota((KBLK, KBLK), 0) <= _iota((KBLK, KBLK), 1)).astype(BF16)
        dmat = _iota((tq, KBLK), 1) - _iota((tq, KBLK), 0)
        chunks = [slice(r, r + ROW_CHUNK) for r in range(0, tq, ROW_CHUNK)]
        heads = [slice(hh * LANES, (hh + 1) * LANES) for hh in range(2)]

        def locate(v):
            second = v >= nk_a
            return ((v >= 0) & (v < ntot), second.astype(jnp.int32), jnp.where(second, tiles[1], tiles[0]),
                    jnp.where(second, v - nk_a, v))

        def before(v):
            valid, _, tile, b = locate(v)
            return jnp.where(valid, tile * tq - b * KBLK, jnp.iinfo(jnp.int32).min)

        def key_block(v):
            return jnp.clip(locate(v)[3], 0, nblk - 1)

        def block_rows(v):
            return pl.ds(pl.multiple_of(key_block(v) * KBLK, KBLK), KBLK)

        def per_head(block):
            return jnp.concatenate([jnp.where(m, block, jnp.zeros_like(block)) for m in hmask], axis=0)

        feature_lo = _iota((LANES, KBLK), 0) < HEAD_DIM

        def own_features(side_by_side):
            return jnp.where(feature_lo, side_by_side[:, 0:KBLK], side_by_side[:, KBLK:2 * KBLK])

        def m1(b, p):
            z_ref[p] = _dot_nt(q_st[locate(b)[1]], per_head(k_ref[block_rows(b), :]))

        def v1(b, p):
            t = before(b)
            for hh in range(2):
                for rows in chunks:
                    z = jnp.where(dmat[rows] < t, z_ref[p, rows, heads[hh]], MASKED_SCORE)
                    zw_ref[p, rows, heads[hh]] = z
                    l = -(jnp.maximum(z, 0.0) + jnp.log(1.0 + jnp.exp(-jnp.abs(z))))
                    lb_ref[p, hh, rows, :] = l.astype(BF16)

        def m2(b, p):
            for hh in range(2):
                cum_ref[p, hh] = jnp.dot(lb_ref[p, hh], tri, preferred_element_type=F32)
            da_ref[p] = _dot_nt(do_st[locate(b)[1]], per_head(v_ref[block_rows(b), :]))

        def v2(b, p):
            valid, which, _, blk = locate(b)
            pick = lane == jnp.where(valid, blk, -1)
            for hh in range(2):
                for rows in chunks:
                    c = jnp.sum(jnp.where(pick, cm_st[which, rows, heads[hh]], 0.0), axis=1, keepdims=True)
                    z = zw_ref[p, rows, heads[hh]]
                    a = jnp.exp(z + c + cum_ref[p, hh, rows, :])
                    g = a * da_ref[p, rows, heads[hh]]
                    a_ref[p, rows, heads[hh]] = a.astype(_MXU)
                    g_ref[p, rows, heads[hh]] = g
                    gb_ref[p, hh, rows, :] = g.astype(BF16)
                    sig_ref[p, rows, heads[hh]] = 0.5 * jnp.tanh(0.5 * z) + 0.5

        def m3(b, p):
            for hh in range(2):
                gp_ref[p, hh] = jnp.dot(gb_ref[p, hh], prefix, preferred_element_type=F32)
            dvt_ref[key_block(b)] += own_features(jnp.dot(dot_st[locate(b)[1]], a_ref[p], preferred_element_type=F32))

        def v3(b, p):
            restart = b == nk_a
            for hh in range(2):
                for rows in chunks:
                    gc = jnp.where(restart, 0.0, gcarry_ref[hh, rows, :])
                    g = g_ref[p, rows, heads[hh]]
                    dz = g - sig_ref[p, rows, heads[hh]] * (gc + gp_ref[p, hh, rows, :])
                    dz_ref[p, rows, heads[hh]] = dz.astype(_MXU)
                    gcarry_ref[hh, rows, :] = gc + jnp.sum(g, axis=1, keepdims=True)

        def m4(b, p, dqs):
            which = locate(b)[1]
            kb = k_ref[block_rows(b), :]
            part = None
            for hh in range(2):
                d = jnp.dot(dz_ref[p, :, heads[hh]], jnp.where(hmask[hh], kb, jnp.zeros_like(kb)),
                            preferred_element_type=F32)
                part = d if part is None else part + d
            dkt_ref[key_block(b)] += own_features(jnp.dot(qt_st[which], dz_ref[p], preferred_element_type=F32))
            return dqs[0] + jnp.where(which == 0, part, 0.0), dqs[1] + jnp.where(which == 1, part, 0.0)

        def step(it, dqs):
            for u in range(BWD_UNROLL):
                j = BWD_UNROLL * it + u
                p = u % 2
                dqs = m4(j - 6, p, dqs)
                m3(j - 4, p)
                m2(j - 2, p)
                m1(j, p)
                v3(j - 5, 1 - p)
                v2(j - 3, 1 - p)
                v1(j - 1, 1 - p)
            return dqs

        zw_ref[...] = jnp.full_like(zw_ref, MASKED_SCORE)
        for ref in (z_ref, da_ref, g_ref, sig_ref, cum_ref, gp_ref, gcarry_ref, lb_ref, gb_ref, a_ref, dz_ref):
            ref[...] = jnp.zeros_like(ref)
        zero = doa_ref[...] * 0.0
        dq_a, dq_b = lax.fori_loop(0, (ntot + 6) // BWD_UNROLL, step, (zero, zero))
        dqa_ref[...] = dq_a * 0.125
        dqb_ref[...] = dq_b * 0.125

        @pl.when(i == nq // 2 - 1)
        def _():
            def untranspose(blk, carry):
                rows = pl.ds(pl.multiple_of(blk * KBLK, KBLK), KBLK)
                dk_ref[rows, :] = dkt_ref[blk].T
                dv_ref[rows, :] = dvt_ref[blk].T
                return carry
            lax.fori_loop(0, nblk, untranspose, 0)

    first = lambda width: pl.BlockSpec((tq, width), lambda p, i: (i, p))
    second = lambda width: pl.BlockSpec((tq, width), lambda p, i: (nq - 1 - i, p))
    second_out = pl.BlockSpec((tq, LANES), lambda p, i: (nq // 2 - 1 - i, p))
    nblk = s // KBLK
    full = pl.BlockSpec((s, LANES), lambda p, i: (0, p))
    scratch = ([pltpu.VMEM((2, tq, 2 * KBLK), F32)] * 5 + [pltpu.VMEM((2, 2, tq, KBLK), F32),
               pltpu.VMEM((2, 2, tq, KBLK), F32), pltpu.VMEM((2, tq, KBLK), F32)]
               + [pltpu.VMEM((2, 2, tq, KBLK), BF16)] * 2 + [pltpu.VMEM((2, tq, 2 * KBLK), _MXU)] * 2
               + [pltpu.VMEM((nblk, LANES, KBLK), F32)] * 2
               + [pltpu.VMEM((2, tq, LANES), _MXU)] * 2 + [pltpu.VMEM((2, LANES, tq), _MXU)] * 2
               + [pltpu.VMEM((2, tq, 2 * LANES), F32)])
    res = _pcall(
        _with_exchange(body, ride, 8, 4, len(scratch), (npairs - 1, nq // 2 - 1)), name=name, grid=(npairs, nq // 2),
        in_specs=[first(LANES), pl.BlockSpec((s, LANES), lambda p, i: (0, npairs + p)),
                  pl.BlockSpec((s, LANES), lambda p, i: (0, 2 * npairs + p)), first(2 * LANES), first(LANES),
                  second(LANES), second(2 * LANES), second(LANES)] + [sp for r in rides for sp in r.specs],
        out_specs=[first(LANES), second_out, full, full] + [sp for r in rides for sp in r.specs],
        out_shape=[jax.ShapeDtypeStruct((s // 2, W_B), F32)] * 2 + [jax.ShapeDtypeStruct((s, W_B), F32)] * 2
                  + [sh for r in rides for sh in r.out_shapes],
        scratch_shapes=scratch + [sem for r in rides for sem in r.semaphores],
        compiler_params=_params(("arbitrary", "arbitrary")),
    )(qkv, qkv, qkv, cm, do, qkv, cm, do, *[a for r in rides for a in r.arrs])
    return jnp.concatenate([res[0], res[1]], axis=0), res[2], res[3], list(res[4:])


POOL_HALO = 128


def _pool_window_lane():
    lane = _iota((1, W_C), 1)
    w = jnp.where(lane < 64, POOL_WINDOWS[0], jnp.where(lane < 128, POOL_WINDOWS[1],
                  jnp.where(lane < 192, POOL_WINDOWS[2], POOL_WINDOWS[3])))
    return w.astype(F32)


def _pool_count(tm, i):
    pos = (_iota((tm, W_C), 0) + (i * tm + 1)).astype(F32)
    return jnp.minimum(pos, _pool_window_lane())


def _pool_centered(prev, cur, cnt):
    tm = cur.shape[0]
    xx = jnp.concatenate([prev, cur], axis=0)
    hi, lo = _split(xx)
    t = _iota((tm, tm + POOL_HALO), 0)
    cc = _iota((tm, tm + POOL_HALO), 1) - POOL_HALO
    wsum = jnp.zeros_like(cur)
    for g, w in enumerate(POOL_WINDOWS):
        band = ((cc <= t) & (cc > t - w)).astype(BF16)
        mg = _lane_group_mask(W_C, g)
        wsum += jnp.dot(band, jnp.where(mg, hi, jnp.zeros_like(hi)), preferred_element_type=F32)
        wsum += jnp.dot(band, jnp.where(mg, lo, jnp.zeros_like(lo)), preferred_element_type=F32)
    return wsum / cnt - cur


def _pool_fwd(p, wbd, sc, name):
    s = p.shape[0]
    tm = _tile(s, 256, POOL_HALO)
    r = tm // POOL_HALO

    def body(pp_ref, p_ref, w_ref, sc_ref, y_ref):
        i = pl.program_id(0)
        prev = jnp.where(i > 0, pp_ref[...], 0.0)
        d = _pool_centered(prev, p_ref[...], _pool_count(tm, i))
        y_ref[...] = _dot(d, w_ref[...]) * sc_ref[...]

    return _pcall(
        body, name=name, grid=(s // tm,),
        in_specs=[pl.BlockSpec((POOL_HALO, W_C), lambda i: (jnp.maximum(i * r - 1, 0), 0)),
                  pl.BlockSpec((tm, W_C), lambda i: (i, 0)), pl.BlockSpec((W_C, W_C), lambda i: (0, 0)),
                  pl.BlockSpec((1, W_C), lambda i: (0, 0))],
        out_specs=pl.BlockSpec((tm, W_C), lambda i: (i, 0)),
        out_shape=jax.ShapeDtypeStruct((s, W_C), F32),
        compiler_params=_params(("parallel",)),
    )(p, p, wbd, sc)


def _pool_bwd(p, dy, wbd, wbdt, sc, name):
    s = p.shape[0]
    tm = _tile(s, 256, POOL_HALO)
    r = tm // POOL_HALO
    nt = s // tm

    def body(pp_ref, p_ref, dy_ref, dyn_ref, w_ref, wt_ref, sc_ref, dp_ref, dw_ref, dsc_ref):
        i = pl.program_id(0)

        @pl.when(i == 0)
        def _():
            dw_ref[...] = jnp.zeros_like(dw_ref)
            dsc_ref[...] = jnp.zeros_like(dsc_ref)

        scv = sc_ref[...]
        cnt = _pool_count(tm, i)
        prev = jnp.where(i > 0, pp_ref[...], 0.0)
        d = _pool_centered(prev, p_ref[...], cnt)
        e = _dot(d, w_ref[...])
        dyv = dy_ref[...]
        de = dyv * scv
        dsc_ref[...] += jnp.sum(dyv * e, axis=0, keepdims=True)
        dw_ref[...] += _dot_tn(d, de)
        dd = _dot(de, wt_ref[...])
        ddn = jnp.where(i < nt - 1, _dot(dyn_ref[...] * scv, wt_ref[...]), 0.0)
        yy = jnp.concatenate([dd / cnt, ddn / _pool_window_lane()], axis=0)
        hi, lo = _split(yy)
        t = _iota((tm, tm + POOL_HALO), 0)
        cc = _iota((tm, tm + POOL_HALO), 1)
        acc = jnp.zeros_like(dd)
        for g, w in enumerate(POOL_WINDOWS):
            band = ((cc >= t) & (cc < t + w)).astype(BF16)
            mg = _lane_group_mask(W_C, g)
            acc += jnp.dot(band, jnp.where(mg, hi, jnp.zeros_like(hi)), preferred_element_type=F32)
            acc += jnp.dot(band, jnp.where(mg, lo, jnp.zeros_like(lo)), preferred_element_type=F32)
        dp_ref[...] = acc - dd

    tile = pl.BlockSpec((tm, W_C), lambda i: (i, 0))
    mat = pl.BlockSpec((W_C, W_C), lambda i: (0, 0))
    vec = pl.BlockSpec((1, W_C), lambda i: (0, 0))
    return _pcall(
        body, name=name, grid=(nt,),
        in_specs=[pl.BlockSpec((POOL_HALO, W_C), lambda i: (jnp.maximum(i * r - 1, 0), 0)), tile, tile,
                  pl.BlockSpec((POOL_HALO, W_C), lambda i: (jnp.minimum((i + 1) * r, s // POOL_HALO - 1), 0)),
                  mat, mat, vec],
        out_specs=[tile, mat, vec],
        out_shape=[jax.ShapeDtypeStruct((s, W_C), F32), jax.ShapeDtypeStruct((W_C, W_C), F32),
                   jax.ShapeDtypeStruct((1, W_C), F32)],
        compiler_params=_params(("arbitrary",)),
    )(p, p, dy, dy, wbd, wbdt, sc)


def _mix_cols(ya_ref, yb_ref, yc_ref, cb):
    if cb < 2:
        return ya_ref[:, cb * LANES:(cb + 1) * LANES]
    if cb < 6:
        return yb_ref[:, (cb - 2) * LANES:(cb - 1) * LANES]
    return yc_ref[:, (cb - 6) * LANES:(cb - 5) * LANES]


def _mix_fwd(ya, yb, yc, g, wo, x, name):
    s = x.shape[0]
    tm = _tile(s, ROW_TILE)

    def body(ya_ref, yb_ref, yc_ref, g_ref, w_ref, x_ref, o_ref, yn_ref):
        sel = _group_mat(LANES)
        for cb in range(D_MODEL // LANES):
            y = _mix_cols(ya_ref, yb_ref, yc_ref, cb)
            r = lax.rsqrt(_dot_sel(y * y, sel) * (1.0 / HEAD_DIM) + EPS)
            yn_ref[:, cb * LANES:(cb + 1) * LANES] = (y * r * g_ref[:, cb * LANES:(cb + 1) * LANES]).astype(_MXU)
        o_ref[...] = x_ref[...] + jnp.dot(yn_ref[...], w_ref[...], preferred_element_type=F32)

    row = lambda n: pl.BlockSpec((tm, n), lambda i: (i, 0))
    return _pcall(
        body, name=name, grid=(s // tm,),
        in_specs=[row(W_A), row(W_B), row(W_C), pl.BlockSpec((1, D_MODEL), lambda i: (0, 0)),
                  pl.BlockSpec((D_MODEL, D_MODEL), lambda i: (0, 0)), row(D_MODEL)],
        out_specs=[row(D_MODEL), row(D_MODEL)],
        out_shape=[jax.ShapeDtypeStruct((s, D_MODEL), F32), jax.ShapeDtypeStruct((s, D_MODEL), _MXU)],
        compiler_params=_params(("parallel",)),
    )(ya, yb, yc, g, wo, x)


def _mix_bwd(dx, ya, yb, yc, g, wot, name):
    s = dx.shape[0]
    tm = _tile(s, ROW_TILE)

    def body(dx_ref, ya_ref, yb_ref, yc_ref, g_ref, wt_ref, dya_ref, dyb_ref, dyc_ref, dg_ref):
        @pl.when(pl.program_id(0) == 0)
        def _():
            dg_ref[...] = jnp.zeros_like(dg_ref)

        dyn = _dot(dx_ref[...], wt_ref[...])
        sel = _group_mat(LANES)
        for cb in range(D_MODEL // LANES):
            cols = slice(cb * LANES, (cb + 1) * LANES)
            y = _mix_cols(ya_ref, yb_ref, yc_ref, cb)
            r = lax.rsqrt(_dot_sel(y * y, sel) * (1.0 / HEAD_DIM) + EPS)
            dyc_ = dyn[:, cols]
            dyg = dyc_ * g_ref[:, cols]
            m2 = _dot_sel(dyg * y, sel) * (1.0 / HEAD_DIM)
            dy = r * dyg - y * (r * r * r) * m2
            dg_ref[:, cols] += jnp.sum(dyc_ * y * r, axis=0, keepdims=True)
            if cb < 2:
                dya_ref[:, cb * LANES:(cb + 1) * LANES] = dy
            elif cb < 6:
                dyb_ref[:, (cb - 2) * LANES:(cb - 1) * LANES] = dy
            else:
                dyc_ref[:, (cb - 6) * LANES:(cb - 5) * LANES] = dy

    row = lambda n: pl.BlockSpec((tm, n), lambda i: (i, 0))
    vec = pl.BlockSpec((1, D_MODEL), lambda i: (0, 0))
    return _pcall(
        body, name=name, grid=(s // tm,),
        in_specs=[row(D_MODEL), row(W_A), row(W_B), row(W_C), vec, pl.BlockSpec((D_MODEL, D_MODEL), lambda i: (0, 0))],
        out_specs=[row(W_A), row(W_B), row(W_C), vec],
        out_shape=[jax.ShapeDtypeStruct((s, W_A), F32), jax.ShapeDtypeStruct((s, W_B), F32),
                   jax.ShapeDtypeStruct((s, W_C), F32), jax.ShapeDtypeStruct((1, D_MODEL), F32)],
        compiler_params=_params(("arbitrary",)),
    )(dx, ya, yb, yc, g, wot)


FFN_CHUNK = 1408
FFN_ROWS = 256
N_CHUNKS = D_FF // FFN_CHUNK
CW_ROWS = 8


def _ffn_up_fwd(x, g, w, name):
    s = x.shape[0]
    n = w.shape[1]
    tm, tn = _tile(s, 512), _tile(n, 2816, LANES)

    def body(x_ref, g_ref, w_ref, z_ref, h_ref):
        @pl.when(pl.program_id(1) == 0)
        def _():
            xv = x_ref[...]
            h_ref[...] = (xv * _rms(xv) * g_ref[...]).astype(_MXU)
        z_ref[...] = jnp.dot(h_ref[...], w_ref[...], preferred_element_type=F32)

    return _pcall(
        body, name=name, grid=(s // tm, n // tn),
        in_specs=[pl.BlockSpec((tm, D_MODEL), lambda i, j: (i, 0)), pl.BlockSpec((1, D_MODEL), lambda i, j: (0, 0)),
                  pl.BlockSpec((D_MODEL, tn), lambda i, j: (0, j))],
        out_specs=[pl.BlockSpec((tm, tn), lambda i, j: (i, j)), pl.BlockSpec((tm, D_MODEL), lambda i, j: (i, 0))],
        out_shape=[jax.ShapeDtypeStruct((s, n), F32), jax.ShapeDtypeStruct((s, D_MODEL), _MXU)],
        compiler_params=_params(("parallel", "arbitrary")),
    )(x, g, w)


def _conv(cur, prev8, cw_ref):
    s1 = _shift_down(cur, prev8, 1)
    s2 = _shift_down(cur, prev8, 2)
    zc = cw_ref[3:4, :] + s2 * cw_ref[0:1, :]
    zc = zc + s1 * cw_ref[1:2, :]
    zc = zc + cur * cw_ref[2:3, :]
    return zc, s1, s2


def _halo_specs(tm, s):
    r = tm // SUBLANES
    prev = lambda off: pl.BlockSpec((SUBLANES, FFN_CHUNK), lambda i, j: (jnp.maximum(i * r - 1, 0), j + off))
    nxt = lambda off: pl.BlockSpec((SUBLANES, FFN_CHUNK), lambda i, j: (jnp.minimum((i + 1) * r, s // SUBLANES - 1), j + off))
    return prev, nxt


def _ffn_down_fwd(z, cw, wd, x, name):
    s = x.shape[0]
    tm = _tile(s, FFN_ROWS)
    prev, _ = _halo_specs(tm, s)

    def body(zg_ref, zu_ref, pg_ref, pu_ref, cg_ref, cu_ref, w_ref, x_ref, o_ref, act_ref, acc_ref):
        i, j = pl.program_id(0), pl.program_id(1)
        first = i > 0
        zg, _, _ = _conv(zg_ref[...], jnp.where(first, pg_ref[...], 0.0), cg_ref)
        zu, _, _ = _conv(zu_ref[...], jnp.where(first, pu_ref[...], 0.0), cu_ref)
        act = (zg * jax.nn.sigmoid(zg) * zu).astype(_MXU)
        act_ref[...] = act

        @pl.when(j == 0)
        def _():
            acc_ref[...] = x_ref[...]
        acc_ref[...] += jnp.dot(act, w_ref[...], preferred_element_type=F32)

        @pl.when(j == N_CHUNKS - 1)
        def _():
            o_ref[...] = acc_ref[...]

    zt = lambda off: pl.BlockSpec((tm, FFN_CHUNK), lambda i, j: (i, j + off))
    cwt = lambda off: pl.BlockSpec((CW_ROWS, FFN_CHUNK), lambda i, j: (0, j + off))
    return _pcall(
        body, name=name, grid=(s // tm, N_CHUNKS),
        in_specs=[zt(0), zt(N_CHUNKS), prev(0), prev(N_CHUNKS), cwt(0), cwt(N_CHUNKS),
                  pl.BlockSpec((FFN_CHUNK, D_MODEL), lambda i, j: (j, 0)), pl.BlockSpec((tm, D_MODEL), lambda i, j: (i, 0))],
        out_specs=[pl.BlockSpec((tm, D_MODEL), lambda i, j: (i, 0)), pl.BlockSpec((tm, FFN_CHUNK), lambda i, j: (i, j))],
        out_shape=[jax.ShapeDtypeStruct((s, D_MODEL), F32), jax.ShapeDtypeStruct((s, D_FF), _MXU)],
        scratch_shapes=[pltpu.VMEM((tm, D_MODEL), F32)],
        compiler_params=_params(("parallel", "arbitrary")),
    )(z, z, z, z, cw, cw, wd, x)


def _ffn_down_bwd(dx, z, cw, wdt, name):
    s = dx.shape[0]
    tm = _tile(s, FFN_ROWS)

    def body(dx_ref, zg_ref, zu_ref, pg_ref, pu_ref, cg_ref, cu_ref, wt_ref, dg_ref, du_ref, dcg_ref, dcu_ref):
        i = pl.program_id(1)
        first = i > 0

        @pl.when(i == 0)
        def _():
            dcg_ref[...] = jnp.zeros_like(dcg_ref)
            dcu_ref[...] = jnp.zeros_like(dcu_ref)

        dact = _dot(dx_ref[...], wt_ref[...])
        zg, g1, g2 = _conv(zg_ref[...], jnp.where(first, pg_ref[...], 0.0), cg_ref)
        zu, u1, u2 = _conv(zu_ref[...], jnp.where(first, pu_ref[...], 0.0), cu_ref)
        sg = jax.nn.sigmoid(zg)
        silu = zg * sg
        dzu = dact * silu
        dzg = dact * zu * (sg * (1.0 + zg * (1.0 - sg)))
        dg_ref[...] = dzg
        du_ref[...] = dzu
        for ref, dzc, cur, s1, s2 in ((dcg_ref, dzg, zg_ref[...], g1, g2), (dcu_ref, dzu, zu_ref[...], u1, u2)):
            ref[0:1, :] += jnp.sum(dzc * s2, axis=0, keepdims=True)
            ref[1:2, :] += jnp.sum(dzc * s1, axis=0, keepdims=True)
            ref[2:3, :] += jnp.sum(dzc * cur, axis=0, keepdims=True)
            ref[3:4, :] += jnp.sum(dzc, axis=0, keepdims=True)

    zt = lambda off: pl.BlockSpec((tm, FFN_CHUNK), lambda j, i: (i, j + off))
    r = tm // SUBLANES
    pv = lambda off: pl.BlockSpec((SUBLANES, FFN_CHUNK), lambda j, i: (jnp.maximum(i * r - 1, 0), j + off))
    cwt = lambda off: pl.BlockSpec((CW_ROWS, FFN_CHUNK), lambda j, i: (0, j + off))
    out_t = pl.BlockSpec((tm, FFN_CHUNK), lambda j, i: (i, j))
    dc_t = pl.BlockSpec((CW_ROWS, FFN_CHUNK), lambda j, i: (0, j))
    dzg, dzu, dcg, dcu = _pcall(
        body, name=name, grid=(N_CHUNKS, s // tm),
        in_specs=[pl.BlockSpec((tm, D_MODEL), lambda j, i: (i, 0)), zt(0), zt(N_CHUNKS), pv(0), pv(N_CHUNKS),
                  cwt(0), cwt(N_CHUNKS), pl.BlockSpec((D_MODEL, FFN_CHUNK), lambda j, i: (0, j))],
        out_specs=[out_t, out_t, dc_t, dc_t],
        out_shape=[jax.ShapeDtypeStruct((s, D_FF), F32), jax.ShapeDtypeStruct((s, D_FF), F32),
                   jax.ShapeDtypeStruct((CW_ROWS, D_FF), F32), jax.ShapeDtypeStruct((CW_ROWS, D_FF), F32)],
        compiler_params=_params(("parallel", "arbitrary")),
    )(dx, z, z, z, z, cw, cw, wdt)
    return dzg, dzu, jnp.concatenate([dcg, dcu], axis=1)


def _ffn_up_bwd(dzg, dzu, cw, wut, x, g, dres, name):
    s = x.shape[0]
    tm = _tile(s, FFN_ROWS)
    _, nxt = _halo_specs(tm, s)
    nt = s // tm

    def body(dg_ref, du_ref, ng_ref, nu_ref, cg_ref, cu_ref, wg_ref, wu_ref, x_ref, g_ref, dres_ref,
             dzg_ref, dzu_ref, dx_ref, dgn_ref, acc_ref):
        i, j = pl.program_id(0), pl.program_id(1)
        last = i < nt - 1

        def conv_bwd(cur, nxt8, cw_ref):
            up1 = _shift_up(cur, nxt8, 1)
            up2 = _shift_up(cur, nxt8, 2)
            return cur * cw_ref[2:3, :] + up1 * cw_ref[1:2, :] + up2 * cw_ref[0:1, :]

        dzg_ = conv_bwd(dg_ref[...], jnp.where(last, ng_ref[...], 0.0), cg_ref).astype(_MXU)
        dzu_ = conv_bwd(du_ref[...], jnp.where(last, nu_ref[...], 0.0), cu_ref).astype(_MXU)
        dzg_ref[...] = dzg_
        dzu_ref[...] = dzu_

        @pl.when(j == 0)
        def _():
            acc_ref[...] = jnp.zeros_like(acc_ref)
        acc_ref[...] += (jnp.dot(dzg_, wg_ref[...], preferred_element_type=F32)
                         + jnp.dot(dzu_, wu_ref[...], preferred_element_type=F32))

        @pl.when((i == 0) & (j == 0))
        def _():
            dgn_ref[...] = jnp.zeros_like(dgn_ref)

        @pl.when(j == N_CHUNKS - 1)
        def _():
            dx, dgn = _rms_bwd(x_ref[...], g_ref[...], acc_ref[...])
            dx_ref[...] = dres_ref[...] + dx
            dgn_ref[...] += dgn

    zt = pl.BlockSpec((tm, FFN_CHUNK), lambda i, j: (i, j))
    cwt = lambda off: pl.BlockSpec((CW_ROWS, FFN_CHUNK), lambda i, j: (0, j + off))
    wt = lambda off: pl.BlockSpec((FFN_CHUNK, D_MODEL), lambda i, j: (j + off, 0))
    row = pl.BlockSpec((tm, D_MODEL), lambda i, j: (i, 0))
    vec = pl.BlockSpec((1, D_MODEL), lambda i, j: (0, 0))
    return _pcall(
        body, name=name, grid=(nt, N_CHUNKS),
        in_specs=[zt, zt, nxt(0), nxt(0), cwt(0), cwt(N_CHUNKS), wt(0), wt(N_CHUNKS), row, vec, row],
        out_specs=[zt, zt, row, vec],
        out_shape=[jax.ShapeDtypeStruct((s, D_FF), _MXU), jax.ShapeDtypeStruct((s, D_FF), _MXU),
                   jax.ShapeDtypeStruct((s, D_MODEL), F32), jax.ShapeDtypeStruct((1, D_MODEL), F32)],
        scratch_shapes=[pltpu.VMEM((tm, D_MODEL), F32)],
        compiler_params=_params(("arbitrary", "arbitrary")),
    )(dzg, dzu, dzg, dzu, cw, cw, wut, wut, x, g, dres)


def _final_loss(x, g, tgt, name):
    s = x.shape[0]
    tm = _tile(s, ROW_TILE)

    def body(x_ref, g_ref, t_ref, loss_ref, dx_ref, dg_ref):
        @pl.when(pl.program_id(0) == 0)
        def _():
            loss_ref[...] = jnp.zeros_like(loss_ref)
            dg_ref[...] = jnp.zeros_like(dg_ref)

        xv, gv = x_ref[...], g_ref[...]
        err = xv * _rms(xv) * gv - t_ref[...]
        per_tok = jnp.mean(err * err, axis=-1, keepdims=True)
        loss_ref[...] += 0.5 * jnp.sum(per_tok, axis=0, keepdims=True)
        dx, dg = _rms_bwd(xv, gv, err * (1.0 / D_MODEL))
        dx_ref[...] = dx
        dg_ref[...] += dg

    row = pl.BlockSpec((tm, D_MODEL), lambda i: (i, 0))
    vec = pl.BlockSpec((1, D_MODEL), lambda i: (0, 0))
    return _pcall(
        body, name=name, grid=(s // tm,),
        in_specs=[row, vec, row], out_specs=[pl.BlockSpec((1, 1), lambda i: (0, 0)), row, vec],
        out_shape=[jax.ShapeDtypeStruct((1, 1), F32), jax.ShapeDtypeStruct((s, D_MODEL), F32),
                   jax.ShapeDtypeStruct((1, D_MODEL), F32)],
        compiler_params=_params(("arbitrary",)),
    )(x, g, tgt)


def _adamw(parts, w, m, v, name):
    r, c = w.shape
    tr = _tile(r, 256)
    c1 = 1.0 - ADAM_B1 ** ADAM_STEP
    c2 = 1.0 - ADAM_B2 ** ADAM_STEP

    def body(p_ref, w_ref, m_ref, v_ref, g_ref, d_ref, mo_ref, vo_ref):
        g = p_ref[0].astype(F32)
        for i in range(1, N_DEV):
            g = g + p_ref[i].astype(F32)
        mn = ADAM_B1 * m_ref[...] + (1.0 - ADAM_B1) * g
        vn = ADAM_B2 * v_ref[...] + (1.0 - ADAM_B2) * (g * g)
        g_ref[...] = g
        mo_ref[...] = mn
        vo_ref[...] = vn
        d_ref[...] = -ADAM_LR * ((mn / c1) / (jnp.sqrt(vn / c2) + ADAM_EPS) + ADAM_WD * w_ref[...])

    t2 = pl.BlockSpec((tr, c), lambda i: (i, 0))
    return _pcall(
        body, name=name, grid=(r // tr,),
        in_specs=[pl.BlockSpec((N_DEV, tr, c), lambda i: (0, i, 0)), t2, t2, t2],
        out_specs=[t2] * 4, out_shape=[jax.ShapeDtypeStruct((r, c), F32)] * 4,
        compiler_params=_params(("parallel",)),
    )(parts, w, m, v)


SMALL = ("norm1_g", "sgu_norm_g", "sgu_w", "sgu_b", "pool_w", "pool_scale", "mix_norm_g", "norm2_g", "conv_b", "final_g")
SHARDED = ("w_in", "w_o", "w_up", "conv_w", "w_down")
ORDER = ("norm1_g", "w_in", "sgu_norm_g", "sgu_w", "sgu_b", "pool_w", "pool_scale", "mix_norm_g", "w_o", "norm2_g",
         "w_up", "conv_w", "conv_b", "w_down", "final_g")


def _pack(tree):
    return jnp.concatenate([tree[n].reshape(-1) for n in SMALL]).reshape(-1, LANES)


def _unpack(flat, like):
    out, off = {}, 0
    flat = flat.reshape(-1)
    for n in SMALL:
        size = math.prod(like[n].shape)
        out[n] = flat[off:off + size].reshape(like[n].shape)
        off += size
    return out


def _block_diag(pw):
    z = jnp.zeros((W_C, W_C), pw.dtype)
    for g in range(4):
        z = z.at[g * 64:(g + 1) * 64, g * 64:(g + 1) * 64].set(pw[g])
    return z


def kernel(x, norm1_g, w_in, sgu_norm_g, sgu_w, sgu_b, pool_w, pool_scale, mix_norm_g, w_o, norm2_g, w_up, conv_w, conv_b, w_down, final_g, loss_target, m_norm1_g, m_w_in, m_sgu_norm_g, m_sgu_w, m_sgu_b, m_pool_w, m_pool_scale, m_mix_norm_g, m_w_o, m_norm2_g, m_w_up, m_conv_w, m_conv_b, m_w_down, m_final_g, v_norm1_g, v_w_in, v_sgu_norm_g, v_sgu_w, v_sgu_b, v_pool_w, v_pool_scale, v_mix_norm_g, v_w_o, v_norm2_g, v_w_up, v_conv_w, v_conv_b, v_w_down, v_final_g):
    weights = dict(norm1_g=norm1_g, w_in=w_in, sgu_norm_g=sgu_norm_g, sgu_w=sgu_w, sgu_b=sgu_b, pool_w=pool_w,
                   pool_scale=pool_scale, mix_norm_g=mix_norm_g, w_o=w_o, norm2_g=norm2_g, w_up=w_up, conv_w=conv_w,
                   conv_b=conv_b, w_down=w_down, final_g=final_g)
    mom = dict(norm1_g=m_norm1_g, w_in=m_w_in, sgu_norm_g=m_sgu_norm_g, sgu_w=m_sgu_w, sgu_b=m_sgu_b, pool_w=m_pool_w,
               pool_scale=m_pool_scale, mix_norm_g=m_mix_norm_g, w_o=m_w_o, norm2_g=m_norm2_g, w_up=m_w_up,
               conv_w=m_conv_w, conv_b=m_conv_b, w_down=m_w_down, final_g=m_final_g)
    var = dict(norm1_g=v_norm1_g, w_in=v_w_in, sgu_norm_g=v_sgu_norm_g, sgu_w=v_sgu_w, sgu_b=v_sgu_b, pool_w=v_pool_w,
               pool_scale=v_pool_scale, mix_norm_g=v_mix_norm_g, w_o=v_w_o, norm2_g=v_norm2_g, w_up=v_w_up,
               conv_w=v_conv_w, conv_b=v_conv_b, w_down=v_w_down, final_g=v_final_g)
    depth = w_in.shape[0]
    s = x.shape[1]
    xs = x.reshape(s, D_MODEL)
    tgt = loss_target.reshape(s, D_MODEL)

    assert depth >= 2
    (g_in0,) = _exchange([w_in[0].astype(_MXU)], "gather_w_in0", False)
    w_in0 = jnp.transpose(g_in0, (1, 0, 2)).reshape(D_MODEL, IN_COLS)
    gather_rest = _Exchange([w_in[1:].astype(_MXU), w_o.astype(_MXU), w_up.astype(_MXU), conv_w, w_down.astype(_MXU)], False)

    tril = jnp.tril(jnp.ones((CHUNK, CHUNK), bool))
    layers = []
    for l in range(depth):
        wbd = _block_diag(pool_w[l])
        layers.append(dict(
            g1=norm1_g[l][None], gn=sgu_norm_g[l][None], wm=jnp.where(tril[None], sgu_w[l], 0.0).astype(_MXU),
            bias=jnp.repeat(sgu_b[l].T, HEAD_DIM, axis=1),
            wbd=wbd.astype(_MXU), wbd_t=wbd.T.astype(_MXU), sc=pool_scale[l][None],
            gmix=mix_norm_g[l][None], g2=norm2_g[l][None]))
    layers[0].update(w_in=w_in0, w_in_t=w_in0.T)

    def place_gathered(g_in, g_o, g_up, g_cw, g_dn):
        full_in = jnp.transpose(g_in, (1, 2, 0, 3)).reshape(depth - 1, D_MODEL, IN_COLS)
        full_o = jnp.transpose(g_o, (1, 0, 2, 3)).reshape(depth, D_MODEL, D_MODEL)
        full_up = jnp.transpose(g_up, (1, 2, 0, 3)).reshape(depth, D_MODEL, 2 * D_FF)
        full_cw = jnp.transpose(g_cw, (1, 2, 0, 3)).reshape(depth, 3, 2 * D_FF)
        full_dn = jnp.transpose(g_dn, (1, 0, 2, 3)).reshape(depth, D_FF, D_MODEL)
        for l in range(depth):
            if l > 0:
                layers[l].update(w_in=full_in[l - 1], w_in_t=full_in[l - 1].T)
            layers[l].update(
                w_o=full_o[l], w_o_t=full_o[l].T, w_up=full_up[l], w_up_t=full_up[l].T,
                cw=jnp.concatenate([full_cw[l], conv_b[l][None], jnp.zeros((CW_ROWS - 4, 2 * D_FF), F32)], axis=0),
                w_dn=full_dn[l], w_dn_t=full_dn[l].T)

    saved = []
    cur = xs
    for l, p in enumerate(layers):
        a_in, qkv, p_in, h1 = _inproj_fwd(cur, p["g1"], p["w_in"], f"inproj_fwd{l}")
        y_a = _sgu_fwd(a_in, p["gn"], p["wm"], p["bias"], f"sgu_fwd{l}")
        y_b, cm, gathered_w = _attn_fwd(qkv, f"attn_fwd{l}", gather_rest if l == 0 else None)
        if l == 0:
            place_gathered(*gathered_w)
        y_c = _pool_fwd(p_in, p["wbd"], p["sc"], f"pool_fwd{l}")
        x_mid, yn = _mix_fwd(y_a, y_b, y_c, p["gmix"], p["w_o"], cur, f"mix_fwd{l}")
        z, h2 = _ffn_up_fwd(x_mid, p["g2"], p["w_up"], f"ffn_up_fwd{l}")
        x_out, act = _ffn_down_fwd(z, p["cw"], p["w_dn"], x_mid, f"ffn_down_fwd{l}")
        saved.append(dict(x_in=cur, a_in=a_in, qkv=qkv, p_in=p_in, h1=h1, y_a=y_a, y_b=y_b, cm=cm, y_c=y_c, x_mid=x_mid,
                          yn=yn, z=z, h2=h2, act=act))
        cur = x_out
    loss_part, dx, dg_final = _final_loss(cur, final_g[None], tgt, "final_loss")

    small = {n: [None] * depth for n in SMALL if n != "final_g"}
    big = {n: [None] * depth for n in SHARDED}
    early = [(n, l) for n in SHARDED for l in range(depth) if (n, l) != ("w_in", 0)]
    wire = lambda n, t: t if n == "conv_w" else t.astype(GRAD_WIRE)
    for l in reversed(range(depth)):
        p, sv = layers[l], saved[l]
        dzg, dzu, dcw = _ffn_down_bwd(dx, sv["z"], p["cw"], p["w_dn_t"], f"ffn_down_bwd{l}")
        big["w_down"][l] = _mm_tn(sv["act"], dx, f"dw_down{l}").reshape(N_DEV, D_FF // N_DEV, D_MODEL)
        dzg_b, dzu_b, dx_mid, dg2 = _ffn_up_bwd(dzg, dzu, p["cw"], p["w_up_t"], sv["x_mid"], p["g2"], dx, f"ffn_up_bwd{l}")
        dw_up = jnp.stack([_mm_tn(sv["h2"], dzg_b, f"dw_up_g{l}"), _mm_tn(sv["h2"], dzu_b, f"dw_up_u{l}")])
        big["w_up"][l] = jnp.transpose(dw_up.reshape(2, D_MODEL, N_DEV // 2, 2 * D_FF // N_DEV), (0, 2, 1, 3)).reshape(
            N_DEV, D_MODEL, 2 * D_FF // N_DEV)
        big["conv_w"][l] = jnp.transpose(dcw[0:3].reshape(3, N_DEV, 2 * D_FF // N_DEV), (1, 0, 2))
        small["conv_b"][l] = dcw[3]
        small["norm2_g"][l] = dg2[0]
        dya, dyb, dyc, dgmix = _mix_bwd(dx_mid, sv["y_a"], sv["y_b"], sv["y_c"], p["gmix"], p["w_o_t"], f"mix_bwd{l}")
        small["mix_norm_g"][l] = dgmix[0]
        big["w_o"][l] = _mm_tn(sv["yn"], dx_mid, f"dw_o{l}").reshape(N_DEV, D_MODEL // N_DEV, D_MODEL)
        dp, dwbd, dsc = _pool_bwd(sv["p_in"], dyc, p["wbd"], p["wbd_t"], p["sc"], f"pool_bwd{l}")
        small["pool_w"][l] = jnp.stack([dwbd[g * 64:(g + 1) * 64, g * 64:(g + 1) * 64] for g in range(4)])
        small["pool_scale"][l] = dsc[0]
        scatter_early = _Exchange([wire(n, big[n][ll]) for n, ll in early], True) if l == 0 else None
        dq, dk, dv, recv_early = _attn_bwd(sv["qkv"], sv["cm"], dyb, f"attn_bwd{l}", scatter_early)
        if l == 0:
            recv_early_all = recv_early
        da, dwm, dbias, dgn = _sgu_bwd(sv["a_in"], dya, p["gn"], p["wm"], p["bias"], f"sgu_bwd{l}")
        small["sgu_w"][l] = dwm
        small["sgu_b"][l] = jnp.sum(dbias.reshape(CHUNK, 4, HEAD_DIM), axis=-1).T
        small["sgu_norm_g"][l] = dgn[0]
        dx, dg1 = _inproj_bwd(da, dq, dk, dv, dp, p["w_in_t"], sv["x_in"], p["g1"], dx_mid, f"inproj_bwd{l}")
        small["norm1_g"][l] = dg1[0]
        pieces = (da, dq, dk, dv, dp)
        dw_in = jnp.concatenate([_mm_tn(sv["h1"], t, f"dw_in{i}_{l}") for i, t in enumerate(pieces)], axis=1)
        big["w_in"][l] = jnp.transpose(dw_in.reshape(D_MODEL, N_DEV, IN_COLS // N_DEV), (1, 0, 2))

    part = {n: jnp.stack(small[n]) for n in small}
    part["final_g"] = dg_final[0]
    packed = _pack(part)
    recv_in0, gathered = _exchange([wire("w_in", big["w_in"][0]), jnp.broadcast_to(packed, (N_DEV,) + packed.shape)],
                                   "scatter_last", True)
    recv = dict(zip(early, recv_early_all))
    recv[("w_in", 0)] = recv_in0

    out_g, out_d, out_m, out_v = {}, {}, {}, {}
    for n in SHARDED:
        res = [_adamw(recv[(n, l)], weights[n][l], mom[n][l], var[n][l], f"adamw_{n}{l}") for l in range(depth)]
        out_g[n], out_d[n], out_m[n], out_v[n] = (jnp.stack([r[i] for r in res]) for i in range(4))

    sg, sd, sm, sv_ = _adamw(gathered, _pack(weights), _pack(mom), _pack(var), "adamw_small")
    for tree, flat in ((out_g, sg), (out_d, sd), (out_m, sm), (out_v, sv_)):
        tree.update(_unpack(flat, weights))

    loss = lax.psum(loss_part[0, 0], ("x", "y", "c"))
    grad_x = dx.reshape(1, s, D_MODEL)
    return (loss, grad_x, *[out_g[n] for n in ORDER], *[out_d[n] for n in ORDER], *[out_m[n] for n in ORDER],
            *[out_v[n] for n in ORDER])
```
